```python
import jax
import jax.numpy as jnp
from jax import lax
import numpy as np

D_MODEL = 4096
BATCH = 4
SEQ = 2048
DEPTH = 1
DEC_BATCH = 128
DEC_SEQ = 1
PAST_LEN = 16384
PAGE_SIZE = 128

ML_HEADS = 4
ML_DK = D_MODEL // 16
ML_DV = D_MODEL // 8
ML_WIDTH = ML_HEADS * ML_DV
ML_CHUNK = 64
GATE_SOFTCAP = 15.0
PAD_IGATE = -1e30
RW_HEAD = 64
RW_WIDTH = D_MODEL - ML_WIDTH
RW_HEADS = RW_WIDTH // RW_HEAD
RW_DECAY_LORA = 96
RW_A_LORA = 96
RW_GATE_LORA = 64
RW_GN_EPS = 64e-5
MIX_WIDTH = ML_WIDTH + RW_WIDTH
ML_PROJ = 2 * ML_HEADS * ML_DK + 2 * ML_WIDTH + 2 * ML_HEADS
RW_PROJ = 3 * RW_WIDTH + RW_DECAY_LORA + RW_A_LORA + RW_GATE_LORA
IN_PROJ = ML_PROJ + RW_PROJ
N_GROUPS = 4
EXPERTS_PER_GROUP = 8
N_EXPERTS = N_GROUPS * EXPERTS_PER_GROUP
TOP_K = 2
EXPERT_FF = D_MODEL // 4
MOE_BLOCK = 128
EPS = 1e-6

kernel_name = 'hymba_mlstm_rwkv7_hmoe_step'


def rmsnorm(x, w):
    xf = x.astype(jnp.float32)
    y = xf * lax.rsqrt(jnp.mean(xf * xf, axis=-1, keepdims=True) + EPS)
    return (y * w.astype(jnp.float32)).astype(x.dtype)


def soft_cap(x, cap):
    return cap * jnp.tanh(x / cap)


def to_chunks(a, n_chunks, chunk):
    a = a.reshape((a.shape[0], n_chunks, chunk) + a.shape[2:])
    return jnp.moveaxis(a, (1, 3), (0, 2))


def mlstm_chunked(q, k, v, i_pre, logf, c0, n0, m0):
    bsz, seq = q.shape[0], q.shape[1]
    chunk = min(ML_CHUNK, seq)
    pad = (-seq) % chunk
    if pad:
        def padt(a, val):
            return jnp.pad(a, [(0, 0), (0, pad)] + [(0, 0)] * (a.ndim - 2), constant_values=val)
        q, k, v = padt(q, 0.0), padt(k, 0.0), padt(v, 0.0)
        i_pre, logf = padt(i_pre, PAD_IGATE), padt(logf, 0.0)
    n_chunks = (seq + pad) // chunk
    causal = jnp.tril(jnp.ones((chunk, chunk), dtype=bool))

    def step(carry, inp):
        c, n, m = carry
        qc, kc, vc, ic, fc = inp
        b = jnp.cumsum(fc, axis=-1)
        inter = b + m[..., None]
        dmat = jnp.where(causal, b[..., :, None] - b[..., None, :] + ic[..., None, :], -jnp.inf)
        m_t = jnp.maximum(inter, jnp.max(dmat, axis=-1))
        w_int = jnp.exp(inter - m_t)
        s = jnp.einsum('bhtd,bhsd->bhts', qc, kc) * jnp.exp(dmat - m_t[..., None])
        num = w_int[..., None] * jnp.einsum('bhtd,bhde->bhte', qc, c) + jnp.einsum('bhts,bhse->bhte', s, vc)
        den = w_int * jnp.einsum('bhtd,bhd->bht', qc, n) + jnp.sum(s, axis=-1)
        h = num / jnp.maximum(jnp.abs(den), jnp.exp(-m_t))[..., None]
        m_new = m_t[..., -1]
        a_end = jnp.exp(b[..., -1:] - b + ic - m_new[..., None])
        dec_old = jnp.exp(b[..., -1] + m - m_new)
        c_new = dec_old[..., None, None] * c + jnp.einsum('bhsd,bhse->bhde', kc * a_end[..., None], vc)
        n_new = dec_old[..., None] * n + jnp.einsum('bhs,bhsd->bhd', a_end, kc)
        return (c_new, n_new, m_new), h

    xs = tuple(to_chunks(a, n_chunks, chunk) for a in (q, k, v, i_pre, logf))
    (c, n, m), h = lax.scan(step, (c0, n0, m0), xs)
    h = jnp.moveaxis(h, (0, 2), (1, 3)).reshape(bsz, n_chunks * chunk, ML_HEADS, ML_DV)[:, :seq]
    return h, c, n, m


def mlstm_mix(p, c0, n0, m0, b_i, b_f, norm_w):
    bsz, seq, _ = p.shape
    p = p.astype(jnp.float32)
    qd = ML_HEADS * ML_DK
    q = p[..., :qd].reshape(bsz, seq, ML_HEADS, ML_DK) * (ML_DK ** -0.5)
    k = p[..., qd:2 * qd].reshape(bsz, seq, ML_HEADS, ML_DK)
    v = p[..., 2 * qd:2 * qd + ML_WIDTH].reshape(bsz, seq, ML_HEADS, ML_DV)
    o = p[..., 2 * qd + ML_WIDTH:2 * qd + 2 * ML_WIDTH]
    gi = p[..., 2 * qd + 2 * ML_WIDTH:2 * qd + 2 * ML_WIDTH + ML_HEADS]
    gf = p[..., 2 * qd + 2 * ML_WIDTH + ML_HEADS:]
    i_pre = soft_cap(gi + b_i, GATE_SOFTCAP)
    logf = jax.nn.log_sigmoid(soft_cap(gf + b_f, GATE_SOFTCAP))
    h, c, n, m = mlstm_chunked(q, k, v, i_pre, logf, c0.astype(jnp.float32),
                               n0.astype(jnp.float32), m0.astype(jnp.float32))
    h = h * lax.rsqrt(jnp.mean(h * h, axis=-1, keepdims=True) + EPS) * norm_w.astype(jnp.float32).reshape(ML_HEADS, ML_DV)
    y = h.reshape(bsz, seq, ML_WIDTH) * jax.nn.sigmoid(o)
    return y, c, n, m


def rwkv7_mix(p, shift0, s0, mu, w0, w2, a0, a2, g2, k_k, k_a, r_k, ln_w, ln_b):
    bsz, seq, _ = p.shape
    p = p.astype(jnp.float32)
    prev = jnp.concatenate([shift0.astype(jnp.float32)[:, None], p[:, :-1]], axis=1)
    xs = p + (prev - p) * mu
    W = RW_WIDTH
    r, k, v = xs[..., :W], xs[..., W:2 * W], xs[..., 2 * W:3 * W]
    zw = xs[..., 3 * W:3 * W + RW_DECAY_LORA]
    za = xs[..., 3 * W + RW_DECAY_LORA:3 * W + RW_DECAY_LORA + RW_A_LORA]
    zg = xs[..., 3 * W + RW_DECAY_LORA + RW_A_LORA:]
    w = -jax.nn.softplus(-(w0 + jnp.tanh(zw) @ w2)) - 0.5
    decay = jnp.exp(-jnp.exp(w))
    a = jax.nn.sigmoid(a0 + za @ a2)
    g = jax.nn.sigmoid(zg) @ g2
    hs = lambda t: t.reshape(bsz, seq, RW_HEADS, RW_HEAD)
    kk = hs(k * k_k)
    kk = kk / jnp.maximum(jnp.sqrt(jnp.sum(kk * kk, axis=-1, keepdims=True)), 1e-12)
    k = k * (1.0 + (a - 1.0) * k_a)
    r, k, v, decay, a = hs(r), hs(k), hs(v), hs(decay), hs(a)

    def step(st, inp):
        r_t, d_t, k_t, v_t, kk_t, a_t = inp
        sa = jnp.einsum('bhij,bhj->bhi', st, -kk_t)
        st = st * d_t[:, :, None, :] + sa[..., None] * (kk_t * a_t)[:, :, None, :] + v_t[..., None] * k_t[:, :, None, :]
        return st, jnp.einsum('bhij,bhj->bhi', st, r_t)

    tm = lambda t: jnp.moveaxis(t, 1, 0)
    s_fin, y = lax.scan(step, s0.astype(jnp.float32), (tm(r), tm(decay), tm(k), tm(v), tm(kk), tm(a)))
    y = jnp.moveaxis(y, 0, 1)
    mean = jnp.mean(y, axis=-1, keepdims=True)
    var = jnp.mean(jnp.square(y - mean), axis=-1, keepdims=True)
    y = ((y - mean) * lax.rsqrt(var + RW_GN_EPS)).reshape(bsz, seq, W) * ln_w + ln_b
    y = y + (jnp.sum(r * k * r_k, axis=-1, keepdims=True) * v).reshape(bsz, seq, W)
    return y * g, s_fin, p[:, -1]


def moe_dispatch(x, e_idx, gate, w1, w3, w2):
    T, D = x.shape
    A = T * TOP_K
    e_flat = e_idx.reshape(A)
    tok = jnp.arange(A, dtype=jnp.int32) // TOP_K
    g_flat = gate.reshape(A)
    order = jnp.argsort(e_flat)
    e_s, tok_s, g_s = e_flat[order], tok[order], g_flat[order]
    counts = jnp.zeros((N_EXPERTS,), jnp.int32).at[e_flat].add(1)
    padded = (counts + MOE_BLOCK - 1) // MOE_BLOCK * MOE_BLOCK
    pad_end = jnp.cumsum(padded)
    pad_start = pad_end - padded
    start = jnp.cumsum(counts) - counts
    dest = pad_start[e_s] + jnp.arange(A, dtype=jnp.int32) - start[e_s]
    n_blocks = -(-(A + N_EXPERTS * (MOE_BLOCK - 1)) // MOE_BLOCK)
    P = n_blocks * MOE_BLOCK
    slot_tok = jnp.full((P,), T, jnp.int32).at[dest].set(tok_s)
    slot_gate = jnp.zeros((P,), x.dtype).at[dest].set(g_s)
    block_start = jnp.arange(n_blocks, dtype=jnp.int32) * MOE_BLOCK
    block_expert = jnp.minimum(jnp.searchsorted(pad_end, block_start, side='right'), N_EXPERTS - 1)
    x_pad = jnp.concatenate([x, jnp.zeros((1, D), x.dtype)], axis=0)
    xb = x_pad[slot_tok].reshape(n_blocks, MOE_BLOCK, D)

    def block_ffn(args):
        xb_i, e = args
        h = jax.nn.silu(xb_i @ w1[e]) * (xb_i @ w3[e])
        return h @ w2[e]

    yb = lax.map(block_ffn, (xb, block_expert)).reshape(P, D)
    return jax.ops.segment_sum(yb * slot_gate[:, None], slot_tok, num_segments=T + 1)[:T]


def hier_moe(x, wg, bg, we, be, w1, w3, w2):
    T = x.shape[0]
    xf = x.astype(jnp.float32)
    pg = jax.nn.softmax(xf @ wg.astype(jnp.float32) + bg.astype(jnp.float32), axis=-1)
    g_idx = jnp.argmax(pg, axis=-1).astype(jnp.int32)
    p_sel = jnp.take_along_axis(pg, g_idx[:, None], axis=-1)
    le = (xf @ we.astype(jnp.float32) + be.astype(jnp.float32)).reshape(T, N_GROUPS, EXPERTS_PER_GROUP)
    le = jnp.take_along_axis(le, g_idx[:, None, None], axis=1)[:, 0]
    pe = jax.nn.softmax(le, axis=-1)
    top_v, top_i = lax.top_k(pe, TOP_K)
    gate = p_sel * top_v / jnp.sum(top_v, axis=-1, keepdims=True)
    e_idx = g_idx[:, None] * EXPERTS_PER_GROUP + top_i.astype(jnp.int32)
    return moe_dispatch(x, e_idx, gate.astype(x.dtype), w1, w3, w2)


def setup_inputs(seed: int = 0) -> dict:
    key = jax.random.key(seed)
    ks = iter(jax.random.split(key, 64))
    nrm = lambda shape, scale: jax.random.normal(next(ks), shape, jnp.float32) * scale
    unif = lambda shape, lo, hi: jax.random.uniform(next(ks), shape, jnp.float32, lo, hi)
    L = DEPTH
    return {
        'x_prompt': nrm((BATCH, SEQ, D_MODEL), 1.0),
        'x_sample': nrm((DEC_BATCH, DEC_SEQ, D_MODEL), 1.0),
        'state_mlstm_c': nrm((L, DEC_BATCH, ML_HEADS, ML_DK, ML_DV), 1.0),
        'state_mlstm_n': nrm((L, DEC_BATCH, ML_HEADS, ML_DK), 1.0),
        'state_mlstm_m': nrm((L, DEC_BATCH, ML_HEADS), 0.5),
        'state_rwkv': nrm((L, DEC_BATCH, RW_HEADS, RW_HEAD, RW_HEAD), 0.1),
        'state_rwkv_shift': nrm((L, DEC_BATCH, RW_PROJ), 1.0),
        'norm1_w': 1.0 + nrm((L, D_MODEL), 0.02),
        'w_in': nrm((L, D_MODEL, IN_PROJ), D_MODEL ** -0.5),
        'w_out': nrm((L, MIX_WIDTH, D_MODEL), MIX_WIDTH ** -0.5),
        'ml_b_i': nrm((L, ML_HEADS), 0.5),
        'ml_b_f': unif((L, ML_HEADS), 3.0, 6.0),
        'ml_norm_w': 1.0 + nrm((L, ML_WIDTH), 0.02),
        'rw_mu': unif((L, RW_PROJ), 0.0, 1.0),
        'rw_w0': unif((L, RW_WIDTH), -6.5, -1.5),
        'rw_w2': nrm((L, RW_DECAY_LORA, RW_WIDTH), RW_DECAY_LORA ** -0.5),
        'rw_a0': nrm((L, RW_WIDTH), 0.1),
        'rw_a2': nrm((L, RW_A_LORA, RW_WIDTH), RW_A_LORA ** -0.5),
        'rw_g2': nrm((L, RW_GATE_LORA, RW_WIDTH), RW_GATE_LORA ** -0.5),
        'rw_k_k': 0.85 + nrm((L, RW_WIDTH), 0.02),
        'rw_k_a': 1.0 + nrm((L, RW_WIDTH), 0.02),
        'rw_r_k': nrm((L, RW_HEADS, RW_HEAD), 0.1),
        'rw_ln_w': 1.0 + nrm((L, RW_WIDTH), 0.02),
        'rw_ln_b': nrm((L, RW_WIDTH), 0.02),
        'norm2_w': 1.0 + nrm((L, D_MODEL), 0.02),
        'router_group_w': nrm((L, D_MODEL, N_GROUPS), D_MODEL ** -0.5),
        'router_group_b': nrm((L, N_GROUPS), 0.01),
        'router_expert_w': nrm((L, D_MODEL, N_EXPERTS), D_MODEL ** -0.5),
        'router_expert_b': nrm((L, N_EXPERTS), 0.01),
        'moe_w1': nrm((L, N_EXPERTS, D_MODEL, EXPERT_FF), D_MODEL ** -0.5),
        'moe_w3': nrm((L, N_EXPERTS, D_MODEL, EXPERT_FF), D_MODEL ** -0.5),
        'moe_w2': nrm((L, N_EXPERTS, EXPERT_FF, D_MODEL), EXPERT_FF ** -0.5),
        'final_norm_w': 1.0 + nrm((D_MODEL,), 0.02),
    }


def reference(x_prompt, x_sample, state_mlstm_c, state_mlstm_n, state_mlstm_m, state_rwkv, state_rwkv_shift,
              norm1_w, w_in, w_out, ml_b_i, ml_b_f, ml_norm_w, rw_mu, rw_w0, rw_w2, rw_a0, rw_a2, rw_g2,
              rw_k_k, rw_k_a, rw_r_k, rw_ln_w, rw_ln_b, norm2_w, router_group_w, router_group_b,
              router_expert_w, router_expert_b, moe_w1, moe_w3, moe_w2, final_norm_w):

    def layer(l, x, c0, n0, m0, s0, sh0):
        bsz, seq, d = x.shape
        h = rmsnorm(x, norm1_w[l])
        proj = h @ w_in[l]
        y_ml, c, n, m = mlstm_mix(proj[..., :ML_PROJ], c0, n0, m0, ml_b_i[l], ml_b_f[l], ml_norm_w[l])
        y_rw, s, sh = rwkv7_mix(proj[..., ML_PROJ:], sh0, s0, rw_mu[l], rw_w0[l], rw_w2[l], rw_a0[l], rw_a2[l],
                                rw_g2[l], rw_k_k[l], rw_k_a[l], rw_r_k[l], rw_ln_w[l], rw_ln_b[l])
        x = x + jnp.concatenate([y_ml, y_rw], axis=-1).astype(x.dtype) @ w_out[l]
        h2 = rmsnorm(x, norm2_w[l]).reshape(bsz * seq, d)
        x = x + hier_moe(h2, router_group_w[l], router_group_b[l], router_expert_w[l], router_expert_b[l],
                         moe_w1[l], moe_w3[l], moe_w2[l]).reshape(bsz, seq, d)
        return x, (c, n, m, s, sh)

    def trunk(x, init_states):
        new = []
        for l in range(DEPTH):
            x, st = layer(l, x, *init_states[l])
            new.append(st)
        y = rmsnorm(x, final_norm_w)
        stacked = [jnp.stack([st[j] for st in new]).astype(x.dtype) for j in range(5)]
        return y, stacked

    bp = x_prompt.shape[0]
    f32 = jnp.float32
    zero_states = [(jnp.zeros((bp, ML_HEADS, ML_DK, ML_DV), f32), jnp.zeros((bp, ML_HEADS, ML_DK), f32),
                    jnp.zeros((bp, ML_HEADS), f32), jnp.zeros((bp, RW_HEADS, RW_HEAD, RW_HEAD), f32),
                    jnp.zeros((bp, RW_PROJ), f32)) for _ in range(DEPTH)]
    y_prompt, (p_c, p_n, p_m, p_s, p_sh) = trunk(x_prompt, zero_states)
    past_states = [(state_mlstm_c[l], state_mlstm_n[l], state_mlstm_m[l], state_rwkv[l], state_rwkv_shift[l])
                   for l in range(DEPTH)]
    y_sample, (s_c, s_n, s_m, s_s, s_sh) = trunk(x_sample, past_states)
    return (y_prompt, y_sample, p_c, p_n, p_m, p_s, p_sh, s_c, s_n, s_m, s_s, s_sh)
```

```python
import functools

import jax
import jax.numpy as jnp
from jax import lax
from jax.experimental import pallas as pl
from jax.experimental.pallas import tpu as pltpu

F32 = jnp.float32
BF16 = jnp.bfloat16

D_MODEL = 4096
ML_HEADS = 4
ML_DK = 256
ML_DV = 512
ML_WIDTH = ML_HEADS * ML_DV
GATE_SOFTCAP = 15.0
RW_HEAD = 64
RW_WIDTH = D_MODEL - ML_WIDTH
RW_HEADS = RW_WIDTH // RW_HEAD
RW_DECAY_LORA = 96
RW_A_LORA = 96
RW_GATE_LORA = 64
RW_GN_EPS = 64e-5
N_GROUPS = 4
EXPERTS_PER_GROUP = 8
N_EXPERTS = N_GROUPS * EXPERTS_PER_GROUP
TOP_K = 2
EPS = 1e-6

LANES = 128
RW_GROUP = 256
RW_GHEADS = RW_GROUP // RW_HEAD
RW_LORA = RW_DECAY_LORA + RW_A_LORA + RW_GATE_LORA
RW_CHUNK = 64
ML_CHUNK = 64
VMEM_LIMIT = 56 * 1024 * 1024


def _cparams(n_axes, vmem=None):
    return pltpu.CompilerParams(dimension_semantics=("arbitrary",) * n_axes, vmem_limit_bytes=vmem)


def _sigmoid(x):
    return 1.0 / (1.0 + jnp.exp(-x))


def _softplus(x):
    return jnp.maximum(x, 0.0) + jnp.log(1.0 + jnp.exp(-jnp.abs(x)))


def _soft_cap(x):
    return GATE_SOFTCAP * jnp.tanh(x / GATE_SOFTCAP)


def _dot(a, b):
    return jnp.dot(a.astype(BF16), b.astype(BF16), preferred_element_type=F32)


def _dot_nt(a, b):
    return lax.dot_general(a.astype(BF16), b.astype(BF16), (((1,), (1,)), ((), ())),
                           preferred_element_type=F32)


def _split2(x):
    hi = x.astype(BF16)
    lo = (x - hi.astype(F32)).astype(BF16)
    return hi, lo


def _split3(x):
    hi = x.astype(BF16)
    r = x - hi.astype(F32)
    mid = r.astype(BF16)
    lo = (r - mid.astype(F32)).astype(BF16)
    return hi, mid, lo


def _iota(shape, axis):
    return lax.broadcasted_iota(jnp.int32, shape, axis)


def _head_block_ones(n):
    return jnp.where((_iota((n, n), 0) >> 6) == (_iota((n, n), 1) >> 6), 1.0, 0.0).astype(BF16)


def _seg_sum(x, bd):
    hi, lo = _split2(x)
    return (jnp.dot(hi, bd, preferred_element_type=F32) + jnp.dot(lo, bd, preferred_element_type=F32))


_P_W0, _P_A0, _P_KK, _P_KA, _P_RK, _P_LNW, _P_LNB, _P_MUR, _P_MUK, _P_MUV = range(10)


def _rwkv_prep(p_r, p_k, p_v, p_l, prev_r, prev_k, prev_v, prev_l, prm, mu_l, wl, bd):
    row = lambda i: prm[i:i + 1, :]
    xr = p_r + (prev_r - p_r) * row(_P_MUR)
    xk = p_k + (prev_k - p_k) * row(_P_MUK)
    xv = p_v + (prev_v - p_v) * row(_P_MUV)
    xl = p_l + (prev_l - p_l) * mu_l
    lane = _iota(xl.shape, 1)
    z = jnp.where(lane < RW_DECAY_LORA, jnp.tanh(xl),
                  jnp.where(lane < RW_DECAY_LORA + RW_A_LORA, xl, _sigmoid(xl)))
    lo = _dot(z, wl)
    lw, la, lg = lo[:, :RW_GROUP], lo[:, RW_GROUP:2 * RW_GROUP], lo[:, 2 * RW_GROUP:]
    w = -_softplus(-(row(_P_W0) + lw)) - 0.5
    logd = -jnp.exp(w)
    a = _sigmoid(row(_P_A0) + la)
    kk = xk * row(_P_KK)
    kk = kk / jnp.maximum(jnp.sqrt(_seg_sum(kk * kk, bd)), 1e-12)
    k = xk * (1.0 + (a - 1.0) * row(_P_KA))
    return xr, k, xv, kk, a, logd, lg


def _rwkv_post(y, r, k, v, g, prm, bd):
    row = lambda i: prm[i:i + 1, :]
    mean = _seg_sum(y, bd) * (1.0 / RW_HEAD)
    yc = y - mean
    var = _seg_sum(yc * yc, bd) * (1.0 / RW_HEAD)
    yn = yc * lax.rsqrt(var + RW_GN_EPS) * row(_P_LNW) + row(_P_LNB)
    bonus = _seg_sum(r * k * row(_P_RK), bd) * v
    return (yn + bonus) * g


def _rwkv_chunk(r, k, v, kk, a, logd, G, bd):
    L = r.shape[0]
    t_i = _iota((L, L), 0)
    s_i = _iota((L, L), 1)
    tril = jnp.where(s_i <= t_i, 1.0, 0.0).astype(BF16)
    d_hi, d_mid, d_lo = _split3(logd)
    clog = (jnp.dot(tril, d_hi, preferred_element_type=F32) + jnp.dot(tril, d_mid, preferred_element_type=F32)
            + jnp.dot(tril, d_lo, preferred_element_type=F32))
    clog_l = clog[L - 1:L, :]
    p_in = jnp.exp(clog)
    n_in = jnp.exp(-clog)
    to_end = jnp.exp(clog_l - clog)
    ka = kk * a
    at = -kk * jnp.exp(clog - logd)
    bt = ka * n_in
    kt = k * n_in
    rt = r * p_in
    lane_head = _iota((1, RW_GROUP), 1) >> 6
    masks = [lane_head == h for h in range(RW_GHEADS)]
    zero = jnp.zeros_like(at)
    lhs = jnp.concatenate([jnp.where(m, x, zero) for m in masks for x in (at, rt)], axis=0)
    xb = _dot_nt(lhs, bt)
    xk = _dot_nt(lhs, kt)
    strict = s_i < t_i
    incl = s_i <= t_i
    eye_pad = jnp.where(_iota((L, 2 * L), 1) == _iota((L, 2 * L), 0) + L, 1.0, 0.0)
    right = _iota((L, 2 * L), 1) >= L
    ta = zero
    u = zero
    yv = zero
    arb = []
    for h, m in enumerate(masks):
        o = 2 * L * h
        n_h = jnp.where(strict, xb[o:o + L], 0.0)
        mak = jnp.where(strict, xk[o:o + L], 0.0)
        arb.append(jnp.where(incl, xb[o + L:o + 2 * L], 0.0))
        ark = jnp.where(incl, xk[o + L:o + 2 * L], 0.0)
        z = jnp.concatenate([n_h, jnp.zeros_like(n_h)], axis=1) + eye_pad
        nsteps = max(1, (L - 1).bit_length())
        for _ in range(nsteps):
            z = _dot(z[:, :L], z) + jnp.where(right, z, 0.0)
        t_h = z[:, L:]
        ta = ta + jnp.where(m, _dot(t_h, at), 0.0)
        u = u + jnp.where(m, _dot(t_h, _dot(mak, v)), 0.0)
        yv = yv + jnp.where(m, _dot(ark, v), 0.0)
    ws = _dot_nt(jnp.concatenate([ta, rt], axis=0), G)
    w_all = ws[:L] + u
    y = ws[L:] + yv
    for h, m in enumerate(masks):
        y = y + jnp.where(m, _dot(arb[h], w_all), 0.0)
    upd = _dot(jnp.concatenate([w_all, v], axis=0).T, jnp.concatenate([ka * to_end, k * to_end], axis=0))
    g_new = G * jnp.exp(clog_l) + jnp.where(bd > 0, upd, 0.0)
    return y, g_new


def _shift_rows(x, first_row):
    rolled = pltpu.roll(x, 1, axis=0)
    return jnp.where(_iota(x.shape, 0) == 0, first_row, rolled)


def _rwkv_prompt_kernel(pr_ref, pk_ref, pv_ref, pl_ref, prm_ref, mul_ref, wl_ref, y_ref, s_ref,
                        g_scr, carry_scr, *, n_chunks):
    c = pl.program_id(2)

    @pl.when(c == 0)
    def _():
        g_scr[...] = jnp.zeros_like(g_scr)
        carry_scr[...] = jnp.zeros_like(carry_scr)

    bd = _head_block_ones(RW_GROUP)
    p_r, p_k, p_v, p_l = pr_ref[...], pk_ref[...], pv_ref[...], pl_ref[...]
    L = p_r.shape[0]
    prevs = [_shift_rows(x, carry_scr[i:i + 1, :]) for i, x in enumerate((p_r, p_k, p_v, p_l))]
    for i, x in enumerate((p_r, p_k, p_v, p_l)):
        carry_scr[i:i + 1, :] = x[L - 1:L, :]
    prm = prm_ref[...]
    r, k, v, kk, a, logd, g = _rwkv_prep(p_r, p_k, p_v, p_l, *prevs, prm, mul_ref[...], wl_ref[0], bd)
    y, g_new = _rwkv_chunk(r, k, v, kk, a, logd, g_scr[...], bd)
    g_scr[...] = g_new
    y_ref[...] = _rwkv_post(y, r, k, v, g, prm, bd).astype(y_ref.dtype)

    @pl.when(c == n_chunks - 1)
    def _():
        for h in range(RW_GHEADS):
            s_ref[0, h] = g_new[RW_HEAD * h:RW_HEAD * (h + 1), RW_HEAD * h:RW_HEAD * (h + 1)]


def _rwkv_prompt(proj, prm, mu_l, wl, *, batch, seq, rw_col0, n_groups):
    L = RW_CHUNK
    nc = seq // L
    cb = rw_col0 // RW_GROUP
    width = n_groups * RW_GROUP
    pspec = lambda off: pl.BlockSpec((L, RW_GROUP), lambda b, g, c, off=off: (b * nc + c, cb + off + g))
    lspec = pl.BlockSpec((L, RW_GROUP), lambda b, g, c: (b * nc + c, cb + 3 * n_groups))
    return pl.pallas_call(
        functools.partial(_rwkv_prompt_kernel, n_chunks=nc),
        grid=(batch, n_groups, nc),
        in_specs=[pspec(0), pspec(n_groups), pspec(2 * n_groups), lspec,
                  pl.BlockSpec((16, RW_GROUP), lambda b, g, c: (0, g)),
                  pl.BlockSpec((1, RW_LORA), lambda b, g, c: (0, 0)),
                  pl.BlockSpec((1, RW_LORA, 3 * RW_GROUP), lambda b, g, c: (g, 0, 0))],
        out_specs=[pl.BlockSpec((L, RW_GROUP), lambda b, g, c: (b * nc + c, g)),
                   pl.BlockSpec((1, RW_GHEADS, RW_HEAD, RW_HEAD), lambda b, g, c: (b, g, 0, 0))],
        out_shape=[jax.ShapeDtypeStruct((batch * seq, width), BF16),
                   jax.ShapeDtypeStruct((batch, n_groups * RW_GHEADS, RW_HEAD, RW_HEAD), F32)],
        scratch_shapes=[pltpu.VMEM((RW_GROUP, RW_GROUP), F32), pltpu.VMEM((8, RW_GROUP), F32)],
        compiler_params=_cparams(3),
        name="rwkv_prompt",
    )(proj, proj, proj, proj, prm, mu_l, wl)


def _rwkv_sample_prep_kernel(pr_ref, pk_ref, pv_ref, pl_ref, sr_ref, sk_ref, sv_ref, sl_ref, prm_ref, mul_ref,
                             wl_ref, r_ref, k_ref, v_ref, kk_ref, ka_ref, d_ref, g_ref):
    bd = _head_block_ones(RW_GROUP)
    r, k, v, kk, a, logd, g = _rwkv_prep(pr_ref[...], pk_ref[...], pv_ref[...], pl_ref[...], sr_ref[...],
                                         sk_ref[...], sv_ref[...], sl_ref[...], prm_ref[...], mul_ref[...],
                                         wl_ref[0], bd)
    r_ref[...] = r
    k_ref[...] = k
    v_ref[...] = v
    kk_ref[...] = kk
    ka_ref[...] = kk * a
    d_ref[...] = jnp.exp(logd)
    g_ref[...] = g


def _rwkv_sample_prep(proj, shift0, prm, mu_l, wl, *, row0, rows, rw_col0, n_groups):
    rb = row0 // rows
    cb = rw_col0 // RW_GROUP
    pspec = lambda off: pl.BlockSpec((rows, RW_GROUP), lambda g, off=off: (rb, cb + off + g))
    sspec = lambda off: pl.BlockSpec((rows, RW_GROUP), lambda g, off=off: (0, off + g))
    ospec = pl.BlockSpec((rows, RW_GROUP), lambda g: (0, g))
    width = n_groups * RW_GROUP
    return pl.pallas_call(
        _rwkv_sample_prep_kernel,
        grid=(n_groups,),
        in_specs=[pspec(0), pspec(n_groups), pspec(2 * n_groups),
                  pl.BlockSpec((rows, RW_GROUP), lambda g: (rb, cb + 3 * n_groups)),
                  sspec(0), sspec(n_groups), sspec(2 * n_groups),
                  pl.BlockSpec((rows, RW_GROUP), lambda g: (0, 3 * n_groups)),
                  pl.BlockSpec((16, RW_GROUP), lambda g: (0, g)),
                  pl.BlockSpec((1, RW_LORA), lambda g: (0, 0)),
                  pl.BlockSpec((1, RW_LORA, 3 * RW_GROUP), lambda g: (g, 0, 0))],
        out_specs=[ospec] * 7,
        out_shape=[jax.ShapeDtypeStruct((rows, width), F32)] * 7,
        compiler_params=_cparams(1),
        name="rwkv_sample_prep",
    )(proj, proj, proj, proj, shift0, shift0, shift0, shift0, prm, mu_l, wl)


def _rwkv_sample_state_kernel(s_ref, r_ref, k_ref, kk_ref, ka_ref, d_ref, vt_ref, so_ref, yt_ref, *, n_heads):
    for h in range(n_heads):
        row = lambda ref: ref[0, h:h + 1, :]
        s = s_ref[0, h]
        sa = jnp.sum(s * (-row(kk_ref)), axis=1, keepdims=True)
        v_col = vt_ref[0, :, h:h + 1]
        s_new = s * row(d_ref) + sa * row(ka_ref) + v_col * row(k_ref)
        so_ref[0, h] = s_new
        yt_ref[0, :, h:h + 1] = jnp.sum(s_new * row(r_ref), axis=1, keepdims=True)


def _rwkv_sample_state(state, r, k, kk, ka, d, v):
    b, nh = state.shape[0], state.shape[1]
    heads = lambda x: x.reshape(b, nh, RW_HEAD)
    vt = jnp.swapaxes(heads(v), 1, 2)
    hspec = pl.BlockSpec((1, nh, RW_HEAD), lambda i: (i, 0, 0))
    tspec = pl.BlockSpec((1, RW_HEAD, nh), lambda i: (i, 0, 0))
    sspec = pl.BlockSpec((1, nh, RW_HEAD, RW_HEAD), lambda i: (i, 0, 0, 0))
    s_new, yt = pl.pallas_call(
        functools.partial(_rwkv_sample_state_kernel, n_heads=nh),
        grid=(b,),
        in_specs=[sspec, hspec, hspec, hspec, hspec, hspec, tspec],
        out_specs=[sspec, tspec],
        out_shape=[jax.ShapeDtypeStruct(state.shape, F32), jax.ShapeDtypeStruct((b, RW_HEAD, nh), F32)],
        compiler_params=_cparams(1),
        name="rwkv_sample_state",
    )(state, heads(r), heads(k), heads(kk), heads(ka), heads(d), vt)
    return s_new, jnp.swapaxes(yt, 1, 2).reshape(b, nh * RW_HEAD)


def _rwkv_sample_post_kernel(y_ref, r_ref, k_ref, v_ref, g_ref, prm_ref, o_ref):
    bd = _head_block_ones(RW_GROUP)
    o_ref[...] = _rwkv_post(y_ref[...], r_ref[...], k_ref[...], v_ref[...], g_ref[...], prm_ref[...],
                            bd).astype(o_ref.dtype)


def _rwkv_sample_post(y, r, k, v, g, prm):
    rows, width = y.shape
    spec = pl.BlockSpec((rows, RW_GROUP), lambda i: (0, i))
    return pl.pallas_call(
        _rwkv_sample_post_kernel,
        grid=(width // RW_GROUP,),
        in_specs=[spec] * 5 + [pl.BlockSpec((16, RW_GROUP), lambda i: (0, i))],
        out_specs=spec,
        out_shape=jax.ShapeDtypeStruct((rows, width), BF16),
        compiler_params=_cparams(1),
        name="rwkv_sample_post",
    )(y, r, k, v, g, prm)


def _mlstm_gates(gates, bias, h):
    capped = _soft_cap(gates + bias)
    lane = _iota(gates.shape, 1)
    i_col = jnp.sum(jnp.where(lane == h, capped, 0.0), axis=1, keepdims=True)
    f_col = jnp.sum(jnp.where(lane == h + ML_HEADS, -_softplus(-capped), 0.0), axis=1, keepdims=True)
    return i_col, f_col


def _mlstm_out(hh, o, norm_w):
    hn = hh * lax.rsqrt(jnp.mean(hh * hh, axis=1, keepdims=True) + EPS) * norm_w
    return hn * _sigmoid(o)


def _mlstm_prompt_kernel(q_ref, k_ref, v_ref, o_ref, gt_ref, gb_ref, nw_ref, y_ref, c_ref, nm_ref,
                         c_scr, n_scr, m_scr, *, n_chunks):
    h = pl.program_id(1)
    c = pl.program_id(2)

    @pl.when(c == 0)
    def _():
        c_scr[...] = jnp.zeros_like(c_scr)
        n_scr[...] = jnp.zeros_like(n_scr)
        m_scr[...] = jnp.zeros_like(m_scr)

    q = q_ref[...] * (ML_DK ** -0.5)
    k = k_ref[...]
    v = v_ref[...]
    L = q.shape[0]
    i_col, f_col = _mlstm_gates(gt_ref[...], gb_ref[...], h)
    t_i = _iota((L, L), 0)
    s_i = _iota((L, L), 1)
    causal = s_i <= t_i
    to_row = lambda col: jnp.sum(jnp.where(t_i == s_i, col, 0.0), axis=0, keepdims=True)
    i_row = to_row(i_col)
    f_row = to_row(f_col)
    b_col = jnp.sum(jnp.where(causal, f_row, 0.0), axis=1, keepdims=True)
    b_row = jnp.sum(jnp.where(t_i <= s_i, f_col, 0.0), axis=0, keepdims=True)
    m0 = m_scr[...]
    inter = b_col + m0
    dmat = jnp.where(causal, b_col - b_row + i_row, -1e30)
    m_t = jnp.maximum(inter, jnp.max(dmat, axis=1, keepdims=True))
    w_int = jnp.exp(inter - m_t)
    s = _dot_nt(q, k) * jnp.exp(dmat - m_t)
    c0 = c_scr[...]
    n0 = n_scr[...]
    num = w_int * _dot(q, c0) + _dot(s, v)
    den = w_int * jnp.sum(q * n0, axis=1, keepdims=True) + jnp.sum(s, axis=1, keepdims=True)
    hh = num / jnp.maximum(jnp.abs(den), jnp.exp(-m_t))
    m_new = m_t[L - 1:L, :]
    b_l = b_col[L - 1:L, :]
    a_end = jnp.exp(b_l - b_col + i_col - m_new)
    dec = jnp.exp(b_l + m0 - m_new)
    ka = k * a_end
    c_new = dec * c0 + _dot(ka.T, v)
    n_new = dec * n0 + jnp.sum(ka, axis=0, keepdims=True)
    c_scr[...] = c_new
    n_scr[...] = n_new
    m_scr[...] = m_new
    y_ref[...] = _mlstm_out(hh, o_ref[...], nw_ref[...]).astype(y_ref.dtype)

    @pl.when(c == n_chunks - 1)
    def _():
        c_ref[0, 0] = c_new
        nm_ref[0, 0] = jnp.concatenate([n_new, jnp.broadcast_to(m_new, (7, ML_DK))], axis=0)


def _mlstm_prompt(proj, gate_bias, norm_w, *, batch, seq):
    L = ML_CHUNK
    nc = seq // L
    nh = ML_HEADS
    rows = lambda b, c: b * nc + c
    kq = nh * ML_DK
    return pl.pallas_call(
        functools.partial(_mlstm_prompt_kernel, n_chunks=nc),
        grid=(batch, nh, nc),
        in_specs=[pl.BlockSpec((L, ML_DK), lambda b, h, c: (rows(b, c), h)),
                  pl.BlockSpec((L, ML_DK), lambda b, h, c: (rows(b, c), nh + h)),
                  pl.BlockSpec((L, ML_DV), lambda b, h, c: (rows(b, c), 2 * kq // ML_DV + h)),
                  pl.BlockSpec((L, ML_DV), lambda b, h, c: (rows(b, c), 2 * kq // ML_DV + nh + h)),
                  pl.BlockSpec((L, LANES), lambda b, h, c: (rows(b, c), (2 * kq + 2 * ML_WIDTH) // LANES)),
                  pl.BlockSpec((1, LANES), lambda b, h, c: (0, 0)),
                  pl.BlockSpec((1, ML_DV), lambda b, h, c: (0, h))],
        out_specs=[pl.BlockSpec((L, ML_DV), lambda b, h, c: (rows(b, c), h)),
                   pl.BlockSpec((1, 1, ML_DK, ML_DV), lambda b, h, c: (b, h, 0, 0)),
                   pl.BlockSpec((1, 1, 8, ML_DK), lambda b, h, c: (b, h, 0, 0))],
        out_shape=[jax.ShapeDtypeStruct((batch * seq, ML_WIDTH), BF16),
                   jax.ShapeDtypeStruct((batch, nh, ML_DK, ML_DV), F32),
                   jax.ShapeDtypeStruct((batch, nh, 8, ML_DK), F32)],
        scratch_shapes=[pltpu.VMEM((ML_DK, ML_DV), F32), pltpu.VMEM((1, ML_DK), F32), pltpu.VMEM((1, 1), F32)],
        compiler_params=_cparams(3),
        name="mlstm_prompt",
    )(proj, proj, proj, proj, proj, gate_bias, norm_w)


def _mlstm_sample_kernel(q_ref, k_ref, v_ref, o_ref, gt_ref, gb_ref, nw_ref, c_ref, n_ref, m_ref,
                         y_ref, co_ref, no_ref, mo_ref):
    gates = gt_ref[0]
    eye = jnp.where(_iota((ML_DK, ML_DK), 0) == _iota((ML_DK, ML_DK), 1), 1.0, 0.0).astype(BF16)
    for h in range(ML_HEADS):
        q = q_ref[0][:, h * ML_DK:(h + 1) * ML_DK] * (ML_DK ** -0.5)
        k = k_ref[0][:, h * ML_DK:(h + 1) * ML_DK]
        v = v_ref[0][:, h * ML_DV:(h + 1) * ML_DV]
        i_pre, logf = _mlstm_gates(gates, gb_ref[...], h)
        c0 = c_ref[0, h]
        n0 = n_ref[0, h:h + 1, :]
        m0 = m_ref[0][:, h:h + 1]
        inter = logf + m0
        m_t = jnp.maximum(inter, i_pre)
        w_int = jnp.exp(inter - m_t)
        a_new = jnp.exp(i_pre - m_t)
        s = jnp.sum(q * k, axis=1, keepdims=True) * a_new
        qc = _dot(jnp.broadcast_to(q, (8, ML_DK)), c0)[0:1, :]
        num = w_int * qc + s * v
        den = w_int * jnp.sum(q * n0, axis=1, keepdims=True) + s
        hh = num / jnp.maximum(jnp.abs(den), jnp.exp(-m_t))
        k_hi, k_lo = _split2(jnp.broadcast_to(k, (8, ML_DK)))
        nt = (((1,), (1,)), ((), ()))
        k_col = (lax.dot_general(eye, k_hi, nt, preferred_element_type=F32)
                 + lax.dot_general(eye, k_lo, nt, preferred_element_type=F32))[:, 0:1]
        co_ref[0, h] = w_int * c0 + k_col * (a_new * v)
        no_ref[0, h:h + 1, :] = w_int * n0 + a_new * k
        mo_ref[0, :, h:h + 1] = m_t
        y_ref[0, :, h * ML_DV:(h + 1) * ML_DV] = _mlstm_out(
            hh, o_ref[0][:, h * ML_DV:(h + 1) * ML_DV], nw_ref[:, h * ML_DV:(h + 1) * ML_DV]).astype(y_ref.dtype)


def _mlstm_sample(proj3, gate_bias, norm_w, c0, n0, m0):
    b = proj3.shape[0]
    nh = ML_HEADS
    kq = nh * ML_DK
    p3 = lambda w, blk: pl.BlockSpec((1, 1, w), lambda i, blk=blk: (i, 0, blk))
    cspec = pl.BlockSpec((1, nh, ML_DK, ML_DV), lambda i: (i, 0, 0, 0))
    nspec = pl.BlockSpec((1, nh, ML_DK), lambda i: (i, 0, 0))
    mspec = pl.BlockSpec((1, 1, nh), lambda i: (i, 0, 0))
    return pl.pallas_call(
        _mlstm_sample_kernel,
        grid=(b,),
        in_specs=[p3(kq, 0), p3(kq, 1), p3(ML_WIDTH, 2 * kq // ML_WIDTH), p3(ML_WIDTH, 2 * kq // ML_WIDTH + 1),
                  p3(LANES, (2 * kq + 2 * ML_WIDTH) // LANES),
                  pl.BlockSpec((1, LANES), lambda i: (0, 0)),
                  pl.BlockSpec((1, ML_WIDTH), lambda i: (0, 0)),
                  cspec, nspec, mspec],
        out_specs=[pl.BlockSpec((1, 1, ML_WIDTH), lambda i: (i, 0, 0)), cspec, nspec, mspec],
        out_shape=[jax.ShapeDtypeStruct((b, 1, ML_WIDTH), BF16),
                   jax.ShapeDtypeStruct(c0.shape, F32), jax.ShapeDtypeStruct(n0.shape, F32),
                   jax.ShapeDtypeStruct(m0.shape, F32)],
        compiler_params=_cparams(1),
        name="mlstm_sample",
    )(proj3, proj3, proj3, proj3, proj3, gate_bias, norm_w, c0, n0, m0)


def _rms(x, w):
    return x * lax.rsqrt(jnp.mean(x * x, axis=1, keepdims=True) + EPS) * w


def _in_proj_kernel(x_ref, nw_ref, w_ref, o_ref, h_scr):
    @pl.when(pl.program_id(1) == 0)
    def _():
        h_scr[...] = _rms(x_ref[...], nw_ref[...]).astype(BF16)

    o_ref[...] = jnp.dot(h_scr[...], w_ref[...], preferred_element_type=F32)


def _in_proj(x, norm_w, w, *, tm, tn):
    t, d = x.shape
    n = w.shape[1]
    return pl.pallas_call(
        _in_proj_kernel,
        grid=(t // tm, n // tn),
        in_specs=[pl.BlockSpec((tm, d), lambda i, j: (i, 0)),
                  pl.BlockSpec((1, d), lambda i, j: (0, 0)),
                  pl.BlockSpec((d, tn), lambda i, j: (0, j))],
        out_specs=pl.BlockSpec((tm, tn), lambda i, j: (i, j)),
        out_shape=jax.ShapeDtypeStruct((t, n), F32),
        scratch_shapes=[pltpu.VMEM((tm, d), BF16)],
        compiler_params=_cparams(2, VMEM_LIMIT),
        name="in_proj",
    )(x, norm_w, w)


def _out_proj_kernel(ya_ref, yb_ref, wa_ref, wb_ref, x_ref, o_ref):
    o_ref[...] = (x_ref[...] + jnp.dot(ya_ref[...], wa_ref[...], preferred_element_type=F32)
                  + jnp.dot(yb_ref[...], wb_ref[...], preferred_element_type=F32))


def _out_proj(ya, yb, w, x, *, tm, tn):
    t, kh = ya.shape
    n = w.shape[1]
    return pl.pallas_call(
        _out_proj_kernel,
        grid=(t // tm, n // tn),
        in_specs=[pl.BlockSpec((tm, kh), lambda i, j: (i, 0)),
                  pl.BlockSpec((tm, kh), lambda i, j: (i, 0)),
                  pl.BlockSpec((kh, tn), lambda i, j: (0, j)),
                  pl.BlockSpec((kh, tn), lambda i, j: (1, j)),
                  pl.BlockSpec((tm, tn), lambda i, j: (i, j))],
        out_specs=pl.BlockSpec((tm, tn), lambda i, j: (i, j)),
        out_shape=jax.ShapeDtypeStruct((t, n), F32),
        compiler_params=_cparams(2, VMEM_LIMIT),
        name="out_proj",
    )(ya, yb, w, w, x)


def _router_kernel(x_ref, nw_ref, wr_ref, h_ref, lg_ref):
    h = _rms(x_ref[...], nw_ref[...])
    h_ref[...] = h
    lg_ref[...] = jnp.dot(h, wr_ref[...], preferred_element_type=F32, precision=lax.Precision.HIGHEST)


def _router(x, norm_w, wr, *, tm):
    t, d = x.shape
    return pl.pallas_call(
        _router_kernel,
        grid=(t // tm,),
        in_specs=[pl.BlockSpec((tm, d), lambda i: (i, 0)),
                  pl.BlockSpec((1, d), lambda i: (0, 0)),
                  pl.BlockSpec((d, LANES), lambda i: (0, 0))],
        out_specs=[pl.BlockSpec((tm, d), lambda i: (i, 0)), pl.BlockSpec((tm, LANES), lambda i: (i, 0))],
        out_shape=[jax.ShapeDtypeStruct((t, d), F32), jax.ShapeDtypeStruct((t, LANES), F32)],
        compiler_params=_cparams(1, VMEM_LIMIT),
        name="router",
    )(x, norm_w, wr)


def _gather_rows(src_hbm, dst_vmem, idx_ref, base, n_rows, sem):
    def copy(r):
        return pltpu.make_async_copy(src_hbm.at[pl.ds(idx_ref[base + r], 1), :], dst_vmem.at[pl.ds(r, 1), :], sem)

    def start(r, carry):
        copy(r).start()
        return carry

    def wait(r, carry):
        copy(r).wait()
        return carry

    lax.fori_loop(0, n_rows, start, 0)
    lax.fori_loop(0, n_rows, wait, 0)


def _moe_gather_kernel(tok_ref, nused_ref, h_hbm, o_ref, buf, sem, *, tm):
    i = pl.program_id(0)

    @pl.when(i < nused_ref[0])
    def _():
        _gather_rows(h_hbm, buf, tok_ref, i * tm, tm, sem)
        o_ref[...] = buf[...].astype(o_ref.dtype)

    @pl.when(i >= nused_ref[0])
    def _():
        o_ref[...] = jnp.zeros_like(o_ref)


def _moe_gather(slot_tok, nused, h, *, tm):
    p = slot_tok.shape[0]
    d = h.shape[1]
    return pl.pallas_call(
        functools.partial(_moe_gather_kernel, tm=tm),
        grid_spec=pltpu.PrefetchScalarGridSpec(
            num_scalar_prefetch=2,
            grid=(p // tm,),
            in_specs=[pl.BlockSpec(memory_space=pl.ANY)],
            out_specs=pl.BlockSpec((tm, d), lambda i, tok, nu: (i, 0)),
            scratch_shapes=[pltpu.VMEM((tm, d), F32), pltpu.SemaphoreType.DMA(())]),
        out_shape=jax.ShapeDtypeStruct((p, d), BF16),
        compiler_params=_cparams(1, VMEM_LIMIT),
        name="moe_gather",
    )(slot_tok, nused, h)


def _moe_up_kernel(be_ref, nused_ref, x_ref, w1_ref, w3_ref, o_ref):
    @pl.when(pl.program_id(1) < nused_ref[0])
    def _():
        x = x_ref[...]
        a = jnp.dot(x, w1_ref[0].astype(BF16), preferred_element_type=F32)
        b = jnp.dot(x, w3_ref[0].astype(BF16), preferred_element_type=F32)
        o_ref[...] = (a * _sigmoid(a) * b).astype(o_ref.dtype)

    @pl.when(pl.program_id(1) >= nused_ref[0])
    def _():
        o_ref[...] = jnp.zeros_like(o_ref)


def _moe_up(block_expert, nused, xs, w1, w3, *, tm, tf):
    p, d = xs.shape
    ff = w1.shape[2]
    wspec = pl.BlockSpec((1, d, tf), lambda f, i, be, nu: (be[i], 0, f))
    return pl.pallas_call(
        _moe_up_kernel,
        grid_spec=pltpu.PrefetchScalarGridSpec(
            num_scalar_prefetch=2,
            grid=(ff // tf, p // tm),
            in_specs=[pl.BlockSpec((tm, d), lambda f, i, be, nu: (i, 0)), wspec, wspec],
            out_specs=pl.BlockSpec((tm, tf), lambda f, i, be, nu: (i, f))),
        out_shape=jax.ShapeDtypeStruct((p, ff), BF16),
        compiler_params=_cparams(2, VMEM_LIMIT),
        name="moe_up",
    )(block_expert, nused, xs, w1, w3)


def _moe_down_kernel(be_ref, nused_ref, h_ref, w2_ref, o_ref):
    @pl.when(pl.program_id(1) < nused_ref[0])
    def _():
        o_ref[...] = jnp.dot(h_ref[...], w2_ref[0].astype(BF16), preferred_element_type=F32)

    @pl.when(pl.program_id(1) >= nused_ref[0])
    def _():
        o_ref[...] = jnp.zeros_like(o_ref)


def _moe_down(block_expert, nused, hs, w2, *, tm, tn):
    p, ff = hs.shape
    d = w2.shape[2]
    return pl.pallas_call(
        _moe_down_kernel,
        grid_spec=pltpu.PrefetchScalarGridSpec(
            num_scalar_prefetch=2,
            grid=(d // tn, p // tm),
            in_specs=[pl.BlockSpec((tm, ff), lambda n, i, be, nu: (i, 0)),
                      pl.BlockSpec((1, ff, tn), lambda n, i, be, nu: (be[i], 0, n))],
            out_specs=pl.BlockSpec((tm, tn), lambda n, i, be, nu: (i, n))),
        out_shape=jax.ShapeDtypeStruct((p, d), F32),
        compiler_params=_cparams(2, VMEM_LIMIT),
        name="moe_down",
    )(block_expert, nused, hs, w2)


def _moe_combine_kernel(p0_ref, p1_ref, x_ref, g0_ref, g1_ref, nw_ref, yb_hbm, o_ref, buf0, buf1, sem, *, tc):
    i = pl.program_id(0)
    _gather_rows(yb_hbm, buf0, p0_ref, i * tc, tc, sem)
    _gather_rows(yb_hbm, buf1, p1_ref, i * tc, tc, sem)
    x = x_ref[...] + g0_ref[:, 0:1] * buf0[...] + g1_ref[:, 0:1] * buf1[...]
    o_ref[...] = _rms(x, nw_ref[...])


def _moe_combine(pos0, pos1, x, g0, g1, norm_w, yb, *, tc):
    t, d = x.shape
    return pl.pallas_call(
        functools.partial(_moe_combine_kernel, tc=tc),
        grid_spec=pltpu.PrefetchScalarGridSpec(
            num_scalar_prefetch=2,
            grid=(t // tc,),
            in_specs=[pl.BlockSpec((tc, d), lambda i, a, b: (i, 0)),
                      pl.BlockSpec((tc, LANES), lambda i, a, b: (i, 0)),
                      pl.BlockSpec((tc, LANES), lambda i, a, b: (i, 0)),
                      pl.BlockSpec((1, d), lambda i, a, b: (0, 0)),
                      pl.BlockSpec(memory_space=pl.ANY)],
            out_specs=pl.BlockSpec((tc, d), lambda i, a, b: (i, 0)),
            scratch_shapes=[pltpu.VMEM((tc, d), F32), pltpu.VMEM((tc, d), F32), pltpu.SemaphoreType.DMA(())]),
        out_shape=jax.ShapeDtypeStruct((t, d), F32),
        compiler_params=_cparams(1, VMEM_LIMIT),
        name="moe_combine",
    )(pos0, pos1, x, g0, g1, norm_w, yb)


def _route(logits, bg, be, *, tm):
    t = logits.shape[0]
    pg = jax.nn.softmax(logits[:, :N_GROUPS] + bg, axis=-1)
    g_idx = jnp.argmax(pg, axis=-1).astype(jnp.int32)
    p_sel = jnp.take_along_axis(pg, g_idx[:, None], axis=-1)
    le = (logits[:, N_GROUPS:N_GROUPS + N_EXPERTS] + be).reshape(t, N_GROUPS, EXPERTS_PER_GROUP)
    le = jnp.take_along_axis(le, g_idx[:, None, None], axis=1)[:, 0]
    pe = jax.nn.softmax(le, axis=-1)
    top_v, top_i = lax.top_k(pe, TOP_K)
    gate = p_sel * top_v / jnp.sum(top_v, axis=-1, keepdims=True)
    e_flat = (g_idx[:, None] * EXPERTS_PER_GROUP + top_i.astype(jnp.int32)).reshape(-1)
    a = t * TOP_K
    onehot = (e_flat[:, None] == jnp.arange(N_EXPERTS, dtype=jnp.int32)[None, :]).astype(jnp.int32)
    rank = jnp.sum((jnp.cumsum(onehot, axis=0) - onehot) * onehot, axis=1)
    counts = jnp.sum(onehot, axis=0)
    padded = (counts + tm - 1) // tm * tm
    pad_end = jnp.cumsum(padded)
    dest = (pad_end - padded)[e_flat] + rank
    n_blocks = -(-(a + N_EXPERTS * (tm - 1)) // tm)
    tok = jnp.arange(a, dtype=jnp.int32) // TOP_K
    slot_tok = jnp.zeros((n_blocks * tm,), jnp.int32).at[dest].set(tok)
    block_start = jnp.arange(n_blocks, dtype=jnp.int32) * tm
    block_expert = jnp.minimum(jnp.searchsorted(pad_end, block_start, side='right'), N_EXPERTS - 1)
    nused = (pad_end[-1] // tm).astype(jnp.int32).reshape(1)
    dest = dest.reshape(t, TOP_K)
    return gate, dest[:, 0], dest[:, 1], slot_tok, block_expert.astype(jnp.int32), nused


def _moe(x, norm2_w, wr, bg, be, w1, w3, w2, final_w, *, tm_route, tm_blk, tf, tn, tc):
    h, logits = _router(x, norm2_w, wr, tm=tm_route)
    gate, pos0, pos1, slot_tok, block_expert, nused = _route(logits, bg, be, tm=tm_blk)
    xs = _moe_gather(slot_tok, nused, h, tm=tm_blk)
    hs = _moe_up(block_expert, nused, xs, w1, w3, tm=tm_blk, tf=tf)
    yb = _moe_down(block_expert, nused, hs, w2, tm=tm_blk, tn=tn)
    g0 = jnp.broadcast_to(gate[:, 0:1], (x.shape[0], LANES))
    g1 = jnp.broadcast_to(gate[:, 1:2], (x.shape[0], LANES))
    return _moe_combine(pos0, pos1, x, g0, g1, final_w, yb, tc=tc)


def kernel(x_prompt, x_sample, state_mlstm_c, state_mlstm_n, state_mlstm_m, state_rwkv, state_rwkv_shift, norm1_w, w_in, w_out, ml_b_i, ml_b_f, ml_norm_w, rw_mu, rw_w0, rw_w2, rw_a0, rw_a2, rw_g2, rw_k_k, rw_k_a, rw_r_k, rw_ln_w, rw_ln_b, norm2_w, router_group_w, router_group_b, router_expert_w, router_expert_b, moe_w1, moe_w3, moe_w2, final_norm_w):
    assert w_in.shape[0] == 1, "single-layer trunk"
    bp, seq, d = x_prompt.shape
    bs = x_sample.shape[0]
    tp = bp * seq
    ml_proj = 2 * ML_HEADS * ML_DK + 2 * ML_WIDTH + 2 * ML_HEADS
    rw_proj = 3 * RW_WIDTH + RW_LORA
    rw_col0 = -(-ml_proj // RW_GROUP) * RW_GROUP
    n_groups = RW_WIDTH // RW_GROUP

    w_cat = jnp.concatenate([w_in[0][:, :ml_proj], jnp.zeros((d, rw_col0 - ml_proj), F32), w_in[0][:, ml_proj:]],
                            axis=1).astype(BF16)
    w_o = w_out[0].astype(BF16)
    mu = rw_mu[0]
    rows = [rw_w0[0], rw_a0[0], rw_k_k[0], rw_k_a[0], rw_r_k[0].reshape(-1), rw_ln_w[0], rw_ln_b[0],
            mu[:RW_WIDTH], mu[RW_WIDTH:2 * RW_WIDTH], mu[2 * RW_WIDTH:3 * RW_WIDTH]]
    prm = jnp.concatenate([jnp.stack(rows), jnp.zeros((16 - len(rows), RW_WIDTH), F32)], axis=0)
    mu_l = mu[3 * RW_WIDTH:][None]
    wl = jnp.zeros((RW_LORA, 3, RW_WIDTH), F32)
    wl = wl.at[:RW_DECAY_LORA, 0].set(rw_w2[0]).at[RW_DECAY_LORA:RW_DECAY_LORA + RW_A_LORA, 1].set(rw_a2[0])
    wl = wl.at[RW_DECAY_LORA + RW_A_LORA:, 2].set(rw_g2[0])
    wl = wl.reshape(RW_LORA, 3, n_groups, RW_GROUP).transpose(2, 0, 1, 3).reshape(n_groups, RW_LORA, 3 * RW_GROUP)
    wl = wl.astype(BF16)
    gate_bias = jnp.zeros((1, LANES), F32).at[0, :ML_HEADS].set(ml_b_i[0]).at[0, ML_HEADS:2 * ML_HEADS].set(ml_b_f[0])
    ml_nw = ml_norm_w[0][None]
    wr = jnp.concatenate([router_group_w[0], router_expert_w[0],
                          jnp.zeros((d, LANES - N_GROUPS - N_EXPERTS), F32)], axis=1)

    x_all = jnp.concatenate([x_prompt.reshape(tp, d), x_sample.reshape(bs, d)], axis=0)
    proj = _in_proj(x_all, norm1_w, w_cat, tm=640, tn=512)

    y_ml_p, p_c, p_nm = _mlstm_prompt(proj, gate_bias, ml_nw, batch=bp, seq=seq)
    y_rw_p, p_s = _rwkv_prompt(proj, prm, mu_l, wl, batch=bp, seq=seq, rw_col0=rw_col0, n_groups=n_groups)
    p_sh = proj[:tp].reshape(bp, seq, -1)[:, -1, rw_col0:rw_col0 + rw_proj]

    proj_s = proj[tp:]
    y_ml_s, s_c, s_n, s_m = _mlstm_sample(proj_s[:, None, :], gate_bias, ml_nw, state_mlstm_c[0], state_mlstm_n[0],
                                          state_mlstm_m[0][:, None, :])
    r, k, v, kk, ka, dec, g = _rwkv_sample_prep(proj, state_rwkv_shift[0], prm, mu_l, wl, row0=tp, rows=bs,
                                                rw_col0=rw_col0, n_groups=n_groups)
    s_s, y_s = _rwkv_sample_state(state_rwkv[0], r, k, kk, ka, dec, v)
    y_rw_s = _rwkv_sample_post(y_s, r, k, v, g, prm)
    s_sh = proj_s[:, rw_col0:rw_col0 + rw_proj]

    y_ml = jnp.concatenate([y_ml_p, y_ml_s.reshape(bs, ML_WIDTH)], axis=0)
    y_rw = jnp.concatenate([y_rw_p, y_rw_s], axis=0)
    x2 = _out_proj(y_ml, y_rw, w_o, x_all, tm=640, tn=1024)
    y = _moe(x2, norm2_w, wr, router_group_b[0], router_expert_b[0], moe_w1[0], moe_w3[0], moe_w2[0],
             final_norm_w[None], tm_route=320, tm_blk=256, tf=256, tn=1024, tc=128)

    lead = lambda a: a[None]
    return (y[:tp].reshape(bp, seq, d), y[tp:].reshape(bs, 1, d),
            lead(p_c), lead(p_nm[:, :, 0]), lead(p_nm[:, :, 1, 0]), lead(p_s), lead(p_sh),
            lead(s_c), lead(s_n), lead(s_m[:, 0]), lead(s_s), lead(s_sh))
```

```python
import functools

import jax
import jax.numpy as jnp
from jax import lax
from jax.experimental import pallas as pl
from jax.experimental.pallas import tpu as pltpu

F32 = jnp.float32
BF16 = jnp.bfloat16

D_MODEL = 4096
ML_HEADS = 4
ML_DK = 256
ML_DV = 512
ML_WIDTH = ML_HEADS * ML_DV
GATE_SOFTCAP = 15.0
RW_HEAD = 64
RW_WIDTH = D_MODEL - ML_WIDTH
RW_HEADS = RW_WIDTH // RW_HEAD
RW_DECAY_LORA = 96
RW_A_LORA = 96
RW_GATE_LORA = 64
RW_GN_EPS = 64e-5
N_GROUPS = 4
EXPERTS_PER_GROUP = 8
N_EXPERTS = N_GROUPS * EXPERTS_PER_GROUP
TOP_K = 2
EPS = 1e-6

LANES = 128
RW_GROUP = 256
RW_GHEADS = RW_GROUP // RW_HEAD
RW_LORA = RW_DECAY_LORA + RW_A_LORA + RW_GATE_LORA
RW_CHUNK = 64
ML_CHUNK = 64
_XLU_STAGGER = 3
VMEM_LIMIT = 56 * 1024 * 1024


def _cparams(n_axes, vmem=None):
    return pltpu.CompilerParams(dimension_semantics=("arbitrary",) * n_axes, vmem_limit_bytes=vmem)


def _sigmoid(x):
    return 1.0 / (1.0 + jnp.exp(-x))


def _softplus(x):
    return jnp.maximum(x, 0.0) + jnp.log(1.0 + jnp.exp(-jnp.abs(x)))


def _soft_cap(x):
    return GATE_SOFTCAP * jnp.tanh(x / GATE_SOFTCAP)


def _dot(a, b):
    return jnp.dot(a.astype(BF16), b.astype(BF16), preferred_element_type=F32)


def _dot_nt(a, b):
    return lax.dot_general(a.astype(BF16), b.astype(BF16), (((1,), (1,)), ((), ())),
                           preferred_element_type=F32)


def _split2(x):
    hi = x.astype(BF16)
    lo = (x - hi.astype(F32)).astype(BF16)
    return hi, lo


def _split3(x):
    hi = x.astype(BF16)
    r = x - hi.astype(F32)
    mid = r.astype(BF16)
    lo = (r - mid.astype(F32)).astype(BF16)
    return hi, mid, lo


def _iota(shape, axis):
    return lax.broadcasted_iota(jnp.int32, shape, axis)


def _head_block_ones(n):
    return jnp.where((_iota((n, n), 0) >> 6) == (_iota((n, n), 1) >> 6), 1.0, 0.0).astype(BF16)


def _seg_sum(x, bd):
    hi, lo = _split2(x)
    return (jnp.dot(hi, bd, preferred_element_type=F32) + jnp.dot(lo, bd, preferred_element_type=F32))


_P_W0, _P_A0, _P_KK, _P_KA, _P_RK, _P_LNW, _P_LNB, _P_MUR, _P_MUK, _P_MUV = range(10)


def _rwkv_prep(p_r, p_k, p_v, p_l, prev_r, prev_k, prev_v, prev_l, prm, mu_l, wl, bd):
    row = lambda i: prm[i:i + 1, :]
    xr = p_r + (prev_r - p_r) * row(_P_MUR)
    xk = p_k + (prev_k - p_k) * row(_P_MUK)
    xv = p_v + (prev_v - p_v) * row(_P_MUV)
    xl = p_l + (prev_l - p_l) * mu_l
    lane = _iota(xl.shape, 1)
    z = jnp.where(lane < RW_DECAY_LORA, jnp.tanh(xl),
                  jnp.where(lane < RW_DECAY_LORA + RW_A_LORA, xl, _sigmoid(xl)))
    lo = _dot(z, wl)
    lw, la, lg = lo[:, :RW_GROUP], lo[:, RW_GROUP:2 * RW_GROUP], lo[:, 2 * RW_GROUP:]
    w = -_softplus(-(row(_P_W0) + lw)) - 0.5
    logd = -jnp.exp(w)
    a = _sigmoid(row(_P_A0) + la)
    kk = xk * row(_P_KK)
    kk = kk / jnp.maximum(jnp.sqrt(_seg_sum(kk * kk, bd)), 1e-12)
    k = xk * (1.0 + (a - 1.0) * row(_P_KA))
    return xr, k, xv, kk, a, logd, lg


def _rwkv_post(y, r, k, v, g, prm, bd):
    row = lambda i: prm[i:i + 1, :]
    mean = _seg_sum(y, bd) * (1.0 / RW_HEAD)
    yc = y - mean
    var = _seg_sum(yc * yc, bd) * (1.0 / RW_HEAD)
    yn = yc * lax.rsqrt(var + RW_GN_EPS) * row(_P_LNW) + row(_P_LNB)
    bonus = _seg_sum(r * k * row(_P_RK), bd) * v
    return (yn + bonus) * g


def _rwkv_chunk(r, k, v, kk, a, logd, G, bd):
    L = r.shape[0]
    tril = jnp.where(_iota((L, L), 1) <= _iota((L, L), 0), 1.0, 0.0).astype(BF16)
    d_hi, d_lo = _split2(logd)
    clog = (jnp.dot(tril, d_hi, preferred_element_type=F32)
            + jnp.dot(tril, d_lo, preferred_element_type=F32))
    clog_l = clog[L - 1:L, :]
    n_in = jnp.exp(-clog)
    to_end = jnp.exp(clog_l - clog)
    ka = kk * a
    at = -kk * jnp.exp(clog - logd)
    rt = r * jnp.exp(clog)
    lane_head = _iota((1, RW_GROUP), 1) >> 6
    masks = [lane_head == h for h in range(RW_GHEADS)]
    zero = jnp.zeros_like(at)
    bf = lambda x: x.astype(BF16)
    lhs = jnp.concatenate([jnp.where(m, x, zero) for m in masks for x in (at, rt)], axis=0)
    xx = _dot_nt(lhs, jnp.concatenate([ka * n_in, k * n_in], axis=0))
    t2 = _iota((L, 2 * L), 0)
    c2 = _iota((L, 2 * L), 1)
    s2 = c2 & (L - 1)
    right = c2 >= L
    eye_pad = jnp.where(c2 == t2 + L, 1.0, 0.0)
    zeros_v = jnp.zeros((L, RW_GROUP), BF16)
    v_b = bf(v)
    vz = jnp.concatenate([zeros_v, v_b], axis=0)
    yield
    zs, rbk = [], []
    makv = zero
    for h, m in enumerate(masks):
        o = 2 * L * h
        nk = jnp.where(s2 < t2, xx[o:o + L], 0.0)
        rbk.append(bf(jnp.where(s2 <= t2, xx[o + L:o + 2 * L], 0.0)))
        makv = makv + jnp.where(m, _dot(jnp.where(right, nk, 0.0), vz), 0.0)
        zs.append(jnp.where(right, eye_pad, nk))
    for _ in range((L - 1).bit_length()):
        yield
        zs = [_dot(z[:, :L], z) + jnp.where(right, z, 0.0) for z in zs]
    yield
    ws = _dot_nt(jnp.concatenate([at, rt], axis=0), G)
    xz = jnp.concatenate([zeros_v, bf(ws[:L] + makv)], axis=0)
    yield
    w_all = zero
    for z_h, m in zip(zs, masks):
        w_all = w_all + jnp.where(m, _dot(z_h, xz), 0.0)
    wv = jnp.concatenate([bf(w_all), v_b], axis=0)
    yield
    y = ws[L:]
    for rbk_h, m in zip(rbk, masks):
        y = y + jnp.where(m, jnp.dot(rbk_h, wv, preferred_element_type=F32), 0.0)
    upd = _dot(wv.astype(F32).T, jnp.concatenate([ka * to_end, k * to_end], axis=0))
    g_new = G * jnp.exp(clog_l) + jnp.where(bd > 0, upd, 0.0)
    return y, g_new


def _interleave(gens):
    results = [None] * len(gens)
    live = list(range(len(gens)))
    while live:
        for i in list(live):
            try:
                next(gens[i])
            except StopIteration as stop:
                results[i] = stop.value
                live.remove(i)
    return results


def _shift_rows(x, first_row):
    rolled = pltpu.roll(x, 1, axis=0)
    return jnp.where(_iota(x.shape, 0) == 0, first_row, rolled)


def _rwkv_prompt_kernel(*refs, n_chunks, batch):
    p_refs = refs[:4 * batch]
    prm_ref, mul_ref, wl_ref, y_ref, s_ref, g_scr, carry_scr = refs[4 * batch:]
    c = pl.program_id(1)

    @pl.when(c == 0)
    def _():
        g_scr[...] = jnp.zeros_like(g_scr)
        carry_scr[...] = jnp.zeros_like(carry_scr)

    bd = _head_block_ones(RW_GROUP)
    prm = prm_ref[...]

    def sequence(b):
        ps = [ref[...] for ref in p_refs[4 * b:4 * b + 4]]
        L = ps[0].shape[0]
        prevs = [_shift_rows(x, carry_scr[b, i:i + 1, :]) for i, x in enumerate(ps)]
        r, k, v, kk, a, logd, g = _rwkv_prep(*ps, *prevs, prm, mul_ref[...], wl_ref[0], bd)
        y, g_new = yield from _rwkv_chunk(r, k, v, kk, a, logd, g_scr[b], bd)
        yield
        out = _rwkv_post(y, r, k, v, g, prm, bd).astype(y_ref.dtype)
        return out, g_new, [x[L - 1:L, :] for x in ps]

    results = _interleave([sequence(b) for b in range(batch)])
    finals = []
    for b, (out, g_new, last_rows) in enumerate(results):
        y_ref[b] = out
        g_scr[b] = g_new
        for i, x in enumerate(last_rows):
            carry_scr[b, i:i + 1, :] = x
        finals.append(g_new)

    @pl.when(c == n_chunks - 1)
    def _():
        for b in range(batch):
            for h in range(RW_GHEADS):
                s_ref[b, h] = finals[b][RW_HEAD * h:RW_HEAD * (h + 1), RW_HEAD * h:RW_HEAD * (h + 1)]


def _rwkv_prompt(proj, prm, mu_l, wl, *, batch, seq, rw_col0, n_groups):
    L = RW_CHUNK
    nc = seq // L
    cb = rw_col0 // RW_GROUP
    width = n_groups * RW_GROUP
    pspec = lambda b, off: pl.BlockSpec((L, RW_GROUP), lambda g, c, b=b, off=off: (b * nc + c, cb + off + g))
    lspec = lambda b: pl.BlockSpec((L, RW_GROUP), lambda g, c, b=b: (b * nc + c, cb + 3 * n_groups))
    p_specs = []
    for b in range(batch):
        p_specs += [pspec(b, 0), pspec(b, n_groups), pspec(b, 2 * n_groups), lspec(b)]
    y, s = pl.pallas_call(
        functools.partial(_rwkv_prompt_kernel, n_chunks=nc, batch=batch),
        grid=(n_groups, nc),
        in_specs=p_specs + [pl.BlockSpec((16, RW_GROUP), lambda g, c: (0, g)),
                            pl.BlockSpec((1, RW_LORA), lambda g, c: (0, 0)),
                            pl.BlockSpec((1, RW_LORA, 3 * RW_GROUP), lambda g, c: (g, 0, 0))],
        out_specs=[pl.BlockSpec((batch, L, RW_GROUP), lambda g, c: (0, c, g)),
                   pl.BlockSpec((batch, RW_GHEADS, RW_HEAD, RW_HEAD), lambda g, c: (0, g, 0, 0))],
        out_shape=[jax.ShapeDtypeStruct((batch, seq, width), BF16),
                   jax.ShapeDtypeStruct((batch, n_groups * RW_GHEADS, RW_HEAD, RW_HEAD), F32)],
        scratch_shapes=[pltpu.VMEM((batch, RW_GROUP, RW_GROUP), F32), pltpu.VMEM((batch, 8, RW_GROUP), F32)],
        compiler_params=_cparams(2),
        name="rwkv_prompt",
    )(*([proj] * (4 * batch)), prm, mu_l, wl)
    return y.reshape(batch * seq, width), s


def _rwkv_sample_prep_kernel(pr_ref, pk_ref, pv_ref, pl_ref, sr_ref, sk_ref, sv_ref, sl_ref, prm_ref, mul_ref,
                             wl_ref, r_ref, k_ref, v_ref, kk_ref, ka_ref, d_ref, g_ref):
    bd = _head_block_ones(RW_GROUP)
    r, k, v, kk, a, logd, g = _rwkv_prep(pr_ref[...], pk_ref[...], pv_ref[...], pl_ref[...], sr_ref[...],
                                         sk_ref[...], sv_ref[...], sl_ref[...], prm_ref[...], mul_ref[...],
                                         wl_ref[0], bd)
    r_ref[...] = r
    k_ref[...] = k
    v_ref[...] = v
    kk_ref[...] = kk
    ka_ref[...] = kk * a
    d_ref[...] = jnp.exp(logd)
    g_ref[...] = g


def _rwkv_sample_prep(proj, shift0, prm, mu_l, wl, *, row0, rows, rw_col0, n_groups):
    rb = row0 // rows
    cb = rw_col0 // RW_GROUP
    pspec = lambda off: pl.BlockSpec((rows, RW_GROUP), lambda g, off=off: (rb, cb + off + g))
    sspec = lambda off: pl.BlockSpec((rows, RW_GROUP), lambda g, off=off: (0, off + g))
    ospec = pl.BlockSpec((rows, RW_GROUP), lambda g: (0, g))
    width = n_groups * RW_GROUP
    return pl.pallas_call(
        _rwkv_sample_prep_kernel,
        grid=(n_groups,),
        in_specs=[pspec(0), pspec(n_groups), pspec(2 * n_groups),
                  pl.BlockSpec((rows, RW_GROUP), lambda g: (rb, cb + 3 * n_groups)),
                  sspec(0), sspec(n_groups), sspec(2 * n_groups),
                  pl.BlockSpec((rows, RW_GROUP), lambda g: (0, 3 * n_groups)),
                  pl.BlockSpec((16, RW_GROUP), lambda g: (0, g)),
                  pl.BlockSpec((1, RW_LORA), lambda g: (0, 0)),
                  pl.BlockSpec((1, RW_LORA, 3 * RW_GROUP), lambda g: (g, 0, 0))],
        out_specs=[ospec] * 7,
        out_shape=[jax.ShapeDtypeStruct((rows, width), F32)] * 7,
        compiler_params=_cparams(1),
        name="rwkv_sample_prep",
    )(proj, proj, proj, proj, shift0, shift0, shift0, shift0, prm, mu_l, wl)


def _rwkv_sample_state_kernel(s_ref, r_ref, k_ref, kk_ref, ka_ref, d_ref, vt_ref, so_ref, yt_ref, *, n_heads):
    row = lambda ref, h: ref[0, h:h + 1, :]

    def head_group(g):
        heads = range(g * RW_GHEADS, (g + 1) * RW_GHEADS)
        for _ in range(g):
            yield
        sa = [jnp.sum(s_ref[0, h] * (-row(kk_ref, h)), axis=1, keepdims=True) for h in heads]
        for _ in range(_XLU_STAGGER):
            yield
        y_cols = []
        for h, sa_h in zip(heads, sa):
            v_col = vt_ref[0, :, h:h + 1]
            s_new = s_ref[0, h] * row(d_ref, h) + sa_h * row(ka_ref, h) + v_col * row(k_ref, h)
            so_ref[0, h] = s_new
            y_cols.append(jnp.sum(s_new * row(r_ref, h), axis=1, keepdims=True))
        for _ in range(_XLU_STAGGER):
            yield
        for h, y_col in zip(heads, y_cols):
            yt_ref[0, :, h:h + 1] = y_col

    _interleave([head_group(g) for g in range(n_heads // RW_GHEADS)])


def _rwkv_sample_state(state, r, k, kk, ka, d, v):
    b, nh = state.shape[0], state.shape[1]
    heads = lambda x: x.reshape(b, nh, RW_HEAD)
    vt = jnp.swapaxes(heads(v), 1, 2)
    hspec = pl.BlockSpec((1, nh, RW_HEAD), lambda i: (i, 0, 0))
    tspec = pl.BlockSpec((1, RW_HEAD, nh), lambda i: (i, 0, 0))
    sspec = pl.BlockSpec((1, nh, RW_HEAD, RW_HEAD), lambda i: (i, 0, 0, 0))
    s_new, yt = pl.pallas_call(
        functools.partial(_rwkv_sample_state_kernel, n_heads=nh),
        grid=(b,),
        in_specs=[sspec, hspec, hspec, hspec, hspec, hspec, tspec],
        out_specs=[sspec, tspec],
        out_shape=[jax.ShapeDtypeStruct(state.shape, F32), jax.ShapeDtypeStruct((b, RW_HEAD, nh), F32)],
        compiler_params=_cparams(1),
        name="rwkv_sample_state",
    )(state, heads(r), heads(k), heads(kk), heads(ka), heads(d), vt)
    return s_new, jnp.swapaxes(yt, 1, 2).reshape(b, nh * RW_HEAD)


def _rwkv_sample_post_kernel(y_ref, r_ref, k_ref, v_ref, g_ref, prm_ref, o_ref):
    bd = _head_block_ones(RW_GROUP)
    o_ref[...] = _rwkv_post(y_ref[...], r_ref[...], k_ref[...], v_ref[...], g_ref[...], prm_ref[...],
                            bd).astype(o_ref.dtype)


def _rwkv_sample_post(y, r, k, v, g, prm):
    rows, width = y.shape
    spec = pl.BlockSpec((rows, RW_GROUP), lambda i: (0, i))
    return pl.pallas_call(
        _rwkv_sample_post_kernel,
        grid=(width // RW_GROUP,),
        in_specs=[spec] * 5 + [pl.BlockSpec((16, RW_GROUP), lambda i: (0, i))],
        out_specs=spec,
        out_shape=jax.ShapeDtypeStruct((rows, width), BF16),
        compiler_params=_cparams(1),
        name="rwkv_sample_post",
    )(y, r, k, v, g, prm)


def _mlstm_gates(gates, bias, h):
    capped = _soft_cap(gates + bias)
    lane = _iota(gates.shape, 1)
    i_col = jnp.sum(jnp.where(lane == h, capped, 0.0), axis=1, keepdims=True)
    f_col = jnp.sum(jnp.where(lane == h + ML_HEADS, -_softplus(-capped), 0.0), axis=1, keepdims=True)
    return i_col, f_col


def _mlstm_out(hh, o, norm_w):
    hn = hh * lax.rsqrt(jnp.mean(hh * hh, axis=1, keepdims=True) + EPS) * norm_w
    return hn * _sigmoid(o)


def _mlstm_prompt_kernel(*refs, n_chunks, batch):
    p_refs = refs[:5 * batch]
    gb_ref, nw_ref, y_ref, c_ref, nm_ref, c_scr, n_scr, m_scr = refs[5 * batch:]
    h = pl.program_id(0)
    c = pl.program_id(1)

    @pl.when(c == 0)
    def _():
        c_scr[...] = jnp.zeros_like(c_scr)
        n_scr[...] = jnp.zeros_like(n_scr)
        m_scr[...] = jnp.zeros_like(m_scr)

    def sequence(b):
        q_ref, k_ref, v_ref, o_ref, gt_ref = p_refs[5 * b:5 * b + 5]
        q = q_ref[...] * (ML_DK ** -0.5)
        k = k_ref[...]
        v = v_ref[...]
        L = q.shape[0]
        i_col, f_col = _mlstm_gates(gt_ref[...], gb_ref[...], h)
        t_i = _iota((L, L), 0)
        s_i = _iota((L, L), 1)
        causal = s_i <= t_i
        to_row = lambda col: jnp.sum(jnp.where(t_i == s_i, col, 0.0), axis=0, keepdims=True)
        i_row = to_row(i_col)
        f_row = to_row(f_col)
        yield
        b_col = jnp.sum(jnp.where(causal, f_row, 0.0), axis=1, keepdims=True)
        b_row = jnp.sum(jnp.where(t_i <= s_i, f_col, 0.0), axis=0, keepdims=True)
        m0 = m_scr[b]
        inter = b_col + m0
        dmat = jnp.where(causal, b_col - b_row + i_row, -1e30)
        yield
        m_t = jnp.maximum(inter, jnp.max(dmat, axis=1, keepdims=True))
        w_int = jnp.exp(inter - m_t)
        s = _dot_nt(q, k) * jnp.exp(dmat - m_t)
        c0 = c_scr[b]
        n0 = n_scr[b]
        yield
        num = w_int * _dot(q, c0) + _dot(s, v)
        den = w_int * jnp.sum(q * n0, axis=1, keepdims=True) + jnp.sum(s, axis=1, keepdims=True)
        hh = num / jnp.maximum(jnp.abs(den), jnp.exp(-m_t))
        m_new = m_t[L - 1:L, :]
        b_l = b_col[L - 1:L, :]
        a_end = jnp.exp(b_l - b_col + i_col - m_new)
        dec = jnp.exp(b_l + m0 - m_new)
        ka = k * a_end
        yield
        c_new = dec * c0 + _dot(ka.T, v)
        n_new = dec * n0 + jnp.sum(ka, axis=0, keepdims=True)
        out = _mlstm_out(hh, o_ref[...], nw_ref[...]).astype(y_ref.dtype)
        return out, c_new, n_new, m_new

    results = _interleave([sequence(b) for b in range(batch)])
    for b, (out, c_new, n_new, m_new) in enumerate(results):
        y_ref[b] = out
        c_scr[b] = c_new
        n_scr[b] = n_new
        m_scr[b] = m_new

    @pl.when(c == n_chunks - 1)
    def _():
        for b, (_, c_new, n_new, m_new) in enumerate(results):
            c_ref[b, 0] = c_new
            nm_ref[b, 0] = jnp.concatenate([n_new, jnp.broadcast_to(m_new, (7, ML_DK))], axis=0)


def _mlstm_prompt(proj, gate_bias, norm_w, *, batch, seq):
    L = ML_CHUNK
    nc = seq // L
    nh = ML_HEADS
    kq = nh * ML_DK
    p_specs = []
    for b in range(batch):
        rows = lambda c, b=b: b * nc + c
        p_specs += [pl.BlockSpec((L, ML_DK), lambda h, c, rows=rows: (rows(c), h)),
                    pl.BlockSpec((L, ML_DK), lambda h, c, rows=rows: (rows(c), nh + h)),
                    pl.BlockSpec((L, ML_DV), lambda h, c, rows=rows: (rows(c), 2 * kq // ML_DV + h)),
                    pl.BlockSpec((L, ML_DV), lambda h, c, rows=rows: (rows(c), 2 * kq // ML_DV + nh + h)),
                    pl.BlockSpec((L, LANES), lambda h, c, rows=rows: (rows(c), (2 * kq + 2 * ML_WIDTH) // LANES))]
    y, c_fin, nm = pl.pallas_call(
        functools.partial(_mlstm_prompt_kernel, n_chunks=nc, batch=batch),
        grid=(nh, nc),
        in_specs=p_specs + [pl.BlockSpec((1, LANES), lambda h, c: (0, 0)),
                            pl.BlockSpec((1, ML_DV), lambda h, c: (0, h))],
        out_specs=[pl.BlockSpec((batch, L, ML_DV), lambda h, c: (0, c, h)),
                   pl.BlockSpec((batch, 1, ML_DK, ML_DV), lambda h, c: (0, h, 0, 0)),
                   pl.BlockSpec((batch, 1, 8, ML_DK), lambda h, c: (0, h, 0, 0))],
        out_shape=[jax.ShapeDtypeStruct((batch, seq, ML_WIDTH), BF16),
                   jax.ShapeDtypeStruct((batch, nh, ML_DK, ML_DV), F32),
                   jax.ShapeDtypeStruct((batch, nh, 8, ML_DK), F32)],
        scratch_shapes=[pltpu.VMEM((batch, ML_DK, ML_DV), F32), pltpu.VMEM((batch, 1, ML_DK), F32),
                        pltpu.VMEM((batch, 1, 1), F32)],
        compiler_params=_cparams(2),
        name="mlstm_prompt",
    )(*([proj] * (5 * batch)), gate_bias, norm_w)
    return y.reshape(batch * seq, ML_WIDTH), c_fin, nm


def _mlstm_sample_kernel(q_ref, k_ref, v_ref, o_ref, gt_ref, gb_ref, nw_ref, c_ref, n_ref, m_ref,
                         y_ref, co_ref, no_ref, mo_ref):
    gates = gt_ref[0]
    eye = jnp.where(_iota((ML_DK, ML_DK), 0) == _iota((ML_DK, ML_DK), 1), 1.0, 0.0).astype(BF16)
    for h in range(ML_HEADS):
        q = q_ref[0][:, h * ML_DK:(h + 1) * ML_DK] * (ML_DK ** -0.5)
        k = k_ref[0][:, h * ML_DK:(h + 1) * ML_DK]
        v = v_ref[0][:, h * ML_DV:(h + 1) * ML_DV]
        i_pre, logf = _mlstm_gates(gates, gb_ref[...], h)
        c0 = c_ref[0, h]
        n0 = n_ref[0, h:h + 1, :]
        m0 = m_ref[0][:, h:h + 1]
        inter = logf + m0
        m_t = jnp.maximum(inter, i_pre)
        w_int = jnp.exp(inter - m_t)
        a_new = jnp.exp(i_pre - m_t)
        s = jnp.sum(q * k, axis=1, keepdims=True) * a_new
        qc = _dot(jnp.broadcast_to(q, (8, ML_DK)), c0)[0:1, :]
        num = w_int * qc + s * v
        den = w_int * jnp.sum(q * n0, axis=1, keepdims=True) + s
        hh = num / jnp.maximum(jnp.abs(den), jnp.exp(-m_t))
        k_hi, k_lo = _split2(jnp.broadcast_to(k, (8, ML_DK)))
        nt = (((1,), (1,)), ((), ()))
        k_col = (lax.dot_general(eye, k_hi, nt, preferred_element_type=F32)
                 + lax.dot_general(eye, k_lo, nt, preferred_element_type=F32))[:, 0:1]
        co_ref[0, h] = w_int * c0 + k_col * (a_new * v)
        no_ref[0, h:h + 1, :] = w_int * n0 + a_new * k
        mo_ref[0, :, h:h + 1] = m_t
        y_ref[0, :, h * ML_DV:(h + 1) * ML_DV] = _mlstm_out(
            hh, o_ref[0][:, h * ML_DV:(h + 1) * ML_DV], nw_ref[:, h * ML_DV:(h + 1) * ML_DV]).astype(y_ref.dtype)


def _mlstm_sample(proj3, gate_bias, norm_w, c0, n0, m0):
    b = proj3.shape[0]
    nh = ML_HEADS
    kq = nh * ML_DK
    p3 = lambda w, blk: pl.BlockSpec((1, 1, w), lambda i, blk=blk: (i, 0, blk))
    cspec = pl.BlockSpec((1, nh, ML_DK, ML_DV), lambda i: (i, 0, 0, 0))
    nspec = pl.BlockSpec((1, nh, ML_DK), lambda i: (i, 0, 0))
    mspec = pl.BlockSpec((1, 1, nh), lambda i: (i, 0, 0))
    return pl.pallas_call(
        _mlstm_sample_kernel,
        grid=(b,),
        in_specs=[p3(kq, 0), p3(kq, 1), p3(ML_WIDTH, 2 * kq // ML_WIDTH), p3(ML_WIDTH, 2 * kq // ML_WIDTH + 1),
                  p3(LANES, (2 * kq + 2 * ML_WIDTH) // LANES),
                  pl.BlockSpec((1, LANES), lambda i: (0, 0)),
                  pl.BlockSpec((1, ML_WIDTH), lambda i: (0, 0)),
                  cspec, nspec, mspec],
        out_specs=[pl.BlockSpec((1, 1, ML_WIDTH), lambda i: (i, 0, 0)), cspec, nspec, mspec],
        out_shape=[jax.ShapeDtypeStruct((b, 1, ML_WIDTH), BF16),
                   jax.ShapeDtypeStruct(c0.shape, F32), jax.ShapeDtypeStruct(n0.shape, F32),
                   jax.ShapeDtypeStruct(m0.shape, F32)],
        compiler_params=_cparams(1),
        name="mlstm_sample",
    )(proj3, proj3, proj3, proj3, proj3, gate_bias, norm_w, c0, n0, m0)


def _rms(x, w):
    return x * lax.rsqrt(jnp.mean(x * x, axis=1, keepdims=True) + EPS) * w


def _in_proj_kernel(x_ref, nw_ref, wa_ref, wb_ref, o_ref, h_scr, *, n_a):
    j = pl.program_id(1)

    @pl.when(j == 0)
    def _():
        h_scr[...] = _rms(x_ref[...], nw_ref[...]).astype(BF16)

    @pl.when(j < n_a)
    def _():
        o_ref[...] = jnp.dot(h_scr[...], wa_ref[...], preferred_element_type=F32)

    @pl.when(j >= n_a)
    def _():
        o_ref[...] = jnp.dot(h_scr[...], wb_ref[...], preferred_element_type=F32)


def _in_proj(x, norm_w, wa, wb, *, tm, tn):
    t, d = x.shape
    n_a, n_b = wa.shape[1] // tn, wb.shape[1] // tn
    return pl.pallas_call(
        functools.partial(_in_proj_kernel, n_a=n_a),
        grid=(t // tm, n_a + n_b),
        in_specs=[pl.BlockSpec((tm, d), lambda i, j: (i, 0)),
                  pl.BlockSpec((1, d), lambda i, j: (0, 0)),
                  pl.BlockSpec((d, tn), lambda i, j: (0, jnp.minimum(j, n_a - 1))),
                  pl.BlockSpec((d, tn), lambda i, j: (0, jnp.maximum(j - n_a, 0)))],
        out_specs=pl.BlockSpec((tm, tn), lambda i, j: (i, j)),
        out_shape=jax.ShapeDtypeStruct((t, (n_a + n_b) * tn), F32),
        scratch_shapes=[pltpu.VMEM((tm, d), BF16)],
        compiler_params=_cparams(2, VMEM_LIMIT),
        name="in_proj",
    )(x, norm_w, wa, wb)


def _out_proj_kernel(ya_ref, yb_ref, wa_ref, wb_ref, x_ref, o_ref):
    o_ref[...] = (x_ref[...] + jnp.dot(ya_ref[...], wa_ref[...], preferred_element_type=F32)
                  + jnp.dot(yb_ref[...], wb_ref[...], preferred_element_type=F32))


def _out_proj(ya, yb, w, x, *, tm, tn):
    t, kh = ya.shape
    n = w.shape[1]
    return pl.pallas_call(
        _out_proj_kernel,
        grid=(t // tm, n // tn),
        in_specs=[pl.BlockSpec((tm, kh), lambda i, j: (i, 0)),
                  pl.BlockSpec((tm, kh), lambda i, j: (i, 0)),
                  pl.BlockSpec((kh, tn), lambda i, j: (0, j)),
                  pl.BlockSpec((kh, tn), lambda i, j: (1, j)),
                  pl.BlockSpec((tm, tn), lambda i, j: (i, j))],
        out_specs=pl.BlockSpec((tm, tn), lambda i, j: (i, j)),
        out_shape=jax.ShapeDtypeStruct((t, n), F32),
        compiler_params=_cparams(2, VMEM_LIMIT),
        name="out_proj",
    )(ya, yb, w, w, x)


def _router_kernel(xp_ref, xs_ref, nw_ref, wr_ref, h_ref, lg_ref, *, nb_p):
    def emit(x_ref):
        h = _rms(x_ref[...], nw_ref[...])
        h_ref[...] = h
        lg_ref[...] = jnp.dot(h, wr_ref[...], preferred_element_type=F32, precision=lax.Precision.HIGHEST)

    pl.when(pl.program_id(0) < nb_p)(lambda: emit(xp_ref))
    pl.when(pl.program_id(0) >= nb_p)(lambda: emit(xs_ref))


def _router(xp, xs, norm_w, wr, *, tm):
    d = xp.shape[1]
    nb_p, nb_s = xp.shape[0] // tm, xs.shape[0] // tm
    t = xp.shape[0] + xs.shape[0]
    return pl.pallas_call(
        functools.partial(_router_kernel, nb_p=nb_p),
        grid=(nb_p + nb_s,),
        in_specs=[pl.BlockSpec((tm, d), lambda i: (jnp.minimum(i, nb_p - 1), 0)),
                  pl.BlockSpec((tm, d), lambda i: (jnp.maximum(i - nb_p, 0), 0)),
                  pl.BlockSpec((1, d), lambda i: (0, 0)),
                  pl.BlockSpec((d, LANES), lambda i: (0, 0))],
        out_specs=[pl.BlockSpec((tm, d), lambda i: (i, 0)), pl.BlockSpec((tm, LANES), lambda i: (i, 0))],
        out_shape=[jax.ShapeDtypeStruct((t, d), F32), jax.ShapeDtypeStruct((t, LANES), F32)],
        compiler_params=_cparams(1, VMEM_LIMIT),
        name="router",
    )(xp, xs, norm_w, wr)


def _gather_rows(src_hbm, dst_vmem, idx_ref, base, n_rows, sem):
    def copy(r):
        return pltpu.make_async_copy(src_hbm.at[pl.ds(idx_ref[base + r], 1), :], dst_vmem.at[pl.ds(r, 1), :], sem)

    def start(r, carry):
        copy(r).start()
        return carry

    def wait(r, carry):
        copy(r).wait()
        return carry

    lax.fori_loop(0, n_rows, start, 0)
    lax.fori_loop(0, n_rows, wait, 0)


def _moe_gather_kernel(tok_ref, nused_ref, h_hbm, o_ref, buf, sem, *, tm):
    i = pl.program_id(0)

    @pl.when(i < nused_ref[0])
    def _():
        _gather_rows(h_hbm, buf, tok_ref, i * tm, tm, sem)
        o_ref[...] = buf[...].astype(o_ref.dtype)

    @pl.when(i >= nused_ref[0])
    def _():
        o_ref[...] = jnp.zeros_like(o_ref)


def _moe_gather(slot_tok, nused, h, *, tm):
    p = slot_tok.shape[0]
    d = h.shape[1]
    return pl.pallas_call(
        functools.partial(_moe_gather_kernel, tm=tm),
        grid_spec=pltpu.PrefetchScalarGridSpec(
            num_scalar_prefetch=2,
            grid=(p // tm,),
            in_specs=[pl.BlockSpec(memory_space=pl.ANY)],
            out_specs=pl.BlockSpec((tm, d), lambda i, tok, nu: (i, 0)),
            scratch_shapes=[pltpu.VMEM((tm, d), F32), pltpu.SemaphoreType.DMA(())]),
        out_shape=jax.ShapeDtypeStruct((p, d), BF16),
        compiler_params=_cparams(1, VMEM_LIMIT),
        name="moe_gather",
    )(slot_tok, nused, h)


def _moe_up_kernel(be_ref, nused_ref, x_ref, w1_ref, w3_ref, o_ref):
    @pl.when(pl.program_id(1) < nused_ref[0])
    def _():
        x = x_ref[...]
        a = jnp.dot(x, w1_ref[0].astype(BF16), preferred_element_type=F32)
        b = jnp.dot(x, w3_ref[0].astype(BF16), preferred_element_type=F32)
        o_ref[...] = (a * _sigmoid(a) * b).astype(o_ref.dtype)

    @pl.when(pl.program_id(1) >= nused_ref[0])
    def _():
        o_ref[...] = jnp.zeros_like(o_ref)


def _moe_up(block_expert, nused, xs, w1, w3, *, tm, tf):
    p, d = xs.shape
    ff = w1.shape[2]
    wspec = pl.BlockSpec((1, d, tf), lambda f, i, be, nu: (be[i], 0, f))
    return pl.pallas_call(
        _moe_up_kernel,
        grid_spec=pltpu.PrefetchScalarGridSpec(
            num_scalar_prefetch=2,
            grid=(ff // tf, p // tm),
            in_specs=[pl.BlockSpec((tm, d), lambda f, i, be, nu: (i, 0)), wspec, wspec],
            out_specs=pl.BlockSpec((tm, tf), lambda f, i, be, nu: (i, f))),
        out_shape=jax.ShapeDtypeStruct((p, ff), BF16),
        compiler_params=_cparams(2, VMEM_LIMIT),
        name="moe_up",
    )(block_expert, nused, xs, w1, w3)


def _moe_down_kernel(be_ref, nused_ref, h_ref, w2_ref, o_ref):
    @pl.when(pl.program_id(1) < nused_ref[0])
    def _():
        o_ref[...] = jnp.dot(h_ref[...], w2_ref[0].astype(BF16), preferred_element_type=F32)

    @pl.when(pl.program_id(1) >= nused_ref[0])
    def _():
        o_ref[...] = jnp.zeros_like(o_ref)


def _moe_down(block_expert, nused, hs, w2, *, tm, tn):
    p, ff = hs.shape
    d = w2.shape[2]
    return pl.pallas_call(
        _moe_down_kernel,
        grid_spec=pltpu.PrefetchScalarGridSpec(
            num_scalar_prefetch=2,
            grid=(d // tn, p // tm),
            in_specs=[pl.BlockSpec((tm, ff), lambda n, i, be, nu: (i, 0)),
                      pl.BlockSpec((1, ff, tn), lambda n, i, be, nu: (be[i], 0, n))],
            out_specs=pl.BlockSpec((tm, tn), lambda n, i, be, nu: (i, n))),
        out_shape=jax.ShapeDtypeStruct((p, d), F32),
        compiler_params=_cparams(2, VMEM_LIMIT),
        name="moe_down",
    )(block_expert, nused, hs, w2)


def _moe_combine_kernel(p0_ref, p1_ref, xp_ref, xs_ref, g0_ref, g1_ref, nw_ref, yb_hbm, op_ref, os_ref,
                        buf0, buf1, sem, *, tc, nb_p):
    i = pl.program_id(0)
    _gather_rows(yb_hbm, buf0, p0_ref, i * tc, tc, sem)
    _gather_rows(yb_hbm, buf1, p1_ref, i * tc, tc, sem)

    def emit(x_ref, o_ref):
        x = x_ref[...] + g0_ref[:, 0:1] * buf0[...] + g1_ref[:, 0:1] * buf1[...]
        o_ref[...] = _rms(x, nw_ref[...])

    pl.when(i < nb_p)(lambda: emit(xp_ref, op_ref))
    pl.when(i >= nb_p)(lambda: emit(xs_ref, os_ref))


def _moe_combine(pos0, pos1, xp, xs, g0, g1, norm_w, yb, *, tc):
    d = xp.shape[1]
    nb_p, nb_s = xp.shape[0] // tc, xs.shape[0] // tc
    p_idx = lambda i, a, b: (jnp.minimum(i, nb_p - 1), 0)
    s_idx = lambda i, a, b: (jnp.maximum(i - nb_p, 0), 0)
    return pl.pallas_call(
        functools.partial(_moe_combine_kernel, tc=tc, nb_p=nb_p),
        grid_spec=pltpu.PrefetchScalarGridSpec(
            num_scalar_prefetch=2,
            grid=(nb_p + nb_s,),
            in_specs=[pl.BlockSpec((tc, d), p_idx),
                      pl.BlockSpec((tc, d), s_idx),
                      pl.BlockSpec((tc, LANES), lambda i, a, b: (i, 0)),
                      pl.BlockSpec((tc, LANES), lambda i, a, b: (i, 0)),
                      pl.BlockSpec((1, d), lambda i, a, b: (0, 0)),
                      pl.BlockSpec(memory_space=pl.ANY)],
            out_specs=[pl.BlockSpec((tc, d), p_idx), pl.BlockSpec((tc, d), s_idx)],
            scratch_shapes=[pltpu.VMEM((tc, d), F32), pltpu.VMEM((tc, d), F32), pltpu.SemaphoreType.DMA(())]),
        out_shape=[jax.ShapeDtypeStruct(xp.shape, F32), jax.ShapeDtypeStruct(xs.shape, F32)],
        compiler_params=_cparams(1, VMEM_LIMIT),
        name="moe_combine",
    )(pos0, pos1, xp, xs, g0, g1, norm_w, yb)


def _route(logits, bg, be, *, tm):
    t = logits.shape[0]
    pg = jax.nn.softmax(logits[:, :N_GROUPS] + bg, axis=-1)
    g_idx = jnp.argmax(pg, axis=-1).astype(jnp.int32)
    p_sel = jnp.take_along_axis(pg, g_idx[:, None], axis=-1)
    le = (logits[:, N_GROUPS:N_GROUPS + N_EXPERTS] + be).reshape(t, N_GROUPS, EXPERTS_PER_GROUP)
    le = jnp.take_along_axis(le, g_idx[:, None, None], axis=1)[:, 0]
    pe = jax.nn.softmax(le, axis=-1)
    top_v, top_i = lax.top_k(pe, TOP_K)
    gate = p_sel * top_v / jnp.sum(top_v, axis=-1, keepdims=True)
    e_flat = (g_idx[:, None] * EXPERTS_PER_GROUP + top_i.astype(jnp.int32)).reshape(-1)
    a = t * TOP_K
    onehot = (e_flat[:, None] == jnp.arange(N_EXPERTS, dtype=jnp.int32)[None, :]).astype(jnp.int32)
    rank = jnp.sum((jnp.cumsum(onehot, axis=0) - onehot) * onehot, axis=1)
    counts = jnp.sum(onehot, axis=0)
    padded = (counts + tm - 1) // tm * tm
    pad_end = jnp.cumsum(padded)
    dest = (pad_end - padded)[e_flat] + rank
    n_blocks = -(-(a + N_EXPERTS * (tm - 1)) // tm)
    tok = jnp.arange(a, dtype=jnp.int32) // TOP_K
    slot_tok = jnp.zeros((n_blocks * tm,), jnp.int32).at[dest].set(tok)
    block_start = jnp.arange(n_blocks, dtype=jnp.int32) * tm
    block_expert = jnp.minimum(jnp.searchsorted(pad_end, block_start, side='right'), N_EXPERTS - 1)
    nused = (pad_end[-1] // tm).astype(jnp.int32).reshape(1)
    dest = dest.reshape(t, TOP_K)
    return gate, dest[:, 0], dest[:, 1], slot_tok, block_expert.astype(jnp.int32), nused


def _moe(xp, xs, norm2_w, wr, bg, be, w1, w3, w2, final_w, *, tm_route, tm_blk, tf, tn, tc):
    h, logits = _router(xp, xs, norm2_w, wr, tm=tm_route)
    gate, pos0, pos1, slot_tok, block_expert, nused = _route(logits, bg, be, tm=tm_blk)
    xg = _moe_gather(slot_tok, nused, h, tm=tm_blk)
    hs = _moe_up(block_expert, nused, xg, w1, w3, tm=tm_blk, tf=tf)
    yb = _moe_down(block_expert, nused, hs, w2, tm=tm_blk, tn=tn)
    g0 = jnp.broadcast_to(gate[:, 0:1], (h.shape[0], LANES))
    g1 = jnp.broadcast_to(gate[:, 1:2], (h.shape[0], LANES))
    return _moe_combine(pos0, pos1, xp, xs, g0, g1, final_w, yb, tc=tc)


def kernel(x_prompt, x_sample, state_mlstm_c, state_mlstm_n, state_mlstm_m, state_rwkv, state_rwkv_shift, norm1_w, w_in, w_out, ml_b_i, ml_b_f, ml_norm_w, rw_mu, rw_w0, rw_w2, rw_a0, rw_a2, rw_g2, rw_k_k, rw_k_a, rw_r_k, rw_ln_w, rw_ln_b, norm2_w, router_group_w, router_group_b, router_expert_w, router_expert_b, moe_w1, moe_w3, moe_w2, final_norm_w):
    assert w_in.shape[0] == 1, "single-layer trunk"
    bp, seq, d = x_prompt.shape
    bs = x_sample.shape[0]
    tp = bp * seq
    ml_proj = 2 * ML_HEADS * ML_DK + 2 * ML_WIDTH + 2 * ML_HEADS
    rw_proj = 3 * RW_WIDTH + RW_LORA
    rw_col0 = -(-ml_proj // RW_GROUP) * RW_GROUP
    n_groups = RW_WIDTH // RW_GROUP

    w_ml = w_in[0][:, :rw_col0].astype(BF16)
    w_rw = w_in[0][:, ml_proj:].astype(BF16)
    w_o = w_out[0].astype(BF16)
    mu = rw_mu[0]
    rows = [rw_w0[0], rw_a0[0], rw_k_k[0], rw_k_a[0], rw_r_k[0].reshape(-1), rw_ln_w[0], rw_ln_b[0],
            mu[:RW_WIDTH], mu[RW_WIDTH:2 * RW_WIDTH], mu[2 * RW_WIDTH:3 * RW_WIDTH]]
    prm = jnp.concatenate([jnp.stack(rows), jnp.zeros((16 - len(rows), RW_WIDTH), F32)], axis=0)
    mu_l = mu[3 * RW_WIDTH:][None]
    wl = jnp.zeros((RW_LORA, 3, RW_WIDTH), F32)
    wl = wl.at[:RW_DECAY_LORA, 0].set(rw_w2[0]).at[RW_DECAY_LORA:RW_DECAY_LORA + RW_A_LORA, 1].set(rw_a2[0])
    wl = wl.at[RW_DECAY_LORA + RW_A_LORA:, 2].set(rw_g2[0])
    wl = wl.reshape(RW_LORA, 3, n_groups, RW_GROUP).transpose(2, 0, 1, 3).reshape(n_groups, RW_LORA, 3 * RW_GROUP)
    wl = wl.astype(BF16)
    gate_bias = jnp.zeros((1, LANES), F32).at[0, :ML_HEADS].set(ml_b_i[0]).at[0, ML_HEADS:2 * ML_HEADS].set(ml_b_f[0])
    ml_nw = ml_norm_w[0][None]
    wr = jnp.concatenate([router_group_w[0], router_expert_w[0],
                          jnp.zeros((d, LANES - N_GROUPS - N_EXPERTS), F32)], axis=1)

    xp = x_prompt.reshape(tp, d)
    xs = x_sample.reshape(bs, d)
    proj_p = _in_proj(xp, norm1_w, w_ml, w_rw, tm=512, tn=640)
    proj_s = _in_proj(xs, norm1_w, w_ml, w_rw, tm=bs, tn=640)

    y_ml_p, p_c, p_nm = _mlstm_prompt(proj_p, gate_bias, ml_nw, batch=bp, seq=seq)
    y_rw_p, p_s = _rwkv_prompt(proj_p, prm, mu_l, wl, batch=bp, seq=seq, rw_col0=rw_col0, n_groups=n_groups)
    p_sh = lax.slice(proj_p, (seq - 1, rw_col0), (tp, rw_col0 + rw_proj), (seq, 1))

    y_ml_s, s_c, s_n, s_m = _mlstm_sample(proj_s[:, None, :], gate_bias, ml_nw, state_mlstm_c[0], state_mlstm_n[0],
                                          state_mlstm_m[0][:, None, :])
    r, k, v, kk, ka, dec, g = _rwkv_sample_prep(proj_s, state_rwkv_shift[0], prm, mu_l, wl, row0=0, rows=bs,
                                                rw_col0=rw_col0, n_groups=n_groups)
    s_s, y_state = _rwkv_sample_state(state_rwkv[0], r, k, kk, ka, dec, v)
    y_rw_s = _rwkv_sample_post(y_state, r, k, v, g, prm)
    s_sh = proj_s[:, rw_col0:rw_col0 + rw_proj]

    x2_p = _out_proj(y_ml_p, y_rw_p, w_o, xp, tm=512, tn=1024)
    x2_s = _out_proj(y_ml_s.reshape(bs, ML_WIDTH), y_rw_s, w_o, xs, tm=bs, tn=1024)
    y_p, y_s = _moe(x2_p, x2_s, norm2_w, wr, router_group_b[0], router_expert_b[0], moe_w1[0], moe_w3[0],
                    moe_w2[0], final_norm_w[None], tm_route=128, tm_blk=256, tf=256, tn=1024, tc=128)

    lead = lambda a: a[None]
    return (y_p.reshape(bp, seq, d), y_s.reshape(bs, 1, d),
            lead(p_c), lead(p_nm[:, :, 0]), lead(p_nm[:, :, 1, 0]), lead(p_s), lead(p_sh),
            lead(s_c), lead(s_n), lead(s_m[:, 0]), lead(s_s), lead(s_sh))
```

```python
import functools

import jax
import jax.numpy as jnp
from jax import lax
from jax.experimental import pallas as pl
from jax.experimental.pallas import tpu as pltpu

F32 = jnp.float32
BF16 = jnp.bfloat16

D_MODEL = 4096
ML_HEADS = 4
ML_DK = 256
ML_DV = 512
ML_WIDTH = ML_HEADS * ML_DV
GATE_SOFTCAP = 15.0
RW_HEAD = 64
RW_WIDTH = D_MODEL - ML_WIDTH
RW_HEADS = RW_WIDTH // RW_HEAD
RW_DECAY_LORA = 96
RW_A_LORA = 96
RW_GATE_LORA = 64
RW_GN_EPS = 64e-5
N_GROUPS = 4
EXPERTS_PER_GROUP = 8
N_EXPERTS = N_GROUPS * EXPERTS_PER_GROUP
TOP_K = 2
EPS = 1e-6

LANES = 128
RW_GROUP = 256
RW_GHEADS = RW_GROUP // RW_HEAD
RW_LORA = RW_DECAY_LORA + RW_A_LORA + RW_GATE_LORA
RW_CHUNK = 64
ML_CHUNK = 64
_XLU_STAGGER = 3
VMEM_LIMIT = 56 * 1024 * 1024


def _cparams(n_axes, vmem=None):
    return pltpu.CompilerParams(dimension_semantics=("arbitrary",) * n_axes, vmem_limit_bytes=vmem)


def _sigmoid(x):
    return 1.0 / (1.0 + jnp.exp(-x))


def _softplus(x):
    return jnp.maximum(x, 0.0) + jnp.log(1.0 + jnp.exp(-jnp.abs(x)))


def _soft_cap(x):
    return GATE_SOFTCAP * jnp.tanh(x / GATE_SOFTCAP)


def _dot(a, b):
    return jnp.dot(a.astype(BF16), b.astype(BF16), preferred_element_type=F32)


def _dot_nt(a, b):
    return lax.dot_general(a.astype(BF16), b.astype(BF16), (((1,), (1,)), ((), ())),
                           preferred_element_type=F32)


def _split2(x):
    hi = x.astype(BF16)
    lo = (x - hi.astype(F32)).astype(BF16)
    return hi, lo


def _split3(x):
    hi = x.astype(BF16)
    r = x - hi.astype(F32)
    mid = r.astype(BF16)
    lo = (r - mid.astype(F32)).astype(BF16)
    return hi, mid, lo


def _iota(shape, axis):
    return lax.broadcasted_iota(jnp.int32, shape, axis)


def _pack_pairs(x, half):
    bits = lax.bitcast_convert_type(x.astype(BF16).astype(F32), jnp.uint32)
    parts = []
    for g in range(x.shape[1] // (2 * half)):
        lo = bits[:, 2 * half * g:2 * half * g + half]
        hi = bits[:, 2 * half * g + half:2 * half * (g + 1)]
        parts.append((lo >> 16) | hi)
    return parts[0] if len(parts) == 1 else jnp.concatenate(parts, axis=1)


def _unpack_pairs(p, half):
    parts = []
    for g in range(p.shape[1] // half):
        w = p[:, half * g:half * (g + 1)]
        parts.append(lax.bitcast_convert_type(w << 16, F32))
        parts.append(lax.bitcast_convert_type(w & jnp.uint32(0xFFFF0000), F32))
    return jnp.concatenate(parts, axis=1)


def _head_block_ones(n):
    return jnp.where((_iota((n, n), 0) >> 6) == (_iota((n, n), 1) >> 6), 1.0, 0.0).astype(BF16)


def _seg_sum(x, bd):
    hi, lo = _split2(x)
    return (jnp.dot(hi, bd, preferred_element_type=F32) + jnp.dot(lo, bd, preferred_element_type=F32))


_P_W0, _P_A0, _P_KK, _P_KA, _P_RK, _P_LNW, _P_LNB, _P_MUR, _P_MUK, _P_MUV = range(10)


def _rwkv_prep(p_r, p_k, p_v, p_l, prev_r, prev_k, prev_v, prev_l, prm, mu_l, wl, bd):
    row = lambda i: prm[i:i + 1, :]
    xr = p_r + (prev_r - p_r) * row(_P_MUR)
    xk = p_k + (prev_k - p_k) * row(_P_MUK)
    xv = p_v + (prev_v - p_v) * row(_P_MUV)
    xl = p_l + (prev_l - p_l) * mu_l
    lane = _iota(xl.shape, 1)
    z = jnp.where(lane < RW_DECAY_LORA, jnp.tanh(xl),
                  jnp.where(lane < RW_DECAY_LORA + RW_A_LORA, xl, _sigmoid(xl)))
    lo = _dot(z, wl)
    lw, la, lg = lo[:, :RW_GROUP], lo[:, RW_GROUP:2 * RW_GROUP], lo[:, 2 * RW_GROUP:]
    w = -_softplus(-(row(_P_W0) + lw)) - 0.5
    logd = -jnp.exp(w)
    a = _sigmoid(row(_P_A0) + la)
    kk = xk * row(_P_KK)
    kk = kk / jnp.maximum(jnp.sqrt(_seg_sum(kk * kk, bd)), 1e-12)
    k = xk * (1.0 + (a - 1.0) * row(_P_KA))
    return xr, k, xv, kk, a, logd, lg


def _rwkv_post(y, r, k, v, g, prm, bd):
    row = lambda i: prm[i:i + 1, :]
    mean = _seg_sum(y, bd) * (1.0 / RW_HEAD)
    yc = y - mean
    var = _seg_sum(yc * yc, bd) * (1.0 / RW_HEAD)
    yn = yc * lax.rsqrt(var + RW_GN_EPS) * row(_P_LNW) + row(_P_LNB)
    bonus = _seg_sum(r * k * row(_P_RK), bd) * v
    return (yn + bonus) * g


def _rwkv_chunk(r, k, v, kk, a, logd, G, bd):
    L = r.shape[0]
    tril = jnp.where(_iota((L, L), 1) <= _iota((L, L), 0), 1.0, 0.0).astype(BF16)
    d_hi, d_lo = _split2(logd)
    clog = (jnp.dot(tril, d_hi, preferred_element_type=F32)
            + jnp.dot(tril, d_lo, preferred_element_type=F32))
    clog_l = clog[L - 1:L, :]
    n_in = jnp.exp(-clog)
    to_end = jnp.exp(clog_l - clog)
    ka = kk * a
    at = -kk * jnp.exp(clog - logd)
    rt = r * jnp.exp(clog)
    lane_head = _iota((1, RW_GROUP), 1) >> 6
    masks = [lane_head == h for h in range(RW_GHEADS)]
    zero = jnp.zeros_like(at)
    bf = lambda x: x.astype(BF16)
    lhs = jnp.concatenate([jnp.where(m, x, zero) for m in masks for x in (at, rt)], axis=0)
    xx = _dot_nt(lhs, jnp.concatenate([ka * n_in, k * n_in], axis=0))
    t2 = _iota((L, 2 * L), 0)
    c2 = _iota((L, 2 * L), 1)
    s2 = c2 & (L - 1)
    right = c2 >= L
    eye_pad = jnp.where(c2 == t2 + L, 1.0, 0.0)
    zeros_v = jnp.zeros((L, RW_GROUP), BF16)
    v_b = bf(v)
    vz = jnp.concatenate([zeros_v, v_b], axis=0)
    yield
    zs, rbk = [], []
    makv = zero
    for h, m in enumerate(masks):
        o = 2 * L * h
        nk = jnp.where(s2 < t2, xx[o:o + L], 0.0)
        rbk.append(bf(jnp.where(s2 <= t2, xx[o + L:o + 2 * L], 0.0)))
        makv = makv + jnp.where(m, _dot(jnp.where(right, nk, 0.0), vz), 0.0)
        zs.append(jnp.where(right, eye_pad, nk))
    for _ in range((L - 1).bit_length()):
        yield
        zs = [_dot(z[:, :L], z) + jnp.where(right, z, 0.0) for z in zs]
    yield
    ws = _dot_nt(jnp.concatenate([at, rt], axis=0), G)
    xz = jnp.concatenate([zeros_v, bf(ws[:L] + makv)], axis=0)
    yield
    w_all = zero
    for z_h, m in zip(zs, masks):
        w_all = w_all + jnp.where(m, _dot(z_h, xz), 0.0)
    wv = jnp.concatenate([bf(w_all), v_b], axis=0)
    yield
    y = ws[L:]
    for rbk_h, m in zip(rbk, masks):
        y = y + jnp.where(m, jnp.dot(rbk_h, wv, preferred_element_type=F32), 0.0)
    upd = _dot(wv.astype(F32).T, jnp.concatenate([ka * to_end, k * to_end], axis=0))
    g_new = G * jnp.exp(clog_l) + jnp.where(bd > 0, upd, 0.0)
    return y, g_new


def _interleave(gens):
    results = [None] * len(gens)
    live = list(range(len(gens)))
    while live:
        for i in list(live):
            try:
                next(gens[i])
            except StopIteration as stop:
                results[i] = stop.value
                live.remove(i)
    return results


def _shift_rows(x, first_row):
    rolled = pltpu.roll(x, 1, axis=0)
    return jnp.where(_iota(x.shape, 0) == 0, first_row, rolled)


def _rwkv_prompt_kernel(*refs, n_chunks, batch):
    p_refs = refs[:4 * batch]
    prm_ref, mul_ref, wl_ref, y_ref, s_ref, g_scr, carry_scr = refs[4 * batch:]
    c = pl.program_id(1)

    @pl.when(c == 0)
    def _():
        g_scr[...] = jnp.zeros_like(g_scr)
        carry_scr[...] = jnp.zeros_like(carry_scr)

    bd = _head_block_ones(RW_GROUP)
    prm = prm_ref[...]

    def sequence(b):
        ps = [ref[...] for ref in p_refs[4 * b:4 * b + 4]]
        L = ps[0].shape[0]
        prevs = [_shift_rows(x, carry_scr[b, i:i + 1, :]) for i, x in enumerate(ps)]
        r, k, v, kk, a, logd, g = _rwkv_prep(*ps, *prevs, prm, mul_ref[...], wl_ref[0], bd)
        y, g_new = yield from _rwkv_chunk(r, k, v, kk, a, logd, g_scr[b], bd)
        yield
        out = _rwkv_post(y, r, k, v, g, prm, bd).astype(y_ref.dtype)
        return out, g_new, [x[L - 1:L, :] for x in ps]

    results = _interleave([sequence(b) for b in range(batch)])
    finals = []
    for b, (out, g_new, last_rows) in enumerate(results):
        y_ref[b] = out
        g_scr[b] = g_new
        for i, x in enumerate(last_rows):
            carry_scr[b, i:i + 1, :] = x
        finals.append(g_new)

    @pl.when(c == n_chunks - 1)
    def _():
        for b in range(batch):
            for h in range(RW_GHEADS):
                s_ref[b, h] = finals[b][RW_HEAD * h:RW_HEAD * (h + 1), RW_HEAD * h:RW_HEAD * (h + 1)]


def _rwkv_prompt(proj, prm, mu_l, wl, *, batch, seq, rw_col0, n_groups):
    L = RW_CHUNK
    nc = seq // L
    cb = rw_col0 // RW_GROUP
    width = n_groups * RW_GROUP
    pspec = lambda b, off: pl.BlockSpec((L, RW_GROUP), lambda g, c, b=b, off=off: (b * nc + c, cb + off + g))
    lspec = lambda b: pl.BlockSpec((L, RW_GROUP), lambda g, c, b=b: (b * nc + c, cb + 3 * n_groups))
    p_specs = []
    for b in range(batch):
        p_specs += [pspec(b, 0), pspec(b, n_groups), pspec(b, 2 * n_groups), lspec(b)]
    y, s = pl.pallas_call(
        functools.partial(_rwkv_prompt_kernel, n_chunks=nc, batch=batch),
        grid=(n_groups, nc),
        in_specs=p_specs + [pl.BlockSpec((16, RW_GROUP), lambda g, c: (0, g)),
                            pl.BlockSpec((1, RW_LORA), lambda g, c: (0, 0)),
                            pl.BlockSpec((1, RW_LORA, 3 * RW_GROUP), lambda g, c: (g, 0, 0))],
        out_specs=[pl.BlockSpec((batch, L, RW_GROUP), lambda g, c: (0, c, g)),
                   pl.BlockSpec((batch, RW_GHEADS, RW_HEAD, RW_HEAD), lambda g, c: (0, g, 0, 0))],
        out_shape=[jax.ShapeDtypeStruct((batch, seq, width), BF16),
                   jax.ShapeDtypeStruct((batch, n_groups * RW_GHEADS, RW_HEAD, RW_HEAD), F32)],
        scratch_shapes=[pltpu.VMEM((batch, RW_GROUP, RW_GROUP), F32), pltpu.VMEM((batch, 8, RW_GROUP), F32)],
        compiler_params=_cparams(2),
        name="rwkv_prompt",
    )(*([proj] * (4 * batch)), prm, mu_l, wl)
    return y.reshape(batch * seq, width), s


def _rwkv_sample_prep_kernel(pr_ref, pk_ref, pv_ref, pl_ref, sr_ref, sk_ref, sv_ref, sl_ref, prm_ref, mul_ref,
                             wl_ref, r_ref, k_ref, v_ref, kk_ref, ka_ref, d_ref, g_ref):
    bd = _head_block_ones(RW_GROUP)
    r, k, v, kk, a, logd, g = _rwkv_prep(pr_ref[...], pk_ref[...], pv_ref[...], pl_ref[...], sr_ref[...],
                                         sk_ref[...], sv_ref[...], sl_ref[...], prm_ref[...], mul_ref[...],
                                         wl_ref[0], bd)
    r_ref[...] = r
    k_ref[...] = k
    v_ref[...] = v
    kk_ref[...] = kk
    ka_ref[...] = kk * a
    d_ref[...] = jnp.exp(logd)
    g_ref[...] = g


def _rwkv_sample_prep(proj, shift0, prm, mu_l, wl, *, row0, rows, rw_col0, n_groups):
    rb = row0 // rows
    cb = rw_col0 // RW_GROUP
    pspec = lambda off: pl.BlockSpec((rows, RW_GROUP), lambda g, off=off: (rb, cb + off + g))
    sspec = lambda off: pl.BlockSpec((rows, RW_GROUP), lambda g, off=off: (0, off + g))
    ospec = pl.BlockSpec((rows, RW_GROUP), lambda g: (0, g))
    width = n_groups * RW_GROUP
    return pl.pallas_call(
        _rwkv_sample_prep_kernel,
        grid=(n_groups,),
        in_specs=[pspec(0), pspec(n_groups), pspec(2 * n_groups),
                  pl.BlockSpec((rows, RW_GROUP), lambda g: (rb, cb + 3 * n_groups)),
                  sspec(0), sspec(n_groups), sspec(2 * n_groups),
                  pl.BlockSpec((rows, RW_GROUP), lambda g: (0, 3 * n_groups)),
                  pl.BlockSpec((16, RW_GROUP), lambda g: (0, g)),
                  pl.BlockSpec((1, RW_LORA), lambda g: (0, 0)),
                  pl.BlockSpec((1, RW_LORA, 3 * RW_GROUP), lambda g: (g, 0, 0))],
        out_specs=[ospec] * 7,
        out_shape=[jax.ShapeDtypeStruct((rows, width), F32)] * 7,
        compiler_params=_cparams(1),
        name="rwkv_sample_prep",
    )(proj, proj, proj, proj, shift0, shift0, shift0, shift0, prm, mu_l, wl)


def _rwkv_sample_state_kernel(s_ref, r_ref, k_ref, kk_ref, ka_ref, d_ref, vt_ref, so_ref, yt_ref, *, n_heads):
    row = lambda ref, h: ref[0, h:h + 1, :]

    def head_group(g):
        heads = range(g * RW_GHEADS, (g + 1) * RW_GHEADS)
        for _ in range(g):
            yield
        sa = [jnp.sum(s_ref[0, h] * (-row(kk_ref, h)), axis=1, keepdims=True) for h in heads]
        for _ in range(_XLU_STAGGER):
            yield
        y_cols = []
        for h, sa_h in zip(heads, sa):
            v_col = vt_ref[0, :, h:h + 1]
            s_new = s_ref[0, h] * row(d_ref, h) + sa_h * row(ka_ref, h) + v_col * row(k_ref, h)
            so_ref[0, h] = s_new
            y_cols.append(jnp.sum(s_new * row(r_ref, h), axis=1, keepdims=True))
        for _ in range(_XLU_STAGGER):
            yield
        for h, y_col in zip(heads, y_cols):
            yt_ref[0, :, h:h + 1] = y_col

    _interleave([head_group(g) for g in range(n_heads // RW_GHEADS)])


def _rwkv_sample_state(state, r, k, kk, ka, d, v):
    b, nh = state.shape[0], state.shape[1]
    heads = lambda x: x.reshape(b, nh, RW_HEAD)
    vt = jnp.swapaxes(heads(v), 1, 2)
    hspec = pl.BlockSpec((1, nh, RW_HEAD), lambda i: (i, 0, 0))
    tspec = pl.BlockSpec((1, RW_HEAD, nh), lambda i: (i, 0, 0))
    sspec = pl.BlockSpec((1, nh, RW_HEAD, RW_HEAD), lambda i: (i, 0, 0, 0))
    s_new, yt = pl.pallas_call(
        functools.partial(_rwkv_sample_state_kernel, n_heads=nh),
        grid=(b,),
        in_specs=[sspec, hspec, hspec, hspec, hspec, hspec, tspec],
        out_specs=[sspec, tspec],
        out_shape=[jax.ShapeDtypeStruct(state.shape, F32), jax.ShapeDtypeStruct((b, RW_HEAD, nh), F32)],
        compiler_params=_cparams(1),
        name="rwkv_sample_state",
    )(state, heads(r), heads(k), heads(kk), heads(ka), heads(d), vt)
    return s_new, jnp.swapaxes(yt, 1, 2).reshape(b, nh * RW_HEAD)


def _rwkv_sample_post_kernel(y_ref, r_ref, k_ref, v_ref, g_ref, prm_ref, o_ref):
    bd = _head_block_ones(RW_GROUP)
    o_ref[...] = _rwkv_post(y_ref[...], r_ref[...], k_ref[...], v_ref[...], g_ref[...], prm_ref[...],
                            bd).astype(o_ref.dtype)


def _rwkv_sample_post(y, r, k, v, g, prm):
    rows, width = y.shape
    spec = pl.BlockSpec((rows, RW_GROUP), lambda i: (0, i))
    return pl.pallas_call(
        _rwkv_sample_post_kernel,
        grid=(width // RW_GROUP,),
        in_specs=[spec] * 5 + [pl.BlockSpec((16, RW_GROUP), lambda i: (0, i))],
        out_specs=spec,
        out_shape=jax.ShapeDtypeStruct((rows, width), BF16),
        compiler_params=_cparams(1),
        name="rwkv_sample_post",
    )(y, r, k, v, g, prm)


def _mlstm_gates(gates, bias, h):
    capped = _soft_cap(gates + bias)
    lane = _iota(gates.shape, 1)
    i_col = jnp.sum(jnp.where(lane == h, capped, 0.0), axis=1, keepdims=True)
    f_col = jnp.sum(jnp.where(lane == h + ML_HEADS, -_softplus(-capped), 0.0), axis=1, keepdims=True)
    return i_col, f_col


def _mlstm_out(hh, o, norm_w):
    hn = hh * lax.rsqrt(jnp.mean(hh * hh, axis=1, keepdims=True) + EPS) * norm_w
    return hn * _sigmoid(o)


def _mlstm_prompt_kernel(*refs, n_chunks, batch):
    p_refs = refs[:5 * batch]
    gb_ref, nw_ref, y_ref, c_ref, nm_ref, c_scr, n_scr, m_scr = refs[5 * batch:]
    h = pl.program_id(0)
    c = pl.program_id(1)

    @pl.when(c == 0)
    def _():
        c_scr[...] = jnp.zeros_like(c_scr)
        n_scr[...] = jnp.zeros_like(n_scr)
        m_scr[...] = jnp.zeros_like(m_scr)

    def sequence(b):
        q_ref, k_ref, v_ref, o_ref, gt_ref = p_refs[5 * b:5 * b + 5]
        q = q_ref[...] * (ML_DK ** -0.5)
        k = k_ref[...]
        v = v_ref[...]
        L = q.shape[0]
        i_col, f_col = _mlstm_gates(gt_ref[...], gb_ref[...], h)
        t_i = _iota((L, L), 0)
        s_i = _iota((L, L), 1)
        causal = s_i <= t_i
        to_row = lambda col: jnp.sum(jnp.where(t_i == s_i, col, 0.0), axis=0, keepdims=True)
        i_row = to_row(i_col)
        f_row = to_row(f_col)
        yield
        b_col = jnp.sum(jnp.where(causal, f_row, 0.0), axis=1, keepdims=True)
        b_row = jnp.sum(jnp.where(t_i <= s_i, f_col, 0.0), axis=0, keepdims=True)
        m0 = m_scr[b]
        inter = b_col + m0
        dmat = jnp.where(causal, b_col - b_row + i_row, -1e30)
        yield
        m_t = jnp.maximum(inter, jnp.max(dmat, axis=1, keepdims=True))
        w_int = jnp.exp(inter - m_t)
        s = _dot_nt(q, k) * jnp.exp(dmat - m_t)
        c0 = c_scr[b]
        n0 = n_scr[b]
        yield
        num = w_int * _dot(q, c0) + _dot(s, v)
        den = w_int * jnp.sum(q * n0, axis=1, keepdims=True) + jnp.sum(s, axis=1, keepdims=True)
        hh = num / jnp.maximum(jnp.abs(den), jnp.exp(-m_t))
        m_new = m_t[L - 1:L, :]
        b_l = b_col[L - 1:L, :]
        a_end = jnp.exp(b_l - b_col + i_col - m_new)
        dec = jnp.exp(b_l + m0 - m_new)
        ka = k * a_end
        yield
        c_new = dec * c0 + _dot(ka.T, v)
        n_new = dec * n0 + jnp.sum(ka, axis=0, keepdims=True)
        out = _mlstm_out(hh, o_ref[...], nw_ref[...]).astype(y_ref.dtype)
        return out, c_new, n_new, m_new

    results = _interleave([sequence(b) for b in range(batch)])
    for b, (out, c_new, n_new, m_new) in enumerate(results):
        y_ref[b] = out
        c_scr[b] = c_new
        n_scr[b] = n_new
        m_scr[b] = m_new

    @pl.when(c == n_chunks - 1)
    def _():
        for b, (_, c_new, n_new, m_new) in enumerate(results):
            c_ref[b, 0] = c_new
            nm_ref[b, 0] = jnp.concatenate([n_new, jnp.broadcast_to(m_new, (7, ML_DK))], axis=0)


def _mlstm_prompt(proj, gate_bias, norm_w, *, batch, seq):
    L = ML_CHUNK
    nc = seq // L
    nh = ML_HEADS
    kq = nh * ML_DK
    p_specs = []
    for b in range(batch):
        rows = lambda c, b=b: b * nc + c
        p_specs += [pl.BlockSpec((L, ML_DK), lambda h, c, rows=rows: (rows(c), h)),
                    pl.BlockSpec((L, ML_DK), lambda h, c, rows=rows: (rows(c), nh + h)),
                    pl.BlockSpec((L, ML_DV), lambda h, c, rows=rows: (rows(c), 2 * kq // ML_DV + h)),
                    pl.BlockSpec((L, ML_DV), lambda h, c, rows=rows: (rows(c), 2 * kq // ML_DV + nh + h)),
                    pl.BlockSpec((L, LANES), lambda h, c, rows=rows: (rows(c), (2 * kq + 2 * ML_WIDTH) // LANES))]
    y, c_fin, nm = pl.pallas_call(
        functools.partial(_mlstm_prompt_kernel, n_chunks=nc, batch=batch),
        grid=(nh, nc),
        in_specs=p_specs + [pl.BlockSpec((1, LANES), lambda h, c: (0, 0)),
                            pl.BlockSpec((1, ML_DV), lambda h, c: (0, h))],
        out_specs=[pl.BlockSpec((batch, L, ML_DV), lambda h, c: (0, c, h)),
                   pl.BlockSpec((batch, 1, ML_DK, ML_DV), lambda h, c: (0, h, 0, 0)),
                   pl.BlockSpec((batch, 1, 8, ML_DK), lambda h, c: (0, h, 0, 0))],
        out_shape=[jax.ShapeDtypeStruct((batch, seq, ML_WIDTH), BF16),
                   jax.ShapeDtypeStruct((batch, nh, ML_DK, ML_DV), F32),
                   jax.ShapeDtypeStruct((batch, nh, 8, ML_DK), F32)],
        scratch_shapes=[pltpu.VMEM((batch, ML_DK, ML_DV), F32), pltpu.VMEM((batch, 1, ML_DK), F32),
                        pltpu.VMEM((batch, 1, 1), F32)],
        compiler_params=_cparams(2),
        name="mlstm_prompt",
    )(*([proj] * (5 * batch)), gate_bias, norm_w)
    return y.reshape(batch * seq, ML_WIDTH), c_fin, nm


def _mlstm_sample_kernel(q_ref, k_ref, v_ref, o_ref, gt_ref, gb_ref, nw_ref, c_ref, n_ref, m_ref,
                         y_ref, co_ref, no_ref, mo_ref):
    gates = gt_ref[0]
    eye = jnp.where(_iota((ML_DK, ML_DK), 0) == _iota((ML_DK, ML_DK), 1), 1.0, 0.0).astype(BF16)
    for h in range(ML_HEADS):
        q = q_ref[0][:, h * ML_DK:(h + 1) * ML_DK] * (ML_DK ** -0.5)
        k = k_ref[0][:, h * ML_DK:(h + 1) * ML_DK]
        v = v_ref[0][:, h * ML_DV:(h + 1) * ML_DV]
        i_pre, logf = _mlstm_gates(gates, gb_ref[...], h)
        c0 = c_ref[0, h]
        n0 = n_ref[0, h:h + 1, :]
        m0 = m_ref[0][:, h:h + 1]
        inter = logf + m0
        m_t = jnp.maximum(inter, i_pre)
        w_int = jnp.exp(inter - m_t)
        a_new = jnp.exp(i_pre - m_t)
        s = jnp.sum(q * k, axis=1, keepdims=True) * a_new
        qc = _dot(jnp.broadcast_to(q, (8, ML_DK)), c0)[0:1, :]
        num = w_int * qc + s * v
        den = w_int * jnp.sum(q * n0, axis=1, keepdims=True) + s
        hh = num / jnp.maximum(jnp.abs(den), jnp.exp(-m_t))
        k_hi, k_lo = _split2(jnp.broadcast_to(k, (8, ML_DK)))
        nt = (((1,), (1,)), ((), ()))
        k_col = (lax.dot_general(eye, k_hi, nt, preferred_element_type=F32)
                 + lax.dot_general(eye, k_lo, nt, preferred_element_type=F32))[:, 0:1]
        co_ref[0, h] = w_int * c0 + k_col * (a_new * v)
        no_ref[0, h:h + 1, :] = w_int * n0 + a_new * k
        mo_ref[0, :, h:h + 1] = m_t
        y_ref[0, :, h * ML_DV:(h + 1) * ML_DV] = _mlstm_out(
            hh, o_ref[0][:, h * ML_DV:(h + 1) * ML_DV], nw_ref[:, h * ML_DV:(h + 1) * ML_DV]).astype(y_ref.dtype)


def _mlstm_sample(proj3, gate_bias, norm_w, c0, n0, m0):
    b = proj3.shape[0]
    nh = ML_HEADS
    kq = nh * ML_DK
    p3 = lambda w, blk: pl.BlockSpec((1, 1, w), lambda i, blk=blk: (i, 0, blk))
    cspec = pl.BlockSpec((1, nh, ML_DK, ML_DV), lambda i: (i, 0, 0, 0))
    nspec = pl.BlockSpec((1, nh, ML_DK), lambda i: (i, 0, 0))
    mspec = pl.BlockSpec((1, 1, nh), lambda i: (i, 0, 0))
    return pl.pallas_call(
        _mlstm_sample_kernel,
        grid=(b,),
        in_specs=[p3(kq, 0), p3(kq, 1), p3(ML_WIDTH, 2 * kq // ML_WIDTH), p3(ML_WIDTH, 2 * kq // ML_WIDTH + 1),
                  p3(LANES, (2 * kq + 2 * ML_WIDTH) // LANES),
                  pl.BlockSpec((1, LANES), lambda i: (0, 0)),
                  pl.BlockSpec((1, ML_WIDTH), lambda i: (0, 0)),
                  cspec, nspec, mspec],
        out_specs=[pl.BlockSpec((1, 1, ML_WIDTH), lambda i: (i, 0, 0)), cspec, nspec, mspec],
        out_shape=[jax.ShapeDtypeStruct((b, 1, ML_WIDTH), BF16),
                   jax.ShapeDtypeStruct(c0.shape, F32), jax.ShapeDtypeStruct(n0.shape, F32),
                   jax.ShapeDtypeStruct(m0.shape, F32)],
        compiler_params=_cparams(1),
        name="mlstm_sample",
    )(proj3, proj3, proj3, proj3, proj3, gate_bias, norm_w, c0, n0, m0)


def _rms(x, w):
    return x * lax.rsqrt(jnp.mean(x * x, axis=1, keepdims=True) + EPS) * w


def _pack_w_in_kernel(a_ref, b_ref, o_ref, *, n_a, shift):
    j = pl.program_id(0)

    @pl.when(j < n_a)
    def _():
        o_ref[...] = a_ref[...].astype(BF16)

    @pl.when(j >= n_a)
    def _():
        tn = o_ref.shape[1]
        both = jnp.concatenate([a_ref[...], b_ref[...]], axis=1)
        o_ref[...] = both[:, shift:shift + tn].astype(BF16)


def _pack_w_in(w, *, split, width, tn):
    d = w.shape[0]
    n_a = width // tn
    base = split // tn
    shift = split - base * tn
    assert shift < LANES and tn % LANES == 0 and width % tn == 0
    sub = tn // LANES
    return pl.pallas_call(
        functools.partial(_pack_w_in_kernel, n_a=n_a, shift=shift),
        grid=(2 * n_a,),
        in_specs=[pl.BlockSpec((d, tn), lambda j: (0, jnp.where(j < n_a, j, base + j - n_a))),
                  pl.BlockSpec((d, LANES), lambda j: (0, jnp.where(j < n_a, 0, (base + j - n_a + 1) * sub)))],
        out_specs=pl.BlockSpec((d, tn), lambda j: (0, j)),
        out_shape=jax.ShapeDtypeStruct((d, 2 * width), BF16),
        compiler_params=_cparams(1, VMEM_LIMIT),
        name="pack_w_in",
    )(w, w)


def _in_proj_kernel(x_ref, nw_ref, w_ref, o_ref, h_scr):
    @pl.when(pl.program_id(1) == 0)
    def _():
        h_scr[...] = _rms(x_ref[...], nw_ref[...]).astype(BF16)

    o_ref[...] = jnp.dot(h_scr[...], w_ref[...], preferred_element_type=F32)


def _in_proj(x, norm_w, w, *, tm, tn):
    t, d = x.shape
    n = w.shape[1]
    return pl.pallas_call(
        _in_proj_kernel,
        grid=(t // tm, n // tn),
        in_specs=[pl.BlockSpec((tm, d), lambda i, j: (i, 0), pipeline_mode=pl.Buffered(1)),
                  pl.BlockSpec((1, d), lambda i, j: (0, 0)),
                  pl.BlockSpec((d, tn), lambda i, j: (0, j))],
        out_specs=pl.BlockSpec((tm, tn), lambda i, j: (i, j)),
        out_shape=jax.ShapeDtypeStruct((t, n), F32),
        scratch_shapes=[pltpu.VMEM((tm, d), BF16)],
        compiler_params=_cparams(2, VMEM_LIMIT),
        name="in_proj",
    )(x, norm_w, w)


def _out_proj_kernel(ya_ref, yb_ref, wa_ref, wb_ref, x_ref, o_ref):
    o_ref[...] = (x_ref[...] + jnp.dot(ya_ref[...], wa_ref[...], preferred_element_type=F32)
                  + jnp.dot(yb_ref[...], wb_ref[...], preferred_element_type=F32))


def _out_proj(ya, yb, w, x, *, tm, tn):
    t, kh = ya.shape
    n = w.shape[1]
    return pl.pallas_call(
        _out_proj_kernel,
        grid=(t // tm, n // tn),
        in_specs=[pl.BlockSpec((tm, kh), lambda i, j: (i, 0)),
                  pl.BlockSpec((tm, kh), lambda i, j: (i, 0)),
                  pl.BlockSpec((kh, tn), lambda i, j: (0, j)),
                  pl.BlockSpec((kh, tn), lambda i, j: (1, j)),
                  pl.BlockSpec((tm, tn), lambda i, j: (i, j))],
        out_specs=pl.BlockSpec((tm, tn), lambda i, j: (i, j)),
        out_shape=jax.ShapeDtypeStruct((t, n), F32),
        compiler_params=_cparams(2, VMEM_LIMIT),
        name="out_proj",
    )(ya, yb, w, w, x)


def _router_kernel(xp_ref, xs_ref, nw_ref, wr_ref, h_ref, lg_ref, *, nb_p):
    def emit(x_ref):
        h = _rms(x_ref[...], nw_ref[...])
        h_ref[...] = _pack_pairs(h, h.shape[1] // 2)
        lg_ref[...] = jnp.dot(h, wr_ref[...], preferred_element_type=F32, precision=lax.Precision.HIGHEST)

    pl.when(pl.program_id(0) < nb_p)(lambda: emit(xp_ref))
    pl.when(pl.program_id(0) >= nb_p)(lambda: emit(xs_ref))


def _router(xp, xs, norm_w, wr, *, tm):
    d = xp.shape[1]
    nb_p, nb_s = xp.shape[0] // tm, xs.shape[0] // tm
    t = xp.shape[0] + xs.shape[0]
    return pl.pallas_call(
        functools.partial(_router_kernel, nb_p=nb_p),
        grid=(nb_p + nb_s,),
        in_specs=[pl.BlockSpec((tm, d), lambda i: (jnp.minimum(i, nb_p - 1), 0)),
                  pl.BlockSpec((tm, d), lambda i: (jnp.maximum(i - nb_p, 0), 0)),
                  pl.BlockSpec((1, d), lambda i: (0, 0)),
                  pl.BlockSpec((d, LANES), lambda i: (0, 0))],
        out_specs=[pl.BlockSpec((tm, d // 2), lambda i: (i, 0)), pl.BlockSpec((tm, LANES), lambda i: (i, 0))],
        out_shape=[jax.ShapeDtypeStruct((t, d // 2), jnp.uint32), jax.ShapeDtypeStruct((t, LANES), F32)],
        compiler_params=_cparams(1, VMEM_LIMIT),
        name="router",
    )(xp, xs, norm_w, wr)


def _gather_rows(src_hbm, dst_vmem, idx_ref, base, n_rows, sem):
    def copy(r):
        return pltpu.make_async_copy(src_hbm.at[pl.ds(idx_ref[base + r], 1), :], dst_vmem.at[pl.ds(r, 1), :], sem)

    def start(r, carry):
        copy(r).start()
        return carry

    def wait(r, carry):
        copy(r).wait()
        return carry

    lax.fori_loop(0, n_rows, start, 0, unroll=8)
    lax.fori_loop(0, n_rows, wait, 0, unroll=8)


def _moe_gather_kernel(tok_ref, nused_ref, h_hbm, o_ref, buf, sem, *, tm):
    i = pl.program_id(0)

    @pl.when(i < nused_ref[0])
    def _():
        _gather_rows(h_hbm, buf, tok_ref, i * tm, tm, sem)
        o_ref[...] = _unpack_pairs(buf[...], buf.shape[1]).astype(o_ref.dtype)

    @pl.when(i >= nused_ref[0])
    def _():
        o_ref[...] = jnp.zeros_like(o_ref)


def _moe_gather(slot_tok, nused, h, *, tm):
    p = slot_tok.shape[0]
    half = h.shape[1]
    return pl.pallas_call(
        functools.partial(_moe_gather_kernel, tm=tm),
        grid_spec=pltpu.PrefetchScalarGridSpec(
            num_scalar_prefetch=2,
            grid=(p // tm,),
            in_specs=[pl.BlockSpec(memory_space=pl.ANY)],
            out_specs=pl.BlockSpec((tm, 2 * half), lambda i, tok, nu: (i, 0)),
            scratch_shapes=[pltpu.VMEM((tm, half), jnp.uint32), pltpu.SemaphoreType.DMA(())]),
        out_shape=jax.ShapeDtypeStruct((p, 2 * half), BF16),
        compiler_params=_cparams(1, VMEM_LIMIT),
        name="moe_gather",
    )(slot_tok, nused, h)


def _expert_row_loop(n_blocks, in_copy, out_copy, compute):
    in_copy(0, 0).start()

    def body(r, carry):
        slot = r & 1
        pl.when(r + 1 < n_blocks)(lambda: in_copy(r + 1, 1 - slot).start())
        in_copy(r, slot).wait()
        pl.when(r >= 2)(lambda: out_copy(r - 2, slot).wait())
        compute(slot)
        out_copy(r, slot).start()
        return carry

    lax.fori_loop(0, n_blocks, body, 0)
    pl.when(n_blocks >= 2)(lambda: out_copy(n_blocks - 2, n_blocks & 1).wait())
    out_copy(n_blocks - 1, (n_blocks - 1) & 1).wait()


def _moe_up_kernel(bstart_ref, bcount_ref, w1_ref, w3_ref, xs_hbm, _zeros_hbm, h_hbm, w1b, w3b, xbuf, obuf,
                   in_sem, out_sem, *, tm):
    f = pl.program_id(0)
    e = pl.program_id(1)
    n_blocks = bcount_ref[e]
    row = lambda r: pl.multiple_of((bstart_ref[e] + r) * tm, tm)

    @pl.when(n_blocks > 0)
    def _():
        w1b[...] = w1_ref[0].astype(BF16)
        w3b[...] = w3_ref[0].astype(BF16)

        def in_copy(r, slot):
            return pltpu.make_async_copy(xs_hbm.at[pl.ds(row(r), tm), :], xbuf.at[slot], in_sem.at[slot])

        def out_copy(r, slot):
            return pltpu.make_async_copy(obuf.at[slot], h_hbm.at[f, pl.ds(row(r), tm), :], out_sem.at[slot])

        def compute(slot):
            x = xbuf[slot]
            a = jnp.dot(x, w1b[...], preferred_element_type=F32)
            b = jnp.dot(x, w3b[...], preferred_element_type=F32)
            obuf[slot] = (a * _sigmoid(a) * b).astype(BF16)

        _expert_row_loop(n_blocks, in_copy, out_copy, compute)


def _moe_up(bstart, bcount, xs, w1, w3, *, tm, tf):
    p, d = xs.shape
    n_exp, _, ff = w1.shape
    wspec = pl.BlockSpec((1, d, tf), lambda f, e, bs, bc: (e, 0, f))
    return pl.pallas_call(
        functools.partial(_moe_up_kernel, tm=tm),
        grid_spec=pltpu.PrefetchScalarGridSpec(
            num_scalar_prefetch=2,
            grid=(ff // tf, n_exp),
            in_specs=[wspec, wspec, pl.BlockSpec(memory_space=pl.ANY), pl.BlockSpec(memory_space=pl.ANY)],
            out_specs=pl.BlockSpec(memory_space=pl.ANY),
            scratch_shapes=[pltpu.VMEM((d, tf), BF16), pltpu.VMEM((d, tf), BF16),
                            pltpu.VMEM((2, tm, d), BF16), pltpu.VMEM((2, tm, tf), BF16),
                            pltpu.SemaphoreType.DMA((2,)), pltpu.SemaphoreType.DMA((2,))]),
        out_shape=jax.ShapeDtypeStruct((ff // tf, p, tf), BF16),
        input_output_aliases={5: 0},
        compiler_params=_cparams(2, VMEM_LIMIT),
        name="moe_up",
    )(bstart, bcount, w1, w3, xs, jnp.zeros((ff // tf, p, tf), BF16))


def _moe_down_kernel(bstart_ref, bcount_ref, w2_ref, hs_hbm, _zeros_hbm, yb_hbm, w2b, hbuf, obuf, in_sem, out_sem,
                     *, tm, tn):
    e = pl.program_id(0)
    n_blocks = bcount_ref[e]
    row = lambda r: pl.multiple_of((bstart_ref[e] + r) * tm, tm)
    nf, _, tf = hbuf.shape[1:]

    @pl.when(n_blocks > 0)
    def _():
        w2b[...] = w2_ref[0].astype(BF16)

        def in_copy(r, slot):
            return pltpu.make_async_copy(hs_hbm.at[:, pl.ds(row(r), tm), :], hbuf.at[slot], in_sem.at[slot])

        def out_copy(r, slot):
            return pltpu.make_async_copy(obuf.at[slot], yb_hbm.at[pl.ds(row(r), tm), :], out_sem.at[slot])

        def compute(slot):
            for n in range(w2b.shape[1] // tn):
                y = jnp.zeros((tm, tn), F32)
                for f in range(nf):
                    y = y + jnp.dot(hbuf[slot, f], w2b[f * tf:(f + 1) * tf, n * tn:(n + 1) * tn],
                                    preferred_element_type=F32)
                obuf[slot, :, n * tn // 2:(n + 1) * tn // 2] = _pack_pairs(y, tn // 2)

        _expert_row_loop(n_blocks, in_copy, out_copy, compute)


def _moe_down(bstart, bcount, hs, w2, *, tm, tn):
    nf, p, tf = hs.shape
    n_exp, ff, d = w2.shape
    return pl.pallas_call(
        functools.partial(_moe_down_kernel, tm=tm, tn=tn),
        grid_spec=pltpu.PrefetchScalarGridSpec(
            num_scalar_prefetch=2,
            grid=(n_exp,),
            in_specs=[pl.BlockSpec((1, ff, d), lambda e, bs, bc: (e, 0, 0)), pl.BlockSpec(memory_space=pl.ANY),
                      pl.BlockSpec(memory_space=pl.ANY)],
            out_specs=pl.BlockSpec(memory_space=pl.ANY),
            scratch_shapes=[pltpu.VMEM((ff, d), BF16), pltpu.VMEM((2, nf, tm, tf), BF16),
                            pltpu.VMEM((2, tm, d // 2), jnp.uint32),
                            pltpu.SemaphoreType.DMA((2,)), pltpu.SemaphoreType.DMA((2,))]),
        out_shape=jax.ShapeDtypeStruct((p, d // 2), jnp.uint32),
        input_output_aliases={4: 0},
        compiler_params=_cparams(1, VMEM_LIMIT),
        name="moe_down",
    )(bstart, bcount, w2, hs, jnp.zeros((p, d // 2), jnp.uint32))


def _moe_combine_kernel(p0_ref, p1_ref, xp_ref, xs_ref, g0_ref, g1_ref, nw_ref, yb_hbm, op_ref, os_ref,
                        buf0, buf1, sem, *, tc, nb_p, half):
    i = pl.program_id(0)
    _gather_rows(yb_hbm, buf0, p0_ref, i * tc, tc, sem)
    _gather_rows(yb_hbm, buf1, p1_ref, i * tc, tc, sem)

    def emit(x_ref, o_ref):
        x = (x_ref[...] + g0_ref[:, 0:1] * _unpack_pairs(buf0[...], half)
             + g1_ref[:, 0:1] * _unpack_pairs(buf1[...], half))
        o_ref[...] = _rms(x, nw_ref[...])

    pl.when(i < nb_p)(lambda: emit(xp_ref, op_ref))
    pl.when(i >= nb_p)(lambda: emit(xs_ref, os_ref))


def _moe_combine(pos0, pos1, xp, xs, g0, g1, norm_w, yb, *, tc, half):
    d = xp.shape[1]
    nb_p, nb_s = xp.shape[0] // tc, xs.shape[0] // tc
    p_idx = lambda i, a, b: (jnp.minimum(i, nb_p - 1), 0)
    s_idx = lambda i, a, b: (jnp.maximum(i - nb_p, 0), 0)
    return pl.pallas_call(
        functools.partial(_moe_combine_kernel, tc=tc, nb_p=nb_p, half=half),
        grid_spec=pltpu.PrefetchScalarGridSpec(
            num_scalar_prefetch=2,
            grid=(nb_p + nb_s,),
            in_specs=[pl.BlockSpec((tc, d), p_idx),
                      pl.BlockSpec((tc, d), s_idx),
                      pl.BlockSpec((tc, LANES), lambda i, a, b: (i, 0)),
                      pl.BlockSpec((tc, LANES), lambda i, a, b: (i, 0)),
                      pl.BlockSpec((1, d), lambda i, a, b: (0, 0)),
                      pl.BlockSpec(memory_space=pl.ANY)],
            out_specs=[pl.BlockSpec((tc, d), p_idx), pl.BlockSpec((tc, d), s_idx)],
            scratch_shapes=[pltpu.VMEM((tc, d // 2), jnp.uint32), pltpu.VMEM((tc, d // 2), jnp.uint32),
                            pltpu.SemaphoreType.DMA(())]),
        out_shape=[jax.ShapeDtypeStruct(xp.shape, F32), jax.ShapeDtypeStruct(xs.shape, F32)],
        compiler_params=_cparams(1, VMEM_LIMIT),
        name="moe_combine",
    )(pos0, pos1, xp, xs, g0, g1, norm_w, yb)


def _route(logits, bg, be, *, tm):
    t = logits.shape[0]
    pg = jax.nn.softmax(logits[:, :N_GROUPS] + bg, axis=-1)
    g_idx = jnp.argmax(pg, axis=-1).astype(jnp.int32)
    p_sel = jnp.take_along_axis(pg, g_idx[:, None], axis=-1)
    le = (logits[:, N_GROUPS:N_GROUPS + N_EXPERTS] + be).reshape(t, N_GROUPS, EXPERTS_PER_GROUP)
    le = jnp.take_along_axis(le, g_idx[:, None, None], axis=1)[:, 0]
    pe = jax.nn.softmax(le, axis=-1)
    top_v, top_i = lax.top_k(pe, TOP_K)
    gate = p_sel * top_v / jnp.sum(top_v, axis=-1, keepdims=True)
    e_flat = (g_idx[:, None] * EXPERTS_PER_GROUP + top_i.astype(jnp.int32)).reshape(-1)
    a = t * TOP_K
    onehot = (e_flat[:, None] == jnp.arange(N_EXPERTS, dtype=jnp.int32)[None, :]).astype(jnp.int32)
    rank = jnp.sum((jnp.cumsum(onehot, axis=0) - onehot) * onehot, axis=1)
    counts = jnp.sum(onehot, axis=0)
    padded = (counts + tm - 1) // tm * tm
    pad_end = jnp.cumsum(padded)
    dest = (pad_end - padded)[e_flat] + rank
    n_blocks = -(-(a + N_EXPERTS * (tm - 1)) // tm)
    tok = jnp.arange(a, dtype=jnp.int32) // TOP_K
    slot_tok = jnp.zeros((n_blocks * tm,), jnp.int32).at[dest].set(tok)
    bstart = ((pad_end - padded) // tm).astype(jnp.int32)
    bcount = (padded // tm).astype(jnp.int32)
    nused = (pad_end[-1] // tm).astype(jnp.int32).reshape(1)
    dest = dest.reshape(t, TOP_K)
    return gate, dest[:, 0], dest[:, 1], slot_tok, bstart, bcount, nused


def _moe(xp, xs, norm2_w, wr, bg, be, w1, w3, w2, final_w, *, tm_route, tm_blk, tf, tn, tc):
    h, logits = _router(xp, xs, norm2_w, wr, tm=tm_route)
    gate, pos0, pos1, slot_tok, bstart, bcount, nused = _route(logits, bg, be, tm=tm_blk)
    xg = _moe_gather(slot_tok, nused, h, tm=tm_blk)
    hs = _moe_up(bstart, bcount, xg, w1, w3, tm=tm_blk, tf=tf)
    yb = _moe_down(bstart, bcount, hs, w2, tm=tm_blk, tn=tn)
    g0 = jnp.broadcast_to(gate[:, 0:1], (h.shape[0], LANES))
    g1 = jnp.broadcast_to(gate[:, 1:2], (h.shape[0], LANES))
    return _moe_combine(pos0, pos1, xp, xs, g0, g1, final_w, yb, tc=tc, half=tn // 2)


def kernel(x_prompt, x_sample, state_mlstm_c, state_mlstm_n, state_mlstm_m, state_rwkv, state_rwkv_shift, norm1_w, w_in, w_out, ml_b_i, ml_b_f, ml_norm_w, rw_mu, rw_w0, rw_w2, rw_a0, rw_a2, rw_g2, rw_k_k, rw_k_a, rw_r_k, rw_ln_w, rw_ln_b, norm2_w, router_group_w, router_group_b, router_expert_w, router_expert_b, moe_w1, moe_w3, moe_w2, final_norm_w):
    assert w_in.shape[0] == 1, "single-layer trunk"
    bp, seq, d = x_prompt.shape
    bs = x_sample.shape[0]
    tp = bp * seq
    ml_proj = 2 * ML_HEADS * ML_DK + 2 * ML_WIDTH + 2 * ML_HEADS
    rw_proj = 3 * RW_WIDTH + RW_LORA
    rw_col0 = -(-ml_proj // RW_GROUP) * RW_GROUP
    n_groups = RW_WIDTH // RW_GROUP

    w_cat = _pack_w_in(w_in[0], split=ml_proj, width=rw_col0, tn=RW_GROUP)
    w_o = w_out[0].astype(BF16)
    mu = rw_mu[0]
    rows = [rw_w0[0], rw_a0[0], rw_k_k[0], rw_k_a[0], rw_r_k[0].reshape(-1), rw_ln_w[0], rw_ln_b[0],
            mu[:RW_WIDTH], mu[RW_WIDTH:2 * RW_WIDTH], mu[2 * RW_WIDTH:3 * RW_WIDTH]]
    prm = jnp.concatenate([jnp.stack(rows), jnp.zeros((16 - len(rows), RW_WIDTH), F32)], axis=0)
    mu_l = mu[3 * RW_WIDTH:][None]
    wl = jnp.zeros((RW_LORA, 3, RW_WIDTH), F32)
    wl = wl.at[:RW_DECAY_LORA, 0].set(rw_w2[0]).at[RW_DECAY_LORA:RW_DECAY_LORA + RW_A_LORA, 1].set(rw_a2[0])
    wl = wl.at[RW_DECAY_LORA + RW_A_LORA:, 2].set(rw_g2[0])
    wl = wl.reshape(RW_LORA, 3, n_groups, RW_GROUP).transpose(2, 0, 1, 3).reshape(n_groups, RW_LORA, 3 * RW_GROUP)
    wl = wl.astype(BF16)
    gate_bias = jnp.zeros((1, LANES), F32).at[0, :ML_HEADS].set(ml_b_i[0]).at[0, ML_HEADS:2 * ML_HEADS].set(ml_b_f[0])
    ml_nw = ml_norm_w[0][None]
    wr = jnp.concatenate([router_group_w[0], router_expert_w[0],
                          jnp.zeros((d, LANES - N_GROUPS - N_EXPERTS), F32)], axis=1)

    xp = x_prompt.reshape(tp, d)
    xs = x_sample.reshape(bs, d)
    proj_p = _in_proj(xp, norm1_w, w_cat, tm=512, tn=1280)
    proj_s = _in_proj(xs, norm1_w, w_cat, tm=bs, tn=1280)

    y_ml_p, p_c, p_nm = _mlstm_prompt(proj_p, gate_bias, ml_nw, batch=bp, seq=seq)
    y_rw_p, p_s = _rwkv_prompt(proj_p, prm, mu_l, wl, batch=bp, seq=seq, rw_col0=rw_col0, n_groups=n_groups)
    p_sh = lax.slice(proj_p, (seq - 1, rw_col0), (tp, rw_col0 + rw_proj), (seq, 1))

    y_ml_s, s_c, s_n, s_m = _mlstm_sample(proj_s[:, None, :], gate_bias, ml_nw, state_mlstm_c[0], state_mlstm_n[0],
                                          state_mlstm_m[0][:, None, :])
    r, k, v, kk, ka, dec, g = _rwkv_sample_prep(proj_s, state_rwkv_shift[0], prm, mu_l, wl, row0=0, rows=bs,
                                                rw_col0=rw_col0, n_groups=n_groups)
    s_s, y_state = _rwkv_sample_state(state_rwkv[0], r, k, kk, ka, dec, v)
    y_rw_s = _rwkv_sample_post(y_state, r, k, v, g, prm)
    s_sh = proj_s[:, rw_col0:rw_col0 + rw_proj]

    x2_p = _out_proj(y_ml_p, y_rw_p, w_o, xp, tm=512, tn=1024)
    x2_s = _out_proj(y_ml_s.reshape(bs, ML_WIDTH), y_rw_s, w_o, xs, tm=bs, tn=1024)
    y_p, y_s = _moe(x2_p, x2_s, norm2_w, wr, router_group_b[0], router_expert_b[0], moe_w1[0], moe_w3[0],
                    moe_w2[0], final_norm_w[None], tm_route=128, tm_blk=256, tf=256, tn=1024, tc=128)

    lead = lambda a: a[None]
    return (y_p.reshape(bp, seq, d), y_s.reshape(bs, 1, d),
            lead(p_c), lead(p_nm[:, :, 0]), lead(p_nm[:, :, 1, 0]), lead(p_s), lead(p_sh),
            lead(s_c), lead(s_n), lead(s_m[:, 0]), lead(s_s), lead(s_sh))
```

```python
import functools

import jax
import jax.numpy as jnp
from jax import lax
from jax.experimental import pallas as pl
from jax.experimental.pallas import tpu as pltpu

F32 = jnp.float32
BF16 = jnp.bfloat16

D_MODEL = 4096
ML_HEADS = 4
ML_DK = 256
ML_DV = 512
ML_WIDTH = ML_HEADS * ML_DV
GATE_SOFTCAP = 15.0
RW_HEAD = 64
RW_WIDTH = D_MODEL - ML_WIDTH
RW_HEADS = RW_WIDTH // RW_HEAD
RW_DECAY_LORA = 96
RW_A_LORA = 96
RW_GATE_LORA = 64
RW_GN_EPS = 64e-5
N_GROUPS = 4
EXPERTS_PER_GROUP = 8
N_EXPERTS = N_GROUPS * EXPERTS_PER_GROUP
TOP_K = 2
EPS = 1e-6

LANES = 128
RW_GROUP = 256
RW_GHEADS = RW_GROUP // RW_HEAD
RW_LORA = RW_DECAY_LORA + RW_A_LORA + RW_GATE_LORA
RW_CHUNK = 64
ML_CHUNK = 64
_SAMPLE_ROWS_PER_STAGE = 8
VMEM_LIMIT = 56 * 1024 * 1024


def _cparams(n_axes, vmem=None):
    return pltpu.CompilerParams(dimension_semantics=("arbitrary",) * n_axes, vmem_limit_bytes=vmem)


def _sigmoid(x):
    return 1.0 / (1.0 + jnp.exp(-x))


def _softplus(x):
    return jnp.maximum(x, 0.0) + jnp.log(1.0 + jnp.exp(-jnp.abs(x)))


def _soft_cap(x):
    return GATE_SOFTCAP * jnp.tanh(x / GATE_SOFTCAP)


def _dot(a, b):
    return jnp.dot(a.astype(BF16), b.astype(BF16), preferred_element_type=F32)


def _dot_nt(a, b):
    return lax.dot_general(a.astype(BF16), b.astype(BF16), (((1,), (1,)), ((), ())),
                           preferred_element_type=F32)


def _split2(x):
    hi = x.astype(BF16)
    lo = (x - hi.astype(F32)).astype(BF16)
    return hi, lo


def _split3(x):
    hi = x.astype(BF16)
    r = x - hi.astype(F32)
    mid = r.astype(BF16)
    lo = (r - mid.astype(F32)).astype(BF16)
    return hi, mid, lo


def _iota(shape, axis):
    return lax.broadcasted_iota(jnp.int32, shape, axis)


def _pack_pairs(x, half):
    bits = lax.bitcast_convert_type(x.astype(BF16).astype(F32), jnp.uint32)
    parts = []
    for g in range(x.shape[1] // (2 * half)):
        lo = bits[:, 2 * half * g:2 * half * g + half]
        hi = bits[:, 2 * half * g + half:2 * half * (g + 1)]
        parts.append((lo >> 16) | hi)
    return parts[0] if len(parts) == 1 else jnp.concatenate(parts, axis=1)


def _unpack_pairs(p, half):
    parts = []
    for g in range(p.shape[1] // half):
        w = p[:, half * g:half * (g + 1)]
        parts.append(lax.bitcast_convert_type(w << 16, F32))
        parts.append(lax.bitcast_convert_type(w & jnp.uint32(0xFFFF0000), F32))
    return jnp.concatenate(parts, axis=1)


def _head_block_ones(n):
    return jnp.where((_iota((n, n), 0) >> 6) == (_iota((n, n), 1) >> 6), 1.0, 0.0).astype(BF16)


def _seg_sum(x, bd):
    hi, lo = _split2(x)
    return (jnp.dot(hi, bd, preferred_element_type=F32) + jnp.dot(lo, bd, preferred_element_type=F32))


_P_W0, _P_A0, _P_KK, _P_KA, _P_RK, _P_LNW, _P_LNB, _P_MUR, _P_MUK, _P_MUV = range(10)


def _rwkv_prep(p_r, p_k, p_v, p_l, prev_r, prev_k, prev_v, prev_l, prm, mu_l, wl, bd):
    row = lambda i: prm[i:i + 1, :]
    xr = p_r + (prev_r - p_r) * row(_P_MUR)
    xk = p_k + (prev_k - p_k) * row(_P_MUK)
    xv = p_v + (prev_v - p_v) * row(_P_MUV)
    xl = p_l + (prev_l - p_l) * mu_l
    lane = _iota(xl.shape, 1)
    z = jnp.where(lane < RW_DECAY_LORA, jnp.tanh(xl),
                  jnp.where(lane < RW_DECAY_LORA + RW_A_LORA, xl, _sigmoid(xl)))
    lo = _dot(z, wl)
    lw, la, lg = lo[:, :RW_GROUP], lo[:, RW_GROUP:2 * RW_GROUP], lo[:, 2 * RW_GROUP:]
    w = -_softplus(-(row(_P_W0) + lw)) - 0.5
    logd = -jnp.exp(w)
    a = _sigmoid(row(_P_A0) + la)
    kk = xk * row(_P_KK)
    kk = kk / jnp.maximum(jnp.sqrt(_seg_sum(kk * kk, bd)), 1e-12)
    k = xk * (1.0 + (a - 1.0) * row(_P_KA))
    return xr, k, xv, kk, a, logd, lg


def _rwkv_post(y, r, k, v, g, prm, bd):
    row = lambda i: prm[i:i + 1, :]
    mean = _seg_sum(y, bd) * (1.0 / RW_HEAD)
    yc = y - mean
    var = _seg_sum(yc * yc, bd) * (1.0 / RW_HEAD)
    yn = yc * lax.rsqrt(var + RW_GN_EPS) * row(_P_LNW) + row(_P_LNB)
    bonus = _seg_sum(r * k * row(_P_RK), bd) * v
    return (yn + bonus) * g


def _rwkv_chunk(r, k, v, kk, a, logd, G, bd):
    L = r.shape[0]
    tril = jnp.where(_iota((L, L), 1) <= _iota((L, L), 0), 1.0, 0.0).astype(BF16)
    d_hi, d_lo = _split2(logd)
    clog = (jnp.dot(tril, d_hi, preferred_element_type=F32)
            + jnp.dot(tril, d_lo, preferred_element_type=F32))
    clog_l = clog[L - 1:L, :]
    n_in = jnp.exp(-clog)
    to_end = jnp.exp(clog_l - clog)
    ka = kk * a
    at = -kk * jnp.exp(clog - logd)
    rt = r * jnp.exp(clog)
    lane_head = _iota((1, RW_GROUP), 1) >> 6
    masks = [lane_head == h for h in range(RW_GHEADS)]
    zero = jnp.zeros_like(at)
    bf = lambda x: x.astype(BF16)
    lhs = jnp.concatenate([jnp.where(m, x, zero) for m in masks for x in (at, rt)], axis=0)
    xx = _dot_nt(lhs, jnp.concatenate([ka * n_in, k * n_in], axis=0))
    t2 = _iota((L, 2 * L), 0)
    c2 = _iota((L, 2 * L), 1)
    s2 = c2 & (L - 1)
    right = c2 >= L
    eye_pad = jnp.where(c2 == t2 + L, 1.0, 0.0)
    zeros_v = jnp.zeros((L, RW_GROUP), BF16)
    v_b = bf(v)
    vz = jnp.concatenate([zeros_v, v_b], axis=0)
    yield
    zs, rbk = [], []
    makv = zero
    for h, m in enumerate(masks):
        o = 2 * L * h
        nk = jnp.where(s2 < t2, xx[o:o + L], 0.0)
        rbk.append(bf(jnp.where(s2 <= t2, xx[o + L:o + 2 * L], 0.0)))
        makv = makv + jnp.where(m, _dot(jnp.where(right, nk, 0.0), vz), 0.0)
        zs.append(jnp.where(right, eye_pad, nk))
    for _ in range((L - 1).bit_length()):
        yield
        zs = [_dot(z[:, :L], z) + jnp.where(right, z, 0.0) for z in zs]
    yield
    ws = _dot_nt(jnp.concatenate([at, rt], axis=0), G)
    xz = jnp.concatenate([zeros_v, bf(ws[:L] + makv)], axis=0)
    yield
    w_all = zero
    for z_h, m in zip(zs, masks):
        w_all = w_all + jnp.where(m, _dot(z_h, xz), 0.0)
    wv = jnp.concatenate([bf(w_all), v_b], axis=0)
    yield
    y = ws[L:]
    for rbk_h, m in zip(rbk, masks):
        y = y + jnp.where(m, jnp.dot(rbk_h, wv, preferred_element_type=F32), 0.0)
    upd = _dot(wv.astype(F32).T, jnp.concatenate([ka * to_end, k * to_end], axis=0))
    g_new = G * jnp.exp(clog_l) + jnp.where(bd > 0, upd, 0.0)
    return y, g_new


def _interleave(gens):
    results = [None] * len(gens)
    live = list(range(len(gens)))
    while live:
        for i in list(live):
            try:
                next(gens[i])
            except StopIteration as stop:
                results[i] = stop.value
                live.remove(i)
    return results


def _shift_rows(x, first_row):
    rolled = pltpu.roll(x, 1, axis=0)
    return jnp.where(_iota(x.shape, 0) == 0, first_row, rolled)


def _rwkv_prompt_kernel(*refs, n_chunks, batch):
    p_refs = refs[:4 * batch]
    prm_ref, mul_ref, wl_ref, y_ref, s_ref, g_scr, carry_scr = refs[4 * batch:]
    c = pl.program_id(1)

    @pl.when(c == 0)
    def _():
        g_scr[...] = jnp.zeros_like(g_scr)
        carry_scr[...] = jnp.zeros_like(carry_scr)

    bd = _head_block_ones(RW_GROUP)
    prm = prm_ref[...]

    def sequence(b):
        ps = [ref[...] for ref in p_refs[4 * b:4 * b + 4]]
        L = ps[0].shape[0]
        prevs = [_shift_rows(x, carry_scr[b, i:i + 1, :]) for i, x in enumerate(ps)]
        r, k, v, kk, a, logd, g = _rwkv_prep(*ps, *prevs, prm, mul_ref[...], wl_ref[0], bd)
        y, g_new = yield from _rwkv_chunk(r, k, v, kk, a, logd, g_scr[b], bd)
        yield
        out = _rwkv_post(y, r, k, v, g, prm, bd).astype(y_ref.dtype)
        return out, g_new, [x[L - 1:L, :] for x in ps]

    results = _interleave([sequence(b) for b in range(batch)])
    finals = []
    for b, (out, g_new, last_rows) in enumerate(results):
        y_ref[b] = out
        g_scr[b] = g_new
        for i, x in enumerate(last_rows):
            carry_scr[b, i:i + 1, :] = x
        finals.append(g_new)

    @pl.when(c == n_chunks - 1)
    def _():
        for b in range(batch):
            for h in range(RW_GHEADS):
                s_ref[b, h] = finals[b][RW_HEAD * h:RW_HEAD * (h + 1), RW_HEAD * h:RW_HEAD * (h + 1)]


def _rwkv_prompt(proj, prm, mu_l, wl, *, batch, seq, rw_col0, n_groups):
    L = RW_CHUNK
    nc = seq // L
    cb = rw_col0 // RW_GROUP
    width = n_groups * RW_GROUP
    pspec = lambda b, off: pl.BlockSpec((L, RW_GROUP), lambda g, c, b=b, off=off: (b * nc + c, cb + off + g))
    lspec = lambda b: pl.BlockSpec((L, RW_GROUP), lambda g, c, b=b: (b * nc + c, cb + 3 * n_groups))
    p_specs = []
    for b in range(batch):
        p_specs += [pspec(b, 0), pspec(b, n_groups), pspec(b, 2 * n_groups), lspec(b)]
    y, s = pl.pallas_call(
        functools.partial(_rwkv_prompt_kernel, n_chunks=nc, batch=batch),
        grid=(n_groups, nc),
        in_specs=p_specs + [pl.BlockSpec((16, RW_GROUP), lambda g, c: (0, g)),
                            pl.BlockSpec((1, RW_LORA), lambda g, c: (0, 0)),
                            pl.BlockSpec((1, RW_LORA, 3 * RW_GROUP), lambda g, c: (g, 0, 0))],
        out_specs=[pl.BlockSpec((batch, L, RW_GROUP), lambda g, c: (0, c, g)),
                   pl.BlockSpec((batch, RW_GHEADS, RW_HEAD, RW_HEAD), lambda g, c: (0, g, 0, 0))],
        out_shape=[jax.ShapeDtypeStruct((batch, seq, width), BF16),
                   jax.ShapeDtypeStruct((batch, n_groups * RW_GHEADS, RW_HEAD, RW_HEAD), F32)],
        scratch_shapes=[pltpu.VMEM((batch, RW_GROUP, RW_GROUP), F32), pltpu.VMEM((batch, 8, RW_GROUP), F32)],
        compiler_params=_cparams(2),
        name="rwkv_prompt",
    )(*([proj] * (4 * batch)), prm, mu_l, wl)
    return y.reshape(batch * seq, width), s


def _rwkv_sample_prep_kernel(pr_ref, pk_ref, pv_ref, pl_ref, sr_ref, sk_ref, sv_ref, sl_ref, prm_ref, mul_ref,
                             wl_ref, r_ref, k_ref, v_ref, g_ref, rt_ref, kt_ref, vt_ref, kkt_ref, kat_ref, dt_ref):
    bd = _head_block_ones(RW_GROUP)
    r, k, v, kk, a, logd, g = _rwkv_prep(pr_ref[...], pk_ref[...], pv_ref[...], pl_ref[...], sr_ref[...],
                                         sk_ref[...], sv_ref[...], sl_ref[...], prm_ref[...], mul_ref[...],
                                         wl_ref[0], bd)
    r_ref[...] = r
    k_ref[...] = k
    v_ref[...] = v
    g_ref[...] = g
    rt_ref[...] = r.T
    kt_ref[...] = k.T
    vt_ref[...] = v.T
    kkt_ref[...] = kk.T
    kat_ref[...] = (kk * a).T
    dt_ref[...] = jnp.exp(logd).T


def _rwkv_sample_prep(proj, shift0, prm, mu_l, wl, *, rows, rw_col0, n_groups):
    cb = rw_col0 // RW_GROUP
    pspec = lambda off: pl.BlockSpec((rows, RW_GROUP), lambda g, off=off: (0, cb + off + g))
    sspec = lambda off: pl.BlockSpec((rows, RW_GROUP), lambda g, off=off: (0, off + g))
    ospec = pl.BlockSpec((rows, RW_GROUP), lambda g: (0, g))
    tspec = pl.BlockSpec((RW_GROUP, rows), lambda g: (g, 0))
    width = n_groups * RW_GROUP
    return pl.pallas_call(
        _rwkv_sample_prep_kernel,
        grid=(n_groups,),
        in_specs=[pspec(0), pspec(n_groups), pspec(2 * n_groups),
                  pl.BlockSpec((rows, RW_GROUP), lambda g: (0, cb + 3 * n_groups)),
                  sspec(0), sspec(n_groups), sspec(2 * n_groups),
                  pl.BlockSpec((rows, RW_GROUP), lambda g: (0, 3 * n_groups)),
                  pl.BlockSpec((16, RW_GROUP), lambda g: (0, g)),
                  pl.BlockSpec((1, RW_LORA), lambda g: (0, 0)),
                  pl.BlockSpec((1, RW_LORA, 3 * RW_GROUP), lambda g: (g, 0, 0))],
        out_specs=[ospec] * 4 + [tspec] * 6,
        out_shape=[jax.ShapeDtypeStruct((rows, width), F32)] * 4 + [jax.ShapeDtypeStruct((width, rows), F32)] * 6,
        compiler_params=_cparams(1),
        name="rwkv_sample_prep",
    )(proj, proj, proj, proj, shift0, shift0, shift0, shift0, prm, mu_l, wl)


def _rwkv_sample_state_kernel(s_ref, rt_ref, kt_ref, kkt_ref, kat_ref, dt_ref, vt_ref, so_ref, yt_ref):
    n_kk = -kkt_ref[...]
    r, k, ka, d = rt_ref[...], kt_ref[...], kat_ref[...], dt_ref[...]
    for i0 in range(0, s_ref.shape[1], _SAMPLE_ROWS_PER_STAGE):
        rows = range(i0, i0 + _SAMPLE_ROWS_PER_STAGE)
        sa = [jnp.sum(s_ref[0, i] * n_kk, axis=0, keepdims=True) for i in rows]
        ys = []
        for i, sa_i in zip(rows, sa):
            s_new = s_ref[0, i] * d + sa_i * ka + vt_ref[i:i + 1, :] * k
            so_ref[0, i] = s_new
            ys.append(jnp.sum(s_new * r, axis=0, keepdims=True))
        yt_ref[i0:i0 + _SAMPLE_ROWS_PER_STAGE, :] = jnp.concatenate(ys, axis=0)


def _rwkv_sample_state(state_t, rt, kt, kkt, kat, dt, vt):
    nh, n, _, b = state_t.shape
    vspec = pl.BlockSpec((n, b), lambda h: (h, 0))
    sspec = pl.BlockSpec((1, n, n, b), lambda h: (h, 0, 0, 0))
    return pl.pallas_call(
        _rwkv_sample_state_kernel,
        grid=(nh,),
        in_specs=[sspec] + [vspec] * 6,
        out_specs=[sspec, vspec],
        out_shape=[jax.ShapeDtypeStruct(state_t.shape, F32), jax.ShapeDtypeStruct((nh * n, b), F32)],
        compiler_params=_cparams(1),
        name="rwkv_sample_state",
    )(state_t, rt, kt, kkt, kat, dt, vt)


def _rwkv_sample_post_kernel(yt_ref, r_ref, k_ref, v_ref, g_ref, prm_ref, o_ref):
    bd = _head_block_ones(RW_GROUP)
    o_ref[...] = _rwkv_post(yt_ref[...].T, r_ref[...], k_ref[...], v_ref[...], g_ref[...], prm_ref[...],
                            bd).astype(o_ref.dtype)


def _rwkv_sample_post(yt, r, k, v, g, prm):
    rows, width = r.shape
    spec = pl.BlockSpec((rows, RW_GROUP), lambda i: (0, i))
    return pl.pallas_call(
        _rwkv_sample_post_kernel,
        grid=(width // RW_GROUP,),
        in_specs=[pl.BlockSpec((RW_GROUP, rows), lambda i: (i, 0))] + [spec] * 4
                 + [pl.BlockSpec((16, RW_GROUP), lambda i: (0, i))],
        out_specs=spec,
        out_shape=jax.ShapeDtypeStruct((rows, width), BF16),
        compiler_params=_cparams(1),
        name="rwkv_sample_post",
    )(yt, r, k, v, g, prm)


def _mlstm_gates(gates, bias, h):
    capped = _soft_cap(gates + bias)
    lane = _iota(gates.shape, 1)
    i_col = jnp.sum(jnp.where(lane == h, capped, 0.0), axis=1, keepdims=True)
    f_col = jnp.sum(jnp.where(lane == h + ML_HEADS, -_softplus(-capped), 0.0), axis=1, keepdims=True)
    return i_col, f_col


def _mlstm_out(hh, o, norm_w):
    hn = hh * lax.rsqrt(jnp.mean(hh * hh, axis=1, keepdims=True) + EPS) * norm_w
    return hn * _sigmoid(o)


def _mlstm_prompt_kernel(*refs, n_chunks, batch):
    p_refs = refs[:5 * batch]
    gb_ref, nw_ref, y_ref, c_ref, nm_ref, c_scr, n_scr, m_scr = refs[5 * batch:]
    h = pl.program_id(0)
    c = pl.program_id(1)

    @pl.when(c == 0)
    def _():
        c_scr[...] = jnp.zeros_like(c_scr)
        n_scr[...] = jnp.zeros_like(n_scr)
        m_scr[...] = jnp.zeros_like(m_scr)

    def sequence(b):
        q_ref, k_ref, v_ref, o_ref, gt_ref = p_refs[5 * b:5 * b + 5]
        q = q_ref[...] * (ML_DK ** -0.5)
        k = k_ref[...]
        v = v_ref[...]
        L = q.shape[0]
        i_col, f_col = _mlstm_gates(gt_ref[...], gb_ref[...], h)
        t_i = _iota((L, L), 0)
        s_i = _iota((L, L), 1)
        causal = s_i <= t_i
        to_row = lambda col: jnp.sum(jnp.where(t_i == s_i, col, 0.0), axis=0, keepdims=True)
        i_row = to_row(i_col)
        f_row = to_row(f_col)
        yield
        b_col = jnp.sum(jnp.where(causal, f_row, 0.0), axis=1, keepdims=True)
        b_row = jnp.sum(jnp.where(t_i <= s_i, f_col, 0.0), axis=0, keepdims=True)
        m0 = m_scr[b]
        inter = b_col + m0
        dmat = jnp.where(causal, b_col - b_row + i_row, -1e30)
        yield
        m_t = jnp.maximum(inter, jnp.max(dmat, axis=1, keepdims=True))
        w_int = jnp.exp(inter - m_t)
        s = _dot_nt(q, k) * jnp.exp(dmat - m_t)
        c0 = c_scr[b]
        n0 = n_scr[b]
        yield
        num = w_int * _dot(q, c0) + _dot(s, v)
        den = w_int * jnp.sum(q * n0, axis=1, keepdims=True) + jnp.sum(s, axis=1, keepdims=True)
        hh = num / jnp.maximum(jnp.abs(den), jnp.exp(-m_t))
        m_new = m_t[L - 1:L, :]
        b_l = b_col[L - 1:L, :]
        a_end = jnp.exp(b_l - b_col + i_col - m_new)
        dec = jnp.exp(b_l + m0 - m_new)
        ka = k * a_end
        yield
        c_new = dec * c0 + _dot(ka.T, v)
        n_new = dec * n0 + jnp.sum(ka, axis=0, keepdims=True)
        out = _mlstm_out(hh, o_ref[...], nw_ref[...]).astype(y_ref.dtype)
        return out, c_new, n_new, m_new

    results = _interleave([sequence(b) for b in range(batch)])
    for b, (out, c_new, n_new, m_new) in enumerate(results):
        y_ref[b] = out
        c_scr[b] = c_new
        n_scr[b] = n_new
        m_scr[b] = m_new

    @pl.when(c == n_chunks - 1)
    def _():
        for b, (_, c_new, n_new, m_new) in enumerate(results):
            c_ref[b, 0] = c_new
            nm_ref[b, 0] = jnp.concatenate([n_new, jnp.broadcast_to(m_new, (7, ML_DK))], axis=0)


def _mlstm_prompt(proj, gate_bias, norm_w, *, batch, seq):
    L = ML_CHUNK
    nc = seq // L
    nh = ML_HEADS
    kq = nh * ML_DK
    p_specs = []
    for b in range(batch):
        rows = lambda c, b=b: b * nc + c
        p_specs += [pl.BlockSpec((L, ML_DK), lambda h, c, rows=rows: (rows(c), h)),
                    pl.BlockSpec((L, ML_DK), lambda h, c, rows=rows: (rows(c), nh + h)),
                    pl.BlockSpec((L, ML_DV), lambda h, c, rows=rows: (rows(c), 2 * kq // ML_DV + h)),
                    pl.BlockSpec((L, ML_DV), lambda h, c, rows=rows: (rows(c), 2 * kq // ML_DV + nh + h)),
                    pl.BlockSpec((L, LANES), lambda h, c, rows=rows: (rows(c), (2 * kq + 2 * ML_WIDTH) // LANES))]
    y, c_fin, nm = pl.pallas_call(
        functools.partial(_mlstm_prompt_kernel, n_chunks=nc, batch=batch),
        grid=(nh, nc),
        in_specs=p_specs + [pl.BlockSpec((1, LANES), lambda h, c: (0, 0)),
                            pl.BlockSpec((1, ML_DV), lambda h, c: (0, h))],
        out_specs=[pl.BlockSpec((batch, L, ML_DV), lambda h, c: (0, c, h)),
                   pl.BlockSpec((batch, 1, ML_DK, ML_DV), lambda h, c: (0, h, 0, 0)),
                   pl.BlockSpec((batch, 1, 8, ML_DK), lambda h, c: (0, h, 0, 0))],
        out_shape=[jax.ShapeDtypeStruct((batch, seq, ML_WIDTH), BF16),
                   jax.ShapeDtypeStruct((batch, nh, ML_DK, ML_DV), F32),
                   jax.ShapeDtypeStruct((batch, nh, 8, ML_DK), F32)],
        scratch_shapes=[pltpu.VMEM((batch, ML_DK, ML_DV), F32), pltpu.VMEM((batch, 1, ML_DK), F32),
                        pltpu.VMEM((batch, 1, 1), F32)],
        compiler_params=_cparams(2),
        name="mlstm_prompt",
    )(*([proj] * (5 * batch)), gate_bias, norm_w)
    return y.reshape(batch * seq, ML_WIDTH), c_fin, nm


def _mlstm_sample_kernel(q_ref, k_ref, v_ref, o_ref, gt_ref, gb_ref, nw_ref, c_ref, n_ref, m_ref,
                         y_ref, co_ref, no_ref, mo_ref):
    gates = gt_ref[0]
    eye = jnp.where(_iota((ML_DK, ML_DK), 0) == _iota((ML_DK, ML_DK), 1), 1.0, 0.0).astype(BF16)
    for h in range(ML_HEADS):
        q = q_ref[0][:, h * ML_DK:(h + 1) * ML_DK] * (ML_DK ** -0.5)
        k = k_ref[0][:, h * ML_DK:(h + 1) * ML_DK]
        v = v_ref[0][:, h * ML_DV:(h + 1) * ML_DV]
        i_pre, logf = _mlstm_gates(gates, gb_ref[...], h)
        c0 = c_ref[0, h]
        n0 = n_ref[0, h:h + 1, :]
        m0 = m_ref[0][:, h:h + 1]
        inter = logf + m0
        m_t = jnp.maximum(inter, i_pre)
        w_int = jnp.exp(inter - m_t)
        a_new = jnp.exp(i_pre - m_t)
        s = jnp.sum(q * k, axis=1, keepdims=True) * a_new
        qc = _dot(jnp.broadcast_to(q, (8, ML_DK)), c0)[0:1, :]
        num = w_int * qc + s * v
        den = w_int * jnp.sum(q * n0, axis=1, keepdims=True) + s
        hh = num / jnp.maximum(jnp.abs(den), jnp.exp(-m_t))
        k_hi, k_lo = _split2(jnp.broadcast_to(k, (8, ML_DK)))
        nt = (((1,), (1,)), ((), ()))
        k_col = (lax.dot_general(eye, k_hi, nt, preferred_element_type=F32)
                 + lax.dot_general(eye, k_lo, nt, preferred_element_type=F32))[:, 0:1]
        co_ref[0, h] = w_int * c0 + k_col * (a_new * v)
        no_ref[0, h:h + 1, :] = w_int * n0 + a_new * k
        mo_ref[0, :, h:h + 1] = m_t
        y_ref[0, :, h * ML_DV:(h + 1) * ML_DV] = _mlstm_out(
            hh, o_ref[0][:, h * ML_DV:(h + 1) * ML_DV], nw_ref[:, h * ML_DV:(h + 1) * ML_DV]).astype(y_ref.dtype)


def _mlstm_sample(proj3, gate_bias, norm_w, c0, n0, m0):
    b = proj3.shape[0]
    nh = ML_HEADS
    kq = nh * ML_DK
    p3 = lambda w, blk: pl.BlockSpec((1, 1, w), lambda i, blk=blk: (i, 0, blk))
    cspec = pl.BlockSpec((1, nh, ML_DK, ML_DV), lambda i: (i, 0, 0, 0))
    nspec = pl.BlockSpec((1, nh, ML_DK), lambda i: (i, 0, 0))
    mspec = pl.BlockSpec((1, 1, nh), lambda i: (i, 0, 0))
    return pl.pallas_call(
        _mlstm_sample_kernel,
        grid=(b,),
        in_specs=[p3(kq, 0), p3(kq, 1), p3(ML_WIDTH, 2 * kq // ML_WIDTH), p3(ML_WIDTH, 2 * kq // ML_WIDTH + 1),
                  p3(LANES, (2 * kq + 2 * ML_WIDTH) // LANES),
                  pl.BlockSpec((1, LANES), lambda i: (0, 0)),
                  pl.BlockSpec((1, ML_WIDTH), lambda i: (0, 0)),
                  cspec, nspec, mspec],
        out_specs=[pl.BlockSpec((1, 1, ML_WIDTH), lambda i: (i, 0, 0)), cspec, nspec, mspec],
        out_shape=[jax.ShapeDtypeStruct((b, 1, ML_WIDTH), BF16),
                   jax.ShapeDtypeStruct(c0.shape, F32), jax.ShapeDtypeStruct(n0.shape, F32),
                   jax.ShapeDtypeStruct(m0.shape, F32)],
        compiler_params=_cparams(1),
        name="mlstm_sample",
    )(proj3, proj3, proj3, proj3, proj3, gate_bias, norm_w, c0, n0, m0)


def _rms(x, w):
    return x * lax.rsqrt(jnp.mean(x * x, axis=1, keepdims=True) + EPS) * w


def _pack_w_in_kernel(wt_hbm, o_ref, buf, sem, *, n_a, split):
    j = pl.program_id(0)
    tr = o_ref.shape[0]

    def copy(step, slot):
        src = jnp.where(step < n_a, step * tr, split + (step - n_a) * tr)
        return pltpu.make_async_copy(wt_hbm.at[pl.ds(pl.multiple_of(src, 8), tr), :], buf.at[slot], sem.at[slot])

    pl.when(j == 0)(lambda: copy(j, 0).start())
    pl.when(j + 1 < pl.num_programs(0))(lambda: copy(j + 1, (j + 1) & 1).start())
    copy(j, j & 1).wait()
    o_ref[...] = buf[j & 1].astype(BF16)


def _pack_w_in(wt, *, split, width, tr):
    d = wt.shape[1]
    assert split % 8 == 0 and width % tr == 0 and split + width == wt.shape[0]
    n_a = width // tr
    return pl.pallas_call(
        functools.partial(_pack_w_in_kernel, n_a=n_a, split=split),
        grid=(2 * n_a,),
        in_specs=[pl.BlockSpec(memory_space=pl.ANY)],
        out_specs=pl.BlockSpec((tr, d), lambda j: (j, 0)),
        out_shape=jax.ShapeDtypeStruct((2 * width, d), BF16),
        scratch_shapes=[pltpu.VMEM((2, tr, d), F32), pltpu.SemaphoreType.DMA((2,))],
        compiler_params=_cparams(1, VMEM_LIMIT),
        name="pack_w_in",
    )(wt)


def _in_proj_kernel(x_ref, nw_ref, w_ref, o_ref, h_scr):
    @pl.when(pl.program_id(1) == 0)
    def _():
        h_scr[...] = _rms(x_ref[...], nw_ref[...]).astype(BF16)

    o_ref[...] = lax.dot_general(h_scr[...], w_ref[...], (((1,), (1,)), ((), ())), preferred_element_type=F32)


def _in_proj(x, norm_w, wt, *, tm, tn):
    t, d = x.shape
    n = wt.shape[0]
    return pl.pallas_call(
        _in_proj_kernel,
        grid=(t // tm, n // tn),
        in_specs=[pl.BlockSpec((tm, d), lambda i, j: (i, 0), pipeline_mode=pl.Buffered(1)),
                  pl.BlockSpec((1, d), lambda i, j: (0, 0)),
                  pl.BlockSpec((tn, d), lambda i, j: (j, 0))],
        out_specs=pl.BlockSpec((tm, tn), lambda i, j: (i, j)),
        out_shape=jax.ShapeDtypeStruct((t, n), F32),
        scratch_shapes=[pltpu.VMEM((tm, d), BF16)],
        compiler_params=_cparams(2, VMEM_LIMIT),
        name="in_proj",
    )(x, norm_w, wt)


def _out_proj_kernel(ya_ref, yb_ref, wa_ref, wb_ref, x_ref, o_ref):
    o_ref[...] = (x_ref[...] + jnp.dot(ya_ref[...], wa_ref[...], preferred_element_type=F32)
                  + jnp.dot(yb_ref[...], wb_ref[...], preferred_element_type=F32))


def _out_proj(ya, yb, w, x, *, tm, tn):
    t, kh = ya.shape
    n = w.shape[1]
    return pl.pallas_call(
        _out_proj_kernel,
        grid=(t // tm, n // tn),
        in_specs=[pl.BlockSpec((tm, kh), lambda i, j: (i, 0)),
                  pl.BlockSpec((tm, kh), lambda i, j: (i, 0)),
                  pl.BlockSpec((kh, tn), lambda i, j: (0, j)),
                  pl.BlockSpec((kh, tn), lambda i, j: (1, j)),
                  pl.BlockSpec((tm, tn), lambda i, j: (i, j))],
        out_specs=pl.BlockSpec((tm, tn), lambda i, j: (i, j)),
        out_shape=jax.ShapeDtypeStruct((t, n), F32),
        compiler_params=_cparams(2, VMEM_LIMIT),
        name="out_proj",
    )(ya, yb, w, w, x)


def _router_kernel(xp_ref, xs_ref, nw_ref, wr_ref, h_ref, lg_ref, *, nb_p):
    def emit(x_ref):
        h = _rms(x_ref[...], nw_ref[...])
        h_ref[...] = _pack_pairs(h, h.shape[1] // 2)
        lg_ref[...] = jnp.dot(h, wr_ref[...], preferred_element_type=F32, precision=lax.Precision.HIGHEST)

    pl.when(pl.program_id(0) < nb_p)(lambda: emit(xp_ref))
    pl.when(pl.program_id(0) >= nb_p)(lambda: emit(xs_ref))


def _router(xp, xs, norm_w, wr, *, tm):
    d = xp.shape[1]
    nb_p, nb_s = xp.shape[0] // tm, xs.shape[0] // tm
    t = xp.shape[0] + xs.shape[0]
    return pl.pallas_call(
        functools.partial(_router_kernel, nb_p=nb_p),
        grid=(nb_p + nb_s,),
        in_specs=[pl.BlockSpec((tm, d), lambda i: (jnp.minimum(i, nb_p - 1), 0)),
                  pl.BlockSpec((tm, d), lambda i: (jnp.maximum(i - nb_p, 0), 0)),
                  pl.BlockSpec((1, d), lambda i: (0, 0)),
                  pl.BlockSpec((d, LANES), lambda i: (0, 0))],
        out_specs=[pl.BlockSpec((tm, d // 2), lambda i: (i, 0)), pl.BlockSpec((tm, LANES), lambda i: (i, 0))],
        out_shape=[jax.ShapeDtypeStruct((t, d // 2), jnp.uint32), jax.ShapeDtypeStruct((t, LANES), F32)],
        compiler_params=_cparams(1, VMEM_LIMIT),
        name="router",
    )(xp, xs, norm_w, wr)


def _gather_rows(src_hbm, dst_vmem, idx_ref, base, n_rows, sem):
    def copy(r):
        return pltpu.make_async_copy(src_hbm.at[pl.ds(idx_ref[base + r], 1), :], dst_vmem.at[pl.ds(r, 1), :], sem)

    def start(r, carry):
        copy(r).start()
        return carry

    def wait(r, carry):
        copy(r).wait()
        return carry

    lax.fori_loop(0, n_rows, start, 0, unroll=8)
    lax.fori_loop(0, n_rows, wait, 0, unroll=8)


def _moe_gather_kernel(tok_ref, nused_ref, h_hbm, o_ref, buf, sem, *, tm):
    i = pl.program_id(0)

    @pl.when(i < nused_ref[0])
    def _():
        _gather_rows(h_hbm, buf, tok_ref, i * tm, tm, sem)
        o_ref[...] = _unpack_pairs(buf[...], buf.shape[1]).astype(o_ref.dtype)

    @pl.when(i >= nused_ref[0])
    def _():
        o_ref[...] = jnp.zeros_like(o_ref)


def _moe_gather(slot_tok, nused, h, *, tm):
    p = slot_tok.shape[0]
    half = h.shape[1]
    return pl.pallas_call(
        functools.partial(_moe_gather_kernel, tm=tm),
        grid_spec=pltpu.PrefetchScalarGridSpec(
            num_scalar_prefetch=2,
            grid=(p // tm,),
            in_specs=[pl.BlockSpec(memory_space=pl.ANY)],
            out_specs=pl.BlockSpec((tm, 2 * half), lambda i, tok, nu: (i, 0)),
            scratch_shapes=[pltpu.VMEM((tm, half), jnp.uint32), pltpu.SemaphoreType.DMA(())]),
        out_shape=jax.ShapeDtypeStruct((p, 2 * half), BF16),
        compiler_params=_cparams(1, VMEM_LIMIT),
        name="moe_gather",
    )(slot_tok, nused, h)


def _expert_row_loop(n_blocks, in_copy, out_copy, compute):
    in_copy(0, 0).start()

    def body(r, carry):
        slot = r & 1
        pl.when(r + 1 < n_blocks)(lambda: in_copy(r + 1, 1 - slot).start())
        in_copy(r, slot).wait()
        pl.when(r >= 2)(lambda: out_copy(r - 2, slot).wait())
        compute(slot)
        out_copy(r, slot).start()
        return carry

    lax.fori_loop(0, n_blocks, body, 0)
    pl.when(n_blocks >= 2)(lambda: out_copy(n_blocks - 2, n_blocks & 1).wait())
    out_copy(n_blocks - 1, (n_blocks - 1) & 1).wait()


def _moe_up_kernel(bstart_ref, bcount_ref, w1_ref, w3_ref, xs_hbm, _zeros_hbm, h_hbm, w1b, w3b, xbuf, obuf,
                   in_sem, out_sem, *, tm):
    f = pl.program_id(0)
    e = pl.program_id(1)
    n_blocks = bcount_ref[e]
    row = lambda r: pl.multiple_of((bstart_ref[e] + r) * tm, tm)

    @pl.when(n_blocks > 0)
    def _():
        w1b[...] = w1_ref[0].astype(BF16)
        w3b[...] = w3_ref[0].astype(BF16)

        def in_copy(r, slot):
            return pltpu.make_async_copy(xs_hbm.at[pl.ds(row(r), tm), :], xbuf.at[slot], in_sem.at[slot])

        def out_copy(r, slot):
            return pltpu.make_async_copy(obuf.at[slot], h_hbm.at[f, pl.ds(row(r), tm), :], out_sem.at[slot])

        def compute(slot):
            x = xbuf[slot]
            a = jnp.dot(x, w1b[...], preferred_element_type=F32)
            b = jnp.dot(x, w3b[...], preferred_element_type=F32)
            obuf[slot] = (a * _sigmoid(a) * b).astype(BF16)

        _expert_row_loop(n_blocks, in_copy, out_copy, compute)


def _moe_up(bstart, bcount, xs, w1, w3, *, tm, tf):
    p, d = xs.shape
    n_exp, _, ff = w1.shape
    wspec = pl.BlockSpec((1, d, tf), lambda f, e, bs, bc: (e, 0, f))
    return pl.pallas_call(
        functools.partial(_moe_up_kernel, tm=tm),
        grid_spec=pltpu.PrefetchScalarGridSpec(
            num_scalar_prefetch=2,
            grid=(ff // tf, n_exp),
            in_specs=[wspec, wspec, pl.BlockSpec(memory_space=pl.ANY), pl.BlockSpec(memory_space=pl.ANY)],
            out_specs=pl.BlockSpec(memory_space=pl.ANY),
            scratch_shapes=[pltpu.VMEM((d, tf), BF16), pltpu.VMEM((d, tf), BF16),
                            pltpu.VMEM((2, tm, d), BF16), pltpu.VMEM((2, tm, tf), BF16),
                            pltpu.SemaphoreType.DMA((2,)), pltpu.SemaphoreType.DMA((2,))]),
        out_shape=jax.ShapeDtypeStruct((ff // tf, p, tf), BF16),
        input_output_aliases={5: 0},
        compiler_params=_cparams(2, VMEM_LIMIT),
        name="moe_up",
    )(bstart, bcount, w1, w3, xs, jnp.zeros((ff // tf, p, tf), BF16))


def _moe_down_kernel(bstart_ref, bcount_ref, w2_ref, hs_hbm, _zeros_hbm, yb_hbm, w2b, hbuf, obuf, in_sem, out_sem,
                     *, tm, tn):
    e = pl.program_id(0)
    n_blocks = bcount_ref[e]
    row = lambda r: pl.multiple_of((bstart_ref[e] + r) * tm, tm)
    nf, _, tf = hbuf.shape[1:]

    @pl.when(n_blocks > 0)
    def _():
        w2b[...] = w2_ref[0].astype(BF16)

        def in_copy(r, slot):
            return pltpu.make_async_copy(hs_hbm.at[:, pl.ds(row(r), tm), :], hbuf.at[slot], in_sem.at[slot])

        def out_copy(r, slot):
            return pltpu.make_async_copy(obuf.at[slot], yb_hbm.at[pl.ds(row(r), tm), :], out_sem.at[slot])

        def compute(slot):
            for n in range(w2b.shape[1] // tn):
                y = jnp.zeros((tm, tn), F32)
                for f in range(nf):
                    y = y + jnp.dot(hbuf[slot, f], w2b[f * tf:(f + 1) * tf, n * tn:(n + 1) * tn],
                                    preferred_element_type=F32)
                obuf[slot, :, n * tn // 2:(n + 1) * tn // 2] = _pack_pairs(y, tn // 2)

        _expert_row_loop(n_blocks, in_copy, out_copy, compute)


def _moe_down(bstart, bcount, hs, w2, *, tm, tn):
    nf, p, tf = hs.shape
    n_exp, ff, d = w2.shape
    return pl.pallas_call(
        functools.partial(_moe_down_kernel, tm=tm, tn=tn),
        grid_spec=pltpu.PrefetchScalarGridSpec(
            num_scalar_prefetch=2,
            grid=(n_exp,),
            in_specs=[pl.BlockSpec((1, ff, d), lambda e, bs, bc: (e, 0, 0)), pl.BlockSpec(memory_space=pl.ANY),
                      pl.BlockSpec(memory_space=pl.ANY)],
            out_specs=pl.BlockSpec(memory_space=pl.ANY),
            scratch_shapes=[pltpu.VMEM((ff, d), BF16), pltpu.VMEM((2, nf, tm, tf), BF16),
                            pltpu.VMEM((2, tm, d // 2), jnp.uint32),
                            pltpu.SemaphoreType.DMA((2,)), pltpu.SemaphoreType.DMA((2,))]),
        out_shape=jax.ShapeDtypeStruct((p, d // 2), jnp.uint32),
        input_output_aliases={4: 0},
        compiler_params=_cparams(1, VMEM_LIMIT),
        name="moe_down",
    )(bstart, bcount, w2, hs, jnp.zeros((p, d // 2), jnp.uint32))


def _moe_combine_kernel(p0_ref, p1_ref, xp_ref, xs_ref, g0_ref, g1_ref, nw_ref, yb_hbm, op_ref, os_ref,
                        buf0, buf1, sem, *, tc, nb_p, half):
    i = pl.program_id(0)
    _gather_rows(yb_hbm, buf0, p0_ref, i * tc, tc, sem)
    _gather_rows(yb_hbm, buf1, p1_ref, i * tc, tc, sem)

    def emit(x_ref, o_ref):
        x = (x_ref[...] + g0_ref[:, 0:1] * _unpack_pairs(buf0[...], half)
             + g1_ref[:, 0:1] * _unpack_pairs(buf1[...], half))
        o_ref[...] = _rms(x, nw_ref[...])

    pl.when(i < nb_p)(lambda: emit(xp_ref, op_ref))
    pl.when(i >= nb_p)(lambda: emit(xs_ref, os_ref))


def _moe_combine(pos0, pos1, xp, xs, g0, g1, norm_w, yb, *, tc, half):
    d = xp.shape[1]
    nb_p, nb_s = xp.shape[0] // tc, xs.shape[0] // tc
    p_idx = lambda i, a, b: (jnp.minimum(i, nb_p - 1), 0)
    s_idx = lambda i, a, b: (jnp.maximum(i - nb_p, 0), 0)
    return pl.pallas_call(
        functools.partial(_moe_combine_kernel, tc=tc, nb_p=nb_p, half=half),
        grid_spec=pltpu.PrefetchScalarGridSpec(
            num_scalar_prefetch=2,
            grid=(nb_p + nb_s,),
            in_specs=[pl.BlockSpec((tc, d), p_idx),
                      pl.BlockSpec((tc, d), s_idx),
                      pl.BlockSpec((tc, LANES), lambda i, a, b: (i, 0)),
                      pl.BlockSpec((tc, LANES), lambda i, a, b: (i, 0)),
                      pl.BlockSpec((1, d), lambda i, a, b: (0, 0)),
                      pl.BlockSpec(memory_space=pl.ANY)],
            out_specs=[pl.BlockSpec((tc, d), p_idx), pl.BlockSpec((tc, d), s_idx)],
            scratch_shapes=[pltpu.VMEM((tc, d // 2), jnp.uint32), pltpu.VMEM((tc, d // 2), jnp.uint32),
                            pltpu.SemaphoreType.DMA(())]),
        out_shape=[jax.ShapeDtypeStruct(xp.shape, F32), jax.ShapeDtypeStruct(xs.shape, F32)],
        compiler_params=_cparams(1, VMEM_LIMIT),
        name="moe_combine",
    )(pos0, pos1, xp, xs, g0, g1, norm_w, yb)


def _route(logits, bg, be, *, tm):
    t = logits.shape[0]
    pg = jax.nn.softmax(logits[:, :N_GROUPS] + bg, axis=-1)
    g_idx = jnp.argmax(pg, axis=-1).astype(jnp.int32)
    p_sel = jnp.take_along_axis(pg, g_idx[:, None], axis=-1)
    le = (logits[:, N_GROUPS:N_GROUPS + N_EXPERTS] + be).reshape(t, N_GROUPS, EXPERTS_PER_GROUP)
    le = jnp.take_along_axis(le, g_idx[:, None, None], axis=1)[:, 0]
    pe = jax.nn.softmax(le, axis=-1)
    top_v, top_i = lax.top_k(pe, TOP_K)
    gate = p_sel * top_v / jnp.sum(top_v, axis=-1, keepdims=True)
    e_flat = (g_idx[:, None] * EXPERTS_PER_GROUP + top_i.astype(jnp.int32)).reshape(-1)
    a = t * TOP_K
    onehot = (e_flat[:, None] == jnp.arange(N_EXPERTS, dtype=jnp.int32)[None, :]).astype(jnp.int32)
    rank = jnp.sum((jnp.cumsum(onehot, axis=0) - onehot) * onehot, axis=1)
    counts = jnp.sum(onehot, axis=0)
    padded = (counts + tm - 1) // tm * tm
    pad_end = jnp.cumsum(padded)
    dest = (pad_end - padded)[e_flat] + rank
    n_blocks = -(-(a + N_EXPERTS * (tm - 1)) // tm)
    tok = jnp.arange(a, dtype=jnp.int32) // TOP_K
    slot_tok = jnp.zeros((n_blocks * tm,), jnp.int32).at[dest].set(tok)
    bstart = ((pad_end - padded) // tm).astype(jnp.int32)
    bcount = (padded // tm).astype(jnp.int32)
    nused = (pad_end[-1] // tm).astype(jnp.int32).reshape(1)
    dest = dest.reshape(t, TOP_K)
    return gate, dest[:, 0], dest[:, 1], slot_tok, bstart, bcount, nused


def _moe(xp, xs, norm2_w, wr, bg, be, w1, w3, w2, final_w, *, tm_route, tm_blk, tf, tn, tc):
    h, logits = _router(xp, xs, norm2_w, wr, tm=tm_route)
    gate, pos0, pos1, slot_tok, bstart, bcount, nused = _route(logits, bg, be, tm=tm_blk)
    xg = _moe_gather(slot_tok, nused, h, tm=tm_blk)
    hs = _moe_up(bstart, bcount, xg, w1, w3, tm=tm_blk, tf=tf)
    yb = _moe_down(bstart, bcount, hs, w2, tm=tm_blk, tn=tn)
    g0 = jnp.broadcast_to(gate[:, 0:1], (h.shape[0], LANES))
    g1 = jnp.broadcast_to(gate[:, 1:2], (h.shape[0], LANES))
    return _moe_combine(pos0, pos1, xp, xs, g0, g1, final_w, yb, tc=tc, half=tn // 2)


def kernel(x_prompt, x_sample, state_mlstm_c, state_mlstm_n, state_mlstm_m, state_rwkv, state_rwkv_shift, norm1_w, w_in, w_out, ml_b_i, ml_b_f, ml_norm_w, rw_mu, rw_w0, rw_w2, rw_a0, rw_a2, rw_g2, rw_k_k, rw_k_a, rw_r_k, rw_ln_w, rw_ln_b, norm2_w, router_group_w, router_group_b, router_expert_w, router_expert_b, moe_w1, moe_w3, moe_w2, final_norm_w):
    assert w_in.shape[0] == 1, "single-layer trunk"
    bp, seq, d = x_prompt.shape
    bs = x_sample.shape[0]
    tp = bp * seq
    ml_proj = 2 * ML_HEADS * ML_DK + 2 * ML_WIDTH + 2 * ML_HEADS
    rw_proj = 3 * RW_WIDTH + RW_LORA
    rw_col0 = -(-ml_proj // RW_GROUP) * RW_GROUP
    n_groups = RW_WIDTH // RW_GROUP

    w_cat = _pack_w_in(jnp.swapaxes(w_in[0], 0, 1), split=ml_proj, width=rw_col0, tr=RW_GROUP)
    w_o = w_out[0].astype(BF16)
    mu = rw_mu[0]
    rows = [rw_w0[0], rw_a0[0], rw_k_k[0], rw_k_a[0], rw_r_k[0].reshape(-1), rw_ln_w[0], rw_ln_b[0],
            mu[:RW_WIDTH], mu[RW_WIDTH:2 * RW_WIDTH], mu[2 * RW_WIDTH:3 * RW_WIDTH]]
    prm = jnp.concatenate([jnp.stack(rows), jnp.zeros((16 - len(rows), RW_WIDTH), F32)], axis=0)
    mu_l = mu[3 * RW_WIDTH:][None]
    wl = jnp.zeros((RW_LORA, 3, RW_WIDTH), F32)
    wl = wl.at[:RW_DECAY_LORA, 0].set(rw_w2[0]).at[RW_DECAY_LORA:RW_DECAY_LORA + RW_A_LORA, 1].set(rw_a2[0])
    wl = wl.at[RW_DECAY_LORA + RW_A_LORA:, 2].set(rw_g2[0])
    wl = wl.reshape(RW_LORA, 3, n_groups, RW_GROUP).transpose(2, 0, 1, 3).reshape(n_groups, RW_LORA, 3 * RW_GROUP)
    wl = wl.astype(BF16)
    gate_bias = jnp.zeros((1, LANES), F32).at[0, :ML_HEADS].set(ml_b_i[0]).at[0, ML_HEADS:2 * ML_HEADS].set(ml_b_f[0])
    ml_nw = ml_norm_w[0][None]
    wr = jnp.concatenate([router_group_w[0], router_expert_w[0],
                          jnp.zeros((d, LANES - N_GROUPS - N_EXPERTS), F32)], axis=1)

    xp = x_prompt.reshape(tp, d)
    xs = x_sample.reshape(bs, d)
    proj_p = _in_proj(xp, norm1_w, w_cat, tm=512, tn=1280)
    proj_s = _in_proj(xs, norm1_w, w_cat, tm=bs, tn=1280)

    y_ml_p, p_c, p_nm = _mlstm_prompt(proj_p, gate_bias, ml_nw, batch=bp, seq=seq)
    y_rw_p, p_s = _rwkv_prompt(proj_p, prm, mu_l, wl, batch=bp, seq=seq, rw_col0=rw_col0, n_groups=n_groups)
    p_sh = jnp.concatenate([lax.slice(proj_p, (b * seq + seq - 1, rw_col0), (b * seq + seq, rw_col0 + rw_proj))
                            for b in range(bp)], axis=0)

    y_ml_s, s_c, s_n, s_m = _mlstm_sample(proj_s[:, None, :], gate_bias, ml_nw, state_mlstm_c[0], state_mlstm_n[0],
                                          state_mlstm_m[0][:, None, :])
    r, k, v, g, rt, kt, vt, kkt, kat, dt = _rwkv_sample_prep(proj_s, state_rwkv_shift[0], prm, mu_l, wl, rows=bs,
                                                             rw_col0=rw_col0, n_groups=n_groups)
    s_t, y_t = _rwkv_sample_state(jnp.transpose(state_rwkv[0], (1, 2, 3, 0)), rt, kt, kkt, kat, dt, vt)
    s_s = jnp.transpose(s_t, (3, 0, 1, 2))
    y_rw_s = _rwkv_sample_post(y_t, r, k, v, g, prm)
    s_sh = proj_s[:, rw_col0:rw_col0 + rw_proj]

    x2_p = _out_proj(y_ml_p, y_rw_p, w_o, xp, tm=512, tn=1024)
    x2_s = _out_proj(y_ml_s.reshape(bs, ML_WIDTH), y_rw_s, w_o, xs, tm=bs, tn=1024)
    y_p, y_s = _moe(x2_p, x2_s, norm2_w, wr, router_group_b[0], router_expert_b[0], moe_w1[0], moe_w3[0],
                    moe_w2[0], final_norm_w[None], tm_route=128, tm_blk=256, tf=256, tn=1024, tc=128)

    lead = lambda a: a[None]
    return (y_p.reshape(bp, seq, d), y_s.reshape(bs, 1, d),
            lead(p_c), lead(p_nm[:, :, 0]), lead(p_nm[:, :, 1, 0]), lead(p_s), lead(p_sh),
            lead(s_c), lead(s_n), lead(s_m[:, 0]), lead(s_s), lead(s_sh))
```

```python
import functools

import jax
import jax.numpy as jnp
from jax import lax
from jax.experimental import pallas as pl
from jax.experimental.pallas import tpu as pltpu

F32 = jnp.float32
BF16 = jnp.bfloat16

D_MODEL = 4096
ML_HEADS = 4
ML_DK = 256
ML_DV = 512
ML_WIDTH = ML_HEADS * ML_DV
GATE_SOFTCAP = 15.0
RW_HEAD = 64
RW_WIDTH = D_MODEL - ML_WIDTH
RW_HEADS = RW_WIDTH // RW_HEAD
RW_DECAY_LORA = 96
RW_A_LORA = 96
RW_GATE_LORA = 64
RW_GN_EPS = 64e-5
N_GROUPS = 4
EXPERTS_PER_GROUP = 8
N_EXPERTS = N_GROUPS * EXPERTS_PER_GROUP
TOP_K = 2
EPS = 1e-6

LANES = 128
RW_GROUP = 256
RW_GHEADS = RW_GROUP // RW_HEAD
RW_LORA = RW_DECAY_LORA + RW_A_LORA + RW_GATE_LORA
RW_CHUNK = 64
ML_CHUNK = 256
_SAMPLE_ROWS_PER_STAGE = 8
VMEM_LIMIT = 56 * 1024 * 1024


def _cparams(n_axes, vmem=None):
    return pltpu.CompilerParams(dimension_semantics=("arbitrary",) * n_axes, vmem_limit_bytes=vmem)


def _sigmoid(x):
    return 1.0 / (1.0 + jnp.exp(-x))


def _softplus(x):
    return jnp.maximum(x, 0.0) + jnp.log(1.0 + jnp.exp(-jnp.abs(x)))


def _soft_cap(x):
    return GATE_SOFTCAP * jnp.tanh(x / GATE_SOFTCAP)


def _dot(a, b):
    return jnp.dot(a.astype(BF16), b.astype(BF16), preferred_element_type=F32)


def _dot_nt(a, b):
    return lax.dot_general(a.astype(BF16), b.astype(BF16), (((1,), (1,)), ((), ())),
                           preferred_element_type=F32)


def _split2(x):
    hi = x.astype(BF16)
    lo = (x - hi.astype(F32)).astype(BF16)
    return hi, lo


def _split3(x):
    hi = x.astype(BF16)
    r = x - hi.astype(F32)
    mid = r.astype(BF16)
    lo = (r - mid.astype(F32)).astype(BF16)
    return hi, mid, lo


def _iota(shape, axis):
    return lax.broadcasted_iota(jnp.int32, shape, axis)


def _pack_pairs(x, half):
    bits = lax.bitcast_convert_type(x.astype(BF16).astype(F32), jnp.uint32)
    parts = []
    for g in range(x.shape[1] // (2 * half)):
        lo = bits[:, 2 * half * g:2 * half * g + half]
        hi = bits[:, 2 * half * g + half:2 * half * (g + 1)]
        parts.append((lo >> 16) | hi)
    return parts[0] if len(parts) == 1 else jnp.concatenate(parts, axis=1)


def _unpack_pairs(p, half):
    parts = []
    for g in range(p.shape[1] // half):
        w = p[:, half * g:half * (g + 1)]
        parts.append(lax.bitcast_convert_type(w << 16, F32))
        parts.append(lax.bitcast_convert_type(w & jnp.uint32(0xFFFF0000), F32))
    return jnp.concatenate(parts, axis=1)


def _head_block_ones(n):
    return jnp.where((_iota((n, n), 0) >> 6) == (_iota((n, n), 1) >> 6), 1.0, 0.0).astype(BF16)


def _seg_sum(x, bd, passes=2):
    hi, lo = _split2(x)
    out = jnp.dot(hi, bd, preferred_element_type=F32)
    return out + jnp.dot(lo, bd, preferred_element_type=F32) if passes == 2 else out


_P_W0, _P_A0, _P_KK, _P_KA, _P_RK, _P_LNW, _P_LNB, _P_MUR, _P_MUK, _P_MUV = range(10)


def _rwkv_prep(p_r, p_k, p_v, p_l, prev_r, prev_k, prev_v, prev_l, prm, mu_l, wl, bd):
    row = lambda i: prm[i:i + 1, :]
    xr = p_r + (prev_r - p_r) * row(_P_MUR)
    xk = p_k + (prev_k - p_k) * row(_P_MUK)
    xv = p_v + (prev_v - p_v) * row(_P_MUV)
    xl = p_l + (prev_l - p_l) * mu_l
    lane = _iota(xl.shape, 1)
    z = jnp.where(lane < RW_DECAY_LORA, jnp.tanh(xl),
                  jnp.where(lane < RW_DECAY_LORA + RW_A_LORA, xl, _sigmoid(xl)))
    lo = _dot(z, wl)
    lw, la, lg = lo[:, :RW_GROUP], lo[:, RW_GROUP:2 * RW_GROUP], lo[:, 2 * RW_GROUP:]
    w = -_softplus(-(row(_P_W0) + lw)) - 0.5
    logd = -jnp.exp(w)
    a = _sigmoid(row(_P_A0) + la)
    kk = xk * row(_P_KK)
    kk = kk / jnp.maximum(jnp.sqrt(_seg_sum(kk * kk, bd)), 1e-12)
    k = xk * (1.0 + (a - 1.0) * row(_P_KA))
    return xr, k, xv, kk, a, logd, lg


def _rwkv_post(y, r, k, v, g, prm, bd):
    row = lambda i: prm[i:i + 1, :]
    mean = _seg_sum(y, bd, passes=1) * (1.0 / RW_HEAD)
    yc = y - mean
    var = _seg_sum(yc * yc, bd, passes=1) * (1.0 / RW_HEAD)
    yn = yc * lax.rsqrt(var + RW_GN_EPS) * row(_P_LNW) + row(_P_LNB)
    bonus = _seg_sum(r * k * row(_P_RK), bd, passes=1) * v
    return (yn + bonus) * g


def _rwkv_chunk(r, k, v, kk, a, logd, G, bd):
    L = r.shape[0]
    tril = jnp.where(_iota((L, L), 1) <= _iota((L, L), 0), 1.0, 0.0).astype(BF16)
    d_hi, d_lo = _split2(logd)
    clog = (jnp.dot(tril, d_hi, preferred_element_type=F32)
            + jnp.dot(tril, d_lo, preferred_element_type=F32))
    clog_l = clog[L - 1:L, :]
    n_in = jnp.exp(-clog)
    to_end = jnp.exp(clog_l - clog)
    ka = kk * a
    at = -kk * jnp.exp(clog - logd)
    rt = r * jnp.exp(clog)
    lane_head = _iota((1, RW_GROUP), 1) >> 6
    masks = [lane_head == h for h in range(RW_GHEADS)]
    zero = jnp.zeros_like(at)
    bf = lambda x: x.astype(BF16)
    lhs = jnp.concatenate([jnp.where(m, x, zero) for m in masks for x in (at, rt)], axis=0)
    xx = _dot_nt(lhs, jnp.concatenate([ka * n_in, k * n_in], axis=0))
    t2 = _iota((L, 2 * L), 0)
    c2 = _iota((L, 2 * L), 1)
    s2 = c2 & (L - 1)
    right = c2 >= L
    eye_pad = jnp.where(c2 == t2 + L, 1.0, 0.0)
    zeros_v = jnp.zeros((L, RW_GROUP), BF16)
    v_b = bf(v)
    vz = jnp.concatenate([zeros_v, v_b], axis=0)
    yield
    zs, rbk = [], []
    makv = zero
    for h, m in enumerate(masks):
        o = 2 * L * h
        nk = jnp.where(s2 < t2, xx[o:o + L], 0.0)
        rbk.append(bf(jnp.where(s2 <= t2, xx[o + L:o + 2 * L], 0.0)))
        makv = makv + jnp.where(m, _dot(jnp.where(right, nk, 0.0), vz), 0.0)
        zs.append(jnp.where(right, eye_pad, nk))
    for _ in range((L - 1).bit_length()):
        yield
        zs = [_dot(z[:, :L], z) + jnp.where(right, z, 0.0) for z in zs]
    yield
    ws = _dot_nt(jnp.concatenate([at, rt], axis=0), G)
    xz = jnp.concatenate([zeros_v, bf(ws[:L] + makv)], axis=0)
    yield
    w_all = zero
    for z_h, m in zip(zs, masks):
        w_all = w_all + jnp.where(m, _dot(z_h, xz), 0.0)
    wv = jnp.concatenate([bf(w_all), v_b], axis=0)
    yield
    y = ws[L:]
    for rbk_h, m in zip(rbk, masks):
        y = y + jnp.where(m, jnp.dot(rbk_h, wv, preferred_element_type=F32), 0.0)
    upd = _dot(wv.astype(F32).T, jnp.concatenate([ka * to_end, k * to_end], axis=0))
    g_new = G * jnp.exp(clog_l) + jnp.where(bd > 0, upd, 0.0)
    return y, g_new


def _interleave(gens):
    results = [None] * len(gens)
    live = list(range(len(gens)))
    while live:
        for i in list(live):
            try:
                next(gens[i])
            except StopIteration as stop:
                results[i] = stop.value
                live.remove(i)
    return results


def _shift_rows(x, first_row):
    rolled = pltpu.roll(x, 1, axis=0)
    return jnp.where(_iota(x.shape, 0) == 0, first_row, rolled)


def _rwkv_prompt_kernel(*refs, n_chunks, batch):
    p_refs = refs[:4 * batch]
    prm_ref, mul_ref, wl_ref, y_ref, s_ref, g_scr, carry_scr = refs[4 * batch:]
    c = pl.program_id(1)

    @pl.when(c == 0)
    def _():
        g_scr[...] = jnp.zeros_like(g_scr)
        carry_scr[...] = jnp.zeros_like(carry_scr)

    bd = _head_block_ones(RW_GROUP)
    prm = prm_ref[...]

    def sequence(b):
        ps = [ref[...] for ref in p_refs[4 * b:4 * b + 4]]
        L = ps[0].shape[0]
        prevs = [_shift_rows(x, carry_scr[b, i:i + 1, :]) for i, x in enumerate(ps)]
        r, k, v, kk, a, logd, g = _rwkv_prep(*ps, *prevs, prm, mul_ref[...], wl_ref[0], bd)
        y, g_new = yield from _rwkv_chunk(r, k, v, kk, a, logd, g_scr[b], bd)
        yield
        out = _rwkv_post(y, r, k, v, g, prm, bd).astype(y_ref.dtype)
        return out, g_new, [x[L - 1:L, :] for x in ps]

    results = _interleave([sequence(b) for b in range(batch)])
    finals = []
    for b, (out, g_new, last_rows) in enumerate(results):
        y_ref[b] = out
        g_scr[b] = g_new
        for i, x in enumerate(last_rows):
            carry_scr[b, i:i + 1, :] = x
        finals.append(g_new)

    @pl.when(c == n_chunks - 1)
    def _():
        for b in range(batch):
            for h in range(RW_GHEADS):
                s_ref[b, h] = finals[b][RW_HEAD * h:RW_HEAD * (h + 1), RW_HEAD * h:RW_HEAD * (h + 1)]


def _rwkv_prompt(proj, prm, mu_l, wl, *, batch, seq, rw_col0, n_groups):
    L = RW_CHUNK
    nc = seq // L
    cb = rw_col0 // RW_GROUP
    width = n_groups * RW_GROUP
    pspec = lambda b, off: pl.BlockSpec((L, RW_GROUP), lambda g, c, b=b, off=off: (b * nc + c, cb + off + g))
    lspec = lambda b: pl.BlockSpec((L, RW_GROUP), lambda g, c, b=b: (b * nc + c, cb + 3 * n_groups))
    p_specs = []
    for b in range(batch):
        p_specs += [pspec(b, 0), pspec(b, n_groups), pspec(b, 2 * n_groups), lspec(b)]
    y, s = pl.pallas_call(
        functools.partial(_rwkv_prompt_kernel, n_chunks=nc, batch=batch),
        grid=(n_groups, nc),
        in_specs=p_specs + [pl.BlockSpec((16, RW_GROUP), lambda g, c: (0, g)),
                            pl.BlockSpec((1, RW_LORA), lambda g, c: (0, 0)),
                            pl.BlockSpec((1, RW_LORA, 3 * RW_GROUP), lambda g, c: (g, 0, 0))],
        out_specs=[pl.BlockSpec((batch, L, RW_GROUP), lambda g, c: (0, c, g)),
                   pl.BlockSpec((batch, RW_GHEADS, RW_HEAD, RW_HEAD), lambda g, c: (0, g, 0, 0))],
        out_shape=[jax.ShapeDtypeStruct((batch, seq, width), BF16),
                   jax.ShapeDtypeStruct((batch, n_groups * RW_GHEADS, RW_HEAD, RW_HEAD), F32)],
        scratch_shapes=[pltpu.VMEM((batch, RW_GROUP, RW_GROUP), F32), pltpu.VMEM((batch, 8, RW_GROUP), F32)],
        compiler_params=_cparams(2),
        name="rwkv_prompt",
    )(*([proj] * (4 * batch)), prm, mu_l, wl)
    return y.reshape(batch * seq, width), s


def _rwkv_sample_prep_kernel(pr_ref, pk_ref, pv_ref, pl_ref, sr_ref, sk_ref, sv_ref, sl_ref, prm_ref, mul_ref,
                             wl_ref, r_ref, k_ref, v_ref, g_ref, rt_ref, kt_ref, vt_ref, kkt_ref, kat_ref, dt_ref):
    bd = _head_block_ones(RW_GROUP)
    r, k, v, kk, a, logd, g = _rwkv_prep(pr_ref[...], pk_ref[...], pv_ref[...], pl_ref[...], sr_ref[...],
                                         sk_ref[...], sv_ref[...], sl_ref[...], prm_ref[...], mul_ref[...],
                                         wl_ref[0], bd)
    r_ref[...] = r
    k_ref[...] = k
    v_ref[...] = v
    g_ref[...] = g
    rt_ref[...] = r.T
    kt_ref[...] = k.T
    vt_ref[...] = v.T
    kkt_ref[...] = kk.T
    kat_ref[...] = (kk * a).T
    dt_ref[...] = jnp.exp(logd).T


def _rwkv_sample_prep(proj, shift0, prm, mu_l, wl, *, rows, rw_col0, n_groups):
    cb = rw_col0 // RW_GROUP
    pspec = lambda off: pl.BlockSpec((rows, RW_GROUP), lambda g, off=off: (0, cb + off + g))
    sspec = lambda off: pl.BlockSpec((rows, RW_GROUP), lambda g, off=off: (0, off + g))
    ospec = pl.BlockSpec((rows, RW_GROUP), lambda g: (0, g))
    tspec = pl.BlockSpec((RW_GROUP, rows), lambda g: (g, 0))
    width = n_groups * RW_GROUP
    return pl.pallas_call(
        _rwkv_sample_prep_kernel,
        grid=(n_groups,),
        in_specs=[pspec(0), pspec(n_groups), pspec(2 * n_groups),
                  pl.BlockSpec((rows, RW_GROUP), lambda g: (0, cb + 3 * n_groups)),
                  sspec(0), sspec(n_groups), sspec(2 * n_groups),
                  pl.BlockSpec((rows, RW_GROUP), lambda g: (0, 3 * n_groups)),
                  pl.BlockSpec((16, RW_GROUP), lambda g: (0, g)),
                  pl.BlockSpec((1, RW_LORA), lambda g: (0, 0)),
                  pl.BlockSpec((1, RW_LORA, 3 * RW_GROUP), lambda g: (g, 0, 0))],
        out_specs=[ospec] * 4 + [tspec] * 6,
        out_shape=[jax.ShapeDtypeStruct((rows, width), F32)] * 4 + [jax.ShapeDtypeStruct((width, rows), F32)] * 6,
        compiler_params=_cparams(1),
        name="rwkv_sample_prep",
    )(proj, proj, proj, proj, shift0, shift0, shift0, shift0, prm, mu_l, wl)


def _rwkv_sample_state_kernel(s_ref, rt_ref, kt_ref, kkt_ref, kat_ref, dt_ref, vt_ref, so_ref, yt_ref):
    n_kk = -kkt_ref[...]
    r, k, ka, d = rt_ref[...], kt_ref[...], kat_ref[...], dt_ref[...]
    for i0 in range(0, s_ref.shape[1], _SAMPLE_ROWS_PER_STAGE):
        rows = range(i0, i0 + _SAMPLE_ROWS_PER_STAGE)
        sa = [jnp.sum(s_ref[0, i] * n_kk, axis=0, keepdims=True) for i in rows]
        ys = []
        for i, sa_i in zip(rows, sa):
            s_new = s_ref[0, i] * d + sa_i * ka + vt_ref[i:i + 1, :] * k
            so_ref[0, i] = s_new
            ys.append(jnp.sum(s_new * r, axis=0, keepdims=True))
        yt_ref[i0:i0 + _SAMPLE_ROWS_PER_STAGE, :] = jnp.concatenate(ys, axis=0)


def _rwkv_sample_state(state_t, rt, kt, kkt, kat, dt, vt):
    nh, n, _, b = state_t.shape
    vspec = pl.BlockSpec((n, b), lambda h: (h, 0))
    sspec = pl.BlockSpec((1, n, n, b), lambda h: (h, 0, 0, 0))
    return pl.pallas_call(
        _rwkv_sample_state_kernel,
        grid=(nh,),
        in_specs=[sspec] + [vspec] * 6,
        out_specs=[sspec, vspec],
        out_shape=[jax.ShapeDtypeStruct(state_t.shape, F32), jax.ShapeDtypeStruct((nh * n, b), F32)],
        compiler_params=_cparams(1),
        name="rwkv_sample_state",
    )(state_t, rt, kt, kkt, kat, dt, vt)


def _rwkv_sample_post_kernel(yt_ref, r_ref, k_ref, v_ref, g_ref, prm_ref, o_ref):
    bd = _head_block_ones(RW_GROUP)
    o_ref[...] = _rwkv_post(yt_ref[...].T, r_ref[...], k_ref[...], v_ref[...], g_ref[...], prm_ref[...],
                            bd).astype(o_ref.dtype)


def _rwkv_sample_post(yt, r, k, v, g, prm):
    rows, width = r.shape
    spec = pl.BlockSpec((rows, RW_GROUP), lambda i: (0, i))
    return pl.pallas_call(
        _rwkv_sample_post_kernel,
        grid=(width // RW_GROUP,),
        in_specs=[pl.BlockSpec((RW_GROUP, rows), lambda i: (i, 0))] + [spec] * 4
                 + [pl.BlockSpec((16, RW_GROUP), lambda i: (0, i))],
        out_specs=spec,
        out_shape=jax.ShapeDtypeStruct((rows, width), BF16),
        compiler_params=_cparams(1),
        name="rwkv_sample_post",
    )(yt, r, k, v, g, prm)


def _mlstm_gates(gates, bias, h):
    capped = _soft_cap(gates + bias)
    lane = _iota(gates.shape, 1)
    i_col = jnp.sum(jnp.where(lane == h, capped, 0.0), axis=1, keepdims=True)
    f_col = jnp.sum(jnp.where(lane == h + ML_HEADS, -_softplus(-capped), 0.0), axis=1, keepdims=True)
    return i_col, f_col


def _mlstm_out(hh, o, norm_w):
    hn = hh * lax.rsqrt(jnp.mean(hh * hh, axis=1, keepdims=True) + EPS) * norm_w
    return hn * _sigmoid(o)


def _mlstm_prompt_kernel(*refs, n_chunks, batch):
    p_refs = refs[:5 * batch]
    gb_ref, nw_ref, y_ref, c_ref, nm_ref, c_scr, n_scr, m_scr = refs[5 * batch:]
    h = pl.program_id(0)
    c = pl.program_id(1)

    @pl.when(c == 0)
    def _():
        c_scr[...] = jnp.zeros_like(c_scr)
        n_scr[...] = jnp.zeros_like(n_scr)
        m_scr[...] = jnp.zeros_like(m_scr)

    def sequence(b):
        q_ref, k_ref, v_ref, o_ref, gt_ref = p_refs[5 * b:5 * b + 5]
        q = q_ref[...] * (ML_DK ** -0.5)
        k = k_ref[...]
        v = v_ref[...]
        L = q.shape[0]
        i_col, f_col = _mlstm_gates(gt_ref[...], gb_ref[...], h)
        t_i = _iota((L, L), 0)
        s_i = _iota((L, L), 1)
        causal = s_i <= t_i
        to_row = lambda col: jnp.sum(jnp.where(t_i == s_i, col, 0.0), axis=0, keepdims=True)
        i_row = to_row(i_col)
        f_row = to_row(f_col)
        yield
        b_col = jnp.sum(jnp.where(causal, f_row, 0.0), axis=1, keepdims=True)
        b_row = jnp.sum(jnp.where(t_i <= s_i, f_col, 0.0), axis=0, keepdims=True)
        m0 = m_scr[b]
        inter = b_col + m0
        dmat = jnp.where(causal, b_col - b_row + i_row, -1e30)
        yield
        m_t = jnp.maximum(inter, jnp.max(dmat, axis=1, keepdims=True))
        w_int = jnp.exp(inter - m_t)
        s = _dot_nt(q, k) * jnp.exp(dmat - m_t)
        c0 = c_scr[b]
        n0 = n_scr[b]
        yield
        num = w_int * _dot(q, c0) + _dot(s, v)
        den = w_int * jnp.sum(q * n0, axis=1, keepdims=True) + jnp.sum(s, axis=1, keepdims=True)
        hh = num / jnp.maximum(jnp.abs(den), jnp.exp(-m_t))
        m_new = m_t[L - 1:L, :]
        b_l = b_col[L - 1:L, :]
        a_end = jnp.exp(b_l - b_col + i_col - m_new)
        dec = jnp.exp(b_l + m0 - m_new)
        ka = k * a_end
        yield
        c_new = dec * c0 + _dot(ka.T, v)
        n_new = dec * n0 + jnp.sum(ka, axis=0, keepdims=True)
        out = _mlstm_out(hh, o_ref[...], nw_ref[...]).astype(y_ref.dtype)
        return out, c_new, n_new, m_new

    results = _interleave([sequence(b) for b in range(batch)])
    for b, (out, c_new, n_new, m_new) in enumerate(results):
        y_ref[b] = out
        c_scr[b] = c_new
        n_scr[b] = n_new
        m_scr[b] = m_new

    @pl.when(c == n_chunks - 1)
    def _():
        for b, (_, c_new, n_new, m_new) in enumerate(results):
            c_ref[b, 0] = c_new
            nm_ref[b, 0] = jnp.concatenate([n_new, jnp.broadcast_to(m_new, (7, ML_DK))], axis=0)


def _mlstm_prompt(proj, gate_bias, norm_w, *, batch, seq):
    L = ML_CHUNK
    nc = seq // L
    nh = ML_HEADS
    kq = nh * ML_DK
    p_specs = []
    for b in range(batch):
        rows = lambda c, b=b: b * nc + c
        p_specs += [pl.BlockSpec((L, ML_DK), lambda h, c, rows=rows: (rows(c), h)),
                    pl.BlockSpec((L, ML_DK), lambda h, c, rows=rows: (rows(c), nh + h)),
                    pl.BlockSpec((L, ML_DV), lambda h, c, rows=rows: (rows(c), 2 * kq // ML_DV + h)),
                    pl.BlockSpec((L, ML_DV), lambda h, c, rows=rows: (rows(c), 2 * kq // ML_DV + nh + h)),
                    pl.BlockSpec((L, LANES), lambda h, c, rows=rows: (rows(c), (2 * kq + 2 * ML_WIDTH) // LANES))]
    y, c_fin, nm = pl.pallas_call(
        functools.partial(_mlstm_prompt_kernel, n_chunks=nc, batch=batch),
        grid=(nh, nc),
        in_specs=p_specs + [pl.BlockSpec((1, LANES), lambda h, c: (0, 0)),
                            pl.BlockSpec((1, ML_DV), lambda h, c: (0, h))],
        out_specs=[pl.BlockSpec((batch, L, ML_DV), lambda h, c: (0, c, h)),
                   pl.BlockSpec((batch, 1, ML_DK, ML_DV), lambda h, c: (0, h, 0, 0)),
                   pl.BlockSpec((batch, 1, 8, ML_DK), lambda h, c: (0, h, 0, 0))],
        out_shape=[jax.ShapeDtypeStruct((batch, seq, ML_WIDTH), BF16),
                   jax.ShapeDtypeStruct((batch, nh, ML_DK, ML_DV), F32),
                   jax.ShapeDtypeStruct((batch, nh, 8, ML_DK), F32)],
        scratch_shapes=[pltpu.VMEM((batch, ML_DK, ML_DV), F32), pltpu.VMEM((batch, 1, ML_DK), F32),
                        pltpu.VMEM((batch, 1, 1), F32)],
        compiler_params=_cparams(2),
        name="mlstm_prompt",
    )(*([proj] * (5 * batch)), gate_bias, norm_w)
    return y.reshape(batch * seq, ML_WIDTH), c_fin, nm


def _mlstm_sample_kernel(q_ref, k_ref, v_ref, o_ref, gt_ref, gb_ref, nw_ref, c_ref, n_ref, m_ref,
                         y_ref, co_ref, no_ref, mo_ref):
    gates = gt_ref[0]
    eye = jnp.where(_iota((ML_DK, ML_DK), 0) == _iota((ML_DK, ML_DK), 1), 1.0, 0.0).astype(BF16)
    for h in range(ML_HEADS):
        q = q_ref[0][:, h * ML_DK:(h + 1) * ML_DK] * (ML_DK ** -0.5)
        k = k_ref[0][:, h * ML_DK:(h + 1) * ML_DK]
        v = v_ref[0][:, h * ML_DV:(h + 1) * ML_DV]
        i_pre, logf = _mlstm_gates(gates, gb_ref[...], h)
        c0 = c_ref[0, h]
        n0 = n_ref[0, h:h + 1, :]
        m0 = m_ref[0][:, h:h + 1]
        inter = logf + m0
        m_t = jnp.maximum(inter, i_pre)
        w_int = jnp.exp(inter - m_t)
        a_new = jnp.exp(i_pre - m_t)
        s = jnp.sum(q * k, axis=1, keepdims=True) * a_new
        qc = _dot(jnp.broadcast_to(q, (8, ML_DK)), c0)[0:1, :]
        num = w_int * qc + s * v
        den = w_int * jnp.sum(q * n0, axis=1, keepdims=True) + s
        hh = num / jnp.maximum(jnp.abs(den), jnp.exp(-m_t))
        k_hi, k_lo = _split2(jnp.broadcast_to(k, (8, ML_DK)))
        nt = (((1,), (1,)), ((), ()))
        k_col = (lax.dot_general(eye, k_hi, nt, preferred_element_type=F32)
                 + lax.dot_general(eye, k_lo, nt, preferred_element_type=F32))[:, 0:1]
        co_ref[0, h] = w_int * c0 + k_col * (a_new * v)
        no_ref[0, h:h + 1, :] = w_int * n0 + a_new * k
        mo_ref[0, :, h:h + 1] = m_t
        y_ref[0, :, h * ML_DV:(h + 1) * ML_DV] = _mlstm_out(
            hh, o_ref[0][:, h * ML_DV:(h + 1) * ML_DV], nw_ref[:, h * ML_DV:(h + 1) * ML_DV]).astype(y_ref.dtype)


def _mlstm_sample(proj3, gate_bias, norm_w, c0, n0, m0):
    b = proj3.shape[0]
    nh = ML_HEADS
    kq = nh * ML_DK
    p3 = lambda w, blk: pl.BlockSpec((1, 1, w), lambda i, blk=blk: (i, 0, blk))
    cspec = pl.BlockSpec((1, nh, ML_DK, ML_DV), lambda i: (i, 0, 0, 0))
    nspec = pl.BlockSpec((1, nh, ML_DK), lambda i: (i, 0, 0))
    mspec = pl.BlockSpec((1, 1, nh), lambda i: (i, 0, 0))
    return pl.pallas_call(
        _mlstm_sample_kernel,
        grid=(b,),
        in_specs=[p3(kq, 0), p3(kq, 1), p3(ML_WIDTH, 2 * kq // ML_WIDTH), p3(ML_WIDTH, 2 * kq // ML_WIDTH + 1),
                  p3(LANES, (2 * kq + 2 * ML_WIDTH) // LANES),
                  pl.BlockSpec((1, LANES), lambda i: (0, 0)),
                  pl.BlockSpec((1, ML_WIDTH), lambda i: (0, 0)),
                  cspec, nspec, mspec],
        out_specs=[pl.BlockSpec((1, 1, ML_WIDTH), lambda i: (i, 0, 0)), cspec, nspec, mspec],
        out_shape=[jax.ShapeDtypeStruct((b, 1, ML_WIDTH), BF16),
                   jax.ShapeDtypeStruct(c0.shape, F32), jax.ShapeDtypeStruct(n0.shape, F32),
                   jax.ShapeDtypeStruct(m0.shape, F32)],
        compiler_params=_cparams(1),
        name="mlstm_sample",
    )(proj3, proj3, proj3, proj3, proj3, gate_bias, norm_w, c0, n0, m0)


def _rms(x, w):
    return x * lax.rsqrt(jnp.mean(x * x, axis=1, keepdims=True) + EPS) * w


def _pack_w_in_kernel(wt_hbm, o_ref, buf, sem, *, n_a, split):
    j = pl.program_id(0)
    tr = o_ref.shape[0]

    def copy(step, slot):
        src = jnp.where(step < n_a, step * tr, split + (step - n_a) * tr)
        return pltpu.make_async_copy(wt_hbm.at[pl.ds(pl.multiple_of(src, 8), tr), :], buf.at[slot], sem.at[slot])

    pl.when(j == 0)(lambda: copy(j, 0).start())
    pl.when(j + 1 < pl.num_programs(0))(lambda: copy(j + 1, (j + 1) & 1).start())
    copy(j, j & 1).wait()
    o_ref[...] = buf[j & 1].astype(BF16)


def _pack_w_in(wt, *, split, width, tr):
    d = wt.shape[1]
    assert split % 8 == 0 and width % tr == 0 and split + width == wt.shape[0]
    n_a = width // tr
    return pl.pallas_call(
        functools.partial(_pack_w_in_kernel, n_a=n_a, split=split),
        grid=(2 * n_a,),
        in_specs=[pl.BlockSpec(memory_space=pl.ANY)],
        out_specs=pl.BlockSpec((tr, d), lambda j: (j, 0)),
        out_shape=jax.ShapeDtypeStruct((2 * width, d), BF16),
        scratch_shapes=[pltpu.VMEM((2, tr, d), F32), pltpu.SemaphoreType.DMA((2,))],
        compiler_params=_cparams(1, VMEM_LIMIT),
        name="pack_w_in",
    )(wt)


def _in_proj_kernel(x_ref, nw_ref, w_ref, o_ref, h_scr):
    @pl.when(pl.program_id(1) == 0)
    def _():
        h_scr[...] = _rms(x_ref[...], nw_ref[...]).astype(BF16)

    o_ref[...] = lax.dot_general(h_scr[...], w_ref[...], (((1,), (1,)), ((), ())), preferred_element_type=F32)


def _in_proj(x, norm_w, wt, *, tm, tn):
    t, d = x.shape
    n = wt.shape[0]
    return pl.pallas_call(
        _in_proj_kernel,
        grid=(t // tm, n // tn),
        in_specs=[pl.BlockSpec((tm, d), lambda i, j: (i, 0), pipeline_mode=pl.Buffered(1)),
                  pl.BlockSpec((1, d), lambda i, j: (0, 0)),
                  pl.BlockSpec((tn, d), lambda i, j: (j, 0))],
        out_specs=pl.BlockSpec((tm, tn), lambda i, j: (i, j)),
        out_shape=jax.ShapeDtypeStruct((t, n), F32),
        scratch_shapes=[pltpu.VMEM((tm, d), BF16)],
        compiler_params=_cparams(2, VMEM_LIMIT),
        name="in_proj",
    )(x, norm_w, wt)


def _out_proj_kernel(ya_ref, yb_ref, wa_ref, wb_ref, x_ref, o_ref):
    o_ref[...] = (x_ref[...] + jnp.dot(ya_ref[...], wa_ref[...], preferred_element_type=F32)
                  + jnp.dot(yb_ref[...], wb_ref[...], preferred_element_type=F32))


def _out_proj(ya, yb, w, x, *, tm, tn):
    t, kh = ya.shape
    n = w.shape[1]
    return pl.pallas_call(
        _out_proj_kernel,
        grid=(t // tm, n // tn),
        in_specs=[pl.BlockSpec((tm, kh), lambda i, j: (i, 0)),
                  pl.BlockSpec((tm, kh), lambda i, j: (i, 0)),
                  pl.BlockSpec((kh, tn), lambda i, j: (0, j)),
                  pl.BlockSpec((kh, tn), lambda i, j: (1, j)),
                  pl.BlockSpec((tm, tn), lambda i, j: (i, j))],
        out_specs=pl.BlockSpec((tm, tn), lambda i, j: (i, j)),
        out_shape=jax.ShapeDtypeStruct((t, n), F32),
        compiler_params=_cparams(2, VMEM_LIMIT),
        name="out_proj",
    )(ya, yb, w, w, x)


def _router_kernel(xp_ref, xs_ref, nw_ref, wr_ref, h_ref, lg_ref, *, nb_p):
    def emit(x_ref):
        h = _rms(x_ref[...], nw_ref[...])
        h_ref[...] = _pack_pairs(h, h.shape[1] // 2)
        lg_ref[...] = jnp.dot(h, wr_ref[...], preferred_element_type=F32, precision=lax.Precision.HIGHEST)

    pl.when(pl.program_id(0) < nb_p)(lambda: emit(xp_ref))
    pl.when(pl.program_id(0) >= nb_p)(lambda: emit(xs_ref))


def _router(xp, xs, norm_w, wr, *, tm):
    d = xp.shape[1]
    nb_p, nb_s = xp.shape[0] // tm, xs.shape[0] // tm
    t = xp.shape[0] + xs.shape[0]
    return pl.pallas_call(
        functools.partial(_router_kernel, nb_p=nb_p),
        grid=(nb_p + nb_s,),
        in_specs=[pl.BlockSpec((tm, d), lambda i: (jnp.minimum(i, nb_p - 1), 0)),
                  pl.BlockSpec((tm, d), lambda i: (jnp.maximum(i - nb_p, 0), 0)),
                  pl.BlockSpec((1, d), lambda i: (0, 0)),
                  pl.BlockSpec((d, LANES), lambda i: (0, 0))],
        out_specs=[pl.BlockSpec((tm, d // 2), lambda i: (i, 0)), pl.BlockSpec((tm, LANES), lambda i: (i, 0))],
        out_shape=[jax.ShapeDtypeStruct((t, d // 2), jnp.uint32), jax.ShapeDtypeStruct((t, LANES), F32)],
        compiler_params=_cparams(1, VMEM_LIMIT),
        name="router",
    )(xp, xs, norm_w, wr)


def _gather_rows(src_hbm, dst_vmem, idx_ref, base, n_rows, sem):
    def copy(r):
        return pltpu.make_async_copy(src_hbm.at[pl.ds(idx_ref[base + r], 1), :], dst_vmem.at[pl.ds(r, 1), :], sem)

    def start_pair(q, carry):
        copy(2 * q).start(priority=0)
        copy(2 * q + 1).start(priority=1)
        return carry

    def wait(r, carry):
        copy(r).wait()
        return carry

    assert n_rows % 2 == 0
    lax.fori_loop(0, n_rows // 2, start_pair, 0, unroll=4)
    lax.fori_loop(0, n_rows, wait, 0, unroll=8)


def _moe_gather_kernel(tok_ref, nused_ref, h_hbm, o_ref, buf, sem, *, tm):
    i = pl.program_id(0)

    @pl.when(i < nused_ref[0])
    def _():
        _gather_rows(h_hbm, buf, tok_ref, i * tm, tm, sem)
        o_ref[...] = _unpack_pairs(buf[...], buf.shape[1]).astype(o_ref.dtype)

    @pl.when(i >= nused_ref[0])
    def _():
        o_ref[...] = jnp.zeros_like(o_ref)


def _moe_gather(slot_tok, nused, h, *, tm):
    p = slot_tok.shape[0]
    half = h.shape[1]
    return pl.pallas_call(
        functools.partial(_moe_gather_kernel, tm=tm),
        grid_spec=pltpu.PrefetchScalarGridSpec(
            num_scalar_prefetch=2,
            grid=(p // tm,),
            in_specs=[pl.BlockSpec(memory_space=pl.ANY)],
            out_specs=pl.BlockSpec((tm, 2 * half), lambda i, tok, nu: (i, 0)),
            scratch_shapes=[pltpu.VMEM((tm, half), jnp.uint32), pltpu.SemaphoreType.DMA(())]),
        out_shape=jax.ShapeDtypeStruct((p, 2 * half), BF16),
        compiler_params=_cparams(1, VMEM_LIMIT),
        name="moe_gather",
    )(slot_tok, nused, h)


def _expert_row_loop(n_blocks, in_copy, out_copy, compute):
    in_copy(0, 0).start()

    def body(r, carry):
        slot = r & 1
        pl.when(r + 1 < n_blocks)(lambda: in_copy(r + 1, 1 - slot).start())
        in_copy(r, slot).wait()
        pl.when(r >= 2)(lambda: out_copy(r - 2, slot).wait())
        compute(slot)
        out_copy(r, slot).start()
        return carry

    lax.fori_loop(0, n_blocks, body, 0)
    pl.when(n_blocks >= 2)(lambda: out_copy(n_blocks - 2, n_blocks & 1).wait())
    out_copy(n_blocks - 1, (n_blocks - 1) & 1).wait()


def _moe_up_kernel(bstart_ref, bcount_ref, w1_ref, w3_ref, xs_hbm, _zeros_hbm, h_hbm, w1b, w3b, xbuf, obuf,
                   in_sem, out_sem, *, tm):
    f = pl.program_id(0)
    e = pl.program_id(1)
    n_blocks = bcount_ref[e]
    row = lambda r: pl.multiple_of((bstart_ref[e] + r) * tm, tm)

    @pl.when(n_blocks > 0)
    def _():
        w1b[...] = w1_ref[0].astype(BF16)
        w3b[...] = w3_ref[0].astype(BF16)

        def in_copy(r, slot):
            return pltpu.make_async_copy(xs_hbm.at[pl.ds(row(r), tm), :], xbuf.at[slot], in_sem.at[slot])

        def out_copy(r, slot):
            return pltpu.make_async_copy(obuf.at[slot], h_hbm.at[f, pl.ds(row(r), tm), :], out_sem.at[slot])

        def compute(slot):
            x = xbuf[slot]
            a = jnp.dot(x, w1b[...], preferred_element_type=F32)
            b = jnp.dot(x, w3b[...], preferred_element_type=F32)
            obuf[slot] = (a * _sigmoid(a) * b).astype(BF16)

        _expert_row_loop(n_blocks, in_copy, out_copy, compute)


def _moe_up(bstart, bcount, xs, w1, w3, *, tm, tf):
    p, d = xs.shape
    n_exp, _, ff = w1.shape
    wspec = pl.BlockSpec((1, d, tf), lambda f, e, bs, bc: (e, 0, f))
    return pl.pallas_call(
        functools.partial(_moe_up_kernel, tm=tm),
        grid_spec=pltpu.PrefetchScalarGridSpec(
            num_scalar_prefetch=2,
            grid=(ff // tf, n_exp),
            in_specs=[wspec, wspec, pl.BlockSpec(memory_space=pl.ANY), pl.BlockSpec(memory_space=pl.ANY)],
            out_specs=pl.BlockSpec(memory_space=pl.ANY),
            scratch_shapes=[pltpu.VMEM((d, tf), BF16), pltpu.VMEM((d, tf), BF16),
                            pltpu.VMEM((2, tm, d), BF16), pltpu.VMEM((2, tm, tf), BF16),
                            pltpu.SemaphoreType.DMA((2,)), pltpu.SemaphoreType.DMA((2,))]),
        out_shape=jax.ShapeDtypeStruct((ff // tf, p, tf), BF16),
        input_output_aliases={5: 0},
        compiler_params=_cparams(2, VMEM_LIMIT),
        name="moe_up",
    )(bstart, bcount, w1, w3, xs, jnp.zeros((ff // tf, p, tf), BF16))


def _moe_down_kernel(bstart_ref, bcount_ref, w2_ref, hs_hbm, _zeros_hbm, yb_hbm, w2b, hbuf, obuf, in_sem, out_sem,
                     *, tm, tn):
    e = pl.program_id(0)
    n_blocks = bcount_ref[e]
    row = lambda r: pl.multiple_of((bstart_ref[e] + r) * tm, tm)
    nf, _, tf = hbuf.shape[1:]

    @pl.when(n_blocks > 0)
    def _():
        w2b[...] = w2_ref[0].astype(BF16)

        def in_copy(r, slot):
            return pltpu.make_async_copy(hs_hbm.at[:, pl.ds(row(r), tm), :], hbuf.at[slot], in_sem.at[slot])

        def out_copy(r, slot):
            return pltpu.make_async_copy(obuf.at[slot], yb_hbm.at[pl.ds(row(r), tm), :], out_sem.at[slot])

        def compute(slot):
            for n in range(w2b.shape[1] // tn):
                y = jnp.zeros((tm, tn), F32)
                for f in range(nf):
                    y = y + jnp.dot(hbuf[slot, f], w2b[f * tf:(f + 1) * tf, n * tn:(n + 1) * tn],
                                    preferred_element_type=F32)
                obuf[slot, :, n * tn // 2:(n + 1) * tn // 2] = _pack_pairs(y, tn // 2)

        _expert_row_loop(n_blocks, in_copy, out_copy, compute)


def _moe_down(bstart, bcount, hs, w2, *, tm, tn):
    nf, p, tf = hs.shape
    n_exp, ff, d = w2.shape
    return pl.pallas_call(
        functools.partial(_moe_down_kernel, tm=tm, tn=tn),
        grid_spec=pltpu.PrefetchScalarGridSpec(
            num_scalar_prefetch=2,
            grid=(n_exp,),
            in_specs=[pl.BlockSpec((1, ff, d), lambda e, bs, bc: (e, 0, 0)), pl.BlockSpec(memory_space=pl.ANY),
                      pl.BlockSpec(memory_space=pl.ANY)],
            out_specs=pl.BlockSpec(memory_space=pl.ANY),
            scratch_shapes=[pltpu.VMEM((ff, d), BF16), pltpu.VMEM((2, nf, tm, tf), BF16),
                            pltpu.VMEM((2, tm, d // 2), jnp.uint32),
                            pltpu.SemaphoreType.DMA((2,)), pltpu.SemaphoreType.DMA((2,))]),
        out_shape=jax.ShapeDtypeStruct((p, d // 2), jnp.uint32),
        input_output_aliases={4: 0},
        compiler_params=_cparams(1, VMEM_LIMIT),
        name="moe_down",
    )(bstart, bcount, w2, hs, jnp.zeros((p, d // 2), jnp.uint32))


def _moe_combine_kernel(p0_ref, p1_ref, xp_ref, xs_ref, g0_ref, g1_ref, nw_ref, yb_hbm, op_ref, os_ref,
                        buf0, buf1, sem, *, tc, nb_p, half):
    i = pl.program_id(0)
    _gather_rows(yb_hbm, buf0, p0_ref, i * tc, tc, sem)
    _gather_rows(yb_hbm, buf1, p1_ref, i * tc, tc, sem)

    def emit(x_ref, o_ref):
        x = (x_ref[...] + g0_ref[:, 0:1] * _unpack_pairs(buf0[...], half)
             + g1_ref[:, 0:1] * _unpack_pairs(buf1[...], half))
        o_ref[...] = _rms(x, nw_ref[...])

    pl.when(i < nb_p)(lambda: emit(xp_ref, op_ref))
    pl.when(i >= nb_p)(lambda: emit(xs_ref, os_ref))


def _moe_combine(pos0, pos1, xp, xs, g0, g1, norm_w, yb, *, tc, half):
    d = xp.shape[1]
    nb_p, nb_s = xp.shape[0] // tc, xs.shape[0] // tc
    p_idx = lambda i, a, b: (jnp.minimum(i, nb_p - 1), 0)
    s_idx = lambda i, a, b: (jnp.maximum(i - nb_p, 0), 0)
    return pl.pallas_call(
        functools.partial(_moe_combine_kernel, tc=tc, nb_p=nb_p, half=half),
        grid_spec=pltpu.PrefetchScalarGridSpec(
            num_scalar_prefetch=2,
            grid=(nb_p + nb_s,),
            in_specs=[pl.BlockSpec((tc, d), p_idx),
                      pl.BlockSpec((tc, d), s_idx),
                      pl.BlockSpec((tc, LANES), lambda i, a, b: (i, 0)),
                      pl.BlockSpec((tc, LANES), lambda i, a, b: (i, 0)),
                      pl.BlockSpec((1, d), lambda i, a, b: (0, 0)),
                      pl.BlockSpec(memory_space=pl.ANY)],
            out_specs=[pl.BlockSpec((tc, d), p_idx), pl.BlockSpec((tc, d), s_idx)],
            scratch_shapes=[pltpu.VMEM((tc, d // 2), jnp.uint32), pltpu.VMEM((tc, d // 2), jnp.uint32),
                            pltpu.SemaphoreType.DMA(())]),
        out_shape=[jax.ShapeDtypeStruct(xp.shape, F32), jax.ShapeDtypeStruct(xs.shape, F32)],
        compiler_params=_cparams(1, VMEM_LIMIT),
        name="moe_combine",
    )(pos0, pos1, xp, xs, g0, g1, norm_w, yb)


def _route(logits, bg, be, *, tm):
    t = logits.shape[0]
    pg = jax.nn.softmax(logits[:, :N_GROUPS] + bg, axis=-1)
    g_idx = jnp.argmax(pg, axis=-1).astype(jnp.int32)
    p_sel = jnp.take_along_axis(pg, g_idx[:, None], axis=-1)
    le = (logits[:, N_GROUPS:N_GROUPS + N_EXPERTS] + be).reshape(t, N_GROUPS, EXPERTS_PER_GROUP)
    le = jnp.take_along_axis(le, g_idx[:, None, None], axis=1)[:, 0]
    pe = jax.nn.softmax(le, axis=-1)
    top_v, top_i = lax.top_k(pe, TOP_K)
    gate = p_sel * top_v / jnp.sum(top_v, axis=-1, keepdims=True)
    e_flat = (g_idx[:, None] * EXPERTS_PER_GROUP + top_i.astype(jnp.int32)).reshape(-1)
    a = t * TOP_K
    onehot = (e_flat[:, None] == jnp.arange(N_EXPERTS, dtype=jnp.int32)[None, :]).astype(jnp.int32)
    rank = jnp.sum((jnp.cumsum(onehot, axis=0) - onehot) * onehot, axis=1)
    counts = jnp.sum(onehot, axis=0)
    padded = (counts + tm - 1) // tm * tm
    pad_end = jnp.cumsum(padded)
    dest = (pad_end - padded)[e_flat] + rank
    n_blocks = -(-(a + N_EXPERTS * (tm - 1)) // tm)
    tok = jnp.arange(a, dtype=jnp.int32) // TOP_K
    slot_tok = jnp.zeros((n_blocks * tm,), jnp.int32).at[dest].set(tok)
    bstart = ((pad_end - padded) // tm).astype(jnp.int32)
    bcount = (padded // tm).astype(jnp.int32)
    nused = (pad_end[-1] // tm).astype(jnp.int32).reshape(1)
    dest = dest.reshape(t, TOP_K)
    return gate, dest[:, 0], dest[:, 1], slot_tok, bstart, bcount, nused


def _moe(xp, xs, norm2_w, wr, bg, be, w1, w3, w2, final_w, *, tm_route, tm_blk, tf, tn, tc):
    h, logits = _router(xp, xs, norm2_w, wr, tm=tm_route)
    gate, pos0, pos1, slot_tok, bstart, bcount, nused = _route(logits, bg, be, tm=tm_blk)
    xg = _moe_gather(slot_tok, nused, h, tm=tm_blk)
    hs = _moe_up(bstart, bcount, xg, w1, w3, tm=tm_blk, tf=tf)
    yb = _moe_down(bstart, bcount, hs, w2, tm=tm_blk, tn=tn)
    g0 = jnp.broadcast_to(gate[:, 0:1], (h.shape[0], LANES))
    g1 = jnp.broadcast_to(gate[:, 1:2], (h.shape[0], LANES))
    return _moe_combine(pos0, pos1, xp, xs, g0, g1, final_w, yb, tc=tc, half=tn // 2)


def kernel(x_prompt, x_sample, state_mlstm_c, state_mlstm_n, state_mlstm_m, state_rwkv, state_rwkv_shift, norm1_w, w_in, w_out, ml_b_i, ml_b_f, ml_norm_w, rw_mu, rw_w0, rw_w2, rw_a0, rw_a2, rw_g2, rw_k_k, rw_k_a, rw_r_k, rw_ln_w, rw_ln_b, norm2_w, router_group_w, router_group_b, router_expert_w, router_expert_b, moe_w1, moe_w3, moe_w2, final_norm_w):
    assert w_in.shape[0] == 1, "single-layer trunk"
    bp, seq, d = x_prompt.shape
    bs = x_sample.shape[0]
    tp = bp * seq
    ml_proj = 2 * ML_HEADS * ML_DK + 2 * ML_WIDTH + 2 * ML_HEADS
    rw_proj = 3 * RW_WIDTH + RW_LORA
    rw_col0 = -(-ml_proj // RW_GROUP) * RW_GROUP
    n_groups = RW_WIDTH // RW_GROUP

    w_cat = _pack_w_in(jnp.swapaxes(w_in[0], 0, 1), split=ml_proj, width=rw_col0, tr=RW_GROUP)
    w_o = w_out[0].astype(BF16)
    mu = rw_mu[0]
    rows = [rw_w0[0], rw_a0[0], rw_k_k[0], rw_k_a[0], rw_r_k[0].reshape(-1), rw_ln_w[0], rw_ln_b[0],
            mu[:RW_WIDTH], mu[RW_WIDTH:2 * RW_WIDTH], mu[2 * RW_WIDTH:3 * RW_WIDTH]]
    prm = jnp.concatenate([jnp.stack(rows), jnp.zeros((16 - len(rows), RW_WIDTH), F32)], axis=0)
    mu_l = mu[3 * RW_WIDTH:][None]
    wl = jnp.zeros((RW_LORA, 3, RW_WIDTH), F32)
    wl = wl.at[:RW_DECAY_LORA, 0].set(rw_w2[0]).at[RW_DECAY_LORA:RW_DECAY_LORA + RW_A_LORA, 1].set(rw_a2[0])
    wl = wl.at[RW_DECAY_LORA + RW_A_LORA:, 2].set(rw_g2[0])
    wl = wl.reshape(RW_LORA, 3, n_groups, RW_GROUP).transpose(2, 0, 1, 3).reshape(n_groups, RW_LORA, 3 * RW_GROUP)
    wl = wl.astype(BF16)
    gate_bias = jnp.zeros((1, LANES), F32).at[0, :ML_HEADS].set(ml_b_i[0]).at[0, ML_HEADS:2 * ML_HEADS].set(ml_b_f[0])
    ml_nw = ml_norm_w[0][None]
    wr = jnp.concatenate([router_group_w[0], router_expert_w[0],
                          jnp.zeros((d, LANES - N_GROUPS - N_EXPERTS), F32)], axis=1)

    xp = x_prompt.reshape(tp, d)
    xs = x_sample.reshape(bs, d)
    proj_p = _in_proj(xp, norm1_w, w_cat, tm=512, tn=1280)
    proj_s = _in_proj(xs, norm1_w, w_cat, tm=bs, tn=1280)

    y_ml_p, p_c, p_nm = _mlstm_prompt(proj_p, gate_bias, ml_nw, batch=bp, seq=seq)
    y_rw_p, p_s = _rwkv_prompt(proj_p, prm, mu_l, wl, batch=bp, seq=seq, rw_col0=rw_col0, n_groups=n_groups)
    p_sh = jnp.concatenate([lax.slice(proj_p, (b * seq + seq - 1, rw_col0), (b * seq + seq, rw_col0 + rw_proj))
                            for b in range(bp)], axis=0)

    y_ml_s, s_c, s_n, s_m = _mlstm_sample(proj_s[:, None, :], gate_bias, ml_nw, state_mlstm_c[0], state_mlstm_n[0],
                                          state_mlstm_m[0][:, None, :])
    r, k, v, g, rt, kt, vt, kkt, kat, dt = _rwkv_sample_prep(proj_s, state_rwkv_shift[0], prm, mu_l, wl, rows=bs,
                                                             rw_col0=rw_col0, n_groups=n_groups)
    s_t, y_t = _rwkv_sample_state(jnp.transpose(state_rwkv[0], (1, 2, 3, 0)), rt, kt, kkt, kat, dt, vt)
    s_s = jnp.transpose(s_t, (3, 0, 1, 2))
    y_rw_s = _rwkv_sample_post(y_t, r, k, v, g, prm)
    s_sh = proj_s[:, rw_col0:rw_col0 + rw_proj]

    x2_p = _out_proj(y_ml_p, y_rw_p, w_o, xp, tm=512, tn=1024)
    x2_s = _out_proj(y_ml_s.reshape(bs, ML_WIDTH), y_rw_s, w_o, xs, tm=bs, tn=1024)
    y_p, y_s = _moe(x2_p, x2_s, norm2_w, wr, router_group_b[0], router_expert_b[0], moe_w1[0], moe_w3[0],
                    moe_w2[0], final_norm_w[None], tm_route=128, tm_blk=256, tf=512, tn=1024, tc=128)

    lead = lambda a: a[None]
    return (y_p.reshape(bp, seq, d), y_s.reshape(bs, 1, d),
            lead(p_c), lead(p_nm[:, :, 0]), lead(p_nm[:, :, 1, 0]), lead(p_s), lead(p_sh),
            lead(s_c), lead(s_n), lead(s_m[:, 0]), lead(s_s), lead(s_sh))
```

```python
import functools

import jax
import jax.numpy as jnp
from jax import lax
from jax.experimental import pallas as pl
from jax.experimental.pallas import tpu as pltpu

F32 = jnp.float32
BF16 = jnp.bfloat16

D_MODEL = 4096
ML_HEADS = 4
ML_DK = 256
ML_DV = 512
ML_WIDTH = ML_HEADS * ML_DV
GATE_SOFTCAP = 15.0
RW_HEAD = 64
RW_WIDTH = D_MODEL - ML_WIDTH
RW_HEADS = RW_WIDTH // RW_HEAD
RW_DECAY_LORA = 96
RW_A_LORA = 96
RW_GATE_LORA = 64
RW_GN_EPS = 64e-5
N_GROUPS = 4
EXPERTS_PER_GROUP = 8
N_EXPERTS = N_GROUPS * EXPERTS_PER_GROUP
TOP_K = 2
EPS = 1e-6

LANES = 128
RW_GROUP = 256
RW_GHEADS = RW_GROUP // RW_HEAD
RW_LORA = RW_DECAY_LORA + RW_A_LORA + RW_GATE_LORA
RW_CHUNK = 64
ML_CHUNK = 256
_SAMPLE_ROWS_PER_STAGE = 8
VMEM_LIMIT = 56 * 1024 * 1024


def _cparams(n_axes, vmem=None):
    return pltpu.CompilerParams(dimension_semantics=("arbitrary",) * n_axes, vmem_limit_bytes=vmem)


def _sigmoid(x):
    return 1.0 / (1.0 + jnp.exp(-x))


def _softplus(x):
    return jnp.maximum(x, 0.0) + jnp.log(1.0 + jnp.exp(-jnp.abs(x)))


def _soft_cap(x):
    return GATE_SOFTCAP * jnp.tanh(x / GATE_SOFTCAP)


def _dot(a, b):
    return jnp.dot(a.astype(BF16), b.astype(BF16), preferred_element_type=F32)


def _dot_nt(a, b):
    return lax.dot_general(a.astype(BF16), b.astype(BF16), (((1,), (1,)), ((), ())),
                           preferred_element_type=F32)


def _split2(x):
    hi = x.astype(BF16)
    lo = (x - hi.astype(F32)).astype(BF16)
    return hi, lo


def _split3(x):
    hi = x.astype(BF16)
    r = x - hi.astype(F32)
    mid = r.astype(BF16)
    lo = (r - mid.astype(F32)).astype(BF16)
    return hi, mid, lo


def _iota(shape, axis):
    return lax.broadcasted_iota(jnp.int32, shape, axis)


def _pack_pairs(x, half):
    bits = lax.bitcast_convert_type(x.astype(BF16).astype(F32), jnp.uint32)
    parts = []
    for g in range(x.shape[1] // (2 * half)):
        lo = bits[:, 2 * half * g:2 * half * g + half]
        hi = bits[:, 2 * half * g + half:2 * half * (g + 1)]
        parts.append((lo >> 16) | hi)
    return parts[0] if len(parts) == 1 else jnp.concatenate(parts, axis=1)


def _unpack_pairs(p, half):
    parts = []
    for g in range(p.shape[1] // half):
        w = p[:, half * g:half * (g + 1)]
        parts.append(lax.bitcast_convert_type(w << 16, F32))
        parts.append(lax.bitcast_convert_type(w & jnp.uint32(0xFFFF0000), F32))
    return jnp.concatenate(parts, axis=1)


def _head_block_ones(n):
    return jnp.where((_iota((n, n), 0) >> 6) == (_iota((n, n), 1) >> 6), 1.0, 0.0).astype(BF16)


def _seg_sum(x, bd, passes=2):
    hi, lo = _split2(x)
    out = jnp.dot(hi, bd, preferred_element_type=F32)
    return out + jnp.dot(lo, bd, preferred_element_type=F32) if passes == 2 else out


_P_W0, _P_A0, _P_KK, _P_KA, _P_RK, _P_LNW, _P_LNB, _P_MUR, _P_MUK, _P_MUV = range(10)


def _rwkv_prep(p_r, p_k, p_v, p_l, prev_r, prev_k, prev_v, prev_l, prm, mu_l, wl, bd):
    row = lambda i: prm[i:i + 1, :]
    xr = p_r + (prev_r - p_r) * row(_P_MUR)
    xk = p_k + (prev_k - p_k) * row(_P_MUK)
    xv = p_v + (prev_v - p_v) * row(_P_MUV)
    xl = p_l + (prev_l - p_l) * mu_l
    lane = _iota(xl.shape, 1)
    z = jnp.where(lane < RW_DECAY_LORA, jnp.tanh(xl),
                  jnp.where(lane < RW_DECAY_LORA + RW_A_LORA, xl, _sigmoid(xl)))
    lo = _dot(z, wl)
    lw, la, lg = lo[:, :RW_GROUP], lo[:, RW_GROUP:2 * RW_GROUP], lo[:, 2 * RW_GROUP:]
    w = -_softplus(-(row(_P_W0) + lw)) - 0.5
    logd = -jnp.exp(w)
    a = _sigmoid(row(_P_A0) + la)
    kk = xk * row(_P_KK)
    kk = kk / jnp.maximum(jnp.sqrt(_seg_sum(kk * kk, bd)), 1e-12)
    k = xk * (1.0 + (a - 1.0) * row(_P_KA))
    return xr, k, xv, kk, a, logd, lg


def _rwkv_post(y, r, k, v, g, prm, bd):
    row = lambda i: prm[i:i + 1, :]
    mean = _seg_sum(y, bd, passes=1) * (1.0 / RW_HEAD)
    yc = y - mean
    var = _seg_sum(yc * yc, bd, passes=1) * (1.0 / RW_HEAD)
    yn = yc * lax.rsqrt(var + RW_GN_EPS) * row(_P_LNW) + row(_P_LNB)
    bonus = _seg_sum(r * k * row(_P_RK), bd, passes=1) * v
    return (yn + bonus) * g


def _rwkv_chunk(r, k, v, kk, a, logd, G, bd):
    L = r.shape[0]
    tril = jnp.where(_iota((L, L), 1) <= _iota((L, L), 0), 1.0, 0.0).astype(BF16)
    d_hi, d_lo = _split2(logd)
    clog = (jnp.dot(tril, d_hi, preferred_element_type=F32)
            + jnp.dot(tril, d_lo, preferred_element_type=F32))
    clog_l = clog[L - 1:L, :]
    n_in = jnp.exp(-clog)
    to_end = jnp.exp(clog_l - clog)
    ka = kk * a
    at = -kk * jnp.exp(clog - logd)
    rt = r * jnp.exp(clog)
    lane_head = _iota((1, RW_GROUP), 1) >> 6
    masks = [lane_head == h for h in range(RW_GHEADS)]
    zero = jnp.zeros_like(at)
    bf = lambda x: x.astype(BF16)
    lhs = jnp.concatenate([jnp.where(m, x, zero) for m in masks for x in (at, rt)], axis=0)
    xx = _dot_nt(lhs, jnp.concatenate([ka * n_in, k * n_in], axis=0))
    t2 = _iota((L, 2 * L), 0)
    c2 = _iota((L, 2 * L), 1)
    s2 = c2 & (L - 1)
    right = c2 >= L
    eye_pad = jnp.where(c2 == t2 + L, 1.0, 0.0)
    zeros_v = jnp.zeros((L, RW_GROUP), BF16)
    v_b = bf(v)
    vz = jnp.concatenate([zeros_v, v_b], axis=0)
    yield
    zs, rbk = [], []
    makv = zero
    for h, m in enumerate(masks):
        o = 2 * L * h
        nk = jnp.where(s2 < t2, xx[o:o + L], 0.0)
        rbk.append(bf(jnp.where(s2 <= t2, xx[o + L:o + 2 * L], 0.0)))
        makv = makv + jnp.where(m, _dot(jnp.where(right, nk, 0.0), vz), 0.0)
        zs.append(jnp.where(right, eye_pad, nk))
    for _ in range((L - 1).bit_length()):
        yield
        zs = [_dot(z[:, :L], z) + jnp.where(right, z, 0.0) for z in zs]
    yield
    ws = _dot_nt(jnp.concatenate([at, rt], axis=0), G)
    xz = jnp.concatenate([zeros_v, bf(ws[:L] + makv)], axis=0)
    yield
    w_all = zero
    for z_h, m in zip(zs, masks):
        w_all = w_all + jnp.where(m, _dot(z_h, xz), 0.0)
    wv = jnp.concatenate([bf(w_all), v_b], axis=0)
    yield
    y = ws[L:]
    for rbk_h, m in zip(rbk, masks):
        y = y + jnp.where(m, jnp.dot(rbk_h, wv, preferred_element_type=F32), 0.0)
    upd = _dot(wv.astype(F32).T, jnp.concatenate([ka * to_end, k * to_end], axis=0))
    g_new = G * jnp.exp(clog_l) + jnp.where(bd > 0, upd, 0.0)
    return y, g_new


def _interleave(gens):
    results = [None] * len(gens)
    live = list(range(len(gens)))
    while live:
        for i in list(live):
            try:
                next(gens[i])
            except StopIteration as stop:
                results[i] = stop.value
                live.remove(i)
    return results


def _shift_rows(x, first_row):
    rolled = pltpu.roll(x, 1, axis=0)
    return jnp.where(_iota(x.shape, 0) == 0, first_row, rolled)


def _rwkv_prompt_kernel(*refs, n_chunks, batch, gps):
    n_chain = gps * batch
    p_refs = refs[:4 * n_chain]
    prm_ref, mul_ref, wl_ref, y_ref, s_ref, g_scr, carry_scr = refs[4 * n_chain:]
    c = pl.program_id(1)

    @pl.when(c == 0)
    def _():
        g_scr[...] = jnp.zeros_like(g_scr)
        carry_scr[...] = jnp.zeros_like(carry_scr)

    bd = _head_block_ones(RW_GROUP)

    def chain(q):
        gi = q // batch
        prm = prm_ref[:, gi * RW_GROUP:(gi + 1) * RW_GROUP]
        ps = [ref[...] for ref in p_refs[4 * q:4 * q + 4]]
        L = ps[0].shape[0]
        prevs = [_shift_rows(x, carry_scr[q, i:i + 1, :]) for i, x in enumerate(ps)]
        r, k, v, kk, a, logd, g = _rwkv_prep(*ps, *prevs, prm, mul_ref[...], wl_ref[gi], bd)
        y, g_new = yield from _rwkv_chunk(r, k, v, kk, a, logd, g_scr[q], bd)
        yield
        out = _rwkv_post(y, r, k, v, g, prm, bd).astype(y_ref.dtype)
        return out, g_new, [x[L - 1:L, :] for x in ps]

    results = _interleave([chain(q) for q in range(n_chain)])
    for q, (out, g_new, last_rows) in enumerate(results):
        gi, b = divmod(q, batch)
        y_ref[b, :, gi * RW_GROUP:(gi + 1) * RW_GROUP] = out
        g_scr[q] = g_new
        for i, x in enumerate(last_rows):
            carry_scr[q, i:i + 1, :] = x

    @pl.when(c == n_chunks - 1)
    def _():
        for q, (_, g_new, _) in enumerate(results):
            gi, b = divmod(q, batch)
            for h in range(RW_GHEADS):
                s_ref[b, gi * RW_GHEADS + h] = g_new[RW_HEAD * h:RW_HEAD * (h + 1), RW_HEAD * h:RW_HEAD * (h + 1)]


def _rwkv_prompt(proj, prm, mu_l, wl, *, batch, seq, rw_col0, n_groups, gps):
    L = RW_CHUNK
    nc = seq // L
    cb = rw_col0 // RW_GROUP
    width = n_groups * RW_GROUP
    p_specs = []
    for gi in range(gps):
        for b in range(batch):
            col = lambda g, off, gi=gi: cb + off + g * gps + gi
            rows = lambda c, b=b: b * nc + c
            p_specs += [pl.BlockSpec((L, RW_GROUP), lambda g, c, o=o, col=col, rows=rows: (rows(c), col(g, o)))
                        for o in (0, n_groups, 2 * n_groups)]
            p_specs += [pl.BlockSpec((L, RW_GROUP), lambda g, c, rows=rows: (rows(c), cb + 3 * n_groups))]
    n_chain = gps * batch
    y, s = pl.pallas_call(
        functools.partial(_rwkv_prompt_kernel, n_chunks=nc, batch=batch, gps=gps),
        grid=(n_groups // gps, nc),
        in_specs=p_specs + [pl.BlockSpec((16, gps * RW_GROUP), lambda g, c: (0, g)),
                            pl.BlockSpec((1, RW_LORA), lambda g, c: (0, 0)),
                            pl.BlockSpec((gps, RW_LORA, 3 * RW_GROUP), lambda g, c: (g, 0, 0))],
        out_specs=[pl.BlockSpec((batch, L, gps * RW_GROUP), lambda g, c: (0, c, g)),
                   pl.BlockSpec((batch, gps * RW_GHEADS, RW_HEAD, RW_HEAD), lambda g, c: (0, g, 0, 0))],
        out_shape=[jax.ShapeDtypeStruct((batch, seq, width), BF16),
                   jax.ShapeDtypeStruct((batch, n_groups * RW_GHEADS, RW_HEAD, RW_HEAD), F32)],
        scratch_shapes=[pltpu.VMEM((n_chain, RW_GROUP, RW_GROUP), F32), pltpu.VMEM((n_chain, 8, RW_GROUP), F32)],
        compiler_params=_cparams(2),
        name="rwkv_prompt",
    )(*([proj] * (4 * n_chain)), prm, mu_l, wl)
    return y.reshape(batch * seq, width), s


def _rwkv_sample_prep_kernel(pr_ref, pk_ref, pv_ref, pl_ref, sr_ref, sk_ref, sv_ref, sl_ref, prm_ref, mul_ref,
                             wl_ref, r_ref, k_ref, v_ref, g_ref, rt_ref, kt_ref, vt_ref, kkt_ref, kat_ref, dt_ref):
    bd = _head_block_ones(RW_GROUP)
    r, k, v, kk, a, logd, g = _rwkv_prep(pr_ref[...], pk_ref[...], pv_ref[...], pl_ref[...], sr_ref[...],
                                         sk_ref[...], sv_ref[...], sl_ref[...], prm_ref[...], mul_ref[...],
                                         wl_ref[0], bd)
    r_ref[...] = r
    k_ref[...] = k
    v_ref[...] = v
    g_ref[...] = g
    rt_ref[...] = r.T
    kt_ref[...] = k.T
    vt_ref[...] = v.T
    kkt_ref[...] = kk.T
    kat_ref[...] = (kk * a).T
    dt_ref[...] = jnp.exp(logd).T


def _rwkv_sample_prep(proj, shift0, prm, mu_l, wl, *, rows, rw_col0, n_groups):
    cb = rw_col0 // RW_GROUP
    pspec = lambda off: pl.BlockSpec((rows, RW_GROUP), lambda g, off=off: (0, cb + off + g))
    sspec = lambda off: pl.BlockSpec((rows, RW_GROUP), lambda g, off=off: (0, off + g))
    ospec = pl.BlockSpec((rows, RW_GROUP), lambda g: (0, g))
    tspec = pl.BlockSpec((RW_GROUP, rows), lambda g: (g, 0))
    width = n_groups * RW_GROUP
    return pl.pallas_call(
        _rwkv_sample_prep_kernel,
        grid=(n_groups,),
        in_specs=[pspec(0), pspec(n_groups), pspec(2 * n_groups),
                  pl.BlockSpec((rows, RW_GROUP), lambda g: (0, cb + 3 * n_groups)),
                  sspec(0), sspec(n_groups), sspec(2 * n_groups),
                  pl.BlockSpec((rows, RW_GROUP), lambda g: (0, 3 * n_groups)),
                  pl.BlockSpec((16, RW_GROUP), lambda g: (0, g)),
                  pl.BlockSpec((1, RW_LORA), lambda g: (0, 0)),
                  pl.BlockSpec((1, RW_LORA, 3 * RW_GROUP), lambda g: (g, 0, 0))],
        out_specs=[ospec] * 4 + [tspec] * 6,
        out_shape=[jax.ShapeDtypeStruct((rows, width), F32)] * 4 + [jax.ShapeDtypeStruct((width, rows), F32)] * 6,
        compiler_params=_cparams(1),
        name="rwkv_sample_prep",
    )(proj, proj, proj, proj, shift0, shift0, shift0, shift0, prm, mu_l, wl)


def _rwkv_sample_state_kernel(s_ref, rt_ref, kt_ref, kkt_ref, kat_ref, dt_ref, vt_ref, so_ref, yt_ref):
    n_kk = -kkt_ref[...]
    r, k, ka, d = rt_ref[...], kt_ref[...], kat_ref[...], dt_ref[...]
    for i0 in range(0, s_ref.shape[1], _SAMPLE_ROWS_PER_STAGE):
        rows = range(i0, i0 + _SAMPLE_ROWS_PER_STAGE)
        sa = [jnp.sum(s_ref[0, i] * n_kk, axis=0, keepdims=True) for i in rows]
        ys = []
        for i, sa_i in zip(rows, sa):
            s_new = s_ref[0, i] * d + sa_i * ka + vt_ref[i:i + 1, :] * k
            so_ref[0, i] = s_new
            ys.append(jnp.sum(s_new * r, axis=0, keepdims=True))
        yt_ref[i0:i0 + _SAMPLE_ROWS_PER_STAGE, :] = jnp.concatenate(ys, axis=0)


def _rwkv_sample_state(state_t, rt, kt, kkt, kat, dt, vt):
    nh, n, _, b = state_t.shape
    vspec = pl.BlockSpec((n, b), lambda h: (h, 0))
    sspec = pl.BlockSpec((1, n, n, b), lambda h: (h, 0, 0, 0))
    return pl.pallas_call(
        _rwkv_sample_state_kernel,
        grid=(nh,),
        in_specs=[sspec] + [vspec] * 6,
        out_specs=[sspec, vspec],
        out_shape=[jax.ShapeDtypeStruct(state_t.shape, F32), jax.ShapeDtypeStruct((nh * n, b), F32)],
        compiler_params=_cparams(1),
        name="rwkv_sample_state",
    )(state_t, rt, kt, kkt, kat, dt, vt)


def _rwkv_sample_post_kernel(yt_ref, r_ref, k_ref, v_ref, g_ref, prm_ref, o_ref):
    bd = _head_block_ones(RW_GROUP)
    o_ref[...] = _rwkv_post(yt_ref[...].T, r_ref[...], k_ref[...], v_ref[...], g_ref[...], prm_ref[...],
                            bd).astype(o_ref.dtype)


def _rwkv_sample_post(yt, r, k, v, g, prm):
    rows, width = r.shape
    spec = pl.BlockSpec((rows, RW_GROUP), lambda i: (0, i))
    return pl.pallas_call(
        _rwkv_sample_post_kernel,
        grid=(width // RW_GROUP,),
        in_specs=[pl.BlockSpec((RW_GROUP, rows), lambda i: (i, 0))] + [spec] * 4
                 + [pl.BlockSpec((16, RW_GROUP), lambda i: (0, i))],
        out_specs=spec,
        out_shape=jax.ShapeDtypeStruct((rows, width), BF16),
        compiler_params=_cparams(1),
        name="rwkv_sample_post",
    )(yt, r, k, v, g, prm)


def _mlstm_gates(gates, bias, h):
    capped = _soft_cap(gates + bias)
    lane = _iota(gates.shape, 1)
    i_col = jnp.sum(jnp.where(lane == h, capped, 0.0), axis=1, keepdims=True)
    f_col = jnp.sum(jnp.where(lane == h + ML_HEADS, -_softplus(-capped), 0.0), axis=1, keepdims=True)
    return i_col, f_col


def _mlstm_out(hh, o, norm_w):
    hn = hh * lax.rsqrt(jnp.mean(hh * hh, axis=1, keepdims=True) + EPS) * norm_w
    return hn * _sigmoid(o)


def _mlstm_prompt_kernel(*refs, n_chunks, batch):
    p_refs = refs[:5 * batch]
    gb_ref, nw_ref, y_ref, c_ref, nm_ref, c_scr, n_scr, m_scr = refs[5 * batch:]
    h = pl.program_id(0)
    c = pl.program_id(1)

    @pl.when(c == 0)
    def _():
        c_scr[...] = jnp.zeros_like(c_scr)
        n_scr[...] = jnp.zeros_like(n_scr)
        m_scr[...] = jnp.zeros_like(m_scr)

    def sequence(b):
        q_ref, k_ref, v_ref, o_ref, gt_ref = p_refs[5 * b:5 * b + 5]
        q = q_ref[...] * (ML_DK ** -0.5)
        k = k_ref[...]
        v = v_ref[...]
        L = q.shape[0]
        i_col, f_col = _mlstm_gates(gt_ref[...], gb_ref[...], h)
        t_i = _iota((L, L), 0)
        s_i = _iota((L, L), 1)
        causal = s_i <= t_i
        to_row = lambda col: jnp.sum(jnp.where(t_i == s_i, col, 0.0), axis=0, keepdims=True)
        i_row = to_row(i_col)
        f_row = to_row(f_col)
        yield
        b_col = jnp.sum(jnp.where(causal, f_row, 0.0), axis=1, keepdims=True)
        b_row = jnp.sum(jnp.where(t_i <= s_i, f_col, 0.0), axis=0, keepdims=True)
        m0 = m_scr[b]
        inter = b_col + m0
        dmat = jnp.where(causal, b_col - b_row + i_row, -1e30)
        yield
        m_t = jnp.maximum(inter, jnp.max(dmat, axis=1, keepdims=True))
        w_int = jnp.exp(inter - m_t)
        s = _dot_nt(q, k) * jnp.exp(dmat - m_t)
        c0 = c_scr[b]
        n0 = n_scr[b]
        yield
        num = w_int * _dot(q, c0) + _dot(s, v)
        den = w_int * jnp.sum(q * n0, axis=1, keepdims=True) + jnp.sum(s, axis=1, keepdims=True)
        hh = num / jnp.maximum(jnp.abs(den), jnp.exp(-m_t))
        m_new = m_t[L - 1:L, :]
        b_l = b_col[L - 1:L, :]
        a_end = jnp.exp(b_l - b_col + i_col - m_new)
        dec = jnp.exp(b_l + m0 - m_new)
        ka = k * a_end
        yield
        c_new = dec * c0 + _dot(ka.T, v)
        n_new = dec * n0 + jnp.sum(ka, axis=0, keepdims=True)
        out = _mlstm_out(hh, o_ref[...], nw_ref[...]).astype(y_ref.dtype)
        return out, c_new, n_new, m_new

    results = _interleave([sequence(b) for b in range(batch)])
    for b, (out, c_new, n_new, m_new) in enumerate(results):
        y_ref[b] = out
        c_scr[b] = c_new
        n_scr[b] = n_new
        m_scr[b] = m_new

    @pl.when(c == n_chunks - 1)
    def _():
        for b, (_, c_new, n_new, m_new) in enumerate(results):
            c_ref[b, 0] = c_new
            nm_ref[b, 0] = jnp.concatenate([n_new, jnp.broadcast_to(m_new, (7, ML_DK))], axis=0)


def _mlstm_prompt(proj, gate_bias, norm_w, *, batch, seq):
    L = ML_CHUNK
    nc = seq // L
    nh = ML_HEADS
    kq = nh * ML_DK
    p_specs = []
    for b in range(batch):
        rows = lambda c, b=b: b * nc + c
        p_specs += [pl.BlockSpec((L, ML_DK), lambda h, c, rows=rows: (rows(c), h)),
                    pl.BlockSpec((L, ML_DK), lambda h, c, rows=rows: (rows(c), nh + h)),
                    pl.BlockSpec((L, ML_DV), lambda h, c, rows=rows: (rows(c), 2 * kq // ML_DV + h)),
                    pl.BlockSpec((L, ML_DV), lambda h, c, rows=rows: (rows(c), 2 * kq // ML_DV + nh + h)),
                    pl.BlockSpec((L, LANES), lambda h, c, rows=rows: (rows(c), (2 * kq + 2 * ML_WIDTH) // LANES))]
    y, c_fin, nm = pl.pallas_call(
        functools.partial(_mlstm_prompt_kernel, n_chunks=nc, batch=batch),
        grid=(nh, nc),
        in_specs=p_specs + [pl.BlockSpec((1, LANES), lambda h, c: (0, 0)),
                            pl.BlockSpec((1, ML_DV), lambda h, c: (0, h))],
        out_specs=[pl.BlockSpec((batch, L, ML_DV), lambda h, c: (0, c, h)),
                   pl.BlockSpec((batch, 1, ML_DK, ML_DV), lambda h, c: (0, h, 0, 0)),
                   pl.BlockSpec((batch, 1, 8, ML_DK), lambda h, c: (0, h, 0, 0))],
        out_shape=[jax.ShapeDtypeStruct((batch, seq, ML_WIDTH), BF16),
                   jax.ShapeDtypeStruct((batch, nh, ML_DK, ML_DV), F32),
                   jax.ShapeDtypeStruct((batch, nh, 8, ML_DK), F32)],
        scratch_shapes=[pltpu.VMEM((batch, ML_DK, ML_DV), F32), pltpu.VMEM((batch, 1, ML_DK), F32),
                        pltpu.VMEM((batch, 1, 1), F32)],
        compiler_params=_cparams(2),
        name="mlstm_prompt",
    )(*([proj] * (5 * batch)), gate_bias, norm_w)
    return y.reshape(batch * seq, ML_WIDTH), c_fin, nm


def _mlstm_sample_kernel(q_ref, k_ref, v_ref, o_ref, gt_ref, gb_ref, nw_ref, c_ref, n_ref, m_ref,
                         y_ref, co_ref, no_ref, mo_ref):
    gates = gt_ref[0]
    eye = jnp.where(_iota((ML_DK, ML_DK), 0) == _iota((ML_DK, ML_DK), 1), 1.0, 0.0).astype(BF16)
    for h in range(ML_HEADS):
        q = q_ref[0][:, h * ML_DK:(h + 1) * ML_DK] * (ML_DK ** -0.5)
        k = k_ref[0][:, h * ML_DK:(h + 1) * ML_DK]
        v = v_ref[0][:, h * ML_DV:(h + 1) * ML_DV]
        i_pre, logf = _mlstm_gates(gates, gb_ref[...], h)
        c0 = c_ref[0, h]
        n0 = n_ref[0, h:h + 1, :]
        m0 = m_ref[0][:, h:h + 1]
        inter = logf + m0
        m_t = jnp.maximum(inter, i_pre)
        w_int = jnp.exp(inter - m_t)
        a_new = jnp.exp(i_pre - m_t)
        s = jnp.sum(q * k, axis=1, keepdims=True) * a_new
        qc = _dot(jnp.broadcast_to(q, (8, ML_DK)), c0)[0:1, :]
        num = w_int * qc + s * v
        den = w_int * jnp.sum(q * n0, axis=1, keepdims=True) + s
        hh = num / jnp.maximum(jnp.abs(den), jnp.exp(-m_t))
        k_hi, k_lo = _split2(jnp.broadcast_to(k, (8, ML_DK)))
        nt = (((1,), (1,)), ((), ()))
        k_col = (lax.dot_general(eye, k_hi, nt, preferred_element_type=F32)
                 + lax.dot_general(eye, k_lo, nt, preferred_element_type=F32))[:, 0:1]
        co_ref[0, h] = w_int * c0 + k_col * (a_new * v)
        no_ref[0, h:h + 1, :] = w_int * n0 + a_new * k
        mo_ref[0, :, h:h + 1] = m_t
        y_ref[0, :, h * ML_DV:(h + 1) * ML_DV] = _mlstm_out(
            hh, o_ref[0][:, h * ML_DV:(h + 1) * ML_DV], nw_ref[:, h * ML_DV:(h + 1) * ML_DV]).astype(y_ref.dtype)


def _mlstm_sample(proj3, gate_bias, norm_w, c0, n0, m0):
    b = proj3.shape[0]
    nh = ML_HEADS
    kq = nh * ML_DK
    p3 = lambda w, blk: pl.BlockSpec((1, 1, w), lambda i, blk=blk: (i, 0, blk))
    cspec = pl.BlockSpec((1, nh, ML_DK, ML_DV), lambda i: (i, 0, 0, 0))
    nspec = pl.BlockSpec((1, nh, ML_DK), lambda i: (i, 0, 0))
    mspec = pl.BlockSpec((1, 1, nh), lambda i: (i, 0, 0))
    return pl.pallas_call(
        _mlstm_sample_kernel,
        grid=(b,),
        in_specs=[p3(kq, 0), p3(kq, 1), p3(ML_WIDTH, 2 * kq // ML_WIDTH), p3(ML_WIDTH, 2 * kq // ML_WIDTH + 1),
                  p3(LANES, (2 * kq + 2 * ML_WIDTH) // LANES),
                  pl.BlockSpec((1, LANES), lambda i: (0, 0)),
                  pl.BlockSpec((1, ML_WIDTH), lambda i: (0, 0)),
                  cspec, nspec, mspec],
        out_specs=[pl.BlockSpec((1, 1, ML_WIDTH), lambda i: (i, 0, 0)), cspec, nspec, mspec],
        out_shape=[jax.ShapeDtypeStruct((b, 1, ML_WIDTH), BF16),
                   jax.ShapeDtypeStruct(c0.shape, F32), jax.ShapeDtypeStruct(n0.shape, F32),
                   jax.ShapeDtypeStruct(m0.shape, F32)],
        compiler_params=_cparams(1),
        name="mlstm_sample",
    )(proj3, proj3, proj3, proj3, proj3, gate_bias, norm_w, c0, n0, m0)


def _rms(x, w):
    return x * lax.rsqrt(jnp.mean(x * x, axis=1, keepdims=True) + EPS) * w


def _pack_w_in_kernel(wt_hbm, o_ref, buf, sem, *, n_a, split):
    j = pl.program_id(0)
    tr = o_ref.shape[0]

    def copy(step, slot):
        src = jnp.where(step < n_a, step * tr, split + (step - n_a) * tr)
        return pltpu.make_async_copy(wt_hbm.at[pl.ds(pl.multiple_of(src, 8), tr), :], buf.at[slot], sem.at[slot])

    pl.when(j == 0)(lambda: copy(j, 0).start())
    pl.when(j + 1 < pl.num_programs(0))(lambda: copy(j + 1, (j + 1) & 1).start())
    copy(j, j & 1).wait()
    o_ref[...] = buf[j & 1].astype(BF16)


def _pack_w_in(wt, *, split, width, tr):
    d = wt.shape[1]
    assert split % 8 == 0 and width % tr == 0 and split + width == wt.shape[0]
    n_a = width // tr
    return pl.pallas_call(
        functools.partial(_pack_w_in_kernel, n_a=n_a, split=split),
        grid=(2 * n_a,),
        in_specs=[pl.BlockSpec(memory_space=pl.ANY)],
        out_specs=pl.BlockSpec((tr, d), lambda j: (j, 0)),
        out_shape=jax.ShapeDtypeStruct((2 * width, d), BF16),
        scratch_shapes=[pltpu.VMEM((2, tr, d), F32), pltpu.SemaphoreType.DMA((2,))],
        compiler_params=_cparams(1, VMEM_LIMIT),
        name="pack_w_in",
    )(wt)


def _in_proj_kernel(x_ref, nw_ref, w_ref, o_ref, h_scr):
    @pl.when(pl.program_id(1) == 0)
    def _():
        h_scr[...] = _rms(x_ref[...], nw_ref[...]).astype(BF16)

    o_ref[...] = lax.dot_general(h_scr[...], w_ref[...], (((1,), (1,)), ((), ())), preferred_element_type=F32)


def _in_proj(x, norm_w, wt, *, tm, tn):
    t, d = x.shape
    n = wt.shape[0]
    return pl.pallas_call(
        _in_proj_kernel,
        grid=(t // tm, n // tn),
        in_specs=[pl.BlockSpec((tm, d), lambda i, j: (i, 0), pipeline_mode=pl.Buffered(1)),
                  pl.BlockSpec((1, d), lambda i, j: (0, 0)),
                  pl.BlockSpec((tn, d), lambda i, j: (j, 0))],
        out_specs=pl.BlockSpec((tm, tn), lambda i, j: (i, j)),
        out_shape=jax.ShapeDtypeStruct((t, n), F32),
        scratch_shapes=[pltpu.VMEM((tm, d), BF16)],
        compiler_params=_cparams(2, VMEM_LIMIT),
        name="in_proj",
    )(x, norm_w, wt)


def _out_proj_kernel(ya_ref, yb_ref, wa_ref, wb_ref, x_ref, o_ref):
    o_ref[...] = (x_ref[...] + jnp.dot(ya_ref[...], wa_ref[...], preferred_element_type=F32)
                  + jnp.dot(yb_ref[...], wb_ref[...], preferred_element_type=F32))


def _out_proj(ya, yb, w, x, *, tm, tn):
    t, kh = ya.shape
    n = w.shape[1]
    return pl.pallas_call(
        _out_proj_kernel,
        grid=(t // tm, n // tn),
        in_specs=[pl.BlockSpec((tm, kh), lambda i, j: (i, 0)),
                  pl.BlockSpec((tm, kh), lambda i, j: (i, 0)),
                  pl.BlockSpec((kh, tn), lambda i, j: (0, j)),
                  pl.BlockSpec((kh, tn), lambda i, j: (1, j)),
                  pl.BlockSpec((tm, tn), lambda i, j: (i, j))],
        out_specs=pl.BlockSpec((tm, tn), lambda i, j: (i, j)),
        out_shape=jax.ShapeDtypeStruct((t, n), F32),
        compiler_params=_cparams(2, VMEM_LIMIT),
        name="out_proj",
    )(ya, yb, w, w, x)


def _router_kernel(xp_ref, xs_ref, nw_ref, wr_ref, h_ref, lg_ref, *, nb_p):
    def emit(x_ref):
        h = _rms(x_ref[...], nw_ref[...])
        h_ref[...] = _pack_pairs(h, h.shape[1] // 2)
        lg_ref[...] = jnp.dot(h, wr_ref[...], preferred_element_type=F32, precision=lax.Precision.HIGHEST)

    pl.when(pl.program_id(0) < nb_p)(lambda: emit(xp_ref))
    pl.when(pl.program_id(0) >= nb_p)(lambda: emit(xs_ref))


def _router(xp, xs, norm_w, wr, *, tm):
    d = xp.shape[1]
    nb_p, nb_s = xp.shape[0] // tm, xs.shape[0] // tm
    t = xp.shape[0] + xs.shape[0]
    return pl.pallas_call(
        functools.partial(_router_kernel, nb_p=nb_p),
        grid=(nb_p + nb_s,),
        in_specs=[pl.BlockSpec((tm, d), lambda i: (jnp.minimum(i, nb_p - 1), 0)),
                  pl.BlockSpec((tm, d), lambda i: (jnp.maximum(i - nb_p, 0), 0)),
                  pl.BlockSpec((1, d), lambda i: (0, 0)),
                  pl.BlockSpec((d, LANES), lambda i: (0, 0))],
        out_specs=[pl.BlockSpec((tm, d // 2), lambda i: (i, 0)), pl.BlockSpec((tm, LANES), lambda i: (i, 0))],
        out_shape=[jax.ShapeDtypeStruct((t, d // 2), jnp.uint32), jax.ShapeDtypeStruct((t, LANES), F32)],
        compiler_params=_cparams(1, VMEM_LIMIT),
        name="router",
    )(xp, xs, norm_w, wr)


def _gather_rows(src_hbm, dst_vmem, idx_ref, base, n_rows, sem):
    def copy(r):
        return pltpu.make_async_copy(src_hbm.at[pl.ds(idx_ref[base + r], 1), :], dst_vmem.at[pl.ds(r, 1), :], sem)

    def start_pair(q, carry):
        copy(2 * q).start(priority=0)
        copy(2 * q + 1).start(priority=1)
        return carry

    def wait(r, carry):
        copy(r).wait()
        return carry

    assert n_rows % 2 == 0
    lax.fori_loop(0, n_rows // 2, start_pair, 0, unroll=4)
    lax.fori_loop(0, n_rows, wait, 0, unroll=8)


def _moe_gather_kernel(tok_ref, nused_ref, h_hbm, o_ref, buf, sem, *, tm):
    i = pl.program_id(0)

    @pl.when(i < nused_ref[0])
    def _():
        _gather_rows(h_hbm, buf, tok_ref, i * tm, tm, sem)
        o_ref[...] = _unpack_pairs(buf[...], buf.shape[1]).astype(o_ref.dtype)

    @pl.when(i >= nused_ref[0])
    def _():
        o_ref[...] = jnp.zeros_like(o_ref)


def _moe_gather(slot_tok, nused, h, *, tm):
    p = slot_tok.shape[0]
    half = h.shape[1]
    return pl.pallas_call(
        functools.partial(_moe_gather_kernel, tm=tm),
        grid_spec=pltpu.PrefetchScalarGridSpec(
            num_scalar_prefetch=2,
            grid=(p // tm,),
            in_specs=[pl.BlockSpec(memory_space=pl.ANY)],
            out_specs=pl.BlockSpec((tm, 2 * half), lambda i, tok, nu: (i, 0)),
            scratch_shapes=[pltpu.VMEM((tm, half), jnp.uint32), pltpu.SemaphoreType.DMA(())]),
        out_shape=jax.ShapeDtypeStruct((p, 2 * half), BF16),
        compiler_params=_cparams(1, VMEM_LIMIT),
        name="moe_gather",
    )(slot_tok, nused, h)


def _expert_row_loop(n_blocks, in_copy, out_copy, compute, after_block=None):
    in_copy(0, 0).start()

    def body(r, carry):
        slot = r & 1
        pl.when(r + 1 < n_blocks)(lambda: in_copy(r + 1, 1 - slot).start())
        in_copy(r, slot).wait()
        pl.when(r >= 2)(lambda: out_copy(r - 2, slot).wait())
        compute(slot)
        out_copy(r, slot).start()
        if after_block is not None:
            after_block(r)
        return carry

    lax.fori_loop(0, n_blocks, body, 0)
    pl.when(n_blocks >= 2)(lambda: out_copy(n_blocks - 2, n_blocks & 1).wait())
    out_copy(n_blocks - 1, (n_blocks - 1) & 1).wait()


_W_SLAB = 256
_SLABS_PER_BLOCK = 8


def _moe_up_kernel(bstart_ref, bcount_ref, w1_hbm, w3_hbm, xs_hbm, _zeros_hbm, h_hbm, wb, stage, xbuf, obuf,
                   w_sem, in_sem, out_sem, *, tm):
    e = pl.program_id(0)
    n_exp = pl.num_programs(0)
    n_slab = wb.shape[2] // _W_SLAB
    cur = e & 1
    n_blocks = bcount_ref[e]
    row = lambda r: pl.multiple_of((bstart_ref[e] + r) * tm, tm)
    slab_rows = lambda s: pl.ds(pl.multiple_of(s * _W_SLAB, _W_SLAB), _W_SLAB)

    def slab_copies(expert, s):
        return [pltpu.make_async_copy(w_hbm.at[expert, slab_rows(s), :], stage.at[s & 1, m], w_sem.at[s & 1])
                for m, w_hbm in enumerate((w1_hbm, w3_hbm))]

    def start_slab(expert, s):
        for cp in slab_copies(expert, s):
            cp.start()

    def convert_slab(expert, s, dst):
        for cp in slab_copies(expert, s):
            cp.wait()
        for m in range(2):
            wb[dst, m, slab_rows(s), :] = stage[s & 1, m].astype(BF16)
        pl.when(s + 2 < n_slab)(lambda: start_slab(expert, s + 2))

    def convert_range(expert, lo, hi, dst):
        def body(s, carry):
            convert_slab(expert, s, dst)
            return carry
        lax.fori_loop(lo, hi, body, 0)

    @pl.when(e == 0)
    def _():
        start_slab(0, 0)
        start_slab(0, 1)
        convert_range(0, 0, n_slab, 0)

    has_next = e + 1 < n_exp

    @pl.when(has_next)
    def _():
        start_slab(e + 1, 0)
        start_slab(e + 1, 1)

    def in_copy(r, slot):
        return pltpu.make_async_copy(xs_hbm.at[pl.ds(row(r), tm), :], xbuf.at[slot], in_sem.at[slot])

    def out_copy(r, slot):
        return pltpu.make_async_copy(obuf.at[slot], h_hbm.at[pl.ds(row(r), tm), :], out_sem.at[slot])

    def compute(slot):
        x = xbuf[slot]
        a = jnp.dot(x, wb[cur, 0], preferred_element_type=F32)
        b = jnp.dot(x, wb[cur, 1], preferred_element_type=F32)
        obuf[slot] = (a * _sigmoid(a) * b).astype(BF16)

    def after_block(r):
        for q in range(_SLABS_PER_BLOCK):
            s = r * _SLABS_PER_BLOCK + q
            pl.when(jnp.logical_and(has_next, s < n_slab))(lambda s=s: convert_slab(e + 1, s, 1 - cur))

    pl.when(n_blocks > 0)(lambda: _expert_row_loop(n_blocks, in_copy, out_copy, compute, after_block))

    @pl.when(has_next)
    def _():
        convert_range(e + 1, jnp.minimum(n_blocks * _SLABS_PER_BLOCK, n_slab), n_slab, 1 - cur)


def _moe_up(bstart, bcount, xs, w1, w3, *, tm):
    p, d = xs.shape
    n_exp, _, ff = w1.shape
    any_spec = pl.BlockSpec(memory_space=pl.ANY)
    return pl.pallas_call(
        functools.partial(_moe_up_kernel, tm=tm),
        grid_spec=pltpu.PrefetchScalarGridSpec(
            num_scalar_prefetch=2,
            grid=(n_exp,),
            in_specs=[any_spec] * 4,
            out_specs=any_spec,
            scratch_shapes=[pltpu.VMEM((2, 2, d, ff), BF16), pltpu.VMEM((2, 2, _W_SLAB, ff), F32),
                            pltpu.VMEM((2, tm, d), BF16), pltpu.VMEM((2, tm, ff), BF16),
                            pltpu.SemaphoreType.DMA((2,)), pltpu.SemaphoreType.DMA((2,)),
                            pltpu.SemaphoreType.DMA((2,))]),
        out_shape=jax.ShapeDtypeStruct((p, ff), BF16),
        input_output_aliases={5: 0},
        compiler_params=_cparams(1, VMEM_LIMIT),
        name="moe_up",
    )(bstart, bcount, w1, w3, xs, jnp.zeros((p, ff), BF16))


def _moe_down_kernel(bstart_ref, bcount_ref, w2_ref, hs_hbm, _zeros_hbm, yb_hbm, w2b, hbuf, obuf, in_sem, out_sem,
                     *, tm, tn):
    e = pl.program_id(0)
    n_blocks = bcount_ref[e]
    row = lambda r: pl.multiple_of((bstart_ref[e] + r) * tm, tm)

    @pl.when(n_blocks > 0)
    def _():
        w2b[...] = w2_ref[0].astype(BF16)

        def in_copy(r, slot):
            return pltpu.make_async_copy(hs_hbm.at[pl.ds(row(r), tm), :], hbuf.at[slot], in_sem.at[slot])

        def out_copy(r, slot):
            return pltpu.make_async_copy(obuf.at[slot], yb_hbm.at[pl.ds(row(r), tm), :], out_sem.at[slot])

        def compute(slot):
            h = hbuf[slot]
            for n in range(w2b.shape[1] // tn):
                y = jnp.dot(h, w2b[:, n * tn:(n + 1) * tn], preferred_element_type=F32)
                obuf[slot, :, n * tn // 2:(n + 1) * tn // 2] = _pack_pairs(y, tn // 2)

        _expert_row_loop(n_blocks, in_copy, out_copy, compute)


def _moe_down(bstart, bcount, hs, w2, *, tm, tn):
    p = hs.shape[0]
    n_exp, ff, d = w2.shape
    return pl.pallas_call(
        functools.partial(_moe_down_kernel, tm=tm, tn=tn),
        grid_spec=pltpu.PrefetchScalarGridSpec(
            num_scalar_prefetch=2,
            grid=(n_exp,),
            in_specs=[pl.BlockSpec((1, ff, d), lambda e, bs, bc: (e, 0, 0)), pl.BlockSpec(memory_space=pl.ANY),
                      pl.BlockSpec(memory_space=pl.ANY)],
            out_specs=pl.BlockSpec(memory_space=pl.ANY),
            scratch_shapes=[pltpu.VMEM((ff, d), BF16), pltpu.VMEM((2, tm, ff), BF16),
                            pltpu.VMEM((2, tm, d // 2), jnp.uint32),
                            pltpu.SemaphoreType.DMA((2,)), pltpu.SemaphoreType.DMA((2,))]),
        out_shape=jax.ShapeDtypeStruct((p, d // 2), jnp.uint32),
        input_output_aliases={4: 0},
        compiler_params=_cparams(1, VMEM_LIMIT),
        name="moe_down",
    )(bstart, bcount, w2, hs, jnp.zeros((p, d // 2), jnp.uint32))


def _moe_combine_kernel(p0_ref, p1_ref, xp_ref, xs_ref, g0_ref, g1_ref, nw_ref, yb_hbm, op_ref, os_ref,
                        buf0, buf1, sem, *, tc, nb_p, half):
    i = pl.program_id(0)
    _gather_rows(yb_hbm, buf0, p0_ref, i * tc, tc, sem)
    _gather_rows(yb_hbm, buf1, p1_ref, i * tc, tc, sem)

    def emit(x_ref, o_ref):
        x = (x_ref[...] + g0_ref[:, 0:1] * _unpack_pairs(buf0[...], half)
             + g1_ref[:, 0:1] * _unpack_pairs(buf1[...], half))
        o_ref[...] = _rms(x, nw_ref[...])

    pl.when(i < nb_p)(lambda: emit(xp_ref, op_ref))
    pl.when(i >= nb_p)(lambda: emit(xs_ref, os_ref))


def _moe_combine(pos0, pos1, xp, xs, g0, g1, norm_w, yb, *, tc, half):
    d = xp.shape[1]
    nb_p, nb_s = xp.shape[0] // tc, xs.shape[0] // tc
    p_idx = lambda i, a, b: (jnp.minimum(i, nb_p - 1), 0)
    s_idx = lambda i, a, b: (jnp.maximum(i - nb_p, 0), 0)
    return pl.pallas_call(
        functools.partial(_moe_combine_kernel, tc=tc, nb_p=nb_p, half=half),
        grid_spec=pltpu.PrefetchScalarGridSpec(
            num_scalar_prefetch=2,
            grid=(nb_p + nb_s,),
            in_specs=[pl.BlockSpec((tc, d), p_idx),
                      pl.BlockSpec((tc, d), s_idx),
                      pl.BlockSpec((tc, LANES), lambda i, a, b: (i, 0)),
                      pl.BlockSpec((tc, LANES), lambda i, a, b: (i, 0)),
                      pl.BlockSpec((1, d), lambda i, a, b: (0, 0)),
                      pl.BlockSpec(memory_space=pl.ANY)],
            out_specs=[pl.BlockSpec((tc, d), p_idx), pl.BlockSpec((tc, d), s_idx)],
            scratch_shapes=[pltpu.VMEM((tc, d // 2), jnp.uint32), pltpu.VMEM((tc, d // 2), jnp.uint32),
                            pltpu.SemaphoreType.DMA(())]),
        out_shape=[jax.ShapeDtypeStruct(xp.shape, F32), jax.ShapeDtypeStruct(xs.shape, F32)],
        compiler_params=_cparams(1, VMEM_LIMIT),
        name="moe_combine",
    )(pos0, pos1, xp, xs, g0, g1, norm_w, yb)


def _route(logits, bg, be, *, tm):
    t = logits.shape[0]
    pg = jax.nn.softmax(logits[:, :N_GROUPS] + bg, axis=-1)
    g_idx = jnp.argmax(pg, axis=-1).astype(jnp.int32)
    p_sel = jnp.take_along_axis(pg, g_idx[:, None], axis=-1)
    le = (logits[:, N_GROUPS:N_GROUPS + N_EXPERTS] + be).reshape(t, N_GROUPS, EXPERTS_PER_GROUP)
    le = jnp.take_along_axis(le, g_idx[:, None, None], axis=1)[:, 0]
    pe = jax.nn.softmax(le, axis=-1)
    top_v, top_i = lax.top_k(pe, TOP_K)
    gate = p_sel * top_v / jnp.sum(top_v, axis=-1, keepdims=True)
    e_flat = (g_idx[:, None] * EXPERTS_PER_GROUP + top_i.astype(jnp.int32)).reshape(-1)
    a = t * TOP_K
    onehot = (e_flat[:, None] == jnp.arange(N_EXPERTS, dtype=jnp.int32)[None, :]).astype(jnp.int32)
    rank = jnp.sum((jnp.cumsum(onehot, axis=0) - onehot) * onehot, axis=1)
    counts = jnp.sum(onehot, axis=0)
    padded = (counts + tm - 1) // tm * tm
    pad_end = jnp.cumsum(padded)
    dest = (pad_end - padded)[e_flat] + rank
    n_blocks = -(-(a + N_EXPERTS * (tm - 1)) // tm)
    tok = jnp.arange(a, dtype=jnp.int32) // TOP_K
    slot_tok = jnp.zeros((n_blocks * tm,), jnp.int32).at[dest].set(tok)
    bstart = ((pad_end - padded) // tm).astype(jnp.int32)
    bcount = (padded // tm).astype(jnp.int32)
    nused = (pad_end[-1] // tm).astype(jnp.int32).reshape(1)
    dest = dest.reshape(t, TOP_K)
    return gate, dest[:, 0], dest[:, 1], slot_tok, bstart, bcount, nused


def _moe(xp, xs, norm2_w, wr, bg, be, w1, w3, w2, final_w, *, tm_route, tm_blk, tn, tc):
    h, logits = _router(xp, xs, norm2_w, wr, tm=tm_route)
    gate, pos0, pos1, slot_tok, bstart, bcount, nused = _route(logits, bg, be, tm=tm_blk)
    xg = _moe_gather(slot_tok, nused, h, tm=tm_blk)
    hs = _moe_up(bstart, bcount, xg, w1, w3, tm=tm_blk)
    yb = _moe_down(bstart, bcount, hs, w2, tm=tm_blk, tn=tn)
    g0 = jnp.broadcast_to(gate[:, 0:1], (h.shape[0], LANES))
    g1 = jnp.broadcast_to(gate[:, 1:2], (h.shape[0], LANES))
    return _moe_combine(pos0, pos1, xp, xs, g0, g1, final_w, yb, tc=tc, half=tn // 2)


def kernel(x_prompt, x_sample, state_mlstm_c, state_mlstm_n, state_mlstm_m, state_rwkv, state_rwkv_shift, norm1_w, w_in, w_out, ml_b_i, ml_b_f, ml_norm_w, rw_mu, rw_w0, rw_w2, rw_a0, rw_a2, rw_g2, rw_k_k, rw_k_a, rw_r_k, rw_ln_w, rw_ln_b, norm2_w, router_group_w, router_group_b, router_expert_w, router_expert_b, moe_w1, moe_w3, moe_w2, final_norm_w):
    assert w_in.shape[0] == 1, "single-layer trunk"
    bp, seq, d = x_prompt.shape
    bs = x_sample.shape[0]
    tp = bp * seq
    ml_proj = 2 * ML_HEADS * ML_DK + 2 * ML_WIDTH + 2 * ML_HEADS
    rw_proj = 3 * RW_WIDTH + RW_LORA
    rw_col0 = -(-ml_proj // RW_GROUP) * RW_GROUP
    n_groups = RW_WIDTH // RW_GROUP

    w_cat = _pack_w_in(jnp.swapaxes(w_in[0], 0, 1), split=ml_proj, width=rw_col0, tr=RW_GROUP)
    w_o = w_out[0].astype(BF16)
    mu = rw_mu[0]
    rows = [rw_w0[0], rw_a0[0], rw_k_k[0], rw_k_a[0], rw_r_k[0].reshape(-1), rw_ln_w[0], rw_ln_b[0],
            mu[:RW_WIDTH], mu[RW_WIDTH:2 * RW_WIDTH], mu[2 * RW_WIDTH:3 * RW_WIDTH]]
    prm = jnp.concatenate([jnp.stack(rows), jnp.zeros((16 - len(rows), RW_WIDTH), F32)], axis=0)
    mu_l = mu[3 * RW_WIDTH:][None]
    wl = jnp.zeros((RW_LORA, 3, RW_WIDTH), F32)
    wl = wl.at[:RW_DECAY_LORA, 0].set(rw_w2[0]).at[RW_DECAY_LORA:RW_DECAY_LORA + RW_A_LORA, 1].set(rw_a2[0])
    wl = wl.at[RW_DECAY_LORA + RW_A_LORA:, 2].set(rw_g2[0])
    wl = wl.reshape(RW_LORA, 3, n_groups, RW_GROUP).transpose(2, 0, 1, 3).reshape(n_groups, RW_LORA, 3 * RW_GROUP)
    wl = wl.astype(BF16)
    gate_bias = jnp.zeros((1, LANES), F32).at[0, :ML_HEADS].set(ml_b_i[0]).at[0, ML_HEADS:2 * ML_HEADS].set(ml_b_f[0])
    ml_nw = ml_norm_w[0][None]
    wr = jnp.concatenate([router_group_w[0], router_expert_w[0],
                          jnp.zeros((d, LANES - N_GROUPS - N_EXPERTS), F32)], axis=1)

    xp = x_prompt.reshape(tp, d)
    xs = x_sample.reshape(bs, d)
    proj_p = _in_proj(xp, norm1_w, w_cat, tm=512, tn=1280)
    proj_s = _in_proj(xs, norm1_w, w_cat, tm=bs, tn=1280)

    y_ml_p, p_c, p_nm = _mlstm_prompt(proj_p, gate_bias, ml_nw, batch=bp, seq=seq)
    y_rw_p, p_s = _rwkv_prompt(proj_p, prm, mu_l, wl, batch=bp, seq=seq, rw_col0=rw_col0, n_groups=n_groups, gps=2)
    p_sh = jnp.concatenate([lax.slice(proj_p, (b * seq + seq - 1, rw_col0), (b * seq + seq, rw_col0 + rw_proj))
                            for b in range(bp)], axis=0)

    y_ml_s, s_c, s_n, s_m = _mlstm_sample(proj_s[:, None, :], gate_bias, ml_nw, state_mlstm_c[0], state_mlstm_n[0],
                                          state_mlstm_m[0][:, None, :])
    r, k, v, g, rt, kt, vt, kkt, kat, dt = _rwkv_sample_prep(proj_s, state_rwkv_shift[0], prm, mu_l, wl, rows=bs,
                                                             rw_col0=rw_col0, n_groups=n_groups)
    s_t, y_t = _rwkv_sample_state(jnp.transpose(state_rwkv[0], (1, 2, 3, 0)), rt, kt, kkt, kat, dt, vt)
    s_s = jnp.transpose(s_t, (3, 0, 1, 2))
    y_rw_s = _rwkv_sample_post(y_t, r, k, v, g, prm)
    s_sh = proj_s[:, rw_col0:rw_col0 + rw_proj]

    x2_p = _out_proj(y_ml_p, y_rw_p, w_o, xp, tm=512, tn=1024)
    x2_s = _out_proj(y_ml_s.reshape(bs, ML_WIDTH), y_rw_s, w_o, xs, tm=bs, tn=1024)
    y_p, y_s = _moe(x2_p, x2_s, norm2_w, wr, router_group_b[0], router_expert_b[0], moe_w1[0], moe_w3[0],
                    moe_w2[0], final_norm_w[None], tm_route=128, tm_blk=256, tn=1024, tc=128)

    lead = lambda a: a[None]
    return (y_p.reshape(bp, seq, d), y_s.reshape(bs, 1, d),
            lead(p_c), lead(p_nm[:, :, 0]), lead(p_nm[:, :, 1, 0]), lead(p_s), lead(p_sh),
            lead(s_c), lead(s_n), lead(s_m[:, 0]), lead(s_s), lead(s_sh))
```

```python
import functools

import jax
import jax.numpy as jnp
from jax import lax
from jax.experimental import pallas as pl
from jax.experimental.pallas import tpu as pltpu

F32 = jnp.float32
BF16 = jnp.bfloat16

D_MODEL = 4096
ML_HEADS = 4
ML_DK = 256
ML_DV = 512
ML_WIDTH = ML_HEADS * ML_DV
GATE_SOFTCAP = 15.0
RW_HEAD = 64
RW_WIDTH = D_MODEL - ML_WIDTH
RW_HEADS = RW_WIDTH // RW_HEAD
RW_DECAY_LORA = 96
RW_A_LORA = 96
RW_GATE_LORA = 64
RW_GN_EPS = 64e-5
N_GROUPS = 4
EXPERTS_PER_GROUP = 8
N_EXPERTS = N_GROUPS * EXPERTS_PER_GROUP
TOP_K = 2
EPS = 1e-6

LANES = 128
RW_GROUP = 256
RW_GHEADS = RW_GROUP // RW_HEAD
RW_LORA = RW_DECAY_LORA + RW_A_LORA + RW_GATE_LORA
RW_CHUNK = 64
ML_CHUNK = 256
_SAMPLE_ROWS_PER_STAGE = 8
VMEM_LIMIT = 56 * 1024 * 1024


def _cparams(n_axes, vmem=None):
    return pltpu.CompilerParams(dimension_semantics=("arbitrary",) * n_axes, vmem_limit_bytes=vmem)


def _sigmoid(x):
    return 1.0 / (1.0 + jnp.exp(-x))


def _softplus(x):
    return jnp.maximum(x, 0.0) + jnp.log(1.0 + jnp.exp(-jnp.abs(x)))


def _soft_cap(x):
    return GATE_SOFTCAP * jnp.tanh(x / GATE_SOFTCAP)


def _dot(a, b):
    return jnp.dot(a.astype(BF16), b.astype(BF16), preferred_element_type=F32)


def _dot_nt(a, b):
    return lax.dot_general(a.astype(BF16), b.astype(BF16), (((1,), (1,)), ((), ())),
                           preferred_element_type=F32)


def _split2(x):
    hi = x.astype(BF16)
    lo = (x - hi.astype(F32)).astype(BF16)
    return hi, lo


def _split3(x):
    hi = x.astype(BF16)
    r = x - hi.astype(F32)
    mid = r.astype(BF16)
    lo = (r - mid.astype(F32)).astype(BF16)
    return hi, mid, lo


def _iota(shape, axis):
    return lax.broadcasted_iota(jnp.int32, shape, axis)


def _pack_pairs(x, half):
    bits = lax.bitcast_convert_type(x.astype(BF16).astype(F32), jnp.uint32)
    parts = []
    for g in range(x.shape[1] // (2 * half)):
        lo = bits[:, 2 * half * g:2 * half * g + half]
        hi = bits[:, 2 * half * g + half:2 * half * (g + 1)]
        parts.append((lo >> 16) | hi)
    return parts[0] if len(parts) == 1 else jnp.concatenate(parts, axis=1)


def _unpack_pairs(p, half):
    parts = []
    for g in range(p.shape[1] // half):
        w = p[:, half * g:half * (g + 1)]
        parts.append(lax.bitcast_convert_type(w << 16, F32))
        parts.append(lax.bitcast_convert_type(w & jnp.uint32(0xFFFF0000), F32))
    return jnp.concatenate(parts, axis=1)


def _head_block_ones(n):
    return jnp.where((_iota((n, n), 0) >> 6) == (_iota((n, n), 1) >> 6), 1.0, 0.0).astype(BF16)


def _seg_sum(x, bd, passes=2):
    hi, lo = _split2(x)
    out = jnp.dot(hi, bd, preferred_element_type=F32)
    return out + jnp.dot(lo, bd, preferred_element_type=F32) if passes == 2 else out


_P_W0, _P_A0, _P_KK, _P_KA, _P_RK, _P_LNW, _P_LNB, _P_MUR, _P_MUK, _P_MUV = range(10)


def _rwkv_prep(p_r, p_k, p_v, p_l, prev_r, prev_k, prev_v, prev_l, prm, mu_l, wl, bd):
    row = lambda i: prm[i:i + 1, :]
    xr = p_r + (prev_r - p_r) * row(_P_MUR)
    xk = p_k + (prev_k - p_k) * row(_P_MUK)
    xv = p_v + (prev_v - p_v) * row(_P_MUV)
    xl = p_l + (prev_l - p_l) * mu_l
    lane = _iota(xl.shape, 1)
    z = jnp.where(lane < RW_DECAY_LORA, jnp.tanh(xl),
                  jnp.where(lane < RW_DECAY_LORA + RW_A_LORA, xl, _sigmoid(xl)))
    lo = _dot(z, wl)
    lw, la, lg = lo[:, :RW_GROUP], lo[:, RW_GROUP:2 * RW_GROUP], lo[:, 2 * RW_GROUP:]
    w = -_softplus(-(row(_P_W0) + lw)) - 0.5
    logd = -jnp.exp(w)
    a = _sigmoid(row(_P_A0) + la)
    kk = xk * row(_P_KK)
    kk = kk / jnp.maximum(jnp.sqrt(_seg_sum(kk * kk, bd)), 1e-12)
    k = xk * (1.0 + (a - 1.0) * row(_P_KA))
    return xr, k, xv, kk, a, logd, lg


def _rwkv_post(y, r, k, v, g, prm, bd):
    row = lambda i: prm[i:i + 1, :]
    mean = _seg_sum(y, bd, passes=1) * (1.0 / RW_HEAD)
    yc = y - mean
    var = _seg_sum(yc * yc, bd, passes=1) * (1.0 / RW_HEAD)
    yn = yc * lax.rsqrt(var + RW_GN_EPS) * row(_P_LNW) + row(_P_LNB)
    bonus = _seg_sum(r * k * row(_P_RK), bd, passes=1) * v
    return (yn + bonus) * g


def _rwkv_chunk(r, k, v, kk, a, logd, G, bd):
    L = r.shape[0]
    tril = jnp.where(_iota((L, L), 1) <= _iota((L, L), 0), 1.0, 0.0).astype(BF16)
    d_hi, d_lo = _split2(logd)
    clog = (jnp.dot(tril, d_hi, preferred_element_type=F32)
            + jnp.dot(tril, d_lo, preferred_element_type=F32))
    clog_l = clog[L - 1:L, :]
    n_in = jnp.exp(-clog)
    to_end = jnp.exp(clog_l - clog)
    ka = kk * a
    at = -kk * jnp.exp(clog - logd)
    rt = r * jnp.exp(clog)
    lane_head = _iota((1, RW_GROUP), 1) >> 6
    masks = [lane_head == h for h in range(RW_GHEADS)]
    zero = jnp.zeros_like(at)
    bf = lambda x: x.astype(BF16)
    lhs = jnp.concatenate([jnp.where(m, x, zero) for m in masks for x in (at, rt)], axis=0)
    xx = _dot_nt(lhs, jnp.concatenate([ka * n_in, k * n_in], axis=0))
    t2 = _iota((L, 2 * L), 0)
    c2 = _iota((L, 2 * L), 1)
    s2 = c2 & (L - 1)
    right = c2 >= L
    eye_pad = jnp.where(c2 == t2 + L, 1.0, 0.0)
    zeros_v = jnp.zeros((L, RW_GROUP), BF16)
    v_b = bf(v)
    vz = jnp.concatenate([zeros_v, v_b], axis=0)
    yield
    zs, rbk = [], []
    makv = zero
    for h, m in enumerate(masks):
        o = 2 * L * h
        nk = jnp.where(s2 < t2, xx[o:o + L], 0.0)
        rbk.append(bf(jnp.where(s2 <= t2, xx[o + L:o + 2 * L], 0.0)))
        makv = makv + jnp.where(m, _dot(jnp.where(right, nk, 0.0), vz), 0.0)
        zs.append(jnp.where(right, eye_pad, nk))
    for _ in range((L - 1).bit_length()):
        yield
        zs = [_dot(z[:, :L], z) + jnp.where(right, z, 0.0) for z in zs]
    yield
    ws = _dot_nt(jnp.concatenate([at, rt], axis=0), G)
    xz = jnp.concatenate([zeros_v, bf(ws[:L] + makv)], axis=0)
    yield
    w_all = zero
    for z_h, m in zip(zs, masks):
        w_all = w_all + jnp.where(m, _dot(z_h, xz), 0.0)
    wv = jnp.concatenate([bf(w_all), v_b], axis=0)
    yield
    y = ws[L:]
    for rbk_h, m in zip(rbk, masks):
        y = y + jnp.where(m, jnp.dot(rbk_h, wv, preferred_element_type=F32), 0.0)
    upd = _dot(wv.astype(F32).T, jnp.concatenate([ka * to_end, k * to_end], axis=0))
    g_new = G * jnp.exp(clog_l) + jnp.where(bd > 0, upd, 0.0)
    return y, g_new


def _interleave(gens):
    results = [None] * len(gens)
    live = list(range(len(gens)))
    while live:
        for i in list(live):
            try:
                next(gens[i])
            except StopIteration as stop:
                results[i] = stop.value
                live.remove(i)
    return results


def _shift_rows(x, first_row):
    rolled = pltpu.roll(x, 1, axis=0)
    return jnp.where(_iota(x.shape, 0) == 0, first_row, rolled)


def _rwkv_prompt_kernel(*refs, n_chunks, batch, gps):
    n_chain = gps * batch
    p_refs = refs[:4 * n_chain]
    prm_ref, mul_ref, wl_ref, y_ref, s_ref, g_scr, carry_scr = refs[4 * n_chain:]
    c = pl.program_id(1)

    @pl.when(c == 0)
    def _():
        g_scr[...] = jnp.zeros_like(g_scr)
        carry_scr[...] = jnp.zeros_like(carry_scr)

    bd = _head_block_ones(RW_GROUP)

    def chain(q):
        gi = q // batch
        prm = prm_ref[:, gi * RW_GROUP:(gi + 1) * RW_GROUP]
        ps = [ref[...] for ref in p_refs[4 * q:4 * q + 4]]
        L = ps[0].shape[0]
        prevs = [_shift_rows(x, carry_scr[q, i:i + 1, :]) for i, x in enumerate(ps)]
        r, k, v, kk, a, logd, g = _rwkv_prep(*ps, *prevs, prm, mul_ref[...], wl_ref[gi], bd)
        y, g_new = yield from _rwkv_chunk(r, k, v, kk, a, logd, g_scr[q], bd)
        yield
        out = _rwkv_post(y, r, k, v, g, prm, bd).astype(y_ref.dtype)
        return out, g_new, [x[L - 1:L, :] for x in ps]

    results = _interleave([chain(q) for q in range(n_chain)])
    for q, (out, g_new, last_rows) in enumerate(results):
        gi, b = divmod(q, batch)
        y_ref[b, :, gi * RW_GROUP:(gi + 1) * RW_GROUP] = out
        g_scr[q] = g_new
        for i, x in enumerate(last_rows):
            carry_scr[q, i:i + 1, :] = x

    @pl.when(c == n_chunks - 1)
    def _():
        for q, (_, g_new, _) in enumerate(results):
            gi, b = divmod(q, batch)
            for h in range(RW_GHEADS):
                s_ref[b, gi * RW_GHEADS + h] = g_new[RW_HEAD * h:RW_HEAD * (h + 1), RW_HEAD * h:RW_HEAD * (h + 1)]


def _rwkv_prompt(proj, prm, mu_l, wl, *, batch, seq, rw_col0, n_groups, gps):
    L = RW_CHUNK
    nc = seq // L
    cb = rw_col0 // RW_GROUP
    width = n_groups * RW_GROUP
    p_specs = []
    for gi in range(gps):
        for b in range(batch):
            col = lambda g, off, gi=gi: cb + off + g * gps + gi
            rows = lambda c, b=b: b * nc + c
            p_specs += [pl.BlockSpec((L, RW_GROUP), lambda g, c, o=o, col=col, rows=rows: (rows(c), col(g, o)))
                        for o in (0, n_groups, 2 * n_groups)]
            p_specs += [pl.BlockSpec((L, RW_GROUP), lambda g, c, rows=rows: (rows(c), cb + 3 * n_groups))]
    n_chain = gps * batch
    y, s = pl.pallas_call(
        functools.partial(_rwkv_prompt_kernel, n_chunks=nc, batch=batch, gps=gps),
        grid=(n_groups // gps, nc),
        in_specs=p_specs + [pl.BlockSpec((16, gps * RW_GROUP), lambda g, c: (0, g)),
                            pl.BlockSpec((1, RW_LORA), lambda g, c: (0, 0)),
                            pl.BlockSpec((gps, RW_LORA, 3 * RW_GROUP), lambda g, c: (g, 0, 0))],
        out_specs=[pl.BlockSpec((batch, L, gps * RW_GROUP), lambda g, c: (0, c, g)),
                   pl.BlockSpec((batch, gps * RW_GHEADS, RW_HEAD, RW_HEAD), lambda g, c: (0, g, 0, 0))],
        out_shape=[jax.ShapeDtypeStruct((batch, seq, width), BF16),
                   jax.ShapeDtypeStruct((batch, n_groups * RW_GHEADS, RW_HEAD, RW_HEAD), F32)],
        scratch_shapes=[pltpu.VMEM((n_chain, RW_GROUP, RW_GROUP), F32), pltpu.VMEM((n_chain, 8, RW_GROUP), F32)],
        compiler_params=_cparams(2),
        name="rwkv_prompt",
    )(*([proj] * (4 * n_chain)), prm, mu_l, wl)
    return y.reshape(batch * seq, width), s


def _rwkv_sample_prep_kernel(pr_ref, pk_ref, pv_ref, pl_ref, sr_ref, sk_ref, sv_ref, sl_ref, prm_ref, mul_ref,
                             wl_ref, r_ref, k_ref, v_ref, g_ref, rt_ref, kt_ref, vt_ref, kkt_ref, kat_ref, dt_ref):
    bd = _head_block_ones(RW_GROUP)
    r, k, v, kk, a, logd, g = _rwkv_prep(pr_ref[...], pk_ref[...], pv_ref[...], pl_ref[...], sr_ref[...],
                                         sk_ref[...], sv_ref[...], sl_ref[...], prm_ref[...], mul_ref[...],
                                         wl_ref[0], bd)
    r_ref[...] = r
    k_ref[...] = k
    v_ref[...] = v
    g_ref[...] = g
    rt_ref[...] = r.T
    kt_ref[...] = k.T
    vt_ref[...] = v.T
    kkt_ref[...] = kk.T
    kat_ref[...] = (kk * a).T
    dt_ref[...] = jnp.exp(logd).T


def _rwkv_sample_prep(proj, shift0, prm, mu_l, wl, *, rows, rw_col0, n_groups):
    cb = rw_col0 // RW_GROUP
    pspec = lambda off: pl.BlockSpec((rows, RW_GROUP), lambda g, off=off: (0, cb + off + g))
    sspec = lambda off: pl.BlockSpec((rows, RW_GROUP), lambda g, off=off: (0, off + g))
    ospec = pl.BlockSpec((rows, RW_GROUP), lambda g: (0, g))
    tspec = pl.BlockSpec((RW_GROUP, rows), lambda g: (g, 0))
    width = n_groups * RW_GROUP
    return pl.pallas_call(
        _rwkv_sample_prep_kernel,
        grid=(n_groups,),
        in_specs=[pspec(0), pspec(n_groups), pspec(2 * n_groups),
                  pl.BlockSpec((rows, RW_GROUP), lambda g: (0, cb + 3 * n_groups)),
                  sspec(0), sspec(n_groups), sspec(2 * n_groups),
                  pl.BlockSpec((rows, RW_GROUP), lambda g: (0, 3 * n_groups)),
                  pl.BlockSpec((16, RW_GROUP), lambda g: (0, g)),
                  pl.BlockSpec((1, RW_LORA), lambda g: (0, 0)),
                  pl.BlockSpec((1, RW_LORA, 3 * RW_GROUP), lambda g: (g, 0, 0))],
        out_specs=[ospec] * 4 + [tspec] * 6,
        out_shape=[jax.ShapeDtypeStruct((rows, width), F32)] * 4 + [jax.ShapeDtypeStruct((width, rows), F32)] * 6,
        compiler_params=_cparams(1),
        name="rwkv_sample_prep",
    )(proj, proj, proj, proj, shift0, shift0, shift0, shift0, prm, mu_l, wl)


def _rwkv_sample_state_kernel(s_ref, rt_ref, kt_ref, kkt_ref, kat_ref, dt_ref, vt_ref, so_ref, yt_ref):
    n_kk = -kkt_ref[...]
    r, k, ka, d = rt_ref[...], kt_ref[...], kat_ref[...], dt_ref[...]
    for i0 in range(0, s_ref.shape[1], _SAMPLE_ROWS_PER_STAGE):
        rows = range(i0, i0 + _SAMPLE_ROWS_PER_STAGE)
        sa = [jnp.sum(s_ref[0, i] * n_kk, axis=0, keepdims=True) for i in rows]
        ys = []
        for i, sa_i in zip(rows, sa):
            s_new = s_ref[0, i] * d + sa_i * ka + vt_ref[i:i + 1, :] * k
            so_ref[0, i] = s_new
            ys.append(jnp.sum(s_new * r, axis=0, keepdims=True))
        yt_ref[i0:i0 + _SAMPLE_ROWS_PER_STAGE, :] = jnp.concatenate(ys, axis=0)


def _rwkv_sample_state(state_t, rt, kt, kkt, kat, dt, vt):
    nh, n, _, b = state_t.shape
    vspec = pl.BlockSpec((n, b), lambda h: (h, 0))
    sspec = pl.BlockSpec((1, n, n, b), lambda h: (h, 0, 0, 0))
    return pl.pallas_call(
        _rwkv_sample_state_kernel,
        grid=(nh,),
        in_specs=[sspec] + [vspec] * 6,
        out_specs=[sspec, vspec],
        out_shape=[jax.ShapeDtypeStruct(state_t.shape, F32), jax.ShapeDtypeStruct((nh * n, b), F32)],
        compiler_params=_cparams(1),
        name="rwkv_sample_state",
    )(state_t, rt, kt, kkt, kat, dt, vt)


def _rwkv_sample_post_kernel(yt_ref, r_ref, k_ref, v_ref, g_ref, prm_ref, o_ref):
    bd = _head_block_ones(RW_GROUP)
    o_ref[...] = _rwkv_post(yt_ref[...].T, r_ref[...], k_ref[...], v_ref[...], g_ref[...], prm_ref[...],
                            bd).astype(o_ref.dtype)


def _rwkv_sample_post(yt, r, k, v, g, prm):
    rows, width = r.shape
    spec = pl.BlockSpec((rows, RW_GROUP), lambda i: (0, i))
    return pl.pallas_call(
        _rwkv_sample_post_kernel,
        grid=(width // RW_GROUP,),
        in_specs=[pl.BlockSpec((RW_GROUP, rows), lambda i: (i, 0))] + [spec] * 4
                 + [pl.BlockSpec((16, RW_GROUP), lambda i: (0, i))],
        out_specs=spec,
        out_shape=jax.ShapeDtypeStruct((rows, width), BF16),
        compiler_params=_cparams(1),
        name="rwkv_sample_post",
    )(yt, r, k, v, g, prm)


def _mlstm_gates(gates, bias, h):
    capped = _soft_cap(gates + bias)
    lane = _iota(gates.shape, 1)
    i_col = jnp.sum(jnp.where(lane == h, capped, 0.0), axis=1, keepdims=True)
    f_col = jnp.sum(jnp.where(lane == h + ML_HEADS, -_softplus(-capped), 0.0), axis=1, keepdims=True)
    return i_col, f_col


def _mlstm_out(hh, o, norm_w):
    hn = hh * lax.rsqrt(jnp.mean(hh * hh, axis=1, keepdims=True) + EPS) * norm_w
    return hn * _sigmoid(o)


def _mlstm_prompt_kernel(*refs, n_chunks, batch):
    p_refs = refs[:5 * batch]
    gb_ref, nw_ref, y_ref, c_ref, nm_ref, c_scr, n_scr, m_scr = refs[5 * batch:]
    h = pl.program_id(0)
    c = pl.program_id(1)

    @pl.when(c == 0)
    def _():
        c_scr[...] = jnp.zeros_like(c_scr)
        n_scr[...] = jnp.zeros_like(n_scr)
        m_scr[...] = jnp.zeros_like(m_scr)

    def sequence(b):
        q_ref, k_ref, v_ref, o_ref, gt_ref = p_refs[5 * b:5 * b + 5]
        q = q_ref[...] * (ML_DK ** -0.5)
        k = k_ref[...]
        v = v_ref[...]
        L = q.shape[0]
        i_col, f_col = _mlstm_gates(gt_ref[...], gb_ref[...], h)
        t_i = _iota((L, L), 0)
        s_i = _iota((L, L), 1)
        causal = s_i <= t_i
        to_row = lambda col: jnp.sum(jnp.where(t_i == s_i, col, 0.0), axis=0, keepdims=True)
        i_row = to_row(i_col)
        f_row = to_row(f_col)
        yield
        b_col = jnp.sum(jnp.where(causal, f_row, 0.0), axis=1, keepdims=True)
        b_row = jnp.sum(jnp.where(t_i <= s_i, f_col, 0.0), axis=0, keepdims=True)
        m0 = m_scr[b]
        inter = b_col + m0
        dmat = jnp.where(causal, b_col - b_row + i_row, -1e30)
        yield
        m_t = jnp.maximum(inter, jnp.max(dmat, axis=1, keepdims=True))
        w_int = jnp.exp(inter - m_t)
        s = _dot_nt(q, k) * jnp.exp(dmat - m_t)
        c0 = c_scr[b]
        n0 = n_scr[b]
        yield
        num = w_int * _dot(q, c0) + _dot(s, v)
        den = w_int * jnp.sum(q * n0, axis=1, keepdims=True) + jnp.sum(s, axis=1, keepdims=True)
        hh = num / jnp.maximum(jnp.abs(den), jnp.exp(-m_t))
        m_new = m_t[L - 1:L, :]
        b_l = b_col[L - 1:L, :]
        a_end = jnp.exp(b_l - b_col + i_col - m_new)
        dec = jnp.exp(b_l + m0 - m_new)
        ka = k * a_end
        yield
        c_new = dec * c0 + _dot(ka.T, v)
        n_new = dec * n0 + jnp.sum(ka, axis=0, keepdims=True)
        out = _mlstm_out(hh, o_ref[...], nw_ref[...]).astype(y_ref.dtype)
        return out, c_new, n_new, m_new

    results = _interleave([sequence(b) for b in range(batch)])
    for b, (out, c_new, n_new, m_new) in enumerate(results):
        y_ref[b] = out
        c_scr[b] = c_new
        n_scr[b] = n_new
        m_scr[b] = m_new

    @pl.when(c == n_chunks - 1)
    def _():
        for b, (_, c_new, n_new, m_new) in enumerate(results):
            c_ref[b, 0] = c_new
            nm_ref[b, 0] = jnp.concatenate([n_new, jnp.broadcast_to(m_new, (7, ML_DK))], axis=0)


def _mlstm_prompt(proj, gate_bias, norm_w, *, batch, seq):
    L = ML_CHUNK
    nc = seq // L
    nh = ML_HEADS
    kq = nh * ML_DK
    p_specs = []
    for b in range(batch):
        rows = lambda c, b=b: b * nc + c
        p_specs += [pl.BlockSpec((L, ML_DK), lambda h, c, rows=rows: (rows(c), h)),
                    pl.BlockSpec((L, ML_DK), lambda h, c, rows=rows: (rows(c), nh + h)),
                    pl.BlockSpec((L, ML_DV), lambda h, c, rows=rows: (rows(c), 2 * kq // ML_DV + h)),
                    pl.BlockSpec((L, ML_DV), lambda h, c, rows=rows: (rows(c), 2 * kq // ML_DV + nh + h)),
                    pl.BlockSpec((L, LANES), lambda h, c, rows=rows: (rows(c), (2 * kq + 2 * ML_WIDTH) // LANES))]
    y, c_fin, nm = pl.pallas_call(
        functools.partial(_mlstm_prompt_kernel, n_chunks=nc, batch=batch),
        grid=(nh, nc),
        in_specs=p_specs + [pl.BlockSpec((1, LANES), lambda h, c: (0, 0)),
                            pl.BlockSpec((1, ML_DV), lambda h, c: (0, h))],
        out_specs=[pl.BlockSpec((batch, L, ML_DV), lambda h, c: (0, c, h)),
                   pl.BlockSpec((batch, 1, ML_DK, ML_DV), lambda h, c: (0, h, 0, 0)),
                   pl.BlockSpec((batch, 1, 8, ML_DK), lambda h, c: (0, h, 0, 0))],
        out_shape=[jax.ShapeDtypeStruct((batch, seq, ML_WIDTH), BF16),
                   jax.ShapeDtypeStruct((batch, nh, ML_DK, ML_DV), F32),
                   jax.ShapeDtypeStruct((batch, nh, 8, ML_DK), F32)],
        scratch_shapes=[pltpu.VMEM((batch, ML_DK, ML_DV), F32), pltpu.VMEM((batch, 1, ML_DK), F32),
                        pltpu.VMEM((batch, 1, 1), F32)],
        compiler_params=_cparams(2),
        name="mlstm_prompt",
    )(*([proj] * (5 * batch)), gate_bias, norm_w)
    return y.reshape(batch * seq, ML_WIDTH), c_fin, nm


def _mlstm_sample_kernel(q_ref, k_ref, v_ref, o_ref, gt_ref, gb_ref, nw_ref, c_ref, n_ref, m_ref,
                         y_ref, co_ref, no_ref, mo_ref):
    gates = gt_ref[0]
    eye = jnp.where(_iota((ML_DK, ML_DK), 0) == _iota((ML_DK, ML_DK), 1), 1.0, 0.0).astype(BF16)
    for h in range(ML_HEADS):
        q = q_ref[0][:, h * ML_DK:(h + 1) * ML_DK] * (ML_DK ** -0.5)
        k = k_ref[0][:, h * ML_DK:(h + 1) * ML_DK]
        v = v_ref[0][:, h * ML_DV:(h + 1) * ML_DV]
        i_pre, logf = _mlstm_gates(gates, gb_ref[...], h)
        c0 = c_ref[0, h]
        n0 = n_ref[0, h:h + 1, :]
        m0 = m_ref[0][:, h:h + 1]
        inter = logf + m0
        m_t = jnp.maximum(inter, i_pre)
        w_int = jnp.exp(inter - m_t)
        a_new = jnp.exp(i_pre - m_t)
        s = jnp.sum(q * k, axis=1, keepdims=True) * a_new
        qc = _dot(jnp.broadcast_to(q, (8, ML_DK)), c0)[0:1, :]
        num = w_int * qc + s * v
        den = w_int * jnp.sum(q * n0, axis=1, keepdims=True) + s
        hh = num / jnp.maximum(jnp.abs(den), jnp.exp(-m_t))
        k_hi, k_lo = _split2(jnp.broadcast_to(k, (8, ML_DK)))
        nt = (((1,), (1,)), ((), ()))
        k_col = (lax.dot_general(eye, k_hi, nt, preferred_element_type=F32)
                 + lax.dot_general(eye, k_lo, nt, preferred_element_type=F32))[:, 0:1]
        co_ref[0, h] = w_int * c0 + k_col * (a_new * v)
        no_ref[0, h:h + 1, :] = w_int * n0 + a_new * k
        mo_ref[0, :, h:h + 1] = m_t
        y_ref[0, :, h * ML_DV:(h + 1) * ML_DV] = _mlstm_out(
            hh, o_ref[0][:, h * ML_DV:(h + 1) * ML_DV], nw_ref[:, h * ML_DV:(h + 1) * ML_DV]).astype(y_ref.dtype)


def _mlstm_sample(proj3, gate_bias, norm_w, c0, n0, m0):
    b = proj3.shape[0]
    nh = ML_HEADS
    kq = nh * ML_DK
    p3 = lambda w, blk: pl.BlockSpec((1, 1, w), lambda i, blk=blk: (i, 0, blk))
    cspec = pl.BlockSpec((1, nh, ML_DK, ML_DV), lambda i: (i, 0, 0, 0))
    nspec = pl.BlockSpec((1, nh, ML_DK), lambda i: (i, 0, 0))
    mspec = pl.BlockSpec((1, 1, nh), lambda i: (i, 0, 0))
    return pl.pallas_call(
        _mlstm_sample_kernel,
        grid=(b,),
        in_specs=[p3(kq, 0), p3(kq, 1), p3(ML_WIDTH, 2 * kq // ML_WIDTH), p3(ML_WIDTH, 2 * kq // ML_WIDTH + 1),
                  p3(LANES, (2 * kq + 2 * ML_WIDTH) // LANES),
                  pl.BlockSpec((1, LANES), lambda i: (0, 0)),
                  pl.BlockSpec((1, ML_WIDTH), lambda i: (0, 0)),
                  cspec, nspec, mspec],
        out_specs=[pl.BlockSpec((1, 1, ML_WIDTH), lambda i: (i, 0, 0)), cspec, nspec, mspec],
        out_shape=[jax.ShapeDtypeStruct((b, 1, ML_WIDTH), BF16),
                   jax.ShapeDtypeStruct(c0.shape, F32), jax.ShapeDtypeStruct(n0.shape, F32),
                   jax.ShapeDtypeStruct(m0.shape, F32)],
        compiler_params=_cparams(1),
        name="mlstm_sample",
    )(proj3, proj3, proj3, proj3, proj3, gate_bias, norm_w, c0, n0, m0)


def _rms(x, w):
    return x * lax.rsqrt(jnp.mean(x * x, axis=1, keepdims=True) + EPS) * w


def _pack_w_in_kernel(wt_hbm, o_ref, buf, sem, *, n_a, split):
    j = pl.program_id(0)
    tr = o_ref.shape[0]

    def copy(step, slot):
        src = jnp.where(step < n_a, step * tr, split + (step - n_a) * tr)
        return pltpu.make_async_copy(wt_hbm.at[pl.ds(pl.multiple_of(src, 8), tr), :], buf.at[slot], sem.at[slot])

    pl.when(j == 0)(lambda: copy(j, 0).start())
    pl.when(j + 1 < pl.num_programs(0))(lambda: copy(j + 1, (j + 1) & 1).start())
    copy(j, j & 1).wait()
    o_ref[...] = buf[j & 1].astype(BF16)


def _pack_w_in(wt, *, split, width, tr):
    d = wt.shape[1]
    assert split % 8 == 0 and width % tr == 0 and split + width == wt.shape[0]
    n_a = width // tr
    return pl.pallas_call(
        functools.partial(_pack_w_in_kernel, n_a=n_a, split=split),
        grid=(2 * n_a,),
        in_specs=[pl.BlockSpec(memory_space=pl.ANY)],
        out_specs=pl.BlockSpec((tr, d), lambda j: (j, 0)),
        out_shape=jax.ShapeDtypeStruct((2 * width, d), BF16),
        scratch_shapes=[pltpu.VMEM((2, tr, d), F32), pltpu.SemaphoreType.DMA((2,))],
        compiler_params=_cparams(1, VMEM_LIMIT),
        name="pack_w_in",
    )(wt)


def _in_proj_kernel(x_ref, nw_ref, w_ref, o_ref, h_scr):
    @pl.when(pl.program_id(1) == 0)
    def _():
        h_scr[...] = _rms(x_ref[...], nw_ref[...]).astype(BF16)

    o_ref[...] = lax.dot_general(h_scr[...], w_ref[...], (((1,), (1,)), ((), ())), preferred_element_type=F32)


def _in_proj(x, norm_w, wt, *, tm, tn):
    t, d = x.shape
    n = wt.shape[0]
    return pl.pallas_call(
        _in_proj_kernel,
        grid=(t // tm, n // tn),
        in_specs=[pl.BlockSpec((tm, d), lambda i, j: (i, 0), pipeline_mode=pl.Buffered(1)),
                  pl.BlockSpec((1, d), lambda i, j: (0, 0)),
                  pl.BlockSpec((tn, d), lambda i, j: (j, 0))],
        out_specs=pl.BlockSpec((tm, tn), lambda i, j: (i, j)),
        out_shape=jax.ShapeDtypeStruct((t, n), F32),
        scratch_shapes=[pltpu.VMEM((tm, d), BF16)],
        compiler_params=_cparams(2, VMEM_LIMIT),
        name="in_proj",
    )(x, norm_w, wt)


def _out_proj_kernel(ya_ref, yb_ref, wa_ref, wb_ref, x_ref, o_ref):
    o_ref[...] = (x_ref[...] + jnp.dot(ya_ref[...], wa_ref[...], preferred_element_type=F32)
                  + jnp.dot(yb_ref[...], wb_ref[...], preferred_element_type=F32))


def _out_proj(ya, yb, w, x, *, tm, tn):
    t, kh = ya.shape
    n = w.shape[1]
    return pl.pallas_call(
        _out_proj_kernel,
        grid=(t // tm, n // tn),
        in_specs=[pl.BlockSpec((tm, kh), lambda i, j: (i, 0)),
                  pl.BlockSpec((tm, kh), lambda i, j: (i, 0)),
                  pl.BlockSpec((kh, tn), lambda i, j: (0, j)),
                  pl.BlockSpec((kh, tn), lambda i, j: (1, j)),
                  pl.BlockSpec((tm, tn), lambda i, j: (i, j))],
        out_specs=pl.BlockSpec((tm, tn), lambda i, j: (i, j)),
        out_shape=jax.ShapeDtypeStruct((t, n), F32),
        compiler_params=_cparams(2, VMEM_LIMIT),
        name="out_proj",
    )(ya, yb, w, w, x)


def _router_kernel(xp_ref, xs_ref, nw_ref, wr_ref, h_ref, lg_ref, *, nb_p):
    def emit(x_ref):
        h = _rms(x_ref[...], nw_ref[...])
        h_ref[...] = _pack_pairs(h, h.shape[1] // 2)
        lg_ref[...] = jnp.dot(h, wr_ref[...], preferred_element_type=F32, precision=lax.Precision.HIGHEST)

    pl.when(pl.program_id(0) < nb_p)(lambda: emit(xp_ref))
    pl.when(pl.program_id(0) >= nb_p)(lambda: emit(xs_ref))


def _router(xp, xs, norm_w, wr, *, tm):
    d = xp.shape[1]
    nb_p, nb_s = xp.shape[0] // tm, xs.shape[0] // tm
    t = xp.shape[0] + xs.shape[0]
    return pl.pallas_call(
        functools.partial(_router_kernel, nb_p=nb_p),
        grid=(nb_p + nb_s,),
        in_specs=[pl.BlockSpec((tm, d), lambda i: (jnp.minimum(i, nb_p - 1), 0)),
                  pl.BlockSpec((tm, d), lambda i: (jnp.maximum(i - nb_p, 0), 0)),
                  pl.BlockSpec((1, d), lambda i: (0, 0)),
                  pl.BlockSpec((d, LANES), lambda i: (0, 0))],
        out_specs=[pl.BlockSpec((tm, d // 2), lambda i: (i, 0)), pl.BlockSpec((tm, LANES), lambda i: (i, 0))],
        out_shape=[jax.ShapeDtypeStruct((t, d // 2), jnp.uint32), jax.ShapeDtypeStruct((t, LANES), F32)],
        compiler_params=_cparams(1, VMEM_LIMIT),
        name="router",
    )(xp, xs, norm_w, wr)


def _gather_rows(src_hbm, dst_vmem, idx_ref, base, n_rows, sem):
    def copy(r):
        return pltpu.make_async_copy(src_hbm.at[pl.ds(idx_ref[base + r], 1), :], dst_vmem.at[pl.ds(r, 1), :], sem)

    def start_pair(q, carry):
        copy(2 * q).start(priority=0)
        copy(2 * q + 1).start(priority=1)
        return carry

    def wait(r, carry):
        copy(r).wait()
        return carry

    assert n_rows % 2 == 0
    lax.fori_loop(0, n_rows // 2, start_pair, 0, unroll=4)
    lax.fori_loop(0, n_rows, wait, 0, unroll=8)


def _moe_gather_kernel(tok_ref, nused_ref, h_hbm, o_ref, buf, sem, *, tm):
    i = pl.program_id(0)

    @pl.when(i < nused_ref[0])
    def _():
        _gather_rows(h_hbm, buf, tok_ref, i * tm, tm, sem)
        o_ref[...] = _unpack_pairs(buf[...], buf.shape[1]).astype(o_ref.dtype)

    @pl.when(i >= nused_ref[0])
    def _():
        o_ref[...] = jnp.zeros_like(o_ref)


def _moe_gather(slot_tok, nused, h, *, tm):
    p = slot_tok.shape[0]
    half = h.shape[1]
    return pl.pallas_call(
        functools.partial(_moe_gather_kernel, tm=tm),
        grid_spec=pltpu.PrefetchScalarGridSpec(
            num_scalar_prefetch=2,
            grid=(p // tm,),
            in_specs=[pl.BlockSpec(memory_space=pl.ANY)],
            out_specs=pl.BlockSpec((tm, 2 * half), lambda i, tok, nu: (i, 0)),
            scratch_shapes=[pltpu.VMEM((tm, half), jnp.uint32), pltpu.SemaphoreType.DMA(())]),
        out_shape=jax.ShapeDtypeStruct((p, 2 * half), BF16),
        compiler_params=_cparams(1, VMEM_LIMIT),
        name="moe_gather",
    )(slot_tok, nused, h)


def _expert_row_loop(n_blocks, in_copy, out_copy, compute):
    in_copy(0, 0).start(priority=1)

    def body(r, carry):
        slot = r & 1
        pl.when(r + 1 < n_blocks)(lambda: in_copy(r + 1, 1 - slot).start(priority=1))
        in_copy(r, slot).wait()
        pl.when(r >= 2)(lambda: out_copy(r - 2, slot).wait())
        compute(r, slot)
        out_copy(r, slot).start(priority=1)
        return carry

    lax.fori_loop(0, n_blocks, body, 0)
    pl.when(n_blocks >= 2)(lambda: out_copy(n_blocks - 2, n_blocks & 1).wait())
    out_copy(n_blocks - 1, (n_blocks - 1) & 1).wait()


_W_SLAB = 256
_SLABS_PER_BLOCK = 8


def _moe_up_kernel(bstart_ref, bcount_ref, w1_hbm, w3_hbm, xs_hbm, _zeros_hbm, h_hbm, wb, stage, xbuf, obuf,
                   w_sem, in_sem, out_sem, *, tm):
    e = pl.program_id(0)
    n_exp = pl.num_programs(0)
    n_slab = wb.shape[2] // _W_SLAB
    cur = e & 1
    n_blocks = bcount_ref[e]
    row = lambda r: pl.multiple_of((bstart_ref[e] + r) * tm, tm)
    slab_rows = lambda s: pl.ds(pl.multiple_of(s * _W_SLAB, _W_SLAB), _W_SLAB)

    def slab_copies(expert, s):
        return [pltpu.make_async_copy(w_hbm.at[expert, slab_rows(s), :], stage.at[s & 1, m], w_sem.at[s & 1])
                for m, w_hbm in enumerate((w1_hbm, w3_hbm))]

    def start_slab(expert, s):
        for cp in slab_copies(expert, s):
            cp.start()

    def convert_slab(expert, s, dst):
        for cp in slab_copies(expert, s):
            cp.wait()
        for m in range(2):
            wb[dst, m, slab_rows(s), :] = stage[s & 1, m].astype(BF16)
        pl.when(s + 2 < n_slab)(lambda: start_slab(expert, s + 2))

    def convert_range(expert, lo, hi, dst):
        def body(s, carry):
            convert_slab(expert, s, dst)
            return carry
        lax.fori_loop(lo, hi, body, 0)

    @pl.when(e == 0)
    def _():
        start_slab(0, 0)
        start_slab(0, 1)
        convert_range(0, 0, n_slab, 0)

    has_next = e + 1 < n_exp

    @pl.when(has_next)
    def _():
        start_slab(e + 1, 0)
        start_slab(e + 1, 1)

    def in_copy(r, slot):
        return pltpu.make_async_copy(xs_hbm.at[pl.ds(row(r), tm), :], xbuf.at[slot], in_sem.at[slot])

    def out_copy(r, slot):
        return pltpu.make_async_copy(obuf.at[slot], h_hbm.at[pl.ds(row(r), tm), :], out_sem.at[slot])

    def compute(r, slot):
        n_tiles = _SLABS_PER_BLOCK // 2
        tf = wb.shape[3] // n_tiles
        for t in range(n_tiles):
            x = xbuf[slot]
            cols = pl.ds(t * tf, tf)
            a = jnp.dot(x, wb[cur, 0, :, cols], preferred_element_type=F32)
            b = jnp.dot(x, wb[cur, 1, :, cols], preferred_element_type=F32)
            obuf[slot, :, cols] = (a * _sigmoid(a) * b).astype(BF16)
            for q in range(2):
                s = r * _SLABS_PER_BLOCK + 2 * t + q
                pl.when(jnp.logical_and(has_next, s < n_slab))(lambda s=s: convert_slab(e + 1, s, 1 - cur))

    pl.when(n_blocks > 0)(lambda: _expert_row_loop(n_blocks, in_copy, out_copy, compute))

    @pl.when(has_next)
    def _():
        convert_range(e + 1, jnp.minimum(n_blocks * _SLABS_PER_BLOCK, n_slab), n_slab, 1 - cur)


def _moe_up(bstart, bcount, xs, w1, w3, *, tm):
    p, d = xs.shape
    n_exp, _, ff = w1.shape
    any_spec = pl.BlockSpec(memory_space=pl.ANY)
    return pl.pallas_call(
        functools.partial(_moe_up_kernel, tm=tm),
        grid_spec=pltpu.PrefetchScalarGridSpec(
            num_scalar_prefetch=2,
            grid=(n_exp,),
            in_specs=[any_spec] * 4,
            out_specs=any_spec,
            scratch_shapes=[pltpu.VMEM((2, 2, d, ff), BF16), pltpu.VMEM((2, 2, _W_SLAB, ff), F32),
                            pltpu.VMEM((2, tm, d), BF16), pltpu.VMEM((2, tm, ff), BF16),
                            pltpu.SemaphoreType.DMA((2,)), pltpu.SemaphoreType.DMA((2,)),
                            pltpu.SemaphoreType.DMA((2,))]),
        out_shape=jax.ShapeDtypeStruct((p, ff), BF16),
        input_output_aliases={5: 0},
        compiler_params=_cparams(1, VMEM_LIMIT),
        name="moe_up",
    )(bstart, bcount, w1, w3, xs, jnp.zeros((p, ff), BF16))


def _moe_down_kernel(bstart_ref, bcount_ref, w2_ref, hs_hbm, _zeros_hbm, yb_hbm, w2b, hbuf, obuf, in_sem, out_sem,
                     *, tm, tn):
    e = pl.program_id(0)
    n_blocks = bcount_ref[e]
    row = lambda r: pl.multiple_of((bstart_ref[e] + r) * tm, tm)

    @pl.when(n_blocks > 0)
    def _():
        w2b[...] = w2_ref[0].astype(BF16)

        def in_copy(r, slot):
            return pltpu.make_async_copy(hs_hbm.at[pl.ds(row(r), tm), :], hbuf.at[slot], in_sem.at[slot])

        def out_copy(r, slot):
            return pltpu.make_async_copy(obuf.at[slot], yb_hbm.at[pl.ds(row(r), tm), :], out_sem.at[slot])

        def compute(r, slot):
            h = hbuf[slot]
            for n in range(w2b.shape[1] // tn):
                y = jnp.dot(h, w2b[:, n * tn:(n + 1) * tn], preferred_element_type=F32)
                obuf[slot, :, n * tn // 2:(n + 1) * tn // 2] = _pack_pairs(y, tn // 2)

        _expert_row_loop(n_blocks, in_copy, out_copy, compute)


def _moe_down(bstart, bcount, hs, w2, *, tm, tn):
    p = hs.shape[0]
    n_exp, ff, d = w2.shape
    return pl.pallas_call(
        functools.partial(_moe_down_kernel, tm=tm, tn=tn),
        grid_spec=pltpu.PrefetchScalarGridSpec(
            num_scalar_prefetch=2,
            grid=(n_exp,),
            in_specs=[pl.BlockSpec((1, ff, d), lambda e, bs, bc: (e, 0, 0)), pl.BlockSpec(memory_space=pl.ANY),
                      pl.BlockSpec(memory_space=pl.ANY)],
            out_specs=pl.BlockSpec(memory_space=pl.ANY),
            scratch_shapes=[pltpu.VMEM((ff, d), BF16), pltpu.VMEM((2, tm, ff), BF16),
                            pltpu.VMEM((2, tm, d // 2), jnp.uint32),
                            pltpu.SemaphoreType.DMA((2,)), pltpu.SemaphoreType.DMA((2,))]),
        out_shape=jax.ShapeDtypeStruct((p, d // 2), jnp.uint32),
        input_output_aliases={4: 0},
        compiler_params=_cparams(1, VMEM_LIMIT),
        name="moe_down",
    )(bstart, bcount, w2, hs, jnp.zeros((p, d // 2), jnp.uint32))


def _moe_combine_kernel(p0_ref, p1_ref, xp_ref, xs_ref, g0_ref, g1_ref, nw_ref, yb_hbm, op_ref, os_ref,
                        buf0, buf1, sem, *, tc, nb_p, half):
    i = pl.program_id(0)
    _gather_rows(yb_hbm, buf0, p0_ref, i * tc, tc, sem)
    _gather_rows(yb_hbm, buf1, p1_ref, i * tc, tc, sem)

    def emit(x_ref, o_ref):
        x = (x_ref[...] + g0_ref[:, 0:1] * _unpack_pairs(buf0[...], half)
             + g1_ref[:, 0:1] * _unpack_pairs(buf1[...], half))
        o_ref[...] = _rms(x, nw_ref[...])

    pl.when(i < nb_p)(lambda: emit(xp_ref, op_ref))
    pl.when(i >= nb_p)(lambda: emit(xs_ref, os_ref))


def _moe_combine(pos0, pos1, xp, xs, g0, g1, norm_w, yb, *, tc, half):
    d = xp.shape[1]
    nb_p, nb_s = xp.shape[0] // tc, xs.shape[0] // tc
    p_idx = lambda i, a, b: (jnp.minimum(i, nb_p - 1), 0)
    s_idx = lambda i, a, b: (jnp.maximum(i - nb_p, 0), 0)
    return pl.pallas_call(
        functools.partial(_moe_combine_kernel, tc=tc, nb_p=nb_p, half=half),
        grid_spec=pltpu.PrefetchScalarGridSpec(
            num_scalar_prefetch=2,
            grid=(nb_p + nb_s,),
            in_specs=[pl.BlockSpec((tc, d), p_idx),
                      pl.BlockSpec((tc, d), s_idx),
                      pl.BlockSpec((tc, LANES), lambda i, a, b: (i, 0)),
                      pl.BlockSpec((tc, LANES), lambda i, a, b: (i, 0)),
                      pl.BlockSpec((1, d), lambda i, a, b: (0, 0)),
                      pl.BlockSpec(memory_space=pl.ANY)],
            out_specs=[pl.BlockSpec((tc, d), p_idx), pl.BlockSpec((tc, d), s_idx)],
            scratch_shapes=[pltpu.VMEM((tc, d // 2), jnp.uint32), pltpu.VMEM((tc, d // 2), jnp.uint32),
                            pltpu.SemaphoreType.DMA(())]),
        out_shape=[jax.ShapeDtypeStruct(xp.shape, F32), jax.ShapeDtypeStruct(xs.shape, F32)],
        compiler_params=_cparams(1, VMEM_LIMIT),
        name="moe_combine",
    )(pos0, pos1, xp, xs, g0, g1, norm_w, yb)


def _route(logits, bg, be, *, tm):
    t = logits.shape[0]
    pg = jax.nn.softmax(logits[:, :N_GROUPS] + bg, axis=-1)
    g_idx = jnp.argmax(pg, axis=-1).astype(jnp.int32)
    p_sel = jnp.take_along_axis(pg, g_idx[:, None], axis=-1)
    le = (logits[:, N_GROUPS:N_GROUPS + N_EXPERTS] + be).reshape(t, N_GROUPS, EXPERTS_PER_GROUP)
    le = jnp.take_along_axis(le, g_idx[:, None, None], axis=1)[:, 0]
    pe = jax.nn.softmax(le, axis=-1)
    top_v, top_i = lax.top_k(pe, TOP_K)
    gate = p_sel * top_v / jnp.sum(top_v, axis=-1, keepdims=True)
    e_flat = (g_idx[:, None] * EXPERTS_PER_GROUP + top_i.astype(jnp.int32)).reshape(-1)
    a = t * TOP_K
    onehot = (e_flat[:, None] == jnp.arange(N_EXPERTS, dtype=jnp.int32)[None, :]).astype(jnp.int32)
    rank = jnp.sum((jnp.cumsum(onehot, axis=0) - onehot) * onehot, axis=1)
    counts = jnp.sum(onehot, axis=0)
    padded = (counts + tm - 1) // tm * tm
    pad_end = jnp.cumsum(padded)
    dest = (pad_end - padded)[e_flat] + rank
    n_blocks = -(-(a + N_EXPERTS * (tm - 1)) // tm)
    tok = jnp.arange(a, dtype=jnp.int32) // TOP_K
    slot_tok = jnp.zeros((n_blocks * tm,), jnp.int32).at[dest].set(tok)
    bstart = ((pad_end - padded) // tm).astype(jnp.int32)
    bcount = (padded // tm).astype(jnp.int32)
    nused = (pad_end[-1] // tm).astype(jnp.int32).reshape(1)
    dest = dest.reshape(t, TOP_K)
    return gate, dest[:, 0], dest[:, 1], slot_tok, bstart, bcount, nused


def _moe(xp, xs, norm2_w, wr, bg, be, w1, w3, w2, final_w, *, tm_route, tm_blk, tn, tc):
    h, logits = _router(xp, xs, norm2_w, wr, tm=tm_route)
    gate, pos0, pos1, slot_tok, bstart, bcount, nused = _route(logits, bg, be, tm=tm_blk)
    xg = _moe_gather(slot_tok, nused, h, tm=tm_blk)
    hs = _moe_up(bstart, bcount, xg, w1, w3, tm=tm_blk)
    yb = _moe_down(bstart, bcount, hs, w2, tm=tm_blk, tn=tn)
    g0 = jnp.broadcast_to(gate[:, 0:1], (h.shape[0], LANES))
    g1 = jnp.broadcast_to(gate[:, 1:2], (h.shape[0], LANES))
    return _moe_combine(pos0, pos1, xp, xs, g0, g1, final_w, yb, tc=tc, half=tn // 2)


def kernel(x_prompt, x_sample, state_mlstm_c, state_mlstm_n, state_mlstm_m, state_rwkv, state_rwkv_shift, norm1_w, w_in, w_out, ml_b_i, ml_b_f, ml_norm_w, rw_mu, rw_w0, rw_w2, rw_a0, rw_a2, rw_g2, rw_k_k, rw_k_a, rw_r_k, rw_ln_w, rw_ln_b, norm2_w, router_group_w, router_group_b, router_expert_w, router_expert_b, moe_w1, moe_w3, moe_w2, final_norm_w):
    assert w_in.shape[0] == 1, "single-layer trunk"
    bp, seq, d = x_prompt.shape
    bs = x_sample.shape[0]
    tp = bp * seq
    ml_proj = 2 * ML_HEADS * ML_DK + 2 * ML_WIDTH + 2 * ML_HEADS
    rw_proj = 3 * RW_WIDTH + RW_LORA
    rw_col0 = -(-ml_proj // RW_GROUP) * RW_GROUP
    n_groups = RW_WIDTH // RW_GROUP

    w_cat = _pack_w_in(jnp.swapaxes(w_in[0], 0, 1), split=ml_proj, width=rw_col0, tr=RW_GROUP)
    w_o = w_out[0].astype(BF16)
    mu = rw_mu[0]
    rows = [rw_w0[0], rw_a0[0], rw_k_k[0], rw_k_a[0], rw_r_k[0].reshape(-1), rw_ln_w[0], rw_ln_b[0],
            mu[:RW_WIDTH], mu[RW_WIDTH:2 * RW_WIDTH], mu[2 * RW_WIDTH:3 * RW_WIDTH]]
    prm = jnp.concatenate([jnp.stack(rows), jnp.zeros((16 - len(rows), RW_WIDTH), F32)], axis=0)
    mu_l = mu[3 * RW_WIDTH:][None]
    wl = jnp.zeros((RW_LORA, 3, RW_WIDTH), F32)
    wl = wl.at[:RW_DECAY_LORA, 0].set(rw_w2[0]).at[RW_DECAY_LORA:RW_DECAY_LORA + RW_A_LORA, 1].set(rw_a2[0])
    wl = wl.at[RW_DECAY_LORA + RW_A_LORA:, 2].set(rw_g2[0])
    wl = wl.reshape(RW_LORA, 3, n_groups, RW_GROUP).transpose(2, 0, 1, 3).reshape(n_groups, RW_LORA, 3 * RW_GROUP)
    wl = wl.astype(BF16)
    gate_bias = jnp.zeros((1, LANES), F32).at[0, :ML_HEADS].set(ml_b_i[0]).at[0, ML_HEADS:2 * ML_HEADS].set(ml_b_f[0])
    ml_nw = ml_norm_w[0][None]
    wr = jnp.concatenate([router_group_w[0], router_expert_w[0],
                          jnp.zeros((d, LANES - N_GROUPS - N_EXPERTS), F32)], axis=1)

    xp = x_prompt.reshape(tp, d)
    xs = x_sample.reshape(bs, d)
    proj_p = _in_proj(xp, norm1_w, w_cat, tm=512, tn=1280)
    proj_s = _in_proj(xs, norm1_w, w_cat, tm=bs, tn=1280)

    y_ml_p, p_c, p_nm = _mlstm_prompt(proj_p, gate_bias, ml_nw, batch=bp, seq=seq)
    y_rw_p, p_s = _rwkv_prompt(proj_p, prm, mu_l, wl, batch=bp, seq=seq, rw_col0=rw_col0, n_groups=n_groups, gps=2)
    p_sh = jnp.concatenate([lax.slice(proj_p, (b * seq + seq - 1, rw_col0), (b * seq + seq, rw_col0 + rw_proj))
                            for b in range(bp)], axis=0)

    y_ml_s, s_c, s_n, s_m = _mlstm_sample(proj_s[:, None, :], gate_bias, ml_nw, state_mlstm_c[0], state_mlstm_n[0],
                                          state_mlstm_m[0][:, None, :])
    r, k, v, g, rt, kt, vt, kkt, kat, dt = _rwkv_sample_prep(proj_s, state_rwkv_shift[0], prm, mu_l, wl, rows=bs,
                                                             rw_col0=rw_col0, n_groups=n_groups)
    s_t, y_t = _rwkv_sample_state(jnp.transpose(state_rwkv[0], (1, 2, 3, 0)), rt, kt, kkt, kat, dt, vt)
    s_s = jnp.transpose(s_t, (3, 0, 1, 2))
    y_rw_s = _rwkv_sample_post(y_t, r, k, v, g, prm)
    s_sh = proj_s[:, rw_col0:rw_col0 + rw_proj]

    x2_p = _out_proj(y_ml_p, y_rw_p, w_o, xp, tm=512, tn=1024)
    x2_s = _out_proj(y_ml_s.reshape(bs, ML_WIDTH), y_rw_s, w_o, xs, tm=bs, tn=1024)
    y_p, y_s = _moe(x2_p, x2_s, norm2_w, wr, router_group_b[0], router_expert_b[0], moe_w1[0], moe_w3[0],
                    moe_w2[0], final_norm_w[None], tm_route=128, tm_blk=256, tn=1024, tc=128)

    lead = lambda a: a[None]
    return (y_p.reshape(bp, seq, d), y_s.reshape(bs, 1, d),
            lead(p_c), lead(p_nm[:, :, 0]), lead(p_nm[:, :, 1, 0]), lead(p_s), lead(p_sh),
            lead(s_c), lead(s_n), lead(s_m[:, 0]), lead(s_s), lead(s_sh))
```

```python
import functools

import jax
import jax.numpy as jnp
from jax import lax
from jax.experimental import pallas as pl
from jax.experimental.pallas import tpu as pltpu

F32 = jnp.float32
BF16 = jnp.bfloat16

D_MODEL = 4096
ML_HEADS = 4
ML_DK = 256
ML_DV = 512
ML_WIDTH = ML_HEADS * ML_DV
GATE_SOFTCAP = 15.0
RW_HEAD = 64
RW_WIDTH = D_MODEL - ML_WIDTH
RW_HEADS = RW_WIDTH // RW_HEAD
RW_DECAY_LORA = 96
RW_A_LORA = 96
RW_GATE_LORA = 64
RW_GN_EPS = 64e-5
N_GROUPS = 4
EXPERTS_PER_GROUP = 8
N_EXPERTS = N_GROUPS * EXPERTS_PER_GROUP
TOP_K = 2
EPS = 1e-6

LANES = 128
RW_GROUP = 256
RW_GHEADS = RW_GROUP // RW_HEAD
RW_LORA = RW_DECAY_LORA + RW_A_LORA + RW_GATE_LORA
RW_CHUNK = 64
ML_CHUNK = 256
_SAMPLE_ROWS_PER_STAGE = 8
VMEM_LIMIT = 56 * 1024 * 1024


def _cparams(n_axes, vmem=None):
    return pltpu.CompilerParams(dimension_semantics=("arbitrary",) * n_axes, vmem_limit_bytes=vmem)


def _sigmoid(x):
    return 1.0 / (1.0 + jnp.exp(-x))


def _softplus(x):
    return jnp.maximum(x, 0.0) + jnp.log(1.0 + jnp.exp(-jnp.abs(x)))


def _soft_cap(x):
    return GATE_SOFTCAP * jnp.tanh(x / GATE_SOFTCAP)


def _dot(a, b):
    return jnp.dot(a.astype(BF16), b.astype(BF16), preferred_element_type=F32)


def _dot_nt(a, b):
    return lax.dot_general(a.astype(BF16), b.astype(BF16), (((1,), (1,)), ((), ())),
                           preferred_element_type=F32)


def _split2(x):
    hi = x.astype(BF16)
    lo = (x - hi.astype(F32)).astype(BF16)
    return hi, lo


def _split3(x):
    hi = x.astype(BF16)
    r = x - hi.astype(F32)
    mid = r.astype(BF16)
    lo = (r - mid.astype(F32)).astype(BF16)
    return hi, mid, lo


def _iota(shape, axis):
    return lax.broadcasted_iota(jnp.int32, shape, axis)


def _pack_pairs(x, half):
    bits = lax.bitcast_convert_type(x.astype(BF16).astype(F32), jnp.uint32)
    parts = []
    for g in range(x.shape[1] // (2 * half)):
        lo = bits[:, 2 * half * g:2 * half * g + half]
        hi = bits[:, 2 * half * g + half:2 * half * (g + 1)]
        parts.append((lo >> 16) | hi)
    return parts[0] if len(parts) == 1 else jnp.concatenate(parts, axis=1)


def _unpack_pairs(p, half):
    parts = []
    for g in range(p.shape[1] // half):
        w = p[:, half * g:half * (g + 1)]
        parts.append(lax.bitcast_convert_type(w << 16, F32))
        parts.append(lax.bitcast_convert_type(w & jnp.uint32(0xFFFF0000), F32))
    return jnp.concatenate(parts, axis=1)


def _head_block_ones(n):
    return jnp.where((_iota((n, n), 0) >> 6) == (_iota((n, n), 1) >> 6), 1.0, 0.0).astype(BF16)


def _seg_sum(x, bd, passes=2):
    hi, lo = _split2(x)
    out = jnp.dot(hi, bd, preferred_element_type=F32)
    return out + jnp.dot(lo, bd, preferred_element_type=F32) if passes == 2 else out


_P_W0, _P_A0, _P_KK, _P_KA, _P_RK, _P_LNW, _P_LNB, _P_MUR, _P_MUK, _P_MUV = range(10)


def _rwkv_prep(p_r, p_k, p_v, p_l, prev_r, prev_k, prev_v, prev_l, prm, mu_l, wl, bd):
    row = lambda i: prm[i:i + 1, :]
    xr = p_r + (prev_r - p_r) * row(_P_MUR)
    xk = p_k + (prev_k - p_k) * row(_P_MUK)
    xv = p_v + (prev_v - p_v) * row(_P_MUV)
    xl = p_l + (prev_l - p_l) * mu_l
    lane = _iota(xl.shape, 1)
    z = jnp.where(lane < RW_DECAY_LORA, jnp.tanh(xl),
                  jnp.where(lane < RW_DECAY_LORA + RW_A_LORA, xl, _sigmoid(xl)))
    lo = _dot(z, wl)
    lw, la, lg = lo[:, :RW_GROUP], lo[:, RW_GROUP:2 * RW_GROUP], lo[:, 2 * RW_GROUP:]
    w = -_softplus(-(row(_P_W0) + lw)) - 0.5
    logd = -jnp.exp(w)
    a = _sigmoid(row(_P_A0) + la)
    kk = xk * row(_P_KK)
    kk = kk / jnp.maximum(jnp.sqrt(_seg_sum(kk * kk, bd)), 1e-12)
    k = xk * (1.0 + (a - 1.0) * row(_P_KA))
    return xr, k, xv, kk, a, logd, lg


def _rwkv_post(y, r, k, v, g, prm, bd):
    row = lambda i: prm[i:i + 1, :]
    mean = _seg_sum(y, bd, passes=1) * (1.0 / RW_HEAD)
    yc = y - mean
    var = _seg_sum(yc * yc, bd, passes=1) * (1.0 / RW_HEAD)
    yn = yc * lax.rsqrt(var + RW_GN_EPS) * row(_P_LNW) + row(_P_LNB)
    bonus = _seg_sum(r * k * row(_P_RK), bd, passes=1) * v
    return (yn + bonus) * g


def _rwkv_chunk(r, k, v, kk, a, logd, G, bd):
    L = r.shape[0]
    tril = jnp.where(_iota((L, L), 1) <= _iota((L, L), 0), 1.0, 0.0).astype(BF16)
    d_hi, d_lo = _split2(logd)
    clog = (jnp.dot(tril, d_hi, preferred_element_type=F32)
            + jnp.dot(tril, d_lo, preferred_element_type=F32))
    clog_l = clog[L - 1:L, :]
    n_in = jnp.exp(-clog)
    to_end = jnp.exp(clog_l - clog)
    ka = kk * a
    at = -kk * jnp.exp(clog - logd)
    rt = r * jnp.exp(clog)
    lane_head = _iota((1, RW_GROUP), 1) >> 6
    masks = [lane_head == h for h in range(RW_GHEADS)]
    zero = jnp.zeros_like(at)
    bf = lambda x: x.astype(BF16)
    lhs = jnp.concatenate([jnp.where(m, x, zero) for m in masks for x in (at, rt)], axis=0)
    xx = _dot_nt(lhs, jnp.concatenate([ka * n_in, k * n_in], axis=0))
    t2 = _iota((L, 2 * L), 0)
    c2 = _iota((L, 2 * L), 1)
    s2 = c2 & (L - 1)
    right = c2 >= L
    eye_pad = jnp.where(c2 == t2 + L, 1.0, 0.0)
    zeros_v = jnp.zeros((L, RW_GROUP), BF16)
    v_b = bf(v)
    vz = jnp.concatenate([zeros_v, v_b], axis=0)
    yield
    zs, rbk = [], []
    makv = zero
    for h, m in enumerate(masks):
        o = 2 * L * h
        nk = jnp.where(s2 < t2, xx[o:o + L], 0.0)
        rbk.append(bf(jnp.where(s2 <= t2, xx[o + L:o + 2 * L], 0.0)))
        makv = makv + jnp.where(m, _dot(jnp.where(right, nk, 0.0), vz), 0.0)
        zs.append(jnp.where(right, eye_pad, nk))
    for _ in range((L - 1).bit_length()):
        yield
        zs = [_dot(z[:, :L], z) + jnp.where(right, z, 0.0) for z in zs]
    yield
    ws = _dot_nt(jnp.concatenate([at, rt], axis=0), G)
    xz = jnp.concatenate([zeros_v, bf(ws[:L] + makv)], axis=0)
    yield
    w_all = zero
    for z_h, m in zip(zs, masks):
        w_all = w_all + jnp.where(m, _dot(z_h, xz), 0.0)
    wv = jnp.concatenate([bf(w_all), v_b], axis=0)
    yield
    y = ws[L:]
    for rbk_h, m in zip(rbk, masks):
        y = y + jnp.where(m, jnp.dot(rbk_h, wv, preferred_element_type=F32), 0.0)
    upd = _dot(wv.astype(F32).T, jnp.concatenate([ka * to_end, k * to_end], axis=0))
    g_new = G * jnp.exp(clog_l) + jnp.where(bd > 0, upd, 0.0)
    return y, g_new


def _interleave(gens):
    results = [None] * len(gens)
    live = list(range(len(gens)))
    while live:
        for i in list(live):
            try:
                next(gens[i])
            except StopIteration as stop:
                results[i] = stop.value
                live.remove(i)
    return results


def _shift_rows(x, first_row):
    rolled = pltpu.roll(x, 1, axis=0)
    return jnp.where(_iota(x.shape, 0) == 0, first_row, rolled)


def _rwkv_prompt_kernel(*refs, n_chunks, batch, gps):
    n_chain = gps * batch
    p_refs = refs[:4 * n_chain]
    prm_ref, mul_ref, wl_ref, y_ref, s_ref, g_scr, carry_scr = refs[4 * n_chain:]
    c = pl.program_id(1)

    @pl.when(c == 0)
    def _():
        g_scr[...] = jnp.zeros_like(g_scr)
        carry_scr[...] = jnp.zeros_like(carry_scr)

    bd = _head_block_ones(RW_GROUP)

    def chain(q):
        gi = q // batch
        prm = prm_ref[:, gi * RW_GROUP:(gi + 1) * RW_GROUP]
        ps = [ref[...] for ref in p_refs[4 * q:4 * q + 4]]
        L = ps[0].shape[0]
        prevs = [_shift_rows(x, carry_scr[q, i:i + 1, :]) for i, x in enumerate(ps)]
        r, k, v, kk, a, logd, g = _rwkv_prep(*ps, *prevs, prm, mul_ref[...], wl_ref[gi], bd)
        y, g_new = yield from _rwkv_chunk(r, k, v, kk, a, logd, g_scr[q], bd)
        yield
        out = _rwkv_post(y, r, k, v, g, prm, bd).astype(y_ref.dtype)
        return out, g_new, [x[L - 1:L, :] for x in ps]

    results = _interleave([chain(q) for q in range(n_chain)])
    for q, (out, g_new, last_rows) in enumerate(results):
        gi, b = divmod(q, batch)
        y_ref[b, :, gi * RW_GROUP:(gi + 1) * RW_GROUP] = out
        g_scr[q] = g_new
        for i, x in enumerate(last_rows):
            carry_scr[q, i:i + 1, :] = x

    @pl.when(c == n_chunks - 1)
    def _():
        for q, (_, g_new, _) in enumerate(results):
            gi, b = divmod(q, batch)
            for h in range(RW_GHEADS):
                s_ref[b, gi * RW_GHEADS + h] = g_new[RW_HEAD * h:RW_HEAD * (h + 1), RW_HEAD * h:RW_HEAD * (h + 1)]


def _rwkv_prompt(proj, prm, mu_l, wl, *, batch, seq, rw_col0, n_groups, gps):
    L = RW_CHUNK
    nc = seq // L
    cb = rw_col0 // RW_GROUP
    width = n_groups * RW_GROUP
    p_specs = []
    for gi in range(gps):
        for b in range(batch):
            col = lambda g, off, gi=gi: cb + off + g * gps + gi
            rows = lambda c, b=b: b * nc + c
            p_specs += [pl.BlockSpec((L, RW_GROUP), lambda g, c, o=o, col=col, rows=rows: (rows(c), col(g, o)))
                        for o in (0, n_groups, 2 * n_groups)]
            p_specs += [pl.BlockSpec((L, RW_GROUP), lambda g, c, rows=rows: (rows(c), cb + 3 * n_groups))]
    n_chain = gps * batch
    y, s = pl.pallas_call(
        functools.partial(_rwkv_prompt_kernel, n_chunks=nc, batch=batch, gps=gps),
        grid=(n_groups // gps, nc),
        in_specs=p_specs + [pl.BlockSpec((16, gps * RW_GROUP), lambda g, c: (0, g)),
                            pl.BlockSpec((1, RW_LORA), lambda g, c: (0, 0)),
                            pl.BlockSpec((gps, RW_LORA, 3 * RW_GROUP), lambda g, c: (g, 0, 0))],
        out_specs=[pl.BlockSpec((batch, L, gps * RW_GROUP), lambda g, c: (0, c, g)),
                   pl.BlockSpec((batch, gps * RW_GHEADS, RW_HEAD, RW_HEAD), lambda g, c: (0, g, 0, 0))],
        out_shape=[jax.ShapeDtypeStruct((batch, seq, width), BF16),
                   jax.ShapeDtypeStruct((batch, n_groups * RW_GHEADS, RW_HEAD, RW_HEAD), F32)],
        scratch_shapes=[pltpu.VMEM((n_chain, RW_GROUP, RW_GROUP), F32), pltpu.VMEM((n_chain, 8, RW_GROUP), F32)],
        compiler_params=_cparams(2),
        name="rwkv_prompt",
    )(*([proj] * (4 * n_chain)), prm, mu_l, wl)
    return y.reshape(batch * seq, width), s


def _rwkv_sample_prep_kernel(pr_ref, pk_ref, pv_ref, pl_ref, sr_ref, sk_ref, sv_ref, sl_ref, prm_ref, mul_ref,
                             wl_ref, r_ref, k_ref, v_ref, g_ref, rt_ref, kt_ref, vt_ref, kkt_ref, kat_ref, dt_ref):
    bd = _head_block_ones(RW_GROUP)
    r, k, v, kk, a, logd, g = _rwkv_prep(pr_ref[...], pk_ref[...], pv_ref[...], pl_ref[...], sr_ref[...],
                                         sk_ref[...], sv_ref[...], sl_ref[...], prm_ref[...], mul_ref[...],
                                         wl_ref[0], bd)
    r_ref[...] = r
    k_ref[...] = k
    v_ref[...] = v
    g_ref[...] = g
    rt_ref[...] = r.T
    kt_ref[...] = k.T
    vt_ref[...] = v.T
    kkt_ref[...] = kk.T
    kat_ref[...] = (kk * a).T
    dt_ref[...] = jnp.exp(logd).T


def _rwkv_sample_prep(proj, shift0, prm, mu_l, wl, *, rows, rw_col0, n_groups):
    cb = rw_col0 // RW_GROUP
    pspec = lambda off: pl.BlockSpec((rows, RW_GROUP), lambda g, off=off: (0, cb + off + g))
    sspec = lambda off: pl.BlockSpec((rows, RW_GROUP), lambda g, off=off: (0, off + g))
    ospec = pl.BlockSpec((rows, RW_GROUP), lambda g: (0, g))
    tspec = pl.BlockSpec((RW_GROUP, rows), lambda g: (g, 0))
    width = n_groups * RW_GROUP
    return pl.pallas_call(
        _rwkv_sample_prep_kernel,
        grid=(n_groups,),
        in_specs=[pspec(0), pspec(n_groups), pspec(2 * n_groups),
                  pl.BlockSpec((rows, RW_GROUP), lambda g: (0, cb + 3 * n_groups)),
                  sspec(0), sspec(n_groups), sspec(2 * n_groups),
                  pl.BlockSpec((rows, RW_GROUP), lambda g: (0, 3 * n_groups)),
                  pl.BlockSpec((16, RW_GROUP), lambda g: (0, g)),
                  pl.BlockSpec((1, RW_LORA), lambda g: (0, 0)),
                  pl.BlockSpec((1, RW_LORA, 3 * RW_GROUP), lambda g: (g, 0, 0))],
        out_specs=[ospec] * 4 + [tspec] * 6,
        out_shape=[jax.ShapeDtypeStruct((rows, width), F32)] * 4 + [jax.ShapeDtypeStruct((width, rows), F32)] * 6,
        compiler_params=_cparams(1),
        name="rwkv_sample_prep",
    )(proj, proj, proj, proj, shift0, shift0, shift0, shift0, prm, mu_l, wl)


def _rwkv_sample_state_kernel(s_ref, rt_ref, kt_ref, kkt_ref, kat_ref, dt_ref, vt_ref, so_ref, yt_ref):
    n_kk = -kkt_ref[...]
    r, k, ka, d = rt_ref[...], kt_ref[...], kat_ref[...], dt_ref[...]
    for i0 in range(0, s_ref.shape[1], _SAMPLE_ROWS_PER_STAGE):
        rows = range(i0, i0 + _SAMPLE_ROWS_PER_STAGE)
        sa = [jnp.sum(s_ref[0, i] * n_kk, axis=0, keepdims=True) for i in rows]
        ys = []
        for i, sa_i in zip(rows, sa):
            s_new = s_ref[0, i] * d + sa_i * ka + vt_ref[i:i + 1, :] * k
            so_ref[0, i] = s_new
            ys.append(jnp.sum(s_new * r, axis=0, keepdims=True))
        yt_ref[i0:i0 + _SAMPLE_ROWS_PER_STAGE, :] = jnp.concatenate(ys, axis=0)


def _rwkv_sample_state(state_t, rt, kt, kkt, kat, dt, vt):
    nh, n, _, b = state_t.shape
    vspec = pl.BlockSpec((n, b), lambda h: (h, 0))
    sspec = pl.BlockSpec((1, n, n, b), lambda h: (h, 0, 0, 0))
    return pl.pallas_call(
        _rwkv_sample_state_kernel,
        grid=(nh,),
        in_specs=[sspec] + [vspec] * 6,
        out_specs=[sspec, vspec],
        out_shape=[jax.ShapeDtypeStruct(state_t.shape, F32), jax.ShapeDtypeStruct((nh * n, b), F32)],
        compiler_params=_cparams(1),
        name="rwkv_sample_state",
    )(state_t, rt, kt, kkt, kat, dt, vt)


def _rwkv_sample_post_kernel(yt_ref, r_ref, k_ref, v_ref, g_ref, prm_ref, o_ref):
    bd = _head_block_ones(RW_GROUP)
    o_ref[...] = _rwkv_post(yt_ref[...].T, r_ref[...], k_ref[...], v_ref[...], g_ref[...], prm_ref[...],
                            bd).astype(o_ref.dtype)


def _rwkv_sample_post(yt, r, k, v, g, prm):
    rows, width = r.shape
    spec = pl.BlockSpec((rows, RW_GROUP), lambda i: (0, i))
    return pl.pallas_call(
        _rwkv_sample_post_kernel,
        grid=(width // RW_GROUP,),
        in_specs=[pl.BlockSpec((RW_GROUP, rows), lambda i: (i, 0))] + [spec] * 4
                 + [pl.BlockSpec((16, RW_GROUP), lambda i: (0, i))],
        out_specs=spec,
        out_shape=jax.ShapeDtypeStruct((rows, width), BF16),
        compiler_params=_cparams(1),
        name="rwkv_sample_post",
    )(yt, r, k, v, g, prm)


def _mlstm_gates(gates, bias, h):
    capped = _soft_cap(gates + bias)
    lane = _iota(gates.shape, 1)
    i_col = jnp.sum(jnp.where(lane == h, capped, 0.0), axis=1, keepdims=True)
    f_col = jnp.sum(jnp.where(lane == h + ML_HEADS, -_softplus(-capped), 0.0), axis=1, keepdims=True)
    return i_col, f_col


def _mlstm_out(hh, o, norm_w):
    hn = hh * lax.rsqrt(jnp.mean(hh * hh, axis=1, keepdims=True) + EPS) * norm_w
    return hn * _sigmoid(o)


def _mlstm_prompt_kernel(*refs, n_chunks, batch):
    p_refs = refs[:5 * batch]
    gb_ref, nw_ref, y_ref, c_ref, nm_ref, c_scr, n_scr, m_scr = refs[5 * batch:]
    h = pl.program_id(0)
    c = pl.program_id(1)

    @pl.when(c == 0)
    def _():
        c_scr[...] = jnp.zeros_like(c_scr)
        n_scr[...] = jnp.zeros_like(n_scr)
        m_scr[...] = jnp.zeros_like(m_scr)

    def sequence(b):
        q_ref, k_ref, v_ref, o_ref, gt_ref = p_refs[5 * b:5 * b + 5]
        q = q_ref[...] * (ML_DK ** -0.5)
        k = k_ref[...]
        v = v_ref[...]
        L = q.shape[0]
        i_col, f_col = _mlstm_gates(gt_ref[...], gb_ref[...], h)
        t_i = _iota((L, L), 0)
        s_i = _iota((L, L), 1)
        causal = s_i <= t_i
        to_row = lambda col: jnp.sum(jnp.where(t_i == s_i, col, 0.0), axis=0, keepdims=True)
        i_row = to_row(i_col)
        f_row = to_row(f_col)
        yield
        b_col = jnp.sum(jnp.where(causal, f_row, 0.0), axis=1, keepdims=True)
        b_row = jnp.sum(jnp.where(t_i <= s_i, f_col, 0.0), axis=0, keepdims=True)
        m0 = m_scr[b]
        inter = b_col + m0
        dmat = jnp.where(causal, b_col - b_row + i_row, -1e30)
        yield
        m_t = jnp.maximum(inter, jnp.max(dmat, axis=1, keepdims=True))
        w_int = jnp.exp(inter - m_t)
        s = _dot_nt(q, k) * jnp.exp(dmat - m_t)
        c0 = c_scr[b]
        n0 = n_scr[b]
        yield
        num = w_int * _dot(q, c0) + _dot(s, v)
        den = w_int * jnp.sum(q * n0, axis=1, keepdims=True) + jnp.sum(s, axis=1, keepdims=True)
        hh = num / jnp.maximum(jnp.abs(den), jnp.exp(-m_t))
        m_new = m_t[L - 1:L, :]
        b_l = b_col[L - 1:L, :]
        a_end = jnp.exp(b_l - b_col + i_col - m_new)
        dec = jnp.exp(b_l + m0 - m_new)
        ka = k * a_end
        yield
        c_new = dec * c0 + _dot(ka.T, v)
        n_new = dec * n0 + jnp.sum(ka, axis=0, keepdims=True)
        out = _mlstm_out(hh, o_ref[...], nw_ref[...]).astype(y_ref.dtype)
        return out, c_new, n_new, m_new

    results = _interleave([sequence(b) for b in range(batch)])
    for b, (out, c_new, n_new, m_new) in enumerate(results):
        y_ref[b] = out
        c_scr[b] = c_new
        n_scr[b] = n_new
        m_scr[b] = m_new

    @pl.when(c == n_chunks - 1)
    def _():
        for b, (_, c_new, n_new, m_new) in enumerate(results):
            c_ref[b, 0] = c_new
            nm_ref[b, 0] = jnp.concatenate([n_new, jnp.broadcast_to(m_new, (7, ML_DK))], axis=0)


def _mlstm_prompt(proj, gate_bias, norm_w, *, batch, seq):
    L = ML_CHUNK
    nc = seq // L
    nh = ML_HEADS
    kq = nh * ML_DK
    p_specs = []
    for b in range(batch):
        rows = lambda c, b=b: b * nc + c
        p_specs += [pl.BlockSpec((L, ML_DK), lambda h, c, rows=rows: (rows(c), h)),
                    pl.BlockSpec((L, ML_DK), lambda h, c, rows=rows: (rows(c), nh + h)),
                    pl.BlockSpec((L, ML_DV), lambda h, c, rows=rows: (rows(c), 2 * kq // ML_DV + h)),
                    pl.BlockSpec((L, ML_DV), lambda h, c, rows=rows: (rows(c), 2 * kq // ML_DV + nh + h)),
                    pl.BlockSpec((L, LANES), lambda h, c, rows=rows: (rows(c), (2 * kq + 2 * ML_WIDTH) // LANES))]
    y, c_fin, nm = pl.pallas_call(
        functools.partial(_mlstm_prompt_kernel, n_chunks=nc, batch=batch),
        grid=(nh, nc),
        in_specs=p_specs + [pl.BlockSpec((1, LANES), lambda h, c: (0, 0)),
                            pl.BlockSpec((1, ML_DV), lambda h, c: (0, h))],
        out_specs=[pl.BlockSpec((batch, L, ML_DV), lambda h, c: (0, c, h)),
                   pl.BlockSpec((batch, 1, ML_DK, ML_DV), lambda h, c: (0, h, 0, 0)),
                   pl.BlockSpec((batch, 1, 8, ML_DK), lambda h, c: (0, h, 0, 0))],
        out_shape=[jax.ShapeDtypeStruct((batch, seq, ML_WIDTH), BF16),
                   jax.ShapeDtypeStruct((batch, nh, ML_DK, ML_DV), F32),
                   jax.ShapeDtypeStruct((batch, nh, 8, ML_DK), F32)],
        scratch_shapes=[pltpu.VMEM((batch, ML_DK, ML_DV), F32), pltpu.VMEM((batch, 1, ML_DK), F32),
                        pltpu.VMEM((batch, 1, 1), F32)],
        compiler_params=_cparams(2),
        name="mlstm_prompt",
    )(*([proj] * (5 * batch)), gate_bias, norm_w)
    return y.reshape(batch * seq, ML_WIDTH), c_fin, nm


def _mlstm_sample_kernel(q_ref, k_ref, v_ref, o_ref, gt_ref, gb_ref, nw_ref, c_ref, n_ref, m_ref,
                         y_ref, co_ref, no_ref, mo_ref):
    gates = gt_ref[0]
    eye = jnp.where(_iota((ML_DK, ML_DK), 0) == _iota((ML_DK, ML_DK), 1), 1.0, 0.0).astype(BF16)
    for h in range(ML_HEADS):
        q = q_ref[0][:, h * ML_DK:(h + 1) * ML_DK] * (ML_DK ** -0.5)
        k = k_ref[0][:, h * ML_DK:(h + 1) * ML_DK]
        v = v_ref[0][:, h * ML_DV:(h + 1) * ML_DV]
        i_pre, logf = _mlstm_gates(gates, gb_ref[...], h)
        c0 = c_ref[0, h]
        n0 = n_ref[0, h:h + 1, :]
        m0 = m_ref[0][:, h:h + 1]
        inter = logf + m0
        m_t = jnp.maximum(inter, i_pre)
        w_int = jnp.exp(inter - m_t)
        a_new = jnp.exp(i_pre - m_t)
        s = jnp.sum(q * k, axis=1, keepdims=True) * a_new
        qc = _dot(jnp.broadcast_to(q, (8, ML_DK)), c0)[0:1, :]
        num = w_int * qc + s * v
        den = w_int * jnp.sum(q * n0, axis=1, keepdims=True) + s
        hh = num / jnp.maximum(jnp.abs(den), jnp.exp(-m_t))
        k_hi, k_lo = _split2(jnp.broadcast_to(k, (8, ML_DK)))
        nt = (((1,), (1,)), ((), ()))
        k_col = (lax.dot_general(eye, k_hi, nt, preferred_element_type=F32)
                 + lax.dot_general(eye, k_lo, nt, preferred_element_type=F32))[:, 0:1]
        co_ref[0, h] = w_int * c0 + k_col * (a_new * v)
        no_ref[0, h:h + 1, :] = w_int * n0 + a_new * k
        mo_ref[0, :, h:h + 1] = m_t
        y_ref[0, :, h * ML_DV:(h + 1) * ML_DV] = _mlstm_out(
            hh, o_ref[0][:, h * ML_DV:(h + 1) * ML_DV], nw_ref[:, h * ML_DV:(h + 1) * ML_DV]).astype(y_ref.dtype)


def _mlstm_sample(proj3, gate_bias, norm_w, c0, n0, m0):
    b = proj3.shape[0]
    nh = ML_HEADS
    kq = nh * ML_DK
    p3 = lambda w, blk: pl.BlockSpec((1, 1, w), lambda i, blk=blk: (i, 0, blk))
    cspec = pl.BlockSpec((1, nh, ML_DK, ML_DV), lambda i: (i, 0, 0, 0))
    nspec = pl.BlockSpec((1, nh, ML_DK), lambda i: (i, 0, 0))
    mspec = pl.BlockSpec((1, 1, nh), lambda i: (i, 0, 0))
    return pl.pallas_call(
        _mlstm_sample_kernel,
        grid=(b,),
        in_specs=[p3(kq, 0), p3(kq, 1), p3(ML_WIDTH, 2 * kq // ML_WIDTH), p3(ML_WIDTH, 2 * kq // ML_WIDTH + 1),
                  p3(LANES, (2 * kq + 2 * ML_WIDTH) // LANES),
                  pl.BlockSpec((1, LANES), lambda i: (0, 0)),
                  pl.BlockSpec((1, ML_WIDTH), lambda i: (0, 0)),
                  cspec, nspec, mspec],
        out_specs=[pl.BlockSpec((1, 1, ML_WIDTH), lambda i: (i, 0, 0)), cspec, nspec, mspec],
        out_shape=[jax.ShapeDtypeStruct((b, 1, ML_WIDTH), BF16),
                   jax.ShapeDtypeStruct(c0.shape, F32), jax.ShapeDtypeStruct(n0.shape, F32),
                   jax.ShapeDtypeStruct(m0.shape, F32)],
        compiler_params=_cparams(1),
        name="mlstm_sample",
    )(proj3, proj3, proj3, proj3, proj3, gate_bias, norm_w, c0, n0, m0)


def _rms(x, w):
    return x * lax.rsqrt(jnp.mean(x * x, axis=1, keepdims=True) + EPS) * w


def _pack_w_in_kernel(wt_hbm, o_ref, buf, sem, *, n_a, split):
    j = pl.program_id(0)
    tr = o_ref.shape[0]

    def copy(step, slot):
        src = jnp.where(step < n_a, step * tr, split + (step - n_a) * tr)
        return pltpu.make_async_copy(wt_hbm.at[pl.ds(pl.multiple_of(src, 8), tr), :], buf.at[slot], sem.at[slot])

    pl.when(j == 0)(lambda: copy(j, 0).start())
    pl.when(j + 1 < pl.num_programs(0))(lambda: copy(j + 1, (j + 1) & 1).start())
    copy(j, j & 1).wait()
    o_ref[...] = buf[j & 1].astype(BF16)


def _pack_w_in(wt, *, split, width, tr):
    d = wt.shape[1]
    assert split % 8 == 0 and width % tr == 0 and split + width == wt.shape[0]
    n_a = width // tr
    return pl.pallas_call(
        functools.partial(_pack_w_in_kernel, n_a=n_a, split=split),
        grid=(2 * n_a,),
        in_specs=[pl.BlockSpec(memory_space=pl.ANY)],
        out_specs=pl.BlockSpec((tr, d), lambda j: (j, 0)),
        out_shape=jax.ShapeDtypeStruct((2 * width, d), BF16),
        scratch_shapes=[pltpu.VMEM((2, tr, d), F32), pltpu.SemaphoreType.DMA((2,))],
        compiler_params=_cparams(1, VMEM_LIMIT),
        name="pack_w_in",
    )(wt)


def _in_proj_kernel(x_ref, nw_ref, *refs):
    w_refs, o_ref, h_scr = refs[:-2], refs[-2], refs[-1]

    @pl.when(pl.program_id(1) == 0)
    def _():
        h_scr[...] = _rms(x_ref[...], nw_ref[...]).astype(BF16)

    h = h_scr[...]
    sub = w_refs[0].shape[0]
    for q, w_ref in enumerate(w_refs):
        o_ref[:, q * sub:(q + 1) * sub] = lax.dot_general(h, w_ref[...], (((1,), (1,)), ((), ())),
                                                          preferred_element_type=F32)


def _in_proj(x, norm_w, wt, *, tm, tn, n_streams):
    t, d = x.shape
    n = wt.shape[0]
    sub = tn // n_streams
    w_specs = [pl.BlockSpec((sub, d), lambda i, j, q=q: (j * n_streams + q, 0)) for q in range(n_streams)]
    return pl.pallas_call(
        _in_proj_kernel,
        grid=(t // tm, n // tn),
        in_specs=[pl.BlockSpec((tm, d), lambda i, j: (i, 0), pipeline_mode=pl.Buffered(1)),
                  pl.BlockSpec((1, d), lambda i, j: (0, 0))] + w_specs,
        out_specs=pl.BlockSpec((tm, tn), lambda i, j: (i, j)),
        out_shape=jax.ShapeDtypeStruct((t, n), F32),
        scratch_shapes=[pltpu.VMEM((tm, d), BF16)],
        compiler_params=_cparams(2, VMEM_LIMIT),
        name="in_proj",
    )(x, norm_w, *([wt] * n_streams))


def _out_proj_kernel(ya_ref, yb_ref, wa_ref, wb_ref, x_ref, o_ref):
    o_ref[...] = (x_ref[...] + jnp.dot(ya_ref[...], wa_ref[...], preferred_element_type=F32)
                  + jnp.dot(yb_ref[...], wb_ref[...], preferred_element_type=F32))


def _out_proj(ya, yb, w, x, *, tm, tn):
    t, kh = ya.shape
    n = w.shape[1]
    return pl.pallas_call(
        _out_proj_kernel,
        grid=(t // tm, n // tn),
        in_specs=[pl.BlockSpec((tm, kh), lambda i, j: (i, 0)),
                  pl.BlockSpec((tm, kh), lambda i, j: (i, 0)),
                  pl.BlockSpec((kh, tn), lambda i, j: (0, j)),
                  pl.BlockSpec((kh, tn), lambda i, j: (1, j)),
                  pl.BlockSpec((tm, tn), lambda i, j: (i, j))],
        out_specs=pl.BlockSpec((tm, tn), lambda i, j: (i, j)),
        out_shape=jax.ShapeDtypeStruct((t, n), F32),
        compiler_params=_cparams(2, VMEM_LIMIT),
        name="out_proj",
    )(ya, yb, w, w, x)


def _router_kernel(xp_ref, xs_ref, nw_ref, wr_ref, h_ref, lg_ref, *, nb_p):
    def emit(x_ref):
        h = _rms(x_ref[...], nw_ref[...])
        h_ref[...] = _pack_pairs(h, h.shape[1] // 2)
        lg_ref[...] = jnp.dot(h, wr_ref[...], preferred_element_type=F32, precision=lax.Precision.HIGHEST)

    pl.when(pl.program_id(0) < nb_p)(lambda: emit(xp_ref))
    pl.when(pl.program_id(0) >= nb_p)(lambda: emit(xs_ref))


def _router(xp, xs, norm_w, wr, *, tm):
    d = xp.shape[1]
    nb_p, nb_s = xp.shape[0] // tm, xs.shape[0] // tm
    t = xp.shape[0] + xs.shape[0]
    return pl.pallas_call(
        functools.partial(_router_kernel, nb_p=nb_p),
        grid=(nb_p + nb_s,),
        in_specs=[pl.BlockSpec((tm, d), lambda i: (jnp.minimum(i, nb_p - 1), 0)),
                  pl.BlockSpec((tm, d), lambda i: (jnp.maximum(i - nb_p, 0), 0)),
                  pl.BlockSpec((1, d), lambda i: (0, 0)),
                  pl.BlockSpec((d, LANES), lambda i: (0, 0))],
        out_specs=[pl.BlockSpec((tm, d // 2), lambda i: (i, 0)), pl.BlockSpec((tm, LANES), lambda i: (i, 0))],
        out_shape=[jax.ShapeDtypeStruct((t, d // 2), jnp.uint32), jax.ShapeDtypeStruct((t, LANES), F32)],
        compiler_params=_cparams(1, VMEM_LIMIT),
        name="router",
    )(xp, xs, norm_w, wr)


def _gather_rows(src_hbm, dst_vmem, idx_ref, base, n_rows, sem):
    def copy(r):
        return pltpu.make_async_copy(src_hbm.at[pl.ds(idx_ref[base + r], 1), :], dst_vmem.at[pl.ds(r, 1), :], sem)

    def start_pair(q, carry):
        copy(2 * q).start(priority=0)
        copy(2 * q + 1).start(priority=1)
        return carry

    def wait(r, carry):
        copy(r).wait()
        return carry

    assert n_rows % 2 == 0
    lax.fori_loop(0, n_rows // 2, start_pair, 0, unroll=4)
    lax.fori_loop(0, n_rows, wait, 0, unroll=8)


def _moe_gather_kernel(tok_ref, nused_ref, h_hbm, o_ref, buf, sem, *, tm):
    i = pl.program_id(0)

    @pl.when(i < nused_ref[0])
    def _():
        _gather_rows(h_hbm, buf, tok_ref, i * tm, tm, sem)
        o_ref[...] = _unpack_pairs(buf[...], buf.shape[1]).astype(o_ref.dtype)

    @pl.when(i >= nused_ref[0])
    def _():
        o_ref[...] = jnp.zeros_like(o_ref)


def _moe_gather(slot_tok, nused, h, *, tm):
    p = slot_tok.shape[0]
    half = h.shape[1]
    return pl.pallas_call(
        functools.partial(_moe_gather_kernel, tm=tm),
        grid_spec=pltpu.PrefetchScalarGridSpec(
            num_scalar_prefetch=2,
            grid=(p // tm,),
            in_specs=[pl.BlockSpec(memory_space=pl.ANY)],
            out_specs=pl.BlockSpec((tm, 2 * half), lambda i, tok, nu: (i, 0)),
            scratch_shapes=[pltpu.VMEM((tm, half), jnp.uint32), pltpu.SemaphoreType.DMA(())]),
        out_shape=jax.ShapeDtypeStruct((p, 2 * half), BF16),
        compiler_params=_cparams(1, VMEM_LIMIT),
        name="moe_gather",
    )(slot_tok, nused, h)


def _expert_row_loop(n_blocks, in_copy, out_copy, compute):
    in_copy(0, 0).start(priority=1)

    def body(r, carry):
        slot = r & 1
        pl.when(r + 1 < n_blocks)(lambda: in_copy(r + 1, 1 - slot).start(priority=1))
        in_copy(r, slot).wait()
        pl.when(r >= 2)(lambda: out_copy(r - 2, slot).wait())
        compute(r, slot)
        out_copy(r, slot).start(priority=1)
        return carry

    lax.fori_loop(0, n_blocks, body, 0)
    pl.when(n_blocks >= 2)(lambda: out_copy(n_blocks - 2, n_blocks & 1).wait())
    out_copy(n_blocks - 1, (n_blocks - 1) & 1).wait()


_W_SLAB = 256
_SLABS_PER_BLOCK = 8
_SLABS_IN_FLIGHT = 4


def _moe_up_kernel(bstart_ref, bcount_ref, w1_hbm, w3_hbm, xs_hbm, _zeros_hbm, h_hbm, wb, stage, xbuf, obuf,
                   w_sem, in_sem, out_sem, *, tm):
    e = pl.program_id(0)
    n_exp = pl.num_programs(0)
    n_slab = wb.shape[2] // _W_SLAB
    cur = e & 1
    n_blocks = bcount_ref[e]
    row = lambda r: pl.multiple_of((bstart_ref[e] + r) * tm, tm)
    slab_rows = lambda s: pl.ds(pl.multiple_of(s * _W_SLAB, _W_SLAB), _W_SLAB)

    depth = stage.shape[0]

    def slab_copies(expert, s):
        slot = s & (depth - 1)
        return [pltpu.make_async_copy(w_hbm.at[expert, slab_rows(s), :], stage.at[slot, m], w_sem.at[slot])
                for m, w_hbm in enumerate((w1_hbm, w3_hbm))]

    def start_slab(expert, s):
        for cp in slab_copies(expert, s):
            cp.start()

    def convert_slab(expert, s, dst):
        for cp in slab_copies(expert, s):
            cp.wait()
        for m in range(2):
            wb[dst, m, slab_rows(s), :] = stage[s & (depth - 1), m].astype(BF16)
        pl.when(s + depth < n_slab)(lambda: start_slab(expert, s + depth))

    def convert_range(expert, lo, hi, dst):
        def body(s, carry):
            convert_slab(expert, s, dst)
            return carry
        lax.fori_loop(lo, hi, body, 0)

    @pl.when(e == 0)
    def _():
        for s in range(depth):
            start_slab(0, s)
        convert_range(0, 0, n_slab, 0)

    has_next = e + 1 < n_exp

    @pl.when(has_next)
    def _():
        for s in range(depth):
            start_slab(e + 1, s)

    def in_copy(r, slot):
        return pltpu.make_async_copy(xs_hbm.at[pl.ds(row(r), tm), :], xbuf.at[slot], in_sem.at[slot])

    def out_copy(r, slot):
        return pltpu.make_async_copy(obuf.at[slot], h_hbm.at[pl.ds(row(r), tm), :], out_sem.at[slot])

    def compute(r, slot):
        n_tiles = _SLABS_PER_BLOCK // 2
        tf = wb.shape[3] // n_tiles
        for t in range(n_tiles):
            x = xbuf[slot]
            cols = pl.ds(t * tf, tf)
            a = jnp.dot(x, wb[cur, 0, :, cols], preferred_element_type=F32)
            b = jnp.dot(x, wb[cur, 1, :, cols], preferred_element_type=F32)
            obuf[slot, :, cols] = (a * _sigmoid(a) * b).astype(BF16)
            for q in range(2):
                s = r * _SLABS_PER_BLOCK + 2 * t + q
                pl.when(jnp.logical_and(has_next, s < n_slab))(lambda s=s: convert_slab(e + 1, s, 1 - cur))

    pl.when(n_blocks > 0)(lambda: _expert_row_loop(n_blocks, in_copy, out_copy, compute))

    @pl.when(has_next)
    def _():
        convert_range(e + 1, jnp.minimum(n_blocks * _SLABS_PER_BLOCK, n_slab), n_slab, 1 - cur)


def _moe_up(bstart, bcount, xs, w1, w3, *, tm):
    p, d = xs.shape
    n_exp, _, ff = w1.shape
    any_spec = pl.BlockSpec(memory_space=pl.ANY)
    return pl.pallas_call(
        functools.partial(_moe_up_kernel, tm=tm),
        grid_spec=pltpu.PrefetchScalarGridSpec(
            num_scalar_prefetch=2,
            grid=(n_exp,),
            in_specs=[any_spec] * 4,
            out_specs=any_spec,
            scratch_shapes=[pltpu.VMEM((2, 2, d, ff), BF16), pltpu.VMEM((_SLABS_IN_FLIGHT, 2, _W_SLAB, ff), F32),
                            pltpu.VMEM((2, tm, d), BF16), pltpu.VMEM((2, tm, ff), BF16),
                            pltpu.SemaphoreType.DMA((_SLABS_IN_FLIGHT,)), pltpu.SemaphoreType.DMA((2,)),
                            pltpu.SemaphoreType.DMA((2,))]),
        out_shape=jax.ShapeDtypeStruct((p, ff), BF16),
        input_output_aliases={5: 0},
        compiler_params=_cparams(1, VMEM_LIMIT),
        name="moe_up",
    )(bstart, bcount, w1, w3, xs, jnp.zeros((p, ff), BF16))


def _moe_down_kernel(bstart_ref, bcount_ref, *refs, tm, tn, n_streams):
    w2_refs = refs[:n_streams]
    hs_hbm, _zeros_hbm, yb_hbm, w2b, hbuf, obuf, in_sem, out_sem = refs[n_streams:]
    e = pl.program_id(0)
    n_blocks = bcount_ref[e]
    row = lambda r: pl.multiple_of((bstart_ref[e] + r) * tm, tm)

    @pl.when(n_blocks > 0)
    def _():
        sub = w2b.shape[0] // n_streams
        for q, w2_ref in enumerate(w2_refs):
            w2b[q * sub:(q + 1) * sub, :] = w2_ref[0].astype(BF16)

        def in_copy(r, slot):
            return pltpu.make_async_copy(hs_hbm.at[pl.ds(row(r), tm), :], hbuf.at[slot], in_sem.at[slot])

        def out_copy(r, slot):
            return pltpu.make_async_copy(obuf.at[slot], yb_hbm.at[pl.ds(row(r), tm), :], out_sem.at[slot])

        def compute(r, slot):
            h = hbuf[slot]
            for n in range(w2b.shape[1] // tn):
                y = jnp.dot(h, w2b[:, n * tn:(n + 1) * tn], preferred_element_type=F32)
                obuf[slot, :, n * tn // 2:(n + 1) * tn // 2] = _pack_pairs(y, tn // 2)

        _expert_row_loop(n_blocks, in_copy, out_copy, compute)


def _moe_down(bstart, bcount, hs, w2, *, tm, tn, n_streams):
    p = hs.shape[0]
    n_exp, ff, d = w2.shape
    w_specs = [pl.BlockSpec((1, ff // n_streams, d), lambda e, bs, bc, q=q: (e, q, 0)) for q in range(n_streams)]
    return pl.pallas_call(
        functools.partial(_moe_down_kernel, tm=tm, tn=tn, n_streams=n_streams),
        grid_spec=pltpu.PrefetchScalarGridSpec(
            num_scalar_prefetch=2,
            grid=(n_exp,),
            in_specs=w_specs + [pl.BlockSpec(memory_space=pl.ANY), pl.BlockSpec(memory_space=pl.ANY)],
            out_specs=pl.BlockSpec(memory_space=pl.ANY),
            scratch_shapes=[pltpu.VMEM((ff, d), BF16), pltpu.VMEM((2, tm, ff), BF16),
                            pltpu.VMEM((2, tm, d // 2), jnp.uint32),
                            pltpu.SemaphoreType.DMA((2,)), pltpu.SemaphoreType.DMA((2,))]),
        out_shape=jax.ShapeDtypeStruct((p, d // 2), jnp.uint32),
        input_output_aliases={3 + n_streams: 0},
        compiler_params=_cparams(1, VMEM_LIMIT),
        name="moe_down",
    )(bstart, bcount, *([w2] * n_streams), hs, jnp.zeros((p, d // 2), jnp.uint32))


def _moe_combine_kernel(p0_ref, p1_ref, xp_ref, xs_ref, g0_ref, g1_ref, nw_ref, yb_hbm, op_ref, os_ref,
                        buf0, buf1, sem, *, tc, nb_p, half):
    i = pl.program_id(0)
    _gather_rows(yb_hbm, buf0, p0_ref, i * tc, tc, sem)
    _gather_rows(yb_hbm, buf1, p1_ref, i * tc, tc, sem)

    def emit(x_ref, o_ref):
        x = (x_ref[...] + g0_ref[:, 0:1] * _unpack_pairs(buf0[...], half)
             + g1_ref[:, 0:1] * _unpack_pairs(buf1[...], half))
        o_ref[...] = _rms(x, nw_ref[...])

    pl.when(i < nb_p)(lambda: emit(xp_ref, op_ref))
    pl.when(i >= nb_p)(lambda: emit(xs_ref, os_ref))


def _moe_combine(pos0, pos1, xp, xs, g0, g1, norm_w, yb, *, tc, half):
    d = xp.shape[1]
    nb_p, nb_s = xp.shape[0] // tc, xs.shape[0] // tc
    p_idx = lambda i, a, b: (jnp.minimum(i, nb_p - 1), 0)
    s_idx = lambda i, a, b: (jnp.maximum(i - nb_p, 0), 0)
    return pl.pallas_call(
        functools.partial(_moe_combine_kernel, tc=tc, nb_p=nb_p, half=half),
        grid_spec=pltpu.PrefetchScalarGridSpec(
            num_scalar_prefetch=2,
            grid=(nb_p + nb_s,),
            in_specs=[pl.BlockSpec((tc, d), p_idx),
                      pl.BlockSpec((tc, d), s_idx),
                      pl.BlockSpec((tc, LANES), lambda i, a, b: (i, 0)),
                      pl.BlockSpec((tc, LANES), lambda i, a, b: (i, 0)),
                      pl.BlockSpec((1, d), lambda i, a, b: (0, 0)),
                      pl.BlockSpec(memory_space=pl.ANY)],
            out_specs=[pl.BlockSpec((tc, d), p_idx), pl.BlockSpec((tc, d), s_idx)],
            scratch_shapes=[pltpu.VMEM((tc, d // 2), jnp.uint32), pltpu.VMEM((tc, d // 2), jnp.uint32),
                            pltpu.SemaphoreType.DMA(())]),
        out_shape=[jax.ShapeDtypeStruct(xp.shape, F32), jax.ShapeDtypeStruct(xs.shape, F32)],
        compiler_params=_cparams(1, VMEM_LIMIT),
        name="moe_combine",
    )(pos0, pos1, xp, xs, g0, g1, norm_w, yb)


def _route(logits, bg, be, *, tm):
    t = logits.shape[0]
    pg = jax.nn.softmax(logits[:, :N_GROUPS] + bg, axis=-1)
    g_idx = jnp.argmax(pg, axis=-1).astype(jnp.int32)
    p_sel = jnp.take_along_axis(pg, g_idx[:, None], axis=-1)
    le = (logits[:, N_GROUPS:N_GROUPS + N_EXPERTS] + be).reshape(t, N_GROUPS, EXPERTS_PER_GROUP)
    le = jnp.take_along_axis(le, g_idx[:, None, None], axis=1)[:, 0]
    pe = jax.nn.softmax(le, axis=-1)
    top_v, top_i = lax.top_k(pe, TOP_K)
    gate = p_sel * top_v / jnp.sum(top_v, axis=-1, keepdims=True)
    e_flat = (g_idx[:, None] * EXPERTS_PER_GROUP + top_i.astype(jnp.int32)).reshape(-1)
    a = t * TOP_K
    onehot = (e_flat[:, None] == jnp.arange(N_EXPERTS, dtype=jnp.int32)[None, :]).astype(jnp.int32)
    rank = jnp.sum((jnp.cumsum(onehot, axis=0) - onehot) * onehot, axis=1)
    counts = jnp.sum(onehot, axis=0)
    padded = (counts + tm - 1) // tm * tm
    pad_end = jnp.cumsum(padded)
    dest = (pad_end - padded)[e_flat] + rank
    n_blocks = -(-(a + N_EXPERTS * (tm - 1)) // tm)
    tok = jnp.arange(a, dtype=jnp.int32) // TOP_K
    slot_tok = jnp.zeros((n_blocks * tm,), jnp.int32).at[dest].set(tok)
    bstart = ((pad_end - padded) // tm).astype(jnp.int32)
    bcount = (padded // tm).astype(jnp.int32)
    nused = (pad_end[-1] // tm).astype(jnp.int32).reshape(1)
    dest = dest.reshape(t, TOP_K)
    return gate, dest[:, 0], dest[:, 1], slot_tok, bstart, bcount, nused


def _moe(xp, xs, norm2_w, wr, bg, be, w1, w3, w2, final_w, *, tm_route, tm_blk, tn, tc):
    h, logits = _router(xp, xs, norm2_w, wr, tm=tm_route)
    gate, pos0, pos1, slot_tok, bstart, bcount, nused = _route(logits, bg, be, tm=tm_blk)
    xg = _moe_gather(slot_tok, nused, h, tm=tm_blk)
    hs = _moe_up(bstart, bcount, xg, w1, w3, tm=tm_blk)
    yb = _moe_down(bstart, bcount, hs, w2, tm=tm_blk, tn=tn, n_streams=4)
    g0 = jnp.broadcast_to(gate[:, 0:1], (h.shape[0], LANES))
    g1 = jnp.broadcast_to(gate[:, 1:2], (h.shape[0], LANES))
    return _moe_combine(pos0, pos1, xp, xs, g0, g1, final_w, yb, tc=tc, half=tn // 2)


def kernel(x_prompt, x_sample, state_mlstm_c, state_mlstm_n, state_mlstm_m, state_rwkv, state_rwkv_shift, norm1_w, w_in, w_out, ml_b_i, ml_b_f, ml_norm_w, rw_mu, rw_w0, rw_w2, rw_a0, rw_a2, rw_g2, rw_k_k, rw_k_a, rw_r_k, rw_ln_w, rw_ln_b, norm2_w, router_group_w, router_group_b, router_expert_w, router_expert_b, moe_w1, moe_w3, moe_w2, final_norm_w):
    assert w_in.shape[0] == 1, "single-layer trunk"
    bp, seq, d = x_prompt.shape
    bs = x_sample.shape[0]
    tp = bp * seq
    ml_proj = 2 * ML_HEADS * ML_DK + 2 * ML_WIDTH + 2 * ML_HEADS
    rw_proj = 3 * RW_WIDTH + RW_LORA
    rw_col0 = -(-ml_proj // RW_GROUP) * RW_GROUP
    n_groups = RW_WIDTH // RW_GROUP

    w_cat = _pack_w_in(jnp.swapaxes(w_in[0], 0, 1), split=ml_proj, width=rw_col0, tr=RW_GROUP)
    w_o = w_out[0].astype(BF16)
    mu = rw_mu[0]
    rows = [rw_w0[0], rw_a0[0], rw_k_k[0], rw_k_a[0], rw_r_k[0].reshape(-1), rw_ln_w[0], rw_ln_b[0],
            mu[:RW_WIDTH], mu[RW_WIDTH:2 * RW_WIDTH], mu[2 * RW_WIDTH:3 * RW_WIDTH]]
    prm = jnp.concatenate([jnp.stack(rows), jnp.zeros((16 - len(rows), RW_WIDTH), F32)], axis=0)
    mu_l = mu[3 * RW_WIDTH:][None]
    wl = jnp.zeros((RW_LORA, 3, RW_WIDTH), F32)
    wl = wl.at[:RW_DECAY_LORA, 0].set(rw_w2[0]).at[RW_DECAY_LORA:RW_DECAY_LORA + RW_A_LORA, 1].set(rw_a2[0])
    wl = wl.at[RW_DECAY_LORA + RW_A_LORA:, 2].set(rw_g2[0])
    wl = wl.reshape(RW_LORA, 3, n_groups, RW_GROUP).transpose(2, 0, 1, 3).reshape(n_groups, RW_LORA, 3 * RW_GROUP)
    wl = wl.astype(BF16)
    gate_bias = jnp.zeros((1, LANES), F32).at[0, :ML_HEADS].set(ml_b_i[0]).at[0, ML_HEADS:2 * ML_HEADS].set(ml_b_f[0])
    ml_nw = ml_norm_w[0][None]
    wr = jnp.concatenate([router_group_w[0], router_expert_w[0],
                          jnp.zeros((d, LANES - N_GROUPS - N_EXPERTS), F32)], axis=1)

    xp = x_prompt.reshape(tp, d)
    xs = x_sample.reshape(bs, d)
    proj_p = _in_proj(xp, norm1_w, w_cat, tm=512, tn=1280, n_streams=5)
    proj_s = _in_proj(xs, norm1_w, w_cat, tm=bs, tn=1280, n_streams=5)

    y_ml_p, p_c, p_nm = _mlstm_prompt(proj_p, gate_bias, ml_nw, batch=bp, seq=seq)
    y_rw_p, p_s = _rwkv_prompt(proj_p, prm, mu_l, wl, batch=bp, seq=seq, rw_col0=rw_col0, n_groups=n_groups, gps=2)
    p_sh = jnp.concatenate([lax.slice(proj_p, (b * seq + seq - 1, rw_col0), (b * seq + seq, rw_col0 + rw_proj))
                            for b in range(bp)], axis=0)

    y_ml_s, s_c, s_n, s_m = _mlstm_sample(proj_s[:, None, :], gate_bias, ml_nw, state_mlstm_c[0], state_mlstm_n[0],
                                          state_mlstm_m[0][:, None, :])
    r, k, v, g, rt, kt, vt, kkt, kat, dt = _rwkv_sample_prep(proj_s, state_rwkv_shift[0], prm, mu_l, wl, rows=bs,
                                                             rw_col0=rw_col0, n_groups=n_groups)
    s_t, y_t = _rwkv_sample_state(jnp.transpose(state_rwkv[0], (1, 2, 3, 0)), rt, kt, kkt, kat, dt, vt)
    s_s = jnp.transpose(s_t, (3, 0, 1, 2))
    y_rw_s = _rwkv_sample_post(y_t, r, k, v, g, prm)
    s_sh = proj_s[:, rw_col0:rw_col0 + rw_proj]

    x2_p = _out_proj(y_ml_p, y_rw_p, w_o, xp, tm=512, tn=1024)
    x2_s = _out_proj(y_ml_s.reshape(bs, ML_WIDTH), y_rw_s, w_o, xs, tm=bs, tn=1024)
    y_p, y_s = _moe(x2_p, x2_s, norm2_w, wr, router_group_b[0], router_expert_b[0], moe_w1[0], moe_w3[0],
                    moe_w2[0], final_norm_w[None], tm_route=128, tm_blk=256, tn=1024, tc=128)

    lead = lambda a: a[None]
    return (y_p.reshape(bp, seq, d), y_s.reshape(bs, 1, d),
            lead(p_c), lead(p_nm[:, :, 0]), lead(p_nm[:, :, 1, 0]), lead(p_s), lead(p_sh),
            lead(s_c), lead(s_n), lead(s_m[:, 0]), lead(s_s), lead(s_sh))
```

```python
import functools

import jax
import jax.numpy as jnp
from jax import lax
from jax.experimental import pallas as pl
from jax.experimental.pallas import tpu as pltpu

F32 = jnp.float32
BF16 = jnp.bfloat16

D_MODEL = 4096
ML_HEADS = 4
ML_DK = 256
ML_DV = 512
ML_WIDTH = ML_HEADS * ML_DV
GATE_SOFTCAP = 15.0
RW_HEAD = 64
RW_WIDTH = D_MODEL - ML_WIDTH
RW_HEADS = RW_WIDTH // RW_HEAD
RW_DECAY_LORA = 96
RW_A_LORA = 96
RW_GATE_LORA = 64
RW_GN_EPS = 64e-5
N_GROUPS = 4
EXPERTS_PER_GROUP = 8
N_EXPERTS = N_GROUPS * EXPERTS_PER_GROUP
TOP_K = 2
EPS = 1e-6

LANES = 128
RW_GROUP = 256
RW_GHEADS = RW_GROUP // RW_HEAD
RW_LORA = RW_DECAY_LORA + RW_A_LORA + RW_GATE_LORA
RW_CHUNK = 64
ML_CHUNK = 256
_SAMPLE_ROWS_PER_STAGE = 8
VMEM_LIMIT = 56 * 1024 * 1024


def _cparams(n_axes, vmem=None):
    return pltpu.CompilerParams(dimension_semantics=("arbitrary",) * n_axes, vmem_limit_bytes=vmem)


def _sigmoid(x):
    return 1.0 / (1.0 + jnp.exp(-x))


def _softplus(x):
    return jnp.maximum(x, 0.0) + jnp.log(1.0 + jnp.exp(-jnp.abs(x)))


def _soft_cap(x):
    return GATE_SOFTCAP * jnp.tanh(x / GATE_SOFTCAP)


def _dot(a, b):
    return jnp.dot(a.astype(BF16), b.astype(BF16), preferred_element_type=F32)


def _dot_nt(a, b):
    return lax.dot_general(a.astype(BF16), b.astype(BF16), (((1,), (1,)), ((), ())),
                           preferred_element_type=F32)


def _split2(x):
    hi = x.astype(BF16)
    lo = (x - hi.astype(F32)).astype(BF16)
    return hi, lo


def _split3(x):
    hi = x.astype(BF16)
    r = x - hi.astype(F32)
    mid = r.astype(BF16)
    lo = (r - mid.astype(F32)).astype(BF16)
    return hi, mid, lo


def _iota(shape, axis):
    return lax.broadcasted_iota(jnp.int32, shape, axis)


def _pack_pairs(x, half):
    bits = lax.bitcast_convert_type(x.astype(BF16).astype(F32), jnp.uint32)
    parts = []
    for g in range(x.shape[1] // (2 * half)):
        lo = bits[:, 2 * half * g:2 * half * g + half]
        hi = bits[:, 2 * half * g + half:2 * half * (g + 1)]
        parts.append((lo >> 16) | hi)
    return parts[0] if len(parts) == 1 else jnp.concatenate(parts, axis=1)


def _unpack_pairs(p, half):
    parts = []
    for g in range(p.shape[1] // half):
        w = p[:, half * g:half * (g + 1)]
        parts.append(lax.bitcast_convert_type(w << 16, F32))
        parts.append(lax.bitcast_convert_type(w & jnp.uint32(0xFFFF0000), F32))
    return jnp.concatenate(parts, axis=1)


def _head_block_ones(n):
    return jnp.where((_iota((n, n), 0) >> 6) == (_iota((n, n), 1) >> 6), 1.0, 0.0).astype(BF16)


def _seg_sum(x, bd, passes=2):
    hi, lo = _split2(x)
    out = jnp.dot(hi, bd, preferred_element_type=F32)
    return out + jnp.dot(lo, bd, preferred_element_type=F32) if passes == 2 else out


_P_W0, _P_A0, _P_KK, _P_KA, _P_RK, _P_LNW, _P_LNB, _P_MUR, _P_MUK, _P_MUV = range(10)


def _rwkv_prep(p_r, p_k, p_v, p_l, prev_r, prev_k, prev_v, prev_l, prm, mu_l, wl, bd):
    row = lambda i: prm[i:i + 1, :]
    xr = p_r + (prev_r - p_r) * row(_P_MUR)
    xk = p_k + (prev_k - p_k) * row(_P_MUK)
    xv = p_v + (prev_v - p_v) * row(_P_MUV)
    xl = p_l + (prev_l - p_l) * mu_l
    lane = _iota(xl.shape, 1)
    z = jnp.where(lane < RW_DECAY_LORA, jnp.tanh(xl),
                  jnp.where(lane < RW_DECAY_LORA + RW_A_LORA, xl, _sigmoid(xl)))
    lo = _dot(z, wl)
    lw, la, lg = lo[:, :RW_GROUP], lo[:, RW_GROUP:2 * RW_GROUP], lo[:, 2 * RW_GROUP:]
    w = -_softplus(-(row(_P_W0) + lw)) - 0.5
    logd = -jnp.exp(w)
    a = _sigmoid(row(_P_A0) + la)
    kk = xk * row(_P_KK)
    kk = kk / jnp.maximum(jnp.sqrt(_seg_sum(kk * kk, bd)), 1e-12)
    k = xk * (1.0 + (a - 1.0) * row(_P_KA))
    return xr, k, xv, kk, a, logd, lg


def _rwkv_post(y, r, k, v, g, prm, bd):
    row = lambda i: prm[i:i + 1, :]
    mean = _seg_sum(y, bd, passes=1) * (1.0 / RW_HEAD)
    yc = y - mean
    var = _seg_sum(yc * yc, bd, passes=1) * (1.0 / RW_HEAD)
    yn = yc * lax.rsqrt(var + RW_GN_EPS) * row(_P_LNW) + row(_P_LNB)
    bonus = _seg_sum(r * k * row(_P_RK), bd, passes=1) * v
    return (yn + bonus) * g


def _rwkv_chunk(r, k, v, kk, a, logd, G, bd):
    L = r.shape[0]
    tril = jnp.where(_iota((L, L), 1) <= _iota((L, L), 0), 1.0, 0.0).astype(BF16)
    d_hi, d_lo = _split2(logd)
    clog = (jnp.dot(tril, d_hi, preferred_element_type=F32)
            + jnp.dot(tril, d_lo, preferred_element_type=F32))
    clog_l = clog[L - 1:L, :]
    n_in = jnp.exp(-clog)
    to_end = jnp.exp(clog_l - clog)
    ka = kk * a
    at = -kk * jnp.exp(clog - logd)
    rt = r * jnp.exp(clog)
    lane_head = _iota((1, RW_GROUP), 1) >> 6
    masks = [lane_head == h for h in range(RW_GHEADS)]
    zero = jnp.zeros_like(at)
    bf = lambda x: x.astype(BF16)
    lhs = jnp.concatenate([jnp.where(m, x, zero) for m in masks for x in (at, rt)], axis=0)
    xx = _dot_nt(lhs, jnp.concatenate([ka * n_in, k * n_in], axis=0))
    t2 = _iota((L, 2 * L), 0)
    c2 = _iota((L, 2 * L), 1)
    s2 = c2 & (L - 1)
    right = c2 >= L
    eye_pad = jnp.where(c2 == t2 + L, 1.0, 0.0)
    zeros_v = jnp.zeros((L, RW_GROUP), BF16)
    v_b = bf(v)
    vz = jnp.concatenate([zeros_v, v_b], axis=0)
    yield
    zs, rbk = [], []
    makv = zero
    for h, m in enumerate(masks):
        o = 2 * L * h
        nk = jnp.where(s2 < t2, xx[o:o + L], 0.0)
        rbk.append(bf(jnp.where(s2 <= t2, xx[o + L:o + 2 * L], 0.0)))
        makv = makv + jnp.where(m, _dot(jnp.where(right, nk, 0.0), vz), 0.0)
        zs.append(jnp.where(right, eye_pad, nk))
    for _ in range((L - 1).bit_length()):
        yield
        zs = [_dot(z[:, :L], z) + jnp.where(right, z, 0.0) for z in zs]
    yield
    ws = _dot_nt(jnp.concatenate([at, rt], axis=0), G)
    xz = jnp.concatenate([zeros_v, bf(ws[:L] + makv)], axis=0)
    yield
    w_all = zero
    for z_h, m in zip(zs, masks):
        w_all = w_all + jnp.where(m, _dot(z_h, xz), 0.0)
    wv = jnp.concatenate([bf(w_all), v_b], axis=0)
    yield
    y = ws[L:]
    for rbk_h, m in zip(rbk, masks):
        y = y + jnp.where(m, jnp.dot(rbk_h, wv, preferred_element_type=F32), 0.0)
    upd = _dot(wv.astype(F32).T, jnp.concatenate([ka * to_end, k * to_end], axis=0))
    g_new = G * jnp.exp(clog_l) + jnp.where(bd > 0, upd, 0.0)
    return y, g_new


def _interleave(gens):
    results = [None] * len(gens)
    live = list(range(len(gens)))
    while live:
        for i in list(live):
            try:
                next(gens[i])
            except StopIteration as stop:
                results[i] = stop.value
                live.remove(i)
    return results


def _shift_rows(x, first_row):
    rolled = pltpu.roll(x, 1, axis=0)
    return jnp.where(_iota(x.shape, 0) == 0, first_row, rolled)


def _rwkv_prompt_kernel(*refs, n_chunks, batch, gps):
    n_chain = gps * batch
    p_refs = refs[:4 * n_chain]
    prm_ref, mul_ref, wl_ref, y_ref, s_ref, g_scr, carry_scr = refs[4 * n_chain:]
    c = pl.program_id(1)

    @pl.when(c == 0)
    def _():
        g_scr[...] = jnp.zeros_like(g_scr)
        carry_scr[...] = jnp.zeros_like(carry_scr)

    bd = _head_block_ones(RW_GROUP)

    def chain(q):
        gi = q // batch
        prm = prm_ref[:, gi * RW_GROUP:(gi + 1) * RW_GROUP]
        ps = [ref[...] for ref in p_refs[4 * q:4 * q + 4]]
        L = ps[0].shape[0]
        prevs = [_shift_rows(x, carry_scr[q, i:i + 1, :]) for i, x in enumerate(ps)]
        r, k, v, kk, a, logd, g = _rwkv_prep(*ps, *prevs, prm, mul_ref[...], wl_ref[gi], bd)
        y, g_new = yield from _rwkv_chunk(r, k, v, kk, a, logd, g_scr[q], bd)
        yield
        out = _rwkv_post(y, r, k, v, g, prm, bd).astype(y_ref.dtype)
        return out, g_new, [x[L - 1:L, :] for x in ps]

    results = _interleave([chain(q) for q in range(n_chain)])
    for q, (out, g_new, last_rows) in enumerate(results):
        gi, b = divmod(q, batch)
        y_ref[b, :, gi * RW_GROUP:(gi + 1) * RW_GROUP] = out
        g_scr[q] = g_new
        for i, x in enumerate(last_rows):
            carry_scr[q, i:i + 1, :] = x

    @pl.when(c == n_chunks - 1)
    def _():
        for q, (_, g_new, _) in enumerate(results):
            gi, b = divmod(q, batch)
            for h in range(RW_GHEADS):
                s_ref[b, gi * RW_GHEADS + h] = g_new[RW_HEAD * h:RW_HEAD * (h + 1), RW_HEAD * h:RW_HEAD * (h + 1)]


def _rwkv_prompt(proj, prm, mu_l, wl, *, batch, seq, rw_col0, n_groups, gps):
    L = RW_CHUNK
    nc = seq // L
    cb = rw_col0 // RW_GROUP
    width = n_groups * RW_GROUP
    p_specs = []
    for gi in range(gps):
        for b in range(batch):
            col = lambda g, off, gi=gi: cb + off + g * gps + gi
            rows = lambda c, b=b: b * nc + c
            p_specs += [pl.BlockSpec((L, RW_GROUP), lambda g, c, o=o, col=col, rows=rows: (rows(c), col(g, o)))
                        for o in (0, n_groups, 2 * n_groups)]
            p_specs += [pl.BlockSpec((L, RW_GROUP), lambda g, c, rows=rows: (rows(c), cb + 3 * n_groups))]
    n_chain = gps * batch
    y, s = pl.pallas_call(
        functools.partial(_rwkv_prompt_kernel, n_chunks=nc, batch=batch, gps=gps),
        grid=(n_groups // gps, nc),
        in_specs=p_specs + [pl.BlockSpec((16, gps * RW_GROUP), lambda g, c: (0, g)),
                            pl.BlockSpec((1, RW_LORA), lambda g, c: (0, 0)),
                            pl.BlockSpec((gps, RW_LORA, 3 * RW_GROUP), lambda g, c: (g, 0, 0))],
        out_specs=[pl.BlockSpec((batch, L, gps * RW_GROUP), lambda g, c: (0, c, g)),
                   pl.BlockSpec((batch, gps * RW_GHEADS, RW_HEAD, RW_HEAD), lambda g, c: (0, g, 0, 0))],
        out_shape=[jax.ShapeDtypeStruct((batch, seq, width), BF16),
                   jax.ShapeDtypeStruct((batch, n_groups * RW_GHEADS, RW_HEAD, RW_HEAD), F32)],
        scratch_shapes=[pltpu.VMEM((n_chain, RW_GROUP, RW_GROUP), F32), pltpu.VMEM((n_chain, 8, RW_GROUP), F32)],
        compiler_params=_cparams(2),
        name="rwkv_prompt",
    )(*([proj] * (4 * n_chain)), prm, mu_l, wl)
    return y.reshape(batch * seq, width), s


def _rwkv_sample_prep_kernel(pr_ref, pk_ref, pv_ref, pl_ref, sr_ref, sk_ref, sv_ref, sl_ref, prm_ref, mul_ref,
                             wl_ref, r_ref, k_ref, v_ref, g_ref, rt_ref, kt_ref, vt_ref, kkt_ref, kat_ref, dt_ref):
    bd = _head_block_ones(RW_GROUP)
    r, k, v, kk, a, logd, g = _rwkv_prep(pr_ref[...], pk_ref[...], pv_ref[...], pl_ref[...], sr_ref[...],
                                         sk_ref[...], sv_ref[...], sl_ref[...], prm_ref[...], mul_ref[...],
                                         wl_ref[0], bd)
    r_ref[...] = r
    k_ref[...] = k
    v_ref[...] = v
    g_ref[...] = g
    rt_ref[...] = r.T
    kt_ref[...] = k.T
    vt_ref[...] = v.T
    kkt_ref[...] = kk.T
    kat_ref[...] = (kk * a).T
    dt_ref[...] = jnp.exp(logd).T


def _rwkv_sample_prep(proj, shift0, prm, mu_l, wl, *, rows, rw_col0, n_groups):
    cb = rw_col0 // RW_GROUP
    pspec = lambda off: pl.BlockSpec((rows, RW_GROUP), lambda g, off=off: (0, cb + off + g))
    sspec = lambda off: pl.BlockSpec((rows, RW_GROUP), lambda g, off=off: (0, off + g))
    ospec = pl.BlockSpec((rows, RW_GROUP), lambda g: (0, g))
    tspec = pl.BlockSpec((RW_GROUP, rows), lambda g: (g, 0))
    width = n_groups * RW_GROUP
    return pl.pallas_call(
        _rwkv_sample_prep_kernel,
        grid=(n_groups,),
        in_specs=[pspec(0), pspec(n_groups), pspec(2 * n_groups),
                  pl.BlockSpec((rows, RW_GROUP), lambda g: (0, cb + 3 * n_groups)),
                  sspec(0), sspec(n_groups), sspec(2 * n_groups),
                  pl.BlockSpec((rows, RW_GROUP), lambda g: (0, 3 * n_groups)),
                  pl.BlockSpec((16, RW_GROUP), lambda g: (0, g)),
                  pl.BlockSpec((1, RW_LORA), lambda g: (0, 0)),
                  pl.BlockSpec((1, RW_LORA, 3 * RW_GROUP), lambda g: (g, 0, 0))],
        out_specs=[ospec] * 4 + [tspec] * 6,
        out_shape=[jax.ShapeDtypeStruct((rows, width), F32)] * 4 + [jax.ShapeDtypeStruct((width, rows), F32)] * 6,
        compiler_params=_cparams(1),
        name="rwkv_sample_prep",
    )(proj, proj, proj, proj, shift0, shift0, shift0, shift0, prm, mu_l, wl)


def _rwkv_sample_state_kernel(s_ref, rt_ref, kt_ref, kkt_ref, kat_ref, dt_ref, vt_ref, so_ref, yt_ref):
    n_kk = -kkt_ref[...]
    r, k, ka, d = rt_ref[...], kt_ref[...], kat_ref[...], dt_ref[...]
    for i0 in range(0, s_ref.shape[1], _SAMPLE_ROWS_PER_STAGE):
        rows = range(i0, i0 + _SAMPLE_ROWS_PER_STAGE)
        sa = [jnp.sum(s_ref[0, i] * n_kk, axis=0, keepdims=True) for i in rows]
        ys = []
        for i, sa_i in zip(rows, sa):
            s_new = s_ref[0, i] * d + sa_i * ka + vt_ref[i:i + 1, :] * k
            so_ref[0, i] = s_new
            ys.append(jnp.sum(s_new * r, axis=0, keepdims=True))
        yt_ref[i0:i0 + _SAMPLE_ROWS_PER_STAGE, :] = jnp.concatenate(ys, axis=0)


def _rwkv_sample_state(state_t, rt, kt, kkt, kat, dt, vt):
    nh, n, _, b = state_t.shape
    vspec = pl.BlockSpec((n, b), lambda h: (h, 0))
    sspec = pl.BlockSpec((1, n, n, b), lambda h: (h, 0, 0, 0))
    return pl.pallas_call(
        _rwkv_sample_state_kernel,
        grid=(nh,),
        in_specs=[sspec] + [vspec] * 6,
        out_specs=[sspec, vspec],
        out_shape=[jax.ShapeDtypeStruct(state_t.shape, F32), jax.ShapeDtypeStruct((nh * n, b), F32)],
        compiler_params=_cparams(1),
        name="rwkv_sample_state",
    )(state_t, rt, kt, kkt, kat, dt, vt)


def _rwkv_sample_post_kernel(yt_ref, r_ref, k_ref, v_ref, g_ref, prm_ref, o_ref):
    bd = _head_block_ones(RW_GROUP)
    o_ref[...] = _rwkv_post(yt_ref[...].T, r_ref[...], k_ref[...], v_ref[...], g_ref[...], prm_ref[...],
                            bd).astype(o_ref.dtype)


def _rwkv_sample_post(yt, r, k, v, g, prm):
    rows, width = r.shape
    spec = pl.BlockSpec((rows, RW_GROUP), lambda i: (0, i))
    return pl.pallas_call(
        _rwkv_sample_post_kernel,
        grid=(width // RW_GROUP,),
        in_specs=[pl.BlockSpec((RW_GROUP, rows), lambda i: (i, 0))] + [spec] * 4
                 + [pl.BlockSpec((16, RW_GROUP), lambda i: (0, i))],
        out_specs=spec,
        out_shape=jax.ShapeDtypeStruct((rows, width), BF16),
        compiler_params=_cparams(1),
        name="rwkv_sample_post",
    )(yt, r, k, v, g, prm)


def _mlstm_gates(gates, bias, h):
    capped = _soft_cap(gates + bias)
    lane = _iota(gates.shape, 1)
    i_col = jnp.sum(jnp.where(lane == h, capped, 0.0), axis=1, keepdims=True)
    f_col = jnp.sum(jnp.where(lane == h + ML_HEADS, -_softplus(-capped), 0.0), axis=1, keepdims=True)
    return i_col, f_col


def _mlstm_out(hh, o, norm_w):
    hn = hh * lax.rsqrt(jnp.mean(hh * hh, axis=1, keepdims=True) + EPS) * norm_w
    return hn * _sigmoid(o)


def _mlstm_prompt_kernel(*refs, n_chunks, batch):
    p_refs = refs[:5 * batch]
    gb_ref, nw_ref, y_ref, c_ref, nm_ref, c_scr, n_scr, m_scr = refs[5 * batch:]
    h = pl.program_id(0)
    c = pl.program_id(1)

    @pl.when(c == 0)
    def _():
        c_scr[...] = jnp.zeros_like(c_scr)
        n_scr[...] = jnp.zeros_like(n_scr)
        m_scr[...] = jnp.zeros_like(m_scr)

    def sequence(b):
        q_ref, k_ref, v_ref, o_ref, gt_ref = p_refs[5 * b:5 * b + 5]
        q = q_ref[...] * (ML_DK ** -0.5)
        k = k_ref[...]
        v = v_ref[...]
        L = q.shape[0]
        i_col, f_col = _mlstm_gates(gt_ref[...], gb_ref[...], h)
        t_i = _iota((L, L), 0)
        s_i = _iota((L, L), 1)
        causal = s_i <= t_i
        to_row = lambda col: jnp.sum(jnp.where(t_i == s_i, col, 0.0), axis=0, keepdims=True)
        i_row = to_row(i_col)
        f_row = to_row(f_col)
        yield
        b_col = jnp.sum(jnp.where(causal, f_row, 0.0), axis=1, keepdims=True)
        b_row = jnp.sum(jnp.where(t_i <= s_i, f_col, 0.0), axis=0, keepdims=True)
        m0 = m_scr[b]
        inter = b_col + m0
        dmat = jnp.where(causal, b_col - b_row + i_row, -1e30)
        yield
        m_t = jnp.maximum(inter, jnp.max(dmat, axis=1, keepdims=True))
        w_int = jnp.exp(inter - m_t)
        s = _dot_nt(q, k) * jnp.exp(dmat - m_t)
        c0 = c_scr[b]
        n0 = n_scr[b]
        yield
        num = w_int * _dot(q, c0) + _dot(s, v)
        den = w_int * jnp.sum(q * n0, axis=1, keepdims=True) + jnp.sum(s, axis=1, keepdims=True)
        hh = num / jnp.maximum(jnp.abs(den), jnp.exp(-m_t))
        m_new = m_t[L - 1:L, :]
        b_l = b_col[L - 1:L, :]
        a_end = jnp.exp(b_l - b_col + i_col - m_new)
        dec = jnp.exp(b_l + m0 - m_new)
        ka = k * a_end
        yield
        c_new = dec * c0 + _dot(ka.T, v)
        n_new = dec * n0 + jnp.sum(ka, axis=0, keepdims=True)
        out = _mlstm_out(hh, o_ref[...], nw_ref[...]).astype(y_ref.dtype)
        return out, c_new, n_new, m_new

    results = _interleave([sequence(b) for b in range(batch)])
    for b, (out, c_new, n_new, m_new) in enumerate(results):
        y_ref[b] = out
        c_scr[b] = c_new
        n_scr[b] = n_new
        m_scr[b] = m_new

    @pl.when(c == n_chunks - 1)
    def _():
        for b, (_, c_new, n_new, m_new) in enumerate(results):
            c_ref[b, 0] = c_new
            nm_ref[b, 0] = jnp.concatenate([n_new, jnp.broadcast_to(m_new, (7, ML_DK))], axis=0)


def _mlstm_prompt(proj, gate_bias, norm_w, *, batch, seq):
    L = ML_CHUNK
    nc = seq // L
    nh = ML_HEADS
    kq = nh * ML_DK
    p_specs = []
    for b in range(batch):
        rows = lambda c, b=b: b * nc + c
        p_specs += [pl.BlockSpec((L, ML_DK), lambda h, c, rows=rows: (rows(c), h)),
                    pl.BlockSpec((L, ML_DK), lambda h, c, rows=rows: (rows(c), nh + h)),
                    pl.BlockSpec((L, ML_DV), lambda h, c, rows=rows: (rows(c), 2 * kq // ML_DV + h)),
                    pl.BlockSpec((L, ML_DV), lambda h, c, rows=rows: (rows(c), 2 * kq // ML_DV + nh + h)),
                    pl.BlockSpec((L, LANES), lambda h, c, rows=rows: (rows(c), (2 * kq + 2 * ML_WIDTH) // LANES))]
    y, c_fin, nm = pl.pallas_call(
        functools.partial(_mlstm_prompt_kernel, n_chunks=nc, batch=batch),
        grid=(nh, nc),
        in_specs=p_specs + [pl.BlockSpec((1, LANES), lambda h, c: (0, 0)),
                            pl.BlockSpec((1, ML_DV), lambda h, c: (0, h))],
        out_specs=[pl.BlockSpec((batch, L, ML_DV), lambda h, c: (0, c, h)),
                   pl.BlockSpec((batch, 1, ML_DK, ML_DV), lambda h, c: (0, h, 0, 0)),
                   pl.BlockSpec((batch, 1, 8, ML_DK), lambda h, c: (0, h, 0, 0))],
        out_shape=[jax.ShapeDtypeStruct((batch, seq, ML_WIDTH), BF16),
                   jax.ShapeDtypeStruct((batch, nh, ML_DK, ML_DV), F32),
                   jax.ShapeDtypeStruct((batch, nh, 8, ML_DK), F32)],
        scratch_shapes=[pltpu.VMEM((batch, ML_DK, ML_DV), F32), pltpu.VMEM((batch, 1, ML_DK), F32),
                        pltpu.VMEM((batch, 1, 1), F32)],
        compiler_params=_cparams(2),
        name="mlstm_prompt",
    )(*([proj] * (5 * batch)), gate_bias, norm_w)
    return y.reshape(batch * seq, ML_WIDTH), c_fin, nm


def _mlstm_sample_kernel(q_ref, k_ref, v_ref, o_ref, gt_ref, gb_ref, nw_ref, c_ref, n_ref, m_ref,
                         y_ref, co_ref, no_ref, mo_ref):
    gates = gt_ref[0]
    eye = jnp.where(_iota((ML_DK, ML_DK), 0) == _iota((ML_DK, ML_DK), 1), 1.0, 0.0).astype(BF16)
    for h in range(ML_HEADS):
        q = q_ref[0][:, h * ML_DK:(h + 1) * ML_DK] * (ML_DK ** -0.5)
        k = k_ref[0][:, h * ML_DK:(h + 1) * ML_DK]
        v = v_ref[0][:, h * ML_DV:(h + 1) * ML_DV]
        i_pre, logf = _mlstm_gates(gates, gb_ref[...], h)
        c0 = c_ref[0, h]
        n0 = n_ref[0, h:h + 1, :]
        m0 = m_ref[0][:, h:h + 1]
        inter = logf + m0
        m_t = jnp.maximum(inter, i_pre)
        w_int = jnp.exp(inter - m_t)
        a_new = jnp.exp(i_pre - m_t)
        s = jnp.sum(q * k, axis=1, keepdims=True) * a_new
        qc = _dot(jnp.broadcast_to(q, (8, ML_DK)), c0)[0:1, :]
        num = w_int * qc + s * v
        den = w_int * jnp.sum(q * n0, axis=1, keepdims=True) + s
        hh = num / jnp.maximum(jnp.abs(den), jnp.exp(-m_t))
        k_hi, k_lo = _split2(jnp.broadcast_to(k, (8, ML_DK)))
        nt = (((1,), (1,)), ((), ()))
        k_col = (lax.dot_general(eye, k_hi, nt, preferred_element_type=F32)
                 + lax.dot_general(eye, k_lo, nt, preferred_element_type=F32))[:, 0:1]
        co_ref[0, h] = w_int * c0 + k_col * (a_new * v)
        no_ref[0, h:h + 1, :] = w_int * n0 + a_new * k
        mo_ref[0, :, h:h + 1] = m_t
        y_ref[0, :, h * ML_DV:(h + 1) * ML_DV] = _mlstm_out(
            hh, o_ref[0][:, h * ML_DV:(h + 1) * ML_DV], nw_ref[:, h * ML_DV:(h + 1) * ML_DV]).astype(y_ref.dtype)


def _mlstm_sample(proj3, gate_bias, norm_w, c0, n0, m0):
    b = proj3.shape[0]
    nh = ML_HEADS
    kq = nh * ML_DK
    p3 = lambda w, blk: pl.BlockSpec((1, 1, w), lambda i, blk=blk: (i, 0, blk))
    cspec = pl.BlockSpec((1, nh, ML_DK, ML_DV), lambda i: (i, 0, 0, 0))
    nspec = pl.BlockSpec((1, nh, ML_DK), lambda i: (i, 0, 0))
    mspec = pl.BlockSpec((1, 1, nh), lambda i: (i, 0, 0))
    return pl.pallas_call(
        _mlstm_sample_kernel,
        grid=(b,),
        in_specs=[p3(kq, 0), p3(kq, 1), p3(ML_WIDTH, 2 * kq // ML_WIDTH), p3(ML_WIDTH, 2 * kq // ML_WIDTH + 1),
                  p3(LANES, (2 * kq + 2 * ML_WIDTH) // LANES),
                  pl.BlockSpec((1, LANES), lambda i: (0, 0)),
                  pl.BlockSpec((1, ML_WIDTH), lambda i: (0, 0)),
                  cspec, nspec, mspec],
        out_specs=[pl.BlockSpec((1, 1, ML_WIDTH), lambda i: (i, 0, 0)), cspec, nspec, mspec],
        out_shape=[jax.ShapeDtypeStruct((b, 1, ML_WIDTH), BF16),
                   jax.ShapeDtypeStruct(c0.shape, F32), jax.ShapeDtypeStruct(n0.shape, F32),
                   jax.ShapeDtypeStruct(m0.shape, F32)],
        compiler_params=_cparams(1),
        name="mlstm_sample",
    )(proj3, proj3, proj3, proj3, proj3, gate_bias, norm_w, c0, n0, m0)


def _rms(x, w):
    return x * lax.rsqrt(jnp.mean(x * x, axis=1, keepdims=True) + EPS) * w


def _pack_w_in_kernel(wt_hbm, o_ref, buf, sem, *, n_a, split):
    j = pl.program_id(0)
    tr = o_ref.shape[0]

    def copy(step, slot):
        src = jnp.where(step < n_a, step * tr, split + (step - n_a) * tr)
        return pltpu.make_async_copy(wt_hbm.at[pl.ds(pl.multiple_of(src, 8), tr), :], buf.at[slot], sem.at[slot])

    pl.when(j == 0)(lambda: copy(j, 0).start())
    pl.when(j + 1 < pl.num_programs(0))(lambda: copy(j + 1, (j + 1) & 1).start())
    copy(j, j & 1).wait()
    o_ref[...] = buf[j & 1].astype(BF16)


def _pack_w_in(wt, *, split, width, tr):
    d = wt.shape[1]
    assert split % 8 == 0 and width % tr == 0 and split + width == wt.shape[0]
    n_a = width // tr
    return pl.pallas_call(
        functools.partial(_pack_w_in_kernel, n_a=n_a, split=split),
        grid=(2 * n_a,),
        in_specs=[pl.BlockSpec(memory_space=pl.ANY)],
        out_specs=pl.BlockSpec((tr, d), lambda j: (j, 0)),
        out_shape=jax.ShapeDtypeStruct((2 * width, d), BF16),
        scratch_shapes=[pltpu.VMEM((2, tr, d), F32), pltpu.SemaphoreType.DMA((2,))],
        compiler_params=_cparams(1, VMEM_LIMIT),
        name="pack_w_in",
    )(wt)


def _in_proj_kernel(x_ref, nw_ref, *refs):
    w_refs, o_ref, h_scr = refs[:-2], refs[-2], refs[-1]

    @pl.when(pl.program_id(1) == 0)
    def _():
        h_scr[...] = _rms(x_ref[...], nw_ref[...]).astype(BF16)

    h = h_scr[...]
    sub = w_refs[0].shape[0]
    for q, w_ref in enumerate(w_refs):
        o_ref[:, q * sub:(q + 1) * sub] = lax.dot_general(h, w_ref[...], (((1,), (1,)), ((), ())),
                                                          preferred_element_type=F32)


def _in_proj(x, norm_w, wt, *, tm, tn, n_streams):
    t, d = x.shape
    n = wt.shape[0]
    sub = tn // n_streams
    w_specs = [pl.BlockSpec((sub, d), lambda i, j, q=q: (j * n_streams + q, 0)) for q in range(n_streams)]
    return pl.pallas_call(
        _in_proj_kernel,
        grid=(t // tm, n // tn),
        in_specs=[pl.BlockSpec((tm, d), lambda i, j: (i, 0), pipeline_mode=pl.Buffered(1)),
                  pl.BlockSpec((1, d), lambda i, j: (0, 0))] + w_specs,
        out_specs=pl.BlockSpec((tm, tn), lambda i, j: (i, j)),
        out_shape=jax.ShapeDtypeStruct((t, n), F32),
        scratch_shapes=[pltpu.VMEM((tm, d), BF16)],
        compiler_params=_cparams(2, VMEM_LIMIT),
        name="in_proj",
    )(x, norm_w, *([wt] * n_streams))


def _out_proj_kernel(ya_ref, yb_ref, wa_ref, wb_ref, x_ref, o_ref):
    o_ref[...] = (x_ref[...] + jnp.dot(ya_ref[...], wa_ref[...], preferred_element_type=F32)
                  + jnp.dot(yb_ref[...], wb_ref[...], preferred_element_type=F32))


def _out_proj(ya, yb, w, x, *, tm, tn):
    t, kh = ya.shape
    n = w.shape[1]
    return pl.pallas_call(
        _out_proj_kernel,
        grid=(t // tm, n // tn),
        in_specs=[pl.BlockSpec((tm, kh), lambda i, j: (i, 0)),
                  pl.BlockSpec((tm, kh), lambda i, j: (i, 0)),
                  pl.BlockSpec((kh, tn), lambda i, j: (0, j)),
                  pl.BlockSpec((kh, tn), lambda i, j: (1, j)),
                  pl.BlockSpec((tm, tn), lambda i, j: (i, j))],
        out_specs=pl.BlockSpec((tm, tn), lambda i, j: (i, j)),
        out_shape=jax.ShapeDtypeStruct((t, n), F32),
        compiler_params=_cparams(2, VMEM_LIMIT),
        name="out_proj",
    )(ya, yb, w, w, x)


def _router_kernel(xp_ref, xs_ref, nw_ref, wr_ref, h_ref, lg_ref, *, nb_p):
    def emit(x_ref):
        h = _rms(x_ref[...], nw_ref[...])
        h_ref[...] = _pack_pairs(h, h.shape[1] // 2)
        lg_ref[...] = jnp.dot(h, wr_ref[...], preferred_element_type=F32, precision=lax.Precision.HIGHEST)

    pl.when(pl.program_id(0) < nb_p)(lambda: emit(xp_ref))
    pl.when(pl.program_id(0) >= nb_p)(lambda: emit(xs_ref))


def _router(xp, xs, norm_w, wr, *, tm):
    d = xp.shape[1]
    nb_p, nb_s = xp.shape[0] // tm, xs.shape[0] // tm
    t = xp.shape[0] + xs.shape[0]
    return pl.pallas_call(
        functools.partial(_router_kernel, nb_p=nb_p),
        grid=(nb_p + nb_s,),
        in_specs=[pl.BlockSpec((tm, d), lambda i: (jnp.minimum(i, nb_p - 1), 0)),
                  pl.BlockSpec((tm, d), lambda i: (jnp.maximum(i - nb_p, 0), 0)),
                  pl.BlockSpec((1, d), lambda i: (0, 0)),
                  pl.BlockSpec((d, LANES), lambda i: (0, 0))],
        out_specs=[pl.BlockSpec((tm, d // 2), lambda i: (i, 0)), pl.BlockSpec((tm, LANES), lambda i: (i, 0))],
        out_shape=[jax.ShapeDtypeStruct((t, d // 2), jnp.uint32), jax.ShapeDtypeStruct((t, LANES), F32)],
        compiler_params=_cparams(1, VMEM_LIMIT),
        name="router",
    )(xp, xs, norm_w, wr)


def _row_gather(src_hbm, dst_vmem, idx_ref, base, n_rows, sem):
    def copy(r):
        return pltpu.make_async_copy(src_hbm.at[pl.ds(idx_ref[base + r], 1), :], dst_vmem.at[pl.ds(r, 1), :], sem)

    def start():
        def body(r, carry):
            copy(r).start()
            return carry
        lax.fori_loop(0, n_rows, body, 0, unroll=8)

    def wait():
        def body(r, carry):
            copy(r).wait()
            return carry
        lax.fori_loop(0, n_rows, body, 0, unroll=8)

    return start, wait


def _moe_gather_kernel(tok_ref, nused_ref, h_hbm, o_ref, buf, sem, *, tm):
    i = pl.program_id(0)
    nused = nused_ref[0]
    gather = lambda blk: _row_gather(h_hbm, buf.at[blk & 1], tok_ref, blk * tm, tm, sem.at[blk & 1])

    pl.when(jnp.logical_and(i == 0, nused > 0))(lambda: gather(i)[0]())
    pl.when(i + 1 < nused)(lambda: gather(i + 1)[0]())

    @pl.when(i < nused)
    def _():
        gather(i)[1]()
        o_ref[...] = _unpack_pairs(buf[i & 1], buf.shape[2]).astype(o_ref.dtype)

    @pl.when(i >= nused)
    def _():
        o_ref[...] = jnp.zeros_like(o_ref)


def _moe_gather(slot_tok, nused, h, *, tm):
    p = slot_tok.shape[0]
    half = h.shape[1]
    return pl.pallas_call(
        functools.partial(_moe_gather_kernel, tm=tm),
        grid_spec=pltpu.PrefetchScalarGridSpec(
            num_scalar_prefetch=2,
            grid=(p // tm,),
            in_specs=[pl.BlockSpec(memory_space=pl.ANY)],
            out_specs=pl.BlockSpec((tm, 2 * half), lambda i, tok, nu: (i, 0)),
            scratch_shapes=[pltpu.VMEM((2, tm, half), jnp.uint32), pltpu.SemaphoreType.DMA((2,))]),
        out_shape=jax.ShapeDtypeStruct((p, 2 * half), BF16),
        compiler_params=_cparams(1, VMEM_LIMIT),
        name="moe_gather",
    )(slot_tok, nused, h)


def _expert_row_loop(n_blocks, in_copy, out_copy, compute):
    in_copy(0, 0).start(priority=1)

    def body(r, carry):
        slot = r & 1
        pl.when(r + 1 < n_blocks)(lambda: in_copy(r + 1, 1 - slot).start(priority=1))
        in_copy(r, slot).wait()
        pl.when(r >= 2)(lambda: out_copy(r - 2, slot).wait())
        compute(r, slot)
        out_copy(r, slot).start(priority=1)
        return carry

    lax.fori_loop(0, n_blocks, body, 0)
    pl.when(n_blocks >= 2)(lambda: out_copy(n_blocks - 2, n_blocks & 1).wait())
    out_copy(n_blocks - 1, (n_blocks - 1) & 1).wait()


_W_SLAB = 256
_SLABS_PER_BLOCK = 8
_SLABS_IN_FLIGHT = 4


def _moe_up_kernel(bstart_ref, bcount_ref, w1_hbm, w3_hbm, xs_hbm, _zeros_hbm, h_hbm, wb, stage, xbuf, obuf,
                   w_sem, in_sem, out_sem, *, tm):
    e = pl.program_id(0)
    n_exp = pl.num_programs(0)
    n_slab = wb.shape[2] // _W_SLAB
    cur = e & 1
    n_blocks = bcount_ref[e]
    row = lambda r: pl.multiple_of((bstart_ref[e] + r) * tm, tm)
    slab_rows = lambda s: pl.ds(pl.multiple_of(s * _W_SLAB, _W_SLAB), _W_SLAB)

    depth = stage.shape[0]

    def slab_copies(expert, s):
        slot = s & (depth - 1)
        return [pltpu.make_async_copy(w_hbm.at[expert, slab_rows(s), :], stage.at[slot, m], w_sem.at[slot])
                for m, w_hbm in enumerate((w1_hbm, w3_hbm))]

    def start_slab(expert, s):
        for cp in slab_copies(expert, s):
            cp.start()

    def convert_slab(expert, s, dst):
        for cp in slab_copies(expert, s):
            cp.wait()
        for m in range(2):
            wb[dst, m, slab_rows(s), :] = stage[s & (depth - 1), m].astype(BF16)
        pl.when(s + depth < n_slab)(lambda: start_slab(expert, s + depth))

    def convert_range(expert, lo, hi, dst):
        def body(s, carry):
            convert_slab(expert, s, dst)
            return carry
        lax.fori_loop(lo, hi, body, 0)

    @pl.when(e == 0)
    def _():
        for s in range(depth):
            start_slab(0, s)
        convert_range(0, 0, n_slab, 0)

    has_next = e + 1 < n_exp

    @pl.when(has_next)
    def _():
        for s in range(depth):
            start_slab(e + 1, s)

    def in_copy(r, slot):
        return pltpu.make_async_copy(xs_hbm.at[pl.ds(row(r), tm), :], xbuf.at[slot], in_sem.at[slot])

    def out_copy(r, slot):
        return pltpu.make_async_copy(obuf.at[slot], h_hbm.at[pl.ds(row(r), tm), :], out_sem.at[slot])

    def compute(r, slot):
        n_tiles = _SLABS_PER_BLOCK // 2
        tf = wb.shape[3] // n_tiles
        for t in range(n_tiles):
            x = xbuf[slot]
            cols = pl.ds(t * tf, tf)
            a = jnp.dot(x, wb[cur, 0, :, cols], preferred_element_type=F32)
            b = jnp.dot(x, wb[cur, 1, :, cols], preferred_element_type=F32)
            obuf[slot, :, cols] = (a * _sigmoid(a) * b).astype(BF16)
            for q in range(2):
                s = r * _SLABS_PER_BLOCK + 2 * t + q
                pl.when(jnp.logical_and(has_next, s < n_slab))(lambda s=s: convert_slab(e + 1, s, 1 - cur))

    pl.when(n_blocks > 0)(lambda: _expert_row_loop(n_blocks, in_copy, out_copy, compute))

    @pl.when(has_next)
    def _():
        convert_range(e + 1, jnp.minimum(n_blocks * _SLABS_PER_BLOCK, n_slab), n_slab, 1 - cur)


def _moe_up(bstart, bcount, xs, w1, w3, *, tm):
    p, d = xs.shape
    n_exp, _, ff = w1.shape
    any_spec = pl.BlockSpec(memory_space=pl.ANY)
    return pl.pallas_call(
        functools.partial(_moe_up_kernel, tm=tm),
        grid_spec=pltpu.PrefetchScalarGridSpec(
            num_scalar_prefetch=2,
            grid=(n_exp,),
            in_specs=[any_spec] * 4,
            out_specs=any_spec,
            scratch_shapes=[pltpu.VMEM((2, 2, d, ff), BF16), pltpu.VMEM((_SLABS_IN_FLIGHT, 2, _W_SLAB, ff), F32),
                            pltpu.VMEM((2, tm, d), BF16), pltpu.VMEM((2, tm, ff), BF16),
                            pltpu.SemaphoreType.DMA((_SLABS_IN_FLIGHT,)), pltpu.SemaphoreType.DMA((2,)),
                            pltpu.SemaphoreType.DMA((2,))]),
        out_shape=jax.ShapeDtypeStruct((p, ff), BF16),
        input_output_aliases={5: 0},
        compiler_params=_cparams(1, VMEM_LIMIT),
        name="moe_up",
    )(bstart, bcount, w1, w3, xs, jnp.zeros((p, ff), BF16))


def _moe_down_kernel(bstart_ref, bcount_ref, *refs, tm, tn, n_streams):
    w2_refs = refs[:n_streams]
    hs_hbm, _zeros_hbm, yb_hbm, w2b, hbuf, obuf, in_sem, out_sem = refs[n_streams:]
    e = pl.program_id(0)
    n_blocks = bcount_ref[e]
    row = lambda r: pl.multiple_of((bstart_ref[e] + r) * tm, tm)

    @pl.when(n_blocks > 0)
    def _():
        sub = w2b.shape[0] // n_streams
        for q, w2_ref in enumerate(w2_refs):
            w2b[q * sub:(q + 1) * sub, :] = w2_ref[0].astype(BF16)

        def in_copy(r, slot):
            return pltpu.make_async_copy(hs_hbm.at[pl.ds(row(r), tm), :], hbuf.at[slot], in_sem.at[slot])

        def out_copy(r, slot):
            return pltpu.make_async_copy(obuf.at[slot], yb_hbm.at[pl.ds(row(r), tm), :], out_sem.at[slot])

        def compute(r, slot):
            h = hbuf[slot]
            for n in range(w2b.shape[1] // tn):
                y = jnp.dot(h, w2b[:, n * tn:(n + 1) * tn], preferred_element_type=F32)
                obuf[slot, :, n * tn // 2:(n + 1) * tn // 2] = _pack_pairs(y, tn // 2)

        _expert_row_loop(n_blocks, in_copy, out_copy, compute)


def _moe_down(bstart, bcount, hs, w2, *, tm, tn, n_streams):
    p = hs.shape[0]
    n_exp, ff, d = w2.shape
    w_specs = [pl.BlockSpec((1, ff // n_streams, d), lambda e, bs, bc, q=q: (e, q, 0)) for q in range(n_streams)]
    return pl.pallas_call(
        functools.partial(_moe_down_kernel, tm=tm, tn=tn, n_streams=n_streams),
        grid_spec=pltpu.PrefetchScalarGridSpec(
            num_scalar_prefetch=2,
            grid=(n_exp,),
            in_specs=w_specs + [pl.BlockSpec(memory_space=pl.ANY), pl.BlockSpec(memory_space=pl.ANY)],
            out_specs=pl.BlockSpec(memory_space=pl.ANY),
            scratch_shapes=[pltpu.VMEM((ff, d), BF16), pltpu.VMEM((2, tm, ff), BF16),
                            pltpu.VMEM((2, tm, d // 2), jnp.uint32),
                            pltpu.SemaphoreType.DMA((2,)), pltpu.SemaphoreType.DMA((2,))]),
        out_shape=jax.ShapeDtypeStruct((p, d // 2), jnp.uint32),
        input_output_aliases={3 + n_streams: 0},
        compiler_params=_cparams(1, VMEM_LIMIT),
        name="moe_down",
    )(bstart, bcount, *([w2] * n_streams), hs, jnp.zeros((p, d // 2), jnp.uint32))


def _moe_combine_kernel(p0_ref, p1_ref, xp_ref, xs_ref, g0_ref, g1_ref, nw_ref, yb_hbm, op_ref, os_ref,
                        buf0, buf1, sem, *, tc, nb_p, half):
    i = pl.program_id(0)
    slot = i & 1

    def gathers(blk):
        s = blk & 1
        return [_row_gather(yb_hbm, buf.at[s], p_ref, blk * tc, tc, sem.at[s])
                for buf, p_ref in ((buf0, p0_ref), (buf1, p1_ref))]

    def start_all(blk):
        for start, _ in gathers(blk):
            start()

    pl.when(i == 0)(lambda: start_all(i))
    pl.when(i + 1 < pl.num_programs(0))(lambda: start_all(i + 1))
    for _, wait in gathers(i):
        wait()

    def emit(x_ref, o_ref):
        x = (x_ref[...] + g0_ref[:, 0:1] * _unpack_pairs(buf0[slot], half)
             + g1_ref[:, 0:1] * _unpack_pairs(buf1[slot], half))
        o_ref[...] = _rms(x, nw_ref[...])

    pl.when(i < nb_p)(lambda: emit(xp_ref, op_ref))
    pl.when(i >= nb_p)(lambda: emit(xs_ref, os_ref))


def _moe_combine(pos0, pos1, xp, xs, g0, g1, norm_w, yb, *, tc, half):
    d = xp.shape[1]
    nb_p, nb_s = xp.shape[0] // tc, xs.shape[0] // tc
    p_idx = lambda i, a, b: (jnp.minimum(i, nb_p - 1), 0)
    s_idx = lambda i, a, b: (jnp.maximum(i - nb_p, 0), 0)
    return pl.pallas_call(
        functools.partial(_moe_combine_kernel, tc=tc, nb_p=nb_p, half=half),
        grid_spec=pltpu.PrefetchScalarGridSpec(
            num_scalar_prefetch=2,
            grid=(nb_p + nb_s,),
            in_specs=[pl.BlockSpec((tc, d), p_idx),
                      pl.BlockSpec((tc, d), s_idx),
                      pl.BlockSpec((tc, LANES), lambda i, a, b: (i, 0)),
                      pl.BlockSpec((tc, LANES), lambda i, a, b: (i, 0)),
                      pl.BlockSpec((1, d), lambda i, a, b: (0, 0)),
                      pl.BlockSpec(memory_space=pl.ANY)],
            out_specs=[pl.BlockSpec((tc, d), p_idx), pl.BlockSpec((tc, d), s_idx)],
            scratch_shapes=[pltpu.VMEM((2, tc, d // 2), jnp.uint32), pltpu.VMEM((2, tc, d // 2), jnp.uint32),
                            pltpu.SemaphoreType.DMA((2,))]),
        out_shape=[jax.ShapeDtypeStruct(xp.shape, F32), jax.ShapeDtypeStruct(xs.shape, F32)],
        compiler_params=_cparams(1, VMEM_LIMIT),
        name="moe_combine",
    )(pos0, pos1, xp, xs, g0, g1, norm_w, yb)


def _route(logits, bg, be, *, tm):
    t = logits.shape[0]
    pg = jax.nn.softmax(logits[:, :N_GROUPS] + bg, axis=-1)
    g_idx = jnp.argmax(pg, axis=-1).astype(jnp.int32)
    p_sel = jnp.take_along_axis(pg, g_idx[:, None], axis=-1)
    le = (logits[:, N_GROUPS:N_GROUPS + N_EXPERTS] + be).reshape(t, N_GROUPS, EXPERTS_PER_GROUP)
    le = jnp.take_along_axis(le, g_idx[:, None, None], axis=1)[:, 0]
    pe = jax.nn.softmax(le, axis=-1)
    top_v, top_i = lax.top_k(pe, TOP_K)
    gate = p_sel * top_v / jnp.sum(top_v, axis=-1, keepdims=True)
    e_flat = (g_idx[:, None] * EXPERTS_PER_GROUP + top_i.astype(jnp.int32)).reshape(-1)
    a = t * TOP_K
    onehot = (e_flat[:, None] == jnp.arange(N_EXPERTS, dtype=jnp.int32)[None, :]).astype(jnp.int32)
    rank = jnp.sum((jnp.cumsum(onehot, axis=0) - onehot) * onehot, axis=1)
    counts = jnp.sum(onehot, axis=0)
    padded = (counts + tm - 1) // tm * tm
    pad_end = jnp.cumsum(padded)
    dest = (pad_end - padded)[e_flat] + rank
    n_blocks = -(-(a + N_EXPERTS * (tm - 1)) // tm)
    tok = jnp.arange(a, dtype=jnp.int32) // TOP_K
    slot_tok = jnp.zeros((n_blocks * tm,), jnp.int32).at[dest].set(tok)
    bstart = ((pad_end - padded) // tm).astype(jnp.int32)
    bcount = (padded // tm).astype(jnp.int32)
    nused = (pad_end[-1] // tm).astype(jnp.int32).reshape(1)
    dest = dest.reshape(t, TOP_K)
    return gate, dest[:, 0], dest[:, 1], slot_tok, bstart, bcount, nused


def _moe(xp, xs, norm2_w, wr, bg, be, w1, w3, w2, final_w, *, tm_route, tm_blk, tn, tc):
    h, logits = _router(xp, xs, norm2_w, wr, tm=tm_route)
    gate, pos0, pos1, slot_tok, bstart, bcount, nused = _route(logits, bg, be, tm=tm_blk)
    xg = _moe_gather(slot_tok, nused, h, tm=tm_blk)
    hs = _moe_up(bstart, bcount, xg, w1, w3, tm=tm_blk)
    yb = _moe_down(bstart, bcount, hs, w2, tm=tm_blk, tn=tn, n_streams=4)
    g0 = jnp.broadcast_to(gate[:, 0:1], (h.shape[0], LANES))
    g1 = jnp.broadcast_to(gate[:, 1:2], (h.shape[0], LANES))
    return _moe_combine(pos0, pos1, xp, xs, g0, g1, final_w, yb, tc=tc, half=tn // 2)


def kernel(x_prompt, x_sample, state_mlstm_c, state_mlstm_n, state_mlstm_m, state_rwkv, state_rwkv_shift, norm1_w, w_in, w_out, ml_b_i, ml_b_f, ml_norm_w, rw_mu, rw_w0, rw_w2, rw_a0, rw_a2, rw_g2, rw_k_k, rw_k_a, rw_r_k, rw_ln_w, rw_ln_b, norm2_w, router_group_w, router_group_b, router_expert_w, router_expert_b, moe_w1, moe_w3, moe_w2, final_norm_w):
    assert w_in.shape[0] == 1, "single-layer trunk"
    bp, seq, d = x_prompt.shape
    bs = x_sample.shape[0]
    tp = bp * seq
    ml_proj = 2 * ML_HEADS * ML_DK + 2 * ML_WIDTH + 2 * ML_HEADS
    rw_proj = 3 * RW_WIDTH + RW_LORA
    rw_col0 = -(-ml_proj // RW_GROUP) * RW_GROUP
    n_groups = RW_WIDTH // RW_GROUP

    w_cat = _pack_w_in(jnp.swapaxes(w_in[0], 0, 1), split=ml_proj, width=rw_col0, tr=RW_GROUP)
    w_o = w_out[0].astype(BF16)
    mu = rw_mu[0]
    rows = [rw_w0[0], rw_a0[0], rw_k_k[0], rw_k_a[0], rw_r_k[0].reshape(-1), rw_ln_w[0], rw_ln_b[0],
            mu[:RW_WIDTH], mu[RW_WIDTH:2 * RW_WIDTH], mu[2 * RW_WIDTH:3 * RW_WIDTH]]
    prm = jnp.concatenate([jnp.stack(rows), jnp.zeros((16 - len(rows), RW_WIDTH), F32)], axis=0)
    mu_l = mu[3 * RW_WIDTH:][None]
    wl = jnp.zeros((RW_LORA, 3, RW_WIDTH), F32)
    wl = wl.at[:RW_DECAY_LORA, 0].set(rw_w2[0]).at[RW_DECAY_LORA:RW_DECAY_LORA + RW_A_LORA, 1].set(rw_a2[0])
    wl = wl.at[RW_DECAY_LORA + RW_A_LORA:, 2].set(rw_g2[0])
    wl = wl.reshape(RW_LORA, 3, n_groups, RW_GROUP).transpose(2, 0, 1, 3).reshape(n_groups, RW_LORA, 3 * RW_GROUP)
    wl = wl.astype(BF16)
    gate_bias = jnp.zeros((1, LANES), F32).at[0, :ML_HEADS].set(ml_b_i[0]).at[0, ML_HEADS:2 * ML_HEADS].set(ml_b_f[0])
    ml_nw = ml_norm_w[0][None]
    wr = jnp.concatenate([router_group_w[0], router_expert_w[0],
                          jnp.zeros((d, LANES - N_GROUPS - N_EXPERTS), F32)], axis=1)

    xp = x_prompt.reshape(tp, d)
    xs = x_sample.reshape(bs, d)
    proj_p = _in_proj(xp, norm1_w, w_cat, tm=512, tn=1280, n_streams=5)
    proj_s = _in_proj(xs, norm1_w, w_cat, tm=bs, tn=1280, n_streams=5)

    y_ml_p, p_c, p_nm = _mlstm_prompt(proj_p, gate_bias, ml_nw, batch=bp, seq=seq)
    y_rw_p, p_s = _rwkv_prompt(proj_p, prm, mu_l, wl, batch=bp, seq=seq, rw_col0=rw_col0, n_groups=n_groups, gps=2)
    p_sh = jnp.concatenate([lax.slice(proj_p, (b * seq + seq - 1, rw_col0), (b * seq + seq, rw_col0 + rw_proj))
                            for b in range(bp)], axis=0)

    y_ml_s, s_c, s_n, s_m = _mlstm_sample(proj_s[:, None, :], gate_bias, ml_nw, state_mlstm_c[0], state_mlstm_n[0],
                                          state_mlstm_m[0][:, None, :])
    r, k, v, g, rt, kt, vt, kkt, kat, dt = _rwkv_sample_prep(proj_s, state_rwkv_shift[0], prm, mu_l, wl, rows=bs,
                                                             rw_col0=rw_col0, n_groups=n_groups)
    s_t, y_t = _rwkv_sample_state(jnp.transpose(state_rwkv[0], (1, 2, 3, 0)), rt, kt, kkt, kat, dt, vt)
    s_s = jnp.transpose(s_t, (3, 0, 1, 2))
    y_rw_s = _rwkv_sample_post(y_t, r, k, v, g, prm)
    s_sh = proj_s[:, rw_col0:rw_col0 + rw_proj]

    x2_p = _out_proj(y_ml_p, y_rw_p, w_o, xp, tm=512, tn=1024)
    x2_s = _out_proj(y_ml_s.reshape(bs, ML_WIDTH), y_rw_s, w_o, xs, tm=bs, tn=1024)
    y_p, y_s = _moe(x2_p, x2_s, norm2_w, wr, router_group_b[0], router_expert_b[0], moe_w1[0], moe_w3[0],
                    moe_w2[0], final_norm_w[None], tm_route=128, tm_blk=256, tn=1024, tc=128)

    lead = lambda a: a[None]
    return (y_p.reshape(bp, seq, d), y_s.reshape(bs, 1, d),
            lead(p_c), lead(p_nm[:, :, 0]), lead(p_nm[:, :, 1, 0]), lead(p_s), lead(p_sh),
            lead(s_c), lead(s_n), lead(s_m[:, 0]), lead(s_s), lead(s_sh))
```

```python
import functools

import jax
import jax.numpy as jnp
from jax import lax
from jax.experimental import pallas as pl
from jax.experimental.pallas import tpu as pltpu

F32 = jnp.float32
BF16 = jnp.bfloat16

D_MODEL = 4096
ML_HEADS = 4
ML_DK = 256
ML_DV = 512
ML_WIDTH = ML_HEADS * ML_DV
GATE_SOFTCAP = 15.0
RW_HEAD = 64
RW_WIDTH = D_MODEL - ML_WIDTH
RW_HEADS = RW_WIDTH // RW_HEAD
RW_DECAY_LORA = 96
RW_A_LORA = 96
RW_GATE_LORA = 64
RW_GN_EPS = 64e-5
N_GROUPS = 4
EXPERTS_PER_GROUP = 8
N_EXPERTS = N_GROUPS * EXPERTS_PER_GROUP
TOP_K = 2
EPS = 1e-6

LANES = 128
RW_GROUP = 256
RW_GHEADS = RW_GROUP // RW_HEAD
RW_LORA = RW_DECAY_LORA + RW_A_LORA + RW_GATE_LORA
RW_CHUNK = 64
ML_CHUNK = 256
_SAMPLE_ROWS_PER_STAGE = 8
VMEM_LIMIT = 56 * 1024 * 1024


def _cparams(n_axes, vmem=None):
    return pltpu.CompilerParams(dimension_semantics=("arbitrary",) * n_axes, vmem_limit_bytes=vmem)


def _sigmoid(x):
    return 1.0 / (1.0 + jnp.exp(-x))


def _softplus(x):
    return jnp.maximum(x, 0.0) + jnp.log(1.0 + jnp.exp(-jnp.abs(x)))


def _soft_cap(x):
    return GATE_SOFTCAP * jnp.tanh(x / GATE_SOFTCAP)


def _dot(a, b):
    return jnp.dot(a.astype(BF16), b.astype(BF16), preferred_element_type=F32)


def _dot_nt(a, b):
    return lax.dot_general(a.astype(BF16), b.astype(BF16), (((1,), (1,)), ((), ())),
                           preferred_element_type=F32)


def _split2(x):
    hi = x.astype(BF16)
    lo = (x - hi.astype(F32)).astype(BF16)
    return hi, lo


def _split3(x):
    hi = x.astype(BF16)
    r = x - hi.astype(F32)
    mid = r.astype(BF16)
    lo = (r - mid.astype(F32)).astype(BF16)
    return hi, mid, lo


def _iota(shape, axis):
    return lax.broadcasted_iota(jnp.int32, shape, axis)


def _pack_pairs(x, half):
    bits = lax.bitcast_convert_type(x.astype(BF16).astype(F32), jnp.uint32)
    parts = []
    for g in range(x.shape[1] // (2 * half)):
        lo = bits[:, 2 * half * g:2 * half * g + half]
        hi = bits[:, 2 * half * g + half:2 * half * (g + 1)]
        parts.append((lo >> 16) | hi)
    return parts[0] if len(parts) == 1 else jnp.concatenate(parts, axis=1)


def _unpack_pairs(p, half):
    parts = []
    for g in range(p.shape[1] // half):
        w = p[:, half * g:half * (g + 1)]
        parts.append(lax.bitcast_convert_type(w << 16, F32))
        parts.append(lax.bitcast_convert_type(w & jnp.uint32(0xFFFF0000), F32))
    return jnp.concatenate(parts, axis=1)


def _head_block_ones(n):
    return jnp.where((_iota((n, n), 0) >> 6) == (_iota((n, n), 1) >> 6), 1.0, 0.0).astype(BF16)


def _seg_sum(x, bd, passes=2):
    hi, lo = _split2(x)
    out = jnp.dot(hi, bd, preferred_element_type=F32)
    return out + jnp.dot(lo, bd, preferred_element_type=F32) if passes == 2 else out


_P_W0, _P_A0, _P_KK, _P_KA, _P_RK, _P_LNW, _P_LNB, _P_MUR, _P_MUK, _P_MUV = range(10)


def _rwkv_prep(p_r, p_k, p_v, p_l, prev_r, prev_k, prev_v, prev_l, prm, mu_l, wl, bd):
    row = lambda i: prm[i:i + 1, :]
    xr = p_r + (prev_r - p_r) * row(_P_MUR)
    xk = p_k + (prev_k - p_k) * row(_P_MUK)
    xv = p_v + (prev_v - p_v) * row(_P_MUV)
    xl = p_l + (prev_l - p_l) * mu_l
    lane = _iota(xl.shape, 1)
    z = jnp.where(lane < RW_DECAY_LORA, jnp.tanh(xl),
                  jnp.where(lane < RW_DECAY_LORA + RW_A_LORA, xl, _sigmoid(xl)))
    lo = _dot(z, wl)
    lw, la, lg = lo[:, :RW_GROUP], lo[:, RW_GROUP:2 * RW_GROUP], lo[:, 2 * RW_GROUP:]
    w = -_softplus(-(row(_P_W0) + lw)) - 0.5
    logd = -jnp.exp(w)
    a = _sigmoid(row(_P_A0) + la)
    kk = xk * row(_P_KK)
    kk = kk / jnp.maximum(jnp.sqrt(_seg_sum(kk * kk, bd)), 1e-12)
    k = xk * (1.0 + (a - 1.0) * row(_P_KA))
    return xr, k, xv, kk, a, logd, lg


def _rwkv_post(y, r, k, v, g, prm, bd):
    row = lambda i: prm[i:i + 1, :]
    mean = _seg_sum(y, bd, passes=1) * (1.0 / RW_HEAD)
    yc = y - mean
    var = _seg_sum(yc * yc, bd, passes=1) * (1.0 / RW_HEAD)
    yn = yc * lax.rsqrt(var + RW_GN_EPS) * row(_P_LNW) + row(_P_LNB)
    bonus = _seg_sum(r * k * row(_P_RK), bd, passes=1) * v
    return (yn + bonus) * g


def _rwkv_chunk(r, k, v, kk, a, logd, G, bd):
    L = r.shape[0]
    tril = jnp.where(_iota((L, L), 1) <= _iota((L, L), 0), 1.0, 0.0).astype(BF16)
    d_hi, d_lo = _split2(logd)
    clog = (jnp.dot(tril, d_hi, preferred_element_type=F32)
            + jnp.dot(tril, d_lo, preferred_element_type=F32))
    clog_l = clog[L - 1:L, :]
    n_in = jnp.exp(-clog)
    to_end = jnp.exp(clog_l - clog)
    ka = kk * a
    at = -kk * jnp.exp(clog - logd)
    rt = r * jnp.exp(clog)
    lane_head = _iota((1, RW_GROUP), 1) >> 6
    masks = [lane_head == h for h in range(RW_GHEADS)]
    zero = jnp.zeros_like(at)
    bf = lambda x: x.astype(BF16)
    lhs = jnp.concatenate([jnp.where(m, x, zero) for m in masks for x in (at, rt)], axis=0)
    xx = _dot_nt(lhs, jnp.concatenate([ka * n_in, k * n_in], axis=0))
    t2 = _iota((L, 2 * L), 0)
    c2 = _iota((L, 2 * L), 1)
    s2 = c2 & (L - 1)
    right = c2 >= L
    eye_pad = jnp.where(c2 == t2 + L, 1.0, 0.0)
    zeros_v = jnp.zeros((L, RW_GROUP), BF16)
    v_b = bf(v)
    vz = jnp.concatenate([zeros_v, v_b], axis=0)
    yield
    zs, rbk = [], []
    makv = zero
    for h, m in enumerate(masks):
        o = 2 * L * h
        nk = jnp.where(s2 < t2, xx[o:o + L], 0.0)
        rbk.append(bf(jnp.where(s2 <= t2, xx[o + L:o + 2 * L], 0.0)))
        makv = makv + jnp.where(m, _dot(jnp.where(right, nk, 0.0), vz), 0.0)
        zs.append(jnp.where(right, eye_pad, nk))
    for _ in range((L - 1).bit_length()):
        yield
        zs = [_dot(z[:, :L], z) + jnp.where(right, z, 0.0) for z in zs]
    yield
    ws = _dot_nt(jnp.concatenate([at, rt], axis=0), G)
    xz = jnp.concatenate([zeros_v, bf(ws[:L] + makv)], axis=0)
    yield
    w_all = zero
    for z_h, m in zip(zs, masks):
        w_all = w_all + jnp.where(m, _dot(z_h, xz), 0.0)
    wv = jnp.concatenate([bf(w_all), v_b], axis=0)
    yield
    y = ws[L:]
    for rbk_h, m in zip(rbk, masks):
        y = y + jnp.where(m, jnp.dot(rbk_h, wv, preferred_element_type=F32), 0.0)
    upd = _dot(wv.astype(F32).T, jnp.concatenate([ka * to_end, k * to_end], axis=0))
    g_new = G * jnp.exp(clog_l) + jnp.where(bd > 0, upd, 0.0)
    return y, g_new


def _interleave(gens):
    results = [None] * len(gens)
    live = list(range(len(gens)))
    while live:
        for i in list(live):
            try:
                next(gens[i])
            except StopIteration as stop:
                results[i] = stop.value
                live.remove(i)
    return results


def _shift_rows(x, first_row):
    rolled = pltpu.roll(x, 1, axis=0)
    return jnp.where(_iota(x.shape, 0) == 0, first_row, rolled)


def _rwkv_prompt_kernel(*refs, n_chunks, batch, gps):
    n_chain = gps * batch
    p_refs = refs[:4 * n_chain]
    prm_ref, mul_ref, wl_ref, y_ref, s_ref, g_scr, carry_scr = refs[4 * n_chain:]
    c = pl.program_id(1)

    @pl.when(c == 0)
    def _():
        g_scr[...] = jnp.zeros_like(g_scr)
        carry_scr[...] = jnp.zeros_like(carry_scr)

    bd = _head_block_ones(RW_GROUP)

    def chain(q):
        gi = q // batch
        prm = prm_ref[:, gi * RW_GROUP:(gi + 1) * RW_GROUP]
        ps = [ref[...] for ref in p_refs[4 * q:4 * q + 4]]
        L = ps[0].shape[0]
        prevs = [_shift_rows(x, carry_scr[q, i:i + 1, :]) for i, x in enumerate(ps)]
        r, k, v, kk, a, logd, g = _rwkv_prep(*ps, *prevs, prm, mul_ref[...], wl_ref[gi], bd)
        y, g_new = yield from _rwkv_chunk(r, k, v, kk, a, logd, g_scr[q], bd)
        yield
        out = _rwkv_post(y, r, k, v, g, prm, bd).astype(y_ref.dtype)
        return out, g_new, [x[L - 1:L, :] for x in ps]

    results = _interleave([chain(q) for q in range(n_chain)])
    for q, (out, g_new, last_rows) in enumerate(results):
        gi, b = divmod(q, batch)
        y_ref[b, :, gi * RW_GROUP:(gi + 1) * RW_GROUP] = out
        g_scr[q] = g_new
        for i, x in enumerate(last_rows):
            carry_scr[q, i:i + 1, :] = x

    @pl.when(c == n_chunks - 1)
    def _():
        for q, (_, g_new, _) in enumerate(results):
            gi, b = divmod(q, batch)
            for h in range(RW_GHEADS):
                s_ref[b, gi * RW_GHEADS + h] = g_new[RW_HEAD * h:RW_HEAD * (h + 1), RW_HEAD * h:RW_HEAD * (h + 1)]


def _rwkv_prompt(proj, prm, mu_l, wl, *, batch, seq, rw_col0, n_groups, gps):
    L = RW_CHUNK
    nc = seq // L
    cb = rw_col0 // RW_GROUP
    width = n_groups * RW_GROUP
    p_specs = []
    for gi in range(gps):
        for b in range(batch):
            col = lambda g, off, gi=gi: cb + off + g * gps + gi
            rows = lambda c, b=b: b * nc + c
            p_specs += [pl.BlockSpec((L, RW_GROUP), lambda g, c, o=o, col=col, rows=rows: (rows(c), col(g, o)))
                        for o in (0, n_groups, 2 * n_groups)]
            p_specs += [pl.BlockSpec((L, RW_GROUP), lambda g, c, rows=rows: (rows(c), cb + 3 * n_groups))]
    n_chain = gps * batch
    y, s = pl.pallas_call(
        functools.partial(_rwkv_prompt_kernel, n_chunks=nc, batch=batch, gps=gps),
        grid=(n_groups // gps, nc),
        in_specs=p_specs + [pl.BlockSpec((16, gps * RW_GROUP), lambda g, c: (0, g)),
                            pl.BlockSpec((1, RW_LORA), lambda g, c: (0, 0)),
                            pl.BlockSpec((gps, RW_LORA, 3 * RW_GROUP), lambda g, c: (g, 0, 0))],
        out_specs=[pl.BlockSpec((batch, L, gps * RW_GROUP), lambda g, c: (0, c, g)),
                   pl.BlockSpec((batch, gps * RW_GHEADS, RW_HEAD, RW_HEAD), lambda g, c: (0, g, 0, 0))],
        out_shape=[jax.ShapeDtypeStruct((batch, seq, width), BF16),
                   jax.ShapeDtypeStruct((batch, n_groups * RW_GHEADS, RW_HEAD, RW_HEAD), F32)],
        scratch_shapes=[pltpu.VMEM((n_chain, RW_GROUP, RW_GROUP), F32), pltpu.VMEM((n_chain, 8, RW_GROUP), F32)],
        compiler_params=_cparams(2),
        name="rwkv_prompt",
    )(*([proj] * (4 * n_chain)), prm, mu_l, wl)
    return y.reshape(batch * seq, width), s


def _rwkv_sample_prep_kernel(pr_ref, pk_ref, pv_ref, pl_ref, sr_ref, sk_ref, sv_ref, sl_ref, prm_ref, mul_ref,
                             wl_ref, r_ref, k_ref, v_ref, g_ref, rt_ref, kt_ref, vt_ref, kkt_ref, kat_ref, dt_ref):
    bd = _head_block_ones(RW_GROUP)
    r, k, v, kk, a, logd, g = _rwkv_prep(pr_ref[...], pk_ref[...], pv_ref[...], pl_ref[...], sr_ref[...],
                                         sk_ref[...], sv_ref[...], sl_ref[...], prm_ref[...], mul_ref[...],
                                         wl_ref[0], bd)
    r_ref[...] = r
    k_ref[...] = k
    v_ref[...] = v
    g_ref[...] = g
    rt_ref[...] = r.T
    kt_ref[...] = k.T
    vt_ref[...] = v.T
    kkt_ref[...] = kk.T
    kat_ref[...] = (kk * a).T
    dt_ref[...] = jnp.exp(logd).T


def _rwkv_sample_prep(proj, shift0, prm, mu_l, wl, *, rows, rw_col0, n_groups):
    cb = rw_col0 // RW_GROUP
    pspec = lambda off: pl.BlockSpec((rows, RW_GROUP), lambda g, off=off: (0, cb + off + g))
    sspec = lambda off: pl.BlockSpec((rows, RW_GROUP), lambda g, off=off: (0, off + g))
    ospec = pl.BlockSpec((rows, RW_GROUP), lambda g: (0, g))
    tspec = pl.BlockSpec((RW_GROUP, rows), lambda g: (g, 0))
    width = n_groups * RW_GROUP
    return pl.pallas_call(
        _rwkv_sample_prep_kernel,
        grid=(n_groups,),
        in_specs=[pspec(0), pspec(n_groups), pspec(2 * n_groups),
                  pl.BlockSpec((rows, RW_GROUP), lambda g: (0, cb + 3 * n_groups)),
                  sspec(0), sspec(n_groups), sspec(2 * n_groups),
                  pl.BlockSpec((rows, RW_GROUP), lambda g: (0, 3 * n_groups)),
                  pl.BlockSpec((16, RW_GROUP), lambda g: (0, g)),
                  pl.BlockSpec((1, RW_LORA), lambda g: (0, 0)),
                  pl.BlockSpec((1, RW_LORA, 3 * RW_GROUP), lambda g: (g, 0, 0))],
        out_specs=[ospec] * 4 + [tspec] * 6,
        out_shape=[jax.ShapeDtypeStruct((rows, width), F32)] * 4 + [jax.ShapeDtypeStruct((width, rows), F32)] * 6,
        compiler_params=_cparams(1),
        name="rwkv_sample_prep",
    )(proj, proj, proj, proj, shift0, shift0, shift0, shift0, prm, mu_l, wl)


def _rwkv_sample_state_kernel(s_ref, rt_ref, kt_ref, kkt_ref, kat_ref, dt_ref, vt_ref, so_ref, yt_ref):
    n_kk = -kkt_ref[...]
    r, k, ka, d = rt_ref[...], kt_ref[...], kat_ref[...], dt_ref[...]
    for i0 in range(0, s_ref.shape[1], _SAMPLE_ROWS_PER_STAGE):
        rows = range(i0, i0 + _SAMPLE_ROWS_PER_STAGE)
        sa = [jnp.sum(s_ref[0, i] * n_kk, axis=0, keepdims=True) for i in rows]
        ys = []
        for i, sa_i in zip(rows, sa):
            s_new = s_ref[0, i] * d + sa_i * ka + vt_ref[i:i + 1, :] * k
            so_ref[0, i] = s_new
            ys.append(jnp.sum(s_new * r, axis=0, keepdims=True))
        yt_ref[i0:i0 + _SAMPLE_ROWS_PER_STAGE, :] = jnp.concatenate(ys, axis=0)


def _rwkv_sample_state(state_t, rt, kt, kkt, kat, dt, vt):
    nh, n, _, b = state_t.shape
    vspec = pl.BlockSpec((n, b), lambda h: (h, 0))
    sspec = pl.BlockSpec((1, n, n, b), lambda h: (h, 0, 0, 0))
    return pl.pallas_call(
        _rwkv_sample_state_kernel,
        grid=(nh,),
        in_specs=[sspec] + [vspec] * 6,
        out_specs=[sspec, vspec],
        out_shape=[jax.ShapeDtypeStruct(state_t.shape, F32), jax.ShapeDtypeStruct((nh * n, b), F32)],
        compiler_params=_cparams(1),
        name="rwkv_sample_state",
    )(state_t, rt, kt, kkt, kat, dt, vt)


def _rwkv_sample_post_kernel(yt_ref, r_ref, k_ref, v_ref, g_ref, prm_ref, o_ref):
    bd = _head_block_ones(RW_GROUP)
    o_ref[...] = _rwkv_post(yt_ref[...].T, r_ref[...], k_ref[...], v_ref[...], g_ref[...], prm_ref[...],
                            bd).astype(o_ref.dtype)


def _rwkv_sample_post(yt, r, k, v, g, prm):
    rows, width = r.shape
    spec = pl.BlockSpec((rows, RW_GROUP), lambda i: (0, i))
    return pl.pallas_call(
        _rwkv_sample_post_kernel,
        grid=(width // RW_GROUP,),
        in_specs=[pl.BlockSpec((RW_GROUP, rows), lambda i: (i, 0))] + [spec] * 4
                 + [pl.BlockSpec((16, RW_GROUP), lambda i: (0, i))],
        out_specs=spec,
        out_shape=jax.ShapeDtypeStruct((rows, width), BF16),
        compiler_params=_cparams(1),
        name="rwkv_sample_post",
    )(yt, r, k, v, g, prm)


def _mlstm_gates(gates, bias, h):
    capped = _soft_cap(gates + bias)
    lane = _iota(gates.shape, 1)
    i_col = jnp.sum(jnp.where(lane == h, capped, 0.0), axis=1, keepdims=True)
    f_col = jnp.sum(jnp.where(lane == h + ML_HEADS, -_softplus(-capped), 0.0), axis=1, keepdims=True)
    return i_col, f_col


def _mlstm_out(hh, o, norm_w):
    hn = hh * lax.rsqrt(jnp.mean(hh * hh, axis=1, keepdims=True) + EPS) * norm_w
    return hn * _sigmoid(o)


def _mlstm_prompt_kernel(*refs, n_chunks, batch):
    p_refs = refs[:5 * batch]
    gb_ref, nw_ref, y_ref, c_ref, nm_ref, c_scr, n_scr, m_scr = refs[5 * batch:]
    h = pl.program_id(0)
    c = pl.program_id(1)

    @pl.when(c == 0)
    def _():
        c_scr[...] = jnp.zeros_like(c_scr)
        n_scr[...] = jnp.zeros_like(n_scr)
        m_scr[...] = jnp.zeros_like(m_scr)

    def sequence(b):
        q_ref, k_ref, v_ref, o_ref, gt_ref = p_refs[5 * b:5 * b + 5]
        q = q_ref[...] * (ML_DK ** -0.5)
        k = k_ref[...]
        v = v_ref[...]
        L = q.shape[0]
        i_col, f_col = _mlstm_gates(gt_ref[...], gb_ref[...], h)
        t_i = _iota((L, L), 0)
        s_i = _iota((L, L), 1)
        causal = s_i <= t_i
        to_row = lambda col: jnp.sum(jnp.where(t_i == s_i, col, 0.0), axis=0, keepdims=True)
        i_row = to_row(i_col)
        f_row = to_row(f_col)
        yield
        b_col = jnp.sum(jnp.where(causal, f_row, 0.0), axis=1, keepdims=True)
        b_row = jnp.sum(jnp.where(t_i <= s_i, f_col, 0.0), axis=0, keepdims=True)
        m0 = m_scr[b]
        inter = b_col + m0
        dmat = jnp.where(causal, b_col - b_row + i_row, -1e30)
        yield
        m_t = jnp.maximum(inter, jnp.max(dmat, axis=1, keepdims=True))
        w_int = jnp.exp(inter - m_t)
        s = _dot_nt(q, k) * jnp.exp(dmat - m_t)
        c0 = c_scr[b]
        n0 = n_scr[b]
        yield
        num = w_int * _dot(q, c0) + _dot(s, v)
        den = w_int * jnp.sum(q * n0, axis=1, keepdims=True) + jnp.sum(s, axis=1, keepdims=True)
        hh = num / jnp.maximum(jnp.abs(den), jnp.exp(-m_t))
        m_new = m_t[L - 1:L, :]
        b_l = b_col[L - 1:L, :]
        a_end = jnp.exp(b_l - b_col + i_col - m_new)
        dec = jnp.exp(b_l + m0 - m_new)
        ka = k * a_end
        yield
        c_new = dec * c0 + _dot(ka.T, v)
        n_new = dec * n0 + jnp.sum(ka, axis=0, keepdims=True)
        out = _mlstm_out(hh, o_ref[...], nw_ref[...]).astype(y_ref.dtype)
        return out, c_new, n_new, m_new

    results = _interleave([sequence(b) for b in range(batch)])
    for b, (out, c_new, n_new, m_new) in enumerate(results):
        y_ref[b] = out
        c_scr[b] = c_new
        n_scr[b] = n_new
        m_scr[b] = m_new

    @pl.when(c == n_chunks - 1)
    def _():
        for b, (_, c_new, n_new, m_new) in enumerate(results):
            c_ref[b, 0] = c_new
            nm_ref[b, 0] = jnp.concatenate([n_new, jnp.broadcast_to(m_new, (7, ML_DK))], axis=0)


def _mlstm_prompt(proj, gate_bias, norm_w, *, batch, seq):
    L = ML_CHUNK
    nc = seq // L
    nh = ML_HEADS
    kq = nh * ML_DK
    p_specs = []
    for b in range(batch):
        rows = lambda c, b=b: b * nc + c
        p_specs += [pl.BlockSpec((L, ML_DK), lambda h, c, rows=rows: (rows(c), h)),
                    pl.BlockSpec((L, ML_DK), lambda h, c, rows=rows: (rows(c), nh + h)),
                    pl.BlockSpec((L, ML_DV), lambda h, c, rows=rows: (rows(c), 2 * kq // ML_DV + h)),
                    pl.BlockSpec((L, ML_DV), lambda h, c, rows=rows: (rows(c), 2 * kq // ML_DV + nh + h)),
                    pl.BlockSpec((L, LANES), lambda h, c, rows=rows: (rows(c), (2 * kq + 2 * ML_WIDTH) // LANES))]
    y, c_fin, nm = pl.pallas_call(
        functools.partial(_mlstm_prompt_kernel, n_chunks=nc, batch=batch),
        grid=(nh, nc),
        in_specs=p_specs + [pl.BlockSpec((1, LANES), lambda h, c: (0, 0)),
                            pl.BlockSpec((1, ML_DV), lambda h, c: (0, h))],
        out_specs=[pl.BlockSpec((batch, L, ML_DV), lambda h, c: (0, c, h)),
                   pl.BlockSpec((batch, 1, ML_DK, ML_DV), lambda h, c: (0, h, 0, 0)),
                   pl.BlockSpec((batch, 1, 8, ML_DK), lambda h, c: (0, h, 0, 0))],
        out_shape=[jax.ShapeDtypeStruct((batch, seq, ML_WIDTH), BF16),
                   jax.ShapeDtypeStruct((batch, nh, ML_DK, ML_DV), F32),
                   jax.ShapeDtypeStruct((batch, nh, 8, ML_DK), F32)],
        scratch_shapes=[pltpu.VMEM((batch, ML_DK, ML_DV), F32), pltpu.VMEM((batch, 1, ML_DK), F32),
                        pltpu.VMEM((batch, 1, 1), F32)],
        compiler_params=_cparams(2),
        name="mlstm_prompt",
    )(*([proj] * (5 * batch)), gate_bias, norm_w)
    return y.reshape(batch * seq, ML_WIDTH), c_fin, nm


def _mlstm_sample_kernel(q_ref, k_ref, v_ref, o_ref, gt_ref, gb_ref, nw_ref, c_ref, n_ref, m_ref,
                         y_ref, co_ref, no_ref, mo_ref):
    gates = gt_ref[0]
    eye = jnp.where(_iota((ML_DK, ML_DK), 0) == _iota((ML_DK, ML_DK), 1), 1.0, 0.0).astype(BF16)
    for h in range(ML_HEADS):
        q = q_ref[0][:, h * ML_DK:(h + 1) * ML_DK] * (ML_DK ** -0.5)
        k = k_ref[0][:, h * ML_DK:(h + 1) * ML_DK]
        v = v_ref[0][:, h * ML_DV:(h + 1) * ML_DV]
        i_pre, logf = _mlstm_gates(gates, gb_ref[...], h)
        c0 = c_ref[0, h]
        n0 = n_ref[0, h:h + 1, :]
        m0 = m_ref[0][:, h:h + 1]
        inter = logf + m0
        m_t = jnp.maximum(inter, i_pre)
        w_int = jnp.exp(inter - m_t)
        a_new = jnp.exp(i_pre - m_t)
        s = jnp.sum(q * k, axis=1, keepdims=True) * a_new
        qc = _dot(jnp.broadcast_to(q, (8, ML_DK)), c0)[0:1, :]
        num = w_int * qc + s * v
        den = w_int * jnp.sum(q * n0, axis=1, keepdims=True) + s
        hh = num / jnp.maximum(jnp.abs(den), jnp.exp(-m_t))
        k_hi, k_lo = _split2(jnp.broadcast_to(k, (8, ML_DK)))
        nt = (((1,), (1,)), ((), ()))
        k_col = (lax.dot_general(eye, k_hi, nt, preferred_element_type=F32)
                 + lax.dot_general(eye, k_lo, nt, preferred_element_type=F32))[:, 0:1]
        co_ref[0, h] = w_int * c0 + k_col * (a_new * v)
        no_ref[0, h:h + 1, :] = w_int * n0 + a_new * k
        mo_ref[0, :, h:h + 1] = m_t
        y_ref[0, :, h * ML_DV:(h + 1) * ML_DV] = _mlstm_out(
            hh, o_ref[0][:, h * ML_DV:(h + 1) * ML_DV], nw_ref[:, h * ML_DV:(h + 1) * ML_DV]).astype(y_ref.dtype)


def _mlstm_sample(proj3, gate_bias, norm_w, c0, n0, m0):
    b = proj3.shape[0]
    nh = ML_HEADS
    kq = nh * ML_DK
    p3 = lambda w, blk: pl.BlockSpec((1, 1, w), lambda i, blk=blk: (i, 0, blk))
    cspec = pl.BlockSpec((1, nh, ML_DK, ML_DV), lambda i: (i, 0, 0, 0))
    nspec = pl.BlockSpec((1, nh, ML_DK), lambda i: (i, 0, 0))
    mspec = pl.BlockSpec((1, 1, nh), lambda i: (i, 0, 0))
    return pl.pallas_call(
        _mlstm_sample_kernel,
        grid=(b,),
        in_specs=[p3(kq, 0), p3(kq, 1), p3(ML_WIDTH, 2 * kq // ML_WIDTH), p3(ML_WIDTH, 2 * kq // ML_WIDTH + 1),
                  p3(LANES, (2 * kq + 2 * ML_WIDTH) // LANES),
                  pl.BlockSpec((1, LANES), lambda i: (0, 0)),
                  pl.BlockSpec((1, ML_WIDTH), lambda i: (0, 0)),
                  cspec, nspec, mspec],
        out_specs=[pl.BlockSpec((1, 1, ML_WIDTH), lambda i: (i, 0, 0)), cspec, nspec, mspec],
        out_shape=[jax.ShapeDtypeStruct((b, 1, ML_WIDTH), BF16),
                   jax.ShapeDtypeStruct(c0.shape, F32), jax.ShapeDtypeStruct(n0.shape, F32),
                   jax.ShapeDtypeStruct(m0.shape, F32)],
        compiler_params=_cparams(1),
        name="mlstm_sample",
    )(proj3, proj3, proj3, proj3, proj3, gate_bias, norm_w, c0, n0, m0)


def _rms(x, w):
    return x * lax.rsqrt(jnp.mean(x * x, axis=1, keepdims=True) + EPS) * w


def _pack_w_in_kernel(wt_hbm, o_ref, buf, sem, *, n_a, split):
    j = pl.program_id(0)
    tr = o_ref.shape[0]

    def copy(step, slot):
        src = jnp.where(step < n_a, step * tr, split + (step - n_a) * tr)
        return pltpu.make_async_copy(wt_hbm.at[pl.ds(pl.multiple_of(src, 8), tr), :], buf.at[slot], sem.at[slot])

    pl.when(j == 0)(lambda: copy(j, 0).start())
    pl.when(j + 1 < pl.num_programs(0))(lambda: copy(j + 1, (j + 1) & 1).start())
    copy(j, j & 1).wait()
    o_ref[...] = buf[j & 1].astype(BF16)


def _pack_w_in(wt, *, split, width, tr):
    d = wt.shape[1]
    assert split % 8 == 0 and width % tr == 0 and split + width == wt.shape[0]
    n_a = width // tr
    return pl.pallas_call(
        functools.partial(_pack_w_in_kernel, n_a=n_a, split=split),
        grid=(2 * n_a,),
        in_specs=[pl.BlockSpec(memory_space=pl.ANY)],
        out_specs=pl.BlockSpec((tr, d), lambda j: (j, 0)),
        out_shape=jax.ShapeDtypeStruct((2 * width, d), BF16),
        scratch_shapes=[pltpu.VMEM((2, tr, d), F32), pltpu.SemaphoreType.DMA((2,))],
        compiler_params=_cparams(1, VMEM_LIMIT),
        name="pack_w_in",
    )(wt)


def _in_proj_kernel(x_ref, nw_ref, *refs):
    w_refs, o_ref, h_scr = refs[:-2], refs[-2], refs[-1]

    @pl.when(pl.program_id(1) == 0)
    def _():
        h_scr[...] = _rms(x_ref[...], nw_ref[...]).astype(BF16)

    h = h_scr[...]
    sub = w_refs[0].shape[0]
    for q, w_ref in enumerate(w_refs):
        o_ref[:, q * sub:(q + 1) * sub] = lax.dot_general(h, w_ref[...], (((1,), (1,)), ((), ())),
                                                          preferred_element_type=F32)


def _in_proj(x, norm_w, wt, *, tm, tn, n_streams):
    t, d = x.shape
    n = wt.shape[0]
    sub = tn // n_streams
    w_specs = [pl.BlockSpec((sub, d), lambda i, j, q=q: (j * n_streams + q, 0)) for q in range(n_streams)]
    return pl.pallas_call(
        _in_proj_kernel,
        grid=(t // tm, n // tn),
        in_specs=[pl.BlockSpec((tm, d), lambda i, j: (i, 0), pipeline_mode=pl.Buffered(1)),
                  pl.BlockSpec((1, d), lambda i, j: (0, 0))] + w_specs,
        out_specs=pl.BlockSpec((tm, tn), lambda i, j: (i, j)),
        out_shape=jax.ShapeDtypeStruct((t, n), F32),
        scratch_shapes=[pltpu.VMEM((tm, d), BF16)],
        compiler_params=_cparams(2, VMEM_LIMIT),
        name="in_proj",
    )(x, norm_w, *([wt] * n_streams))


def _out_proj_kernel(ya_ref, yb_ref, wa_ref, wb_ref, x_ref, o_ref):
    o_ref[...] = (x_ref[...] + jnp.dot(ya_ref[...], wa_ref[...], preferred_element_type=F32)
                  + jnp.dot(yb_ref[...], wb_ref[...], preferred_element_type=F32))


def _out_proj(ya, yb, w, x, *, tm, tn):
    t, kh = ya.shape
    n = w.shape[1]
    return pl.pallas_call(
        _out_proj_kernel,
        grid=(t // tm, n // tn),
        in_specs=[pl.BlockSpec((tm, kh), lambda i, j: (i, 0)),
                  pl.BlockSpec((tm, kh), lambda i, j: (i, 0)),
                  pl.BlockSpec((kh, tn), lambda i, j: (0, j)),
                  pl.BlockSpec((kh, tn), lambda i, j: (1, j)),
                  pl.BlockSpec((tm, tn), lambda i, j: (i, j))],
        out_specs=pl.BlockSpec((tm, tn), lambda i, j: (i, j)),
        out_shape=jax.ShapeDtypeStruct((t, n), F32),
        compiler_params=_cparams(2, VMEM_LIMIT),
        name="out_proj",
    )(ya, yb, w, w, x)


def _router_kernel(xp_ref, xs_ref, nw_ref, wr_ref, h_ref, lg_ref, *, nb_p):
    def emit(x_ref):
        h = _rms(x_ref[...], nw_ref[...])
        h_ref[...] = _pack_pairs(h, h.shape[1] // 2)
        lg_ref[...] = jnp.dot(h, wr_ref[...], preferred_element_type=F32, precision=lax.Precision.HIGHEST)

    pl.when(pl.program_id(0) < nb_p)(lambda: emit(xp_ref))
    pl.when(pl.program_id(0) >= nb_p)(lambda: emit(xs_ref))


def _router(xp, xs, norm_w, wr, *, tm):
    d = xp.shape[1]
    nb_p, nb_s = xp.shape[0] // tm, xs.shape[0] // tm
    t = xp.shape[0] + xs.shape[0]
    return pl.pallas_call(
        functools.partial(_router_kernel, nb_p=nb_p),
        grid=(nb_p + nb_s,),
        in_specs=[pl.BlockSpec((tm, d), lambda i: (jnp.minimum(i, nb_p - 1), 0)),
                  pl.BlockSpec((tm, d), lambda i: (jnp.maximum(i - nb_p, 0), 0)),
                  pl.BlockSpec((1, d), lambda i: (0, 0)),
                  pl.BlockSpec((d, LANES), lambda i: (0, 0))],
        out_specs=[pl.BlockSpec((tm, d // 2), lambda i: (i, 0)), pl.BlockSpec((tm, LANES), lambda i: (i, 0))],
        out_shape=[jax.ShapeDtypeStruct((t, d // 2), jnp.uint32), jax.ShapeDtypeStruct((t, LANES), F32)],
        compiler_params=_cparams(1, VMEM_LIMIT),
        name="router",
    )(xp, xs, norm_w, wr)


def _row_gather(src_hbm, dst_vmem, idx_ref, base, n_rows, sem):
    def copy(r):
        return pltpu.make_async_copy(src_hbm.at[pl.ds(idx_ref[base + r], 1), :], dst_vmem.at[pl.ds(r, 1), :], sem)

    def start():
        def body(r, carry):
            copy(r).start(priority=1)
            return carry
        lax.fori_loop(0, n_rows, body, 0, unroll=8)

    def wait():
        def body(r, carry):
            copy(r).wait()
            return carry
        lax.fori_loop(0, n_rows, body, 0, unroll=8)

    return start, wait


def _moe_gather_kernel(tok_ref, nused_ref, h_hbm, o_ref, buf, sem, *, tm):
    i = pl.program_id(0)
    nused = nused_ref[0]
    gather = lambda blk: _row_gather(h_hbm, buf.at[blk & 1], tok_ref, blk * tm, tm, sem.at[blk & 1])

    pl.when(jnp.logical_and(i == 0, nused > 0))(lambda: gather(i)[0]())
    pl.when(i + 1 < nused)(lambda: gather(i + 1)[0]())

    @pl.when(i < nused)
    def _():
        gather(i)[1]()
        o_ref[...] = _unpack_pairs(buf[i & 1], buf.shape[2]).astype(o_ref.dtype)

    @pl.when(i >= nused)
    def _():
        o_ref[...] = jnp.zeros_like(o_ref)


def _moe_gather(slot_tok, nused, h, *, tm):
    p = slot_tok.shape[0]
    half = h.shape[1]
    return pl.pallas_call(
        functools.partial(_moe_gather_kernel, tm=tm),
        grid_spec=pltpu.PrefetchScalarGridSpec(
            num_scalar_prefetch=2,
            grid=(p // tm,),
            in_specs=[pl.BlockSpec(memory_space=pl.ANY)],
            out_specs=pl.BlockSpec((tm, 2 * half), lambda i, tok, nu: (i, 0)),
            scratch_shapes=[pltpu.VMEM((2, tm, half), jnp.uint32), pltpu.SemaphoreType.DMA((2,))]),
        out_shape=jax.ShapeDtypeStruct((p, 2 * half), BF16),
        compiler_params=_cparams(1, VMEM_LIMIT),
        name="moe_gather",
    )(slot_tok, nused, h)


def _expert_row_loop(n_blocks, in_copy, out_copy, compute):
    in_copy(0, 0).start(priority=1)

    def body(r, carry):
        slot = r & 1
        pl.when(r + 1 < n_blocks)(lambda: in_copy(r + 1, 1 - slot).start(priority=1))
        in_copy(r, slot).wait()
        pl.when(r >= 2)(lambda: out_copy(r - 2, slot).wait())
        compute(r, slot)
        out_copy(r, slot).start(priority=1)
        return carry

    lax.fori_loop(0, n_blocks, body, 0)
    pl.when(n_blocks >= 2)(lambda: out_copy(n_blocks - 2, n_blocks & 1).wait())
    out_copy(n_blocks - 1, (n_blocks - 1) & 1).wait()


def _zero_tail_blocks(first, zero_block, dst_hbm, sem, tm):
    zero_block[...] = jnp.zeros_like(zero_block)
    copy = lambda t: pltpu.make_async_copy(zero_block, dst_hbm.at[pl.ds(pl.multiple_of(t * tm, tm), tm), :], sem)
    n_total = dst_hbm.shape[0] // tm

    def start(t, carry):
        copy(t).start()
        return carry

    def wait(t, carry):
        copy(t).wait()
        return carry

    lax.fori_loop(first, n_total, start, 0)
    lax.fori_loop(first, n_total, wait, 0)


_W_SLAB = 256
_SLABS_PER_BLOCK = 8
_SLABS_IN_FLIGHT = 4


def _moe_up_kernel(bstart_ref, bcount_ref, w1_hbm, w3_hbm, xs_hbm, h_hbm, wb, stage, xbuf, obuf,
                   w_sem, in_sem, out_sem, *, tm):
    e = pl.program_id(0)
    n_exp = pl.num_programs(0)
    n_slab = wb.shape[2] // _W_SLAB
    cur = e & 1
    n_blocks = bcount_ref[e]
    row = lambda r: pl.multiple_of((bstart_ref[e] + r) * tm, tm)
    slab_rows = lambda s: pl.ds(pl.multiple_of(s * _W_SLAB, _W_SLAB), _W_SLAB)

    depth = stage.shape[0]

    def slab_copies(expert, s):
        slot = s & (depth - 1)
        return [pltpu.make_async_copy(w_hbm.at[expert, slab_rows(s), :], stage.at[slot, m], w_sem.at[slot])
                for m, w_hbm in enumerate((w1_hbm, w3_hbm))]

    def start_slab(expert, s):
        for cp in slab_copies(expert, s):
            cp.start()

    def convert_slab(expert, s, dst):
        for cp in slab_copies(expert, s):
            cp.wait()
        for m in range(2):
            wb[dst, m, slab_rows(s), :] = stage[s & (depth - 1), m].astype(BF16)
        pl.when(s + depth < n_slab)(lambda: start_slab(expert, s + depth))

    def convert_range(expert, lo, hi, dst):
        def body(s, carry):
            convert_slab(expert, s, dst)
            return carry
        lax.fori_loop(lo, hi, body, 0)

    @pl.when(e == 0)
    def _():
        for s in range(depth):
            start_slab(0, s)
        convert_range(0, 0, n_slab, 0)

    has_next = e + 1 < n_exp

    @pl.when(has_next)
    def _():
        for s in range(depth):
            start_slab(e + 1, s)

    def in_copy(r, slot):
        return pltpu.make_async_copy(xs_hbm.at[pl.ds(row(r), tm), :], xbuf.at[slot], in_sem.at[slot])

    def out_copy(r, slot):
        return pltpu.make_async_copy(obuf.at[slot], h_hbm.at[pl.ds(row(r), tm), :], out_sem.at[slot])

    def compute(r, slot):
        n_tiles = _SLABS_PER_BLOCK // 2
        tf = wb.shape[3] // n_tiles
        for t in range(n_tiles):
            x = xbuf[slot]
            cols = pl.ds(t * tf, tf)
            a = jnp.dot(x, wb[cur, 0, :, cols], preferred_element_type=F32)
            b = jnp.dot(x, wb[cur, 1, :, cols], preferred_element_type=F32)
            obuf[slot, :, cols] = (a * _sigmoid(a) * b).astype(BF16)
            for q in range(2):
                s = r * _SLABS_PER_BLOCK + 2 * t + q
                pl.when(jnp.logical_and(has_next, s < n_slab))(lambda s=s: convert_slab(e + 1, s, 1 - cur))

    pl.when(n_blocks > 0)(lambda: _expert_row_loop(n_blocks, in_copy, out_copy, compute))

    @pl.when(has_next)
    def _():
        convert_range(e + 1, jnp.minimum(n_blocks * _SLABS_PER_BLOCK, n_slab), n_slab, 1 - cur)

    @pl.when(e == n_exp - 1)
    def _():
        _zero_tail_blocks(bstart_ref[e] + n_blocks, obuf.at[0], h_hbm, out_sem.at[0], tm)


def _moe_up(bstart, bcount, xs, w1, w3, *, tm):
    p, d = xs.shape
    n_exp, _, ff = w1.shape
    any_spec = pl.BlockSpec(memory_space=pl.ANY)
    return pl.pallas_call(
        functools.partial(_moe_up_kernel, tm=tm),
        grid_spec=pltpu.PrefetchScalarGridSpec(
            num_scalar_prefetch=2,
            grid=(n_exp,),
            in_specs=[any_spec] * 3,
            out_specs=any_spec,
            scratch_shapes=[pltpu.VMEM((2, 2, d, ff), BF16), pltpu.VMEM((_SLABS_IN_FLIGHT, 2, _W_SLAB, ff), F32),
                            pltpu.VMEM((2, tm, d), BF16), pltpu.VMEM((2, tm, ff), BF16),
                            pltpu.SemaphoreType.DMA((_SLABS_IN_FLIGHT,)), pltpu.SemaphoreType.DMA((2,)),
                            pltpu.SemaphoreType.DMA((2,))]),
        out_shape=jax.ShapeDtypeStruct((p, ff), BF16),
        compiler_params=_cparams(1, VMEM_LIMIT),
        name="moe_up",
    )(bstart, bcount, w1, w3, xs)


def _moe_down_kernel(bstart_ref, bcount_ref, *refs, tm, tn, n_streams):
    w2_refs = refs[:n_streams]
    hs_hbm, yb_hbm, w2b, hbuf, obuf, in_sem, out_sem = refs[n_streams:]
    e = pl.program_id(0)
    n_blocks = bcount_ref[e]
    row = lambda r: pl.multiple_of((bstart_ref[e] + r) * tm, tm)

    @pl.when(n_blocks > 0)
    def _():
        sub = w2b.shape[0] // n_streams
        for q, w2_ref in enumerate(w2_refs):
            w2b[q * sub:(q + 1) * sub, :] = w2_ref[0].astype(BF16)

        def in_copy(r, slot):
            return pltpu.make_async_copy(hs_hbm.at[pl.ds(row(r), tm), :], hbuf.at[slot], in_sem.at[slot])

        def out_copy(r, slot):
            return pltpu.make_async_copy(obuf.at[slot], yb_hbm.at[pl.ds(row(r), tm), :], out_sem.at[slot])

        def compute(r, slot):
            h = hbuf[slot]
            for n in range(w2b.shape[1] // tn):
                y = jnp.dot(h, w2b[:, n * tn:(n + 1) * tn], preferred_element_type=F32)
                obuf[slot, :, n * tn // 2:(n + 1) * tn // 2] = _pack_pairs(y, tn // 2)

        _expert_row_loop(n_blocks, in_copy, out_copy, compute)

    @pl.when(e == pl.num_programs(0) - 1)
    def _():
        _zero_tail_blocks(bstart_ref[e] + n_blocks, obuf.at[0], yb_hbm, out_sem.at[0], tm)


def _moe_down(bstart, bcount, hs, w2, *, tm, tn, n_streams):
    p = hs.shape[0]
    n_exp, ff, d = w2.shape
    w_specs = [pl.BlockSpec((1, ff // n_streams, d), lambda e, bs, bc, q=q: (e, q, 0)) for q in range(n_streams)]
    return pl.pallas_call(
        functools.partial(_moe_down_kernel, tm=tm, tn=tn, n_streams=n_streams),
        grid_spec=pltpu.PrefetchScalarGridSpec(
            num_scalar_prefetch=2,
            grid=(n_exp,),
            in_specs=w_specs + [pl.BlockSpec(memory_space=pl.ANY)],
            out_specs=pl.BlockSpec(memory_space=pl.ANY),
            scratch_shapes=[pltpu.VMEM((ff, d), BF16), pltpu.VMEM((2, tm, ff), BF16),
                            pltpu.VMEM((2, tm, d // 2), jnp.uint32),
                            pltpu.SemaphoreType.DMA((2,)), pltpu.SemaphoreType.DMA((2,))]),
        out_shape=jax.ShapeDtypeStruct((p, d // 2), jnp.uint32),
        compiler_params=_cparams(1, VMEM_LIMIT),
        name="moe_down",
    )(bstart, bcount, *([w2] * n_streams), hs)


def _moe_combine_kernel(p0_ref, p1_ref, xp_ref, xs_ref, g0_ref, g1_ref, nw_ref, yb_hbm, op_ref, os_ref,
                        buf0, buf1, sem, *, tc, nb_p, half):
    i = pl.program_id(0)
    slot = i & 1

    def gathers(blk):
        s = blk & 1
        return [_row_gather(yb_hbm, buf.at[s], p_ref, blk * tc, tc, sem.at[s])
                for buf, p_ref in ((buf0, p0_ref), (buf1, p1_ref))]

    def start_all(blk):
        for start, _ in gathers(blk):
            start()

    pl.when(i == 0)(lambda: start_all(i))
    pl.when(i + 1 < pl.num_programs(0))(lambda: start_all(i + 1))
    for _, wait in gathers(i):
        wait()

    def emit(x_ref, o_ref):
        x = (x_ref[...] + g0_ref[:, 0:1] * _unpack_pairs(buf0[slot], half)
             + g1_ref[:, 0:1] * _unpack_pairs(buf1[slot], half))
        o_ref[...] = _rms(x, nw_ref[...])

    pl.when(i < nb_p)(lambda: emit(xp_ref, op_ref))
    pl.when(i >= nb_p)(lambda: emit(xs_ref, os_ref))


def _moe_combine(pos0, pos1, xp, xs, g0, g1, norm_w, yb, *, tc, half):
    d = xp.shape[1]
    nb_p, nb_s = xp.shape[0] // tc, xs.shape[0] // tc
    p_idx = lambda i, a, b: (jnp.minimum(i, nb_p - 1), 0)
    s_idx = lambda i, a, b: (jnp.maximum(i - nb_p, 0), 0)
    return pl.pallas_call(
        functools.partial(_moe_combine_kernel, tc=tc, nb_p=nb_p, half=half),
        grid_spec=pltpu.PrefetchScalarGridSpec(
            num_scalar_prefetch=2,
            grid=(nb_p + nb_s,),
            in_specs=[pl.BlockSpec((tc, d), p_idx),
                      pl.BlockSpec((tc, d), s_idx),
                      pl.BlockSpec((tc, LANES), lambda i, a, b: (i, 0)),
                      pl.BlockSpec((tc, LANES), lambda i, a, b: (i, 0)),
                      pl.BlockSpec((1, d), lambda i, a, b: (0, 0)),
                      pl.BlockSpec(memory_space=pl.ANY)],
            out_specs=[pl.BlockSpec((tc, d), p_idx), pl.BlockSpec((tc, d), s_idx)],
            scratch_shapes=[pltpu.VMEM((2, tc, d // 2), jnp.uint32), pltpu.VMEM((2, tc, d // 2), jnp.uint32),
                            pltpu.SemaphoreType.DMA((2,))]),
        out_shape=[jax.ShapeDtypeStruct(xp.shape, F32), jax.ShapeDtypeStruct(xs.shape, F32)],
        compiler_params=_cparams(1, VMEM_LIMIT),
        name="moe_combine",
    )(pos0, pos1, xp, xs, g0, g1, norm_w, yb)


def _route(logits, bg, be, *, tm):
    t = logits.shape[0]
    pg = jax.nn.softmax(logits[:, :N_GROUPS] + bg, axis=-1)
    g_idx = jnp.argmax(pg, axis=-1).astype(jnp.int32)
    p_sel = jnp.take_along_axis(pg, g_idx[:, None], axis=-1)
    le = (logits[:, N_GROUPS:N_GROUPS + N_EXPERTS] + be).reshape(t, N_GROUPS, EXPERTS_PER_GROUP)
    le = jnp.take_along_axis(le, g_idx[:, None, None], axis=1)[:, 0]
    pe = jax.nn.softmax(le, axis=-1)
    top_v, top_i = lax.top_k(pe, TOP_K)
    gate = p_sel * top_v / jnp.sum(top_v, axis=-1, keepdims=True)
    e_flat = (g_idx[:, None] * EXPERTS_PER_GROUP + top_i.astype(jnp.int32)).reshape(-1)
    a = t * TOP_K
    onehot = (e_flat[:, None] == jnp.arange(N_EXPERTS, dtype=jnp.int32)[None, :]).astype(jnp.int32)
    rank = jnp.sum((jnp.cumsum(onehot, axis=0) - onehot) * onehot, axis=1)
    counts = jnp.sum(onehot, axis=0)
    padded = (counts + tm - 1) // tm * tm
    pad_end = jnp.cumsum(padded)
    dest = (pad_end - padded)[e_flat] + rank
    n_blocks = -(-(a + N_EXPERTS * (tm - 1)) // tm)
    tok = jnp.arange(a, dtype=jnp.int32) // TOP_K
    slot_tok = jnp.zeros((n_blocks * tm,), jnp.int32).at[dest].set(tok)
    bstart = ((pad_end - padded) // tm).astype(jnp.int32)
    bcount = (padded // tm).astype(jnp.int32)
    nused = (pad_end[-1] // tm).astype(jnp.int32).reshape(1)
    dest = dest.reshape(t, TOP_K)
    return gate, dest[:, 0], dest[:, 1], slot_tok, bstart, bcount, nused


def _moe(xp, xs, norm2_w, wr, bg, be, w1, w3, w2, final_w, *, tm_route, tm_blk, tn, tc):
    h, logits = _router(xp, xs, norm2_w, wr, tm=tm_route)
    gate, pos0, pos1, slot_tok, bstart, bcount, nused = _route(logits, bg, be, tm=tm_blk)
    xg = _moe_gather(slot_tok, nused, h, tm=tm_blk)
    hs = _moe_up(bstart, bcount, xg, w1, w3, tm=tm_blk)
    yb = _moe_down(bstart, bcount, hs, w2, tm=tm_blk, tn=tn, n_streams=4)
    g0 = jnp.broadcast_to(gate[:, 0:1], (h.shape[0], LANES))
    g1 = jnp.broadcast_to(gate[:, 1:2], (h.shape[0], LANES))
    return _moe_combine(pos0, pos1, xp, xs, g0, g1, final_w, yb, tc=tc, half=tn // 2)


def kernel(x_prompt, x_sample, state_mlstm_c, state_mlstm_n, state_mlstm_m, state_rwkv, state_rwkv_shift, norm1_w, w_in, w_out, ml_b_i, ml_b_f, ml_norm_w, rw_mu, rw_w0, rw_w2, rw_a0, rw_a2, rw_g2, rw_k_k, rw_k_a, rw_r_k, rw_ln_w, rw_ln_b, norm2_w, router_group_w, router_group_b, router_expert_w, router_expert_b, moe_w1, moe_w3, moe_w2, final_norm_w):
    assert w_in.shape[0] == 1, "single-layer trunk"
    bp, seq, d = x_prompt.shape
    bs = x_sample.shape[0]
    tp = bp * seq
    ml_proj = 2 * ML_HEADS * ML_DK + 2 * ML_WIDTH + 2 * ML_HEADS
    rw_proj = 3 * RW_WIDTH + RW_LORA
    rw_col0 = -(-ml_proj // RW_GROUP) * RW_GROUP
    n_groups = RW_WIDTH // RW_GROUP

    w_cat = _pack_w_in(jnp.swapaxes(w_in[0], 0, 1), split=ml_proj, width=rw_col0, tr=RW_GROUP)
    w_o = w_out[0].astype(BF16)
    mu = rw_mu[0]
    rows = [rw_w0[0], rw_a0[0], rw_k_k[0], rw_k_a[0], rw_r_k[0].reshape(-1), rw_ln_w[0], rw_ln_b[0],
            mu[:RW_WIDTH], mu[RW_WIDTH:2 * RW_WIDTH], mu[2 * RW_WIDTH:3 * RW_WIDTH]]
    prm = jnp.concatenate([jnp.stack(rows), jnp.zeros((16 - len(rows), RW_WIDTH), F32)], axis=0)
    mu_l = mu[3 * RW_WIDTH:][None]
    wl = jnp.zeros((RW_LORA, 3, RW_WIDTH), F32)
    wl = wl.at[:RW_DECAY_LORA, 0].set(rw_w2[0]).at[RW_DECAY_LORA:RW_DECAY_LORA + RW_A_LORA, 1].set(rw_a2[0])
    wl = wl.at[RW_DECAY_LORA + RW_A_LORA:, 2].set(rw_g2[0])
    wl = wl.reshape(RW_LORA, 3, n_groups, RW_GROUP).transpose(2, 0, 1, 3).reshape(n_groups, RW_LORA, 3 * RW_GROUP)
    wl = wl.astype(BF16)
    gate_bias = jnp.zeros((1, LANES), F32).at[0, :ML_HEADS].set(ml_b_i[0]).at[0, ML_HEADS:2 * ML_HEADS].set(ml_b_f[0])
    ml_nw = ml_norm_w[0][None]
    wr = jnp.concatenate([router_group_w[0], router_expert_w[0],
                          jnp.zeros((d, LANES - N_GROUPS - N_EXPERTS), F32)], axis=1)

    xp = x_prompt.reshape(tp, d)
    xs = x_sample.reshape(bs, d)
    proj_p = _in_proj(xp, norm1_w, w_cat, tm=512, tn=1280, n_streams=5)
    proj_s = _in_proj(xs, norm1_w, w_cat, tm=bs, tn=1280, n_streams=5)

    y_ml_p, p_c, p_nm = _mlstm_prompt(proj_p, gate_bias, ml_nw, batch=bp, seq=seq)
    y_rw_p, p_s = _rwkv_prompt(proj_p, prm, mu_l, wl, batch=bp, seq=seq, rw_col0=rw_col0, n_groups=n_groups, gps=2)
    p_sh = jnp.concatenate([lax.slice(proj_p, (b * seq + seq - 1, rw_col0), (b * seq + seq, rw_col0 + rw_proj))
                            for b in range(bp)], axis=0)

    y_ml_s, s_c, s_n, s_m = _mlstm_sample(proj_s[:, None, :], gate_bias, ml_nw, state_mlstm_c[0], state_mlstm_n[0],
                                          state_mlstm_m[0][:, None, :])
    r, k, v, g, rt, kt, vt, kkt, kat, dt = _rwkv_sample_prep(proj_s, state_rwkv_shift[0], prm, mu_l, wl, rows=bs,
                                                             rw_col0=rw_col0, n_groups=n_groups)
    s_t, y_t = _rwkv_sample_state(jnp.transpose(state_rwkv[0], (1, 2, 3, 0)), rt, kt, kkt, kat, dt, vt)
    s_s = jnp.transpose(s_t, (3, 0, 1, 2))
    y_rw_s = _rwkv_sample_post(y_t, r, k, v, g, prm)
    s_sh = proj_s[:, rw_col0:rw_col0 + rw_proj]

    x2_p = _out_proj(y_ml_p, y_rw_p, w_o, xp, tm=512, tn=1024)
    x2_s = _out_proj(y_ml_s.reshape(bs, ML_WIDTH), y_rw_s, w_o, xs, tm=bs, tn=1024)
    y_p, y_s = _moe(x2_p, x2_s, norm2_w, wr, router_group_b[0], router_expert_b[0], moe_w1[0], moe_w3[0],
                    moe_w2[0], final_norm_w[None], tm_route=128, tm_blk=256, tn=1024, tc=128)

    lead = lambda a: a[None]
    return (y_p.reshape(bp, seq, d), y_s.reshape(bs, 1, d),
            lead(p_c), lead(p_nm[:, :, 0]), lead(p_nm[:, :, 1, 0]), lead(p_s), lead(p_sh),
            lead(s_c), lead(s_n), lead(s_m[:, 0]), lead(s_s), lead(s_sh))
```

```python
import functools

import jax
import jax.numpy as jnp
from jax import lax
from jax.experimental import pallas as pl
from jax.experimental.pallas import tpu as pltpu

F32 = jnp.float32
BF16 = jnp.bfloat16

D_MODEL = 4096
ML_HEADS = 4
ML_DK = 256
ML_DV = 512
ML_WIDTH = ML_HEADS * ML_DV
GATE_SOFTCAP = 15.0
RW_HEAD = 64
RW_WIDTH = D_MODEL - ML_WIDTH
RW_HEADS = RW_WIDTH // RW_HEAD
RW_DECAY_LORA = 96
RW_A_LORA = 96
RW_GATE_LORA = 64
RW_GN_EPS = 64e-5
N_GROUPS = 4
EXPERTS_PER_GROUP = 8
N_EXPERTS = N_GROUPS * EXPERTS_PER_GROUP
TOP_K = 2
EPS = 1e-6

LANES = 128
RW_GROUP = 256
RW_GHEADS = RW_GROUP // RW_HEAD
RW_LORA = RW_DECAY_LORA + RW_A_LORA + RW_GATE_LORA
RW_CHUNK = 64
ML_CHUNK = 256
_SAMPLE_ROWS_PER_STAGE = 8
VMEM_LIMIT = 56 * 1024 * 1024


def _cparams(n_axes, vmem=None):
    return pltpu.CompilerParams(dimension_semantics=("arbitrary",) * n_axes, vmem_limit_bytes=vmem)


def _sigmoid(x):
    return 1.0 / (1.0 + jnp.exp(-x))


def _softplus(x):
    return jnp.maximum(x, 0.0) + jnp.log(1.0 + jnp.exp(-jnp.abs(x)))


def _soft_cap(x):
    return GATE_SOFTCAP * jnp.tanh(x / GATE_SOFTCAP)


def _dot(a, b):
    return jnp.dot(a.astype(BF16), b.astype(BF16), preferred_element_type=F32)


def _dot_nt(a, b):
    return lax.dot_general(a.astype(BF16), b.astype(BF16), (((1,), (1,)), ((), ())),
                           preferred_element_type=F32)


def _split2(x):
    hi = x.astype(BF16)
    lo = (x - hi.astype(F32)).astype(BF16)
    return hi, lo


def _split3(x):
    hi = x.astype(BF16)
    r = x - hi.astype(F32)
    mid = r.astype(BF16)
    lo = (r - mid.astype(F32)).astype(BF16)
    return hi, mid, lo


def _iota(shape, axis):
    return lax.broadcasted_iota(jnp.int32, shape, axis)


def _pack_pairs(x, half):
    bits = lax.bitcast_convert_type(x.astype(BF16).astype(F32), jnp.uint32)
    parts = []
    for g in range(x.shape[1] // (2 * half)):
        lo = bits[:, 2 * half * g:2 * half * g + half]
        hi = bits[:, 2 * half * g + half:2 * half * (g + 1)]
        parts.append((lo >> 16) | hi)
    return parts[0] if len(parts) == 1 else jnp.concatenate(parts, axis=1)


def _unpack_pairs(p, half):
    parts = []
    for g in range(p.shape[1] // half):
        w = p[:, half * g:half * (g + 1)]
        parts.append(lax.bitcast_convert_type(w << 16, F32))
        parts.append(lax.bitcast_convert_type(w & jnp.uint32(0xFFFF0000), F32))
    return jnp.concatenate(parts, axis=1)


def _head_block_ones(n):
    return jnp.where((_iota((n, n), 0) >> 6) == (_iota((n, n), 1) >> 6), 1.0, 0.0).astype(BF16)


def _seg_sum(x, bd, passes=2):
    hi, lo = _split2(x)
    out = jnp.dot(hi, bd, preferred_element_type=F32)
    return out + jnp.dot(lo, bd, preferred_element_type=F32) if passes == 2 else out


_P_W0, _P_A0, _P_KK, _P_KA, _P_RK, _P_LNW, _P_LNB, _P_MUR, _P_MUK, _P_MUV = range(10)


def _rwkv_prep(p_r, p_k, p_v, p_l, prev_r, prev_k, prev_v, prev_l, prm, mu_l, wl, bd):
    row = lambda i: prm[i:i + 1, :]
    xr = p_r + (prev_r - p_r) * row(_P_MUR)
    xk = p_k + (prev_k - p_k) * row(_P_MUK)
    xv = p_v + (prev_v - p_v) * row(_P_MUV)
    xl = p_l + (prev_l - p_l) * mu_l
    lane = _iota(xl.shape, 1)
    z = jnp.where(lane < RW_DECAY_LORA, jnp.tanh(xl),
                  jnp.where(lane < RW_DECAY_LORA + RW_A_LORA, xl, _sigmoid(xl)))
    lo = _dot(z, wl)
    lw, la, lg = lo[:, :RW_GROUP], lo[:, RW_GROUP:2 * RW_GROUP], lo[:, 2 * RW_GROUP:]
    w = -_softplus(-(row(_P_W0) + lw)) - 0.5
    logd = -jnp.exp(w)
    a = _sigmoid(row(_P_A0) + la)
    kk = xk * row(_P_KK)
    kk = kk / jnp.maximum(jnp.sqrt(_seg_sum(kk * kk, bd)), 1e-12)
    k = xk * (1.0 + (a - 1.0) * row(_P_KA))
    return xr, k, xv, kk, a, logd, lg


def _rwkv_post(y, r, k, v, g, prm, bd):
    row = lambda i: prm[i:i + 1, :]
    mean = _seg_sum(y, bd, passes=1) * (1.0 / RW_HEAD)
    yc = y - mean
    var = _seg_sum(yc * yc, bd, passes=1) * (1.0 / RW_HEAD)
    yn = yc * lax.rsqrt(var + RW_GN_EPS) * row(_P_LNW) + row(_P_LNB)
    bonus = _seg_sum(r * k * row(_P_RK), bd, passes=1) * v
    return (yn + bonus) * g


def _rwkv_chunk(r, k, v, kk, a, logd, G, bd):
    L = r.shape[0]
    tril = jnp.where(_iota((L, L), 1) <= _iota((L, L), 0), 1.0, 0.0).astype(BF16)
    d_hi, d_lo = _split2(logd)
    clog = (jnp.dot(tril, d_hi, preferred_element_type=F32)
            + jnp.dot(tril, d_lo, preferred_element_type=F32))
    clog_l = clog[L - 1:L, :]
    n_in = jnp.exp(-clog)
    to_end = jnp.exp(clog_l - clog)
    ka = kk * a
    at = -kk * jnp.exp(clog - logd)
    rt = r * jnp.exp(clog)
    lane_head = _iota((1, RW_GROUP), 1) >> 6
    masks = [lane_head == h for h in range(RW_GHEADS)]
    zero = jnp.zeros_like(at)
    bf = lambda x: x.astype(BF16)
    lhs = jnp.concatenate([jnp.where(m, x, zero) for m in masks for x in (at, rt)], axis=0)
    xx = _dot_nt(lhs, jnp.concatenate([ka * n_in, k * n_in], axis=0))
    t2 = _iota((L, 2 * L), 0)
    c2 = _iota((L, 2 * L), 1)
    s2 = c2 & (L - 1)
    right = c2 >= L
    eye_pad = jnp.where(c2 == t2 + L, 1.0, 0.0)
    zeros_v = jnp.zeros((L, RW_GROUP), BF16)
    v_b = bf(v)
    vz = jnp.concatenate([zeros_v, v_b], axis=0)
    yield
    zs, rbk = [], []
    makv = zero
    for h, m in enumerate(masks):
        o = 2 * L * h
        nk = jnp.where(s2 < t2, xx[o:o + L], 0.0)
        rbk.append(bf(jnp.where(s2 <= t2, xx[o + L:o + 2 * L], 0.0)))
        makv = makv + jnp.where(m, _dot(jnp.where(right, nk, 0.0), vz), 0.0)
        zs.append(jnp.where(right, eye_pad, nk))
    for _ in range((L - 1).bit_length()):
        yield
        zs = [_dot(z[:, :L], z) + jnp.where(right, z, 0.0) for z in zs]
    yield
    ws = _dot_nt(jnp.concatenate([at, rt], axis=0), G)
    xz = jnp.concatenate([zeros_v, bf(ws[:L] + makv)], axis=0)
    yield
    w_all = zero
    for z_h, m in zip(zs, masks):
        w_all = w_all + jnp.where(m, _dot(z_h, xz), 0.0)
    wv = jnp.concatenate([bf(w_all), v_b], axis=0)
    yield
    y = ws[L:]
    for rbk_h, m in zip(rbk, masks):
        y = y + jnp.where(m, jnp.dot(rbk_h, wv, preferred_element_type=F32), 0.0)
    upd = _dot(wv.astype(F32).T, jnp.concatenate([ka * to_end, k * to_end], axis=0))
    g_new = G * jnp.exp(clog_l) + jnp.where(bd > 0, upd, 0.0)
    return y, g_new


def _interleave(gens):
    results = [None] * len(gens)
    live = list(range(len(gens)))
    while live:
        for i in list(live):
            try:
                next(gens[i])
            except StopIteration as stop:
                results[i] = stop.value
                live.remove(i)
    return results


def _shift_rows(x, first_row):
    rolled = pltpu.roll(x, 1, axis=0)
    return jnp.where(_iota(x.shape, 0) == 0, first_row, rolled)


def _rwkv_prompt_kernel(*refs, n_chunks, batch, gps):
    n_chain = gps * batch
    p_refs = refs[:4 * n_chain]
    prm_ref, mul_ref, wl_ref, y_ref, s_ref, g_scr, carry_scr = refs[4 * n_chain:]
    c = pl.program_id(1)

    @pl.when(c == 0)
    def _():
        g_scr[...] = jnp.zeros_like(g_scr)
        carry_scr[...] = jnp.zeros_like(carry_scr)

    bd = _head_block_ones(RW_GROUP)

    def chain(q):
        gi = q // batch
        prm = prm_ref[:, gi * RW_GROUP:(gi + 1) * RW_GROUP]
        ps = [ref[...] for ref in p_refs[4 * q:4 * q + 4]]
        L = ps[0].shape[0]
        prevs = [_shift_rows(x, carry_scr[q, i:i + 1, :]) for i, x in enumerate(ps)]
        r, k, v, kk, a, logd, g = _rwkv_prep(*ps, *prevs, prm, mul_ref[...], wl_ref[gi], bd)
        y, g_new = yield from _rwkv_chunk(r, k, v, kk, a, logd, g_scr[q], bd)
        yield
        out = _rwkv_post(y, r, k, v, g, prm, bd).astype(y_ref.dtype)
        return out, g_new, [x[L - 1:L, :] for x in ps]

    results = _interleave([chain(q) for q in range(n_chain)])
    for q, (out, g_new, last_rows) in enumerate(results):
        gi, b = divmod(q, batch)
        y_ref[b, :, gi * RW_GROUP:(gi + 1) * RW_GROUP] = out
        g_scr[q] = g_new
        for i, x in enumerate(last_rows):
            carry_scr[q, i:i + 1, :] = x

    @pl.when(c == n_chunks - 1)
    def _():
        for q, (_, g_new, _) in enumerate(results):
            gi, b = divmod(q, batch)
            for h in range(RW_GHEADS):
                s_ref[b, gi * RW_GHEADS + h] = g_new[RW_HEAD * h:RW_HEAD * (h + 1), RW_HEAD * h:RW_HEAD * (h + 1)]


def _rwkv_prompt(proj, prm, mu_l, wl, *, batch, seq, rw_col0, n_groups, gps):
    L = RW_CHUNK
    nc = seq // L
    cb = rw_col0 // RW_GROUP
    width = n_groups * RW_GROUP
    p_specs = []
    for gi in range(gps):
        for b in range(batch):
            col = lambda g, off, gi=gi: cb + off + g * gps + gi
            rows = lambda c, b=b: b * nc + c
            p_specs += [pl.BlockSpec((L, RW_GROUP), lambda g, c, o=o, col=col, rows=rows: (rows(c), col(g, o)))
                        for o in (0, n_groups, 2 * n_groups)]
            p_specs += [pl.BlockSpec((L, RW_GROUP), lambda g, c, rows=rows: (rows(c), cb + 3 * n_groups))]
    n_chain = gps * batch
    y, s = pl.pallas_call(
        functools.partial(_rwkv_prompt_kernel, n_chunks=nc, batch=batch, gps=gps),
        grid=(n_groups // gps, nc),
        in_specs=p_specs + [pl.BlockSpec((16, gps * RW_GROUP), lambda g, c: (0, g)),
                            pl.BlockSpec((1, RW_LORA), lambda g, c: (0, 0)),
                            pl.BlockSpec((gps, RW_LORA, 3 * RW_GROUP), lambda g, c: (g, 0, 0))],
        out_specs=[pl.BlockSpec((batch, L, gps * RW_GROUP), lambda g, c: (0, c, g)),
                   pl.BlockSpec((batch, gps * RW_GHEADS, RW_HEAD, RW_HEAD), lambda g, c: (0, g, 0, 0))],
        out_shape=[jax.ShapeDtypeStruct((batch, seq, width), BF16),
                   jax.ShapeDtypeStruct((batch, n_groups * RW_GHEADS, RW_HEAD, RW_HEAD), F32)],
        scratch_shapes=[pltpu.VMEM((n_chain, RW_GROUP, RW_GROUP), F32), pltpu.VMEM((n_chain, 8, RW_GROUP), F32)],
        compiler_params=_cparams(2),
        name="rwkv_prompt",
    )(*([proj] * (4 * n_chain)), prm, mu_l, wl)
    return y.reshape(batch * seq, width), s


def _rwkv_sample_prep_kernel(pr_ref, pk_ref, pv_ref, pl_ref, sr_ref, sk_ref, sv_ref, sl_ref, prm_ref, mul_ref,
                             wl_ref, r_ref, k_ref, v_ref, g_ref, rt_ref, kt_ref, vt_ref, kkt_ref, kat_ref, dt_ref):
    bd = _head_block_ones(RW_GROUP)
    r, k, v, kk, a, logd, g = _rwkv_prep(pr_ref[...], pk_ref[...], pv_ref[...], pl_ref[...], sr_ref[...],
                                         sk_ref[...], sv_ref[...], sl_ref[...], prm_ref[...], mul_ref[...],
                                         wl_ref[0], bd)
    r_ref[...] = r
    k_ref[...] = k
    v_ref[...] = v
    g_ref[...] = g
    rt_ref[...] = r.T
    kt_ref[...] = k.T
    vt_ref[...] = v.T
    kkt_ref[...] = kk.T
    kat_ref[...] = (kk * a).T
    dt_ref[...] = jnp.exp(logd).T


def _rwkv_sample_prep(proj, shift0, prm, mu_l, wl, *, rows, rw_col0, n_groups):
    cb = rw_col0 // RW_GROUP
    pspec = lambda off: pl.BlockSpec((rows, RW_GROUP), lambda g, off=off: (0, cb + off + g))
    sspec = lambda off: pl.BlockSpec((rows, RW_GROUP), lambda g, off=off: (0, off + g))
    ospec = pl.BlockSpec((rows, RW_GROUP), lambda g: (0, g))
    tspec = pl.BlockSpec((RW_GROUP, rows), lambda g: (g, 0))
    width = n_groups * RW_GROUP
    return pl.pallas_call(
        _rwkv_sample_prep_kernel,
        grid=(n_groups,),
        in_specs=[pspec(0), pspec(n_groups), pspec(2 * n_groups),
                  pl.BlockSpec((rows, RW_GROUP), lambda g: (0, cb + 3 * n_groups)),
                  sspec(0), sspec(n_groups), sspec(2 * n_groups),
                  pl.BlockSpec((rows, RW_GROUP), lambda g: (0, 3 * n_groups)),
                  pl.BlockSpec((16, RW_GROUP), lambda g: (0, g)),
                  pl.BlockSpec((1, RW_LORA), lambda g: (0, 0)),
                  pl.BlockSpec((1, RW_LORA, 3 * RW_GROUP), lambda g: (g, 0, 0))],
        out_specs=[ospec] * 4 + [tspec] * 6,
        out_shape=[jax.ShapeDtypeStruct((rows, width), F32)] * 4 + [jax.ShapeDtypeStruct((width, rows), F32)] * 6,
        compiler_params=_cparams(1),
        name="rwkv_sample_prep",
    )(proj, proj, proj, proj, shift0, shift0, shift0, shift0, prm, mu_l, wl)


def _rwkv_sample_state_kernel(s_ref, rt_ref, kt_ref, kkt_ref, kat_ref, dt_ref, vt_ref, so_ref, yt_ref):
    n_kk = -kkt_ref[...]
    r, k, ka, d = rt_ref[...], kt_ref[...], kat_ref[...], dt_ref[...]
    for i0 in range(0, s_ref.shape[1], _SAMPLE_ROWS_PER_STAGE):
        rows = range(i0, i0 + _SAMPLE_ROWS_PER_STAGE)
        sa = [jnp.sum(s_ref[0, i] * n_kk, axis=0, keepdims=True) for i in rows]
        ys = []
        for i, sa_i in zip(rows, sa):
            s_new = s_ref[0, i] * d + sa_i * ka + vt_ref[i:i + 1, :] * k
            so_ref[0, i] = s_new
            ys.append(jnp.sum(s_new * r, axis=0, keepdims=True))
        yt_ref[i0:i0 + _SAMPLE_ROWS_PER_STAGE, :] = jnp.concatenate(ys, axis=0)


def _rwkv_sample_state(state_t, rt, kt, kkt, kat, dt, vt):
    nh, n, _, b = state_t.shape
    vspec = pl.BlockSpec((n, b), lambda h: (h, 0))
    sspec = pl.BlockSpec((1, n, n, b), lambda h: (h, 0, 0, 0))
    return pl.pallas_call(
        _rwkv_sample_state_kernel,
        grid=(nh,),
        in_specs=[sspec] + [vspec] * 6,
        out_specs=[sspec, vspec],
        out_shape=[jax.ShapeDtypeStruct(state_t.shape, F32), jax.ShapeDtypeStruct((nh * n, b), F32)],
        compiler_params=_cparams(1),
        name="rwkv_sample_state",
    )(state_t, rt, kt, kkt, kat, dt, vt)


def _rwkv_sample_post_kernel(yt_ref, r_ref, k_ref, v_ref, g_ref, prm_ref, o_ref):
    bd = _head_block_ones(RW_GROUP)
    o_ref[...] = _rwkv_post(yt_ref[...].T, r_ref[...], k_ref[...], v_ref[...], g_ref[...], prm_ref[...],
                            bd).astype(o_ref.dtype)


def _rwkv_sample_post(yt, r, k, v, g, prm):
    rows, width = r.shape
    spec = pl.BlockSpec((rows, RW_GROUP), lambda i: (0, i))
    return pl.pallas_call(
        _rwkv_sample_post_kernel,
        grid=(width // RW_GROUP,),
        in_specs=[pl.BlockSpec((RW_GROUP, rows), lambda i: (i, 0))] + [spec] * 4
                 + [pl.BlockSpec((16, RW_GROUP), lambda i: (0, i))],
        out_specs=spec,
        out_shape=jax.ShapeDtypeStruct((rows, width), BF16),
        compiler_params=_cparams(1),
        name="rwkv_sample_post",
    )(yt, r, k, v, g, prm)


def _mlstm_gates(gates, bias, h):
    capped = _soft_cap(gates + bias)
    lane = _iota(gates.shape, 1)
    i_col = jnp.sum(jnp.where(lane == h, capped, 0.0), axis=1, keepdims=True)
    f_col = jnp.sum(jnp.where(lane == h + ML_HEADS, -_softplus(-capped), 0.0), axis=1, keepdims=True)
    return i_col, f_col


def _mlstm_out(hh, o, norm_w):
    hn = hh * lax.rsqrt(jnp.mean(hh * hh, axis=1, keepdims=True) + EPS) * norm_w
    return hn * _sigmoid(o)


def _mlstm_prompt_kernel(*refs, n_chunks, batch):
    p_refs = refs[:5 * batch]
    gb_ref, nw_ref, y_ref, c_ref, nm_ref, c_scr, n_scr, m_scr = refs[5 * batch:]
    h = pl.program_id(0)
    c = pl.program_id(1)

    @pl.when(c == 0)
    def _():
        c_scr[...] = jnp.zeros_like(c_scr)
        n_scr[...] = jnp.zeros_like(n_scr)
        m_scr[...] = jnp.zeros_like(m_scr)

    def sequence(b):
        q_ref, k_ref, v_ref, o_ref, gt_ref = p_refs[5 * b:5 * b + 5]
        q = q_ref[...] * (ML_DK ** -0.5)
        k = k_ref[...]
        v = v_ref[...]
        L = q.shape[0]
        i_col, f_col = _mlstm_gates(gt_ref[...], gb_ref[...], h)
        t_i = _iota((L, L), 0)
        s_i = _iota((L, L), 1)
        causal = s_i <= t_i
        to_row = lambda col: jnp.sum(jnp.where(t_i == s_i, col, 0.0), axis=0, keepdims=True)
        i_row = to_row(i_col)
        f_row = to_row(f_col)
        yield
        b_col = jnp.sum(jnp.where(causal, f_row, 0.0), axis=1, keepdims=True)
        b_row = jnp.sum(jnp.where(t_i <= s_i, f_col, 0.0), axis=0, keepdims=True)
        m0 = m_scr[b]
        inter = b_col + m0
        dmat = jnp.where(causal, b_col - b_row + i_row, -1e30)
        yield
        m_t = jnp.maximum(inter, jnp.max(dmat, axis=1, keepdims=True))
        w_int = jnp.exp(inter - m_t)
        s = _dot_nt(q, k) * jnp.exp(dmat - m_t)
        c0 = c_scr[b]
        n0 = n_scr[b]
        yield
        num = w_int * _dot(q, c0) + _dot(s, v)
        den = w_int * jnp.sum(q * n0, axis=1, keepdims=True) + jnp.sum(s, axis=1, keepdims=True)
        hh = num / jnp.maximum(jnp.abs(den), jnp.exp(-m_t))
        m_new = m_t[L - 1:L, :]
        b_l = b_col[L - 1:L, :]
        a_end = jnp.exp(b_l - b_col + i_col - m_new)
        dec = jnp.exp(b_l + m0 - m_new)
        ka = k * a_end
        yield
        c_new = dec * c0 + _dot(ka.T, v)
        n_new = dec * n0 + jnp.sum(ka, axis=0, keepdims=True)
        out = _mlstm_out(hh, o_ref[...], nw_ref[...]).astype(y_ref.dtype)
        return out, c_new, n_new, m_new

    results = _interleave([sequence(b) for b in range(batch)])
    for b, (out, c_new, n_new, m_new) in enumerate(results):
        y_ref[b] = out
        c_scr[b] = c_new
        n_scr[b] = n_new
        m_scr[b] = m_new

    @pl.when(c == n_chunks - 1)
    def _():
        for b, (_, c_new, n_new, m_new) in enumerate(results):
            c_ref[b, 0] = c_new
            nm_ref[b, 0] = jnp.concatenate([n_new, jnp.broadcast_to(m_new, (7, ML_DK))], axis=0)


def _mlstm_prompt(proj, gate_bias, norm_w, *, batch, seq):
    L = ML_CHUNK
    nc = seq // L
    nh = ML_HEADS
    kq = nh * ML_DK
    p_specs = []
    for b in range(batch):
        rows = lambda c, b=b: b * nc + c
        p_specs += [pl.BlockSpec((L, ML_DK), lambda h, c, rows=rows: (rows(c), h)),
                    pl.BlockSpec((L, ML_DK), lambda h, c, rows=rows: (rows(c), nh + h)),
                    pl.BlockSpec((L, ML_DV), lambda h, c, rows=rows: (rows(c), 2 * kq // ML_DV + h)),
                    pl.BlockSpec((L, ML_DV), lambda h, c, rows=rows: (rows(c), 2 * kq // ML_DV + nh + h)),
                    pl.BlockSpec((L, LANES), lambda h, c, rows=rows: (rows(c), (2 * kq + 2 * ML_WIDTH) // LANES))]
    y, c_fin, nm = pl.pallas_call(
        functools.partial(_mlstm_prompt_kernel, n_chunks=nc, batch=batch),
        grid=(nh, nc),
        in_specs=p_specs + [pl.BlockSpec((1, LANES), lambda h, c: (0, 0)),
                            pl.BlockSpec((1, ML_DV), lambda h, c: (0, h))],
        out_specs=[pl.BlockSpec((batch, L, ML_DV), lambda h, c: (0, c, h)),
                   pl.BlockSpec((batch, 1, ML_DK, ML_DV), lambda h, c: (0, h, 0, 0)),
                   pl.BlockSpec((batch, 1, 8, ML_DK), lambda h, c: (0, h, 0, 0))],
        out_shape=[jax.ShapeDtypeStruct((batch, seq, ML_WIDTH), BF16),
                   jax.ShapeDtypeStruct((batch, nh, ML_DK, ML_DV), F32),
                   jax.ShapeDtypeStruct((batch, nh, 8, ML_DK), F32)],
        scratch_shapes=[pltpu.VMEM((batch, ML_DK, ML_DV), F32), pltpu.VMEM((batch, 1, ML_DK), F32),
                        pltpu.VMEM((batch, 1, 1), F32)],
        compiler_params=_cparams(2),
        name="mlstm_prompt",
    )(*([proj] * (5 * batch)), gate_bias, norm_w)
    return y.reshape(batch * seq, ML_WIDTH), c_fin, nm


def _mlstm_sample_kernel(q_ref, k_ref, v_ref, o_ref, gt_ref, gb_ref, nw_ref, c_ref, n_ref, m_ref,
                         y_ref, co_ref, no_ref, mo_ref):
    gates = gt_ref[0]
    eye = jnp.where(_iota((ML_DK, ML_DK), 0) == _iota((ML_DK, ML_DK), 1), 1.0, 0.0).astype(BF16)
    for h in range(ML_HEADS):
        q = q_ref[0][:, h * ML_DK:(h + 1) * ML_DK] * (ML_DK ** -0.5)
        k = k_ref[0][:, h * ML_DK:(h + 1) * ML_DK]
        v = v_ref[0][:, h * ML_DV:(h + 1) * ML_DV]
        i_pre, logf = _mlstm_gates(gates, gb_ref[...], h)
        c0 = c_ref[0, h]
        n0 = n_ref[0, h:h + 1, :]
        m0 = m_ref[0][:, h:h + 1]
        inter = logf + m0
        m_t = jnp.maximum(inter, i_pre)
        w_int = jnp.exp(inter - m_t)
        a_new = jnp.exp(i_pre - m_t)
        s = jnp.sum(q * k, axis=1, keepdims=True) * a_new
        qc = _dot(jnp.broadcast_to(q, (8, ML_DK)), c0)[0:1, :]
        num = w_int * qc + s * v
        den = w_int * jnp.sum(q * n0, axis=1, keepdims=True) + s
        hh = num / jnp.maximum(jnp.abs(den), jnp.exp(-m_t))
        k_hi, k_lo = _split2(jnp.broadcast_to(k, (8, ML_DK)))
        nt = (((1,), (1,)), ((), ()))
        k_col = (lax.dot_general(eye, k_hi, nt, preferred_element_type=F32)
                 + lax.dot_general(eye, k_lo, nt, preferred_element_type=F32))[:, 0:1]
        co_ref[0, h] = w_int * c0 + k_col * (a_new * v)
        no_ref[0, h:h + 1, :] = w_int * n0 + a_new * k
        mo_ref[0, :, h:h + 1] = m_t
        y_ref[0, :, h * ML_DV:(h + 1) * ML_DV] = _mlstm_out(
            hh, o_ref[0][:, h * ML_DV:(h + 1) * ML_DV], nw_ref[:, h * ML_DV:(h + 1) * ML_DV]).astype(y_ref.dtype)


def _mlstm_sample(proj3, gate_bias, norm_w, c0, n0, m0):
    b = proj3.shape[0]
    nh = ML_HEADS
    kq = nh * ML_DK
    p3 = lambda w, blk: pl.BlockSpec((1, 1, w), lambda i, blk=blk: (i, 0, blk))
    cspec = pl.BlockSpec((1, nh, ML_DK, ML_DV), lambda i: (i, 0, 0, 0))
    nspec = pl.BlockSpec((1, nh, ML_DK), lambda i: (i, 0, 0))
    mspec = pl.BlockSpec((1, 1, nh), lambda i: (i, 0, 0))
    return pl.pallas_call(
        _mlstm_sample_kernel,
        grid=(b,),
        in_specs=[p3(kq, 0), p3(kq, 1), p3(ML_WIDTH, 2 * kq // ML_WIDTH), p3(ML_WIDTH, 2 * kq // ML_WIDTH + 1),
                  p3(LANES, (2 * kq + 2 * ML_WIDTH) // LANES),
                  pl.BlockSpec((1, LANES), lambda i: (0, 0)),
                  pl.BlockSpec((1, ML_WIDTH), lambda i: (0, 0)),
                  cspec, nspec, mspec],
        out_specs=[pl.BlockSpec((1, 1, ML_WIDTH), lambda i: (i, 0, 0)), cspec, nspec, mspec],
        out_shape=[jax.ShapeDtypeStruct((b, 1, ML_WIDTH), BF16),
                   jax.ShapeDtypeStruct(c0.shape, F32), jax.ShapeDtypeStruct(n0.shape, F32),
                   jax.ShapeDtypeStruct(m0.shape, F32)],
        compiler_params=_cparams(1),
        name="mlstm_sample",
    )(proj3, proj3, proj3, proj3, proj3, gate_bias, norm_w, c0, n0, m0)


def _rms(x, w):
    return x * lax.rsqrt(jnp.mean(x * x, axis=1, keepdims=True) + EPS) * w


def _pack_w_in_kernel(wt_hbm, o_ref, buf, sem, *, n_a, split):
    j = pl.program_id(0)
    tr = o_ref.shape[0]

    def copy(step, slot):
        src = jnp.where(step < n_a, step * tr, split + (step - n_a) * tr)
        return pltpu.make_async_copy(wt_hbm.at[pl.ds(pl.multiple_of(src, 8), tr), :], buf.at[slot], sem.at[slot])

    pl.when(j == 0)(lambda: copy(j, 0).start())
    pl.when(j + 1 < pl.num_programs(0))(lambda: copy(j + 1, (j + 1) & 1).start())
    copy(j, j & 1).wait()
    o_ref[...] = buf[j & 1].astype(BF16)


def _pack_w_in(wt, *, split, width, tr):
    d = wt.shape[1]
    assert split % 8 == 0 and width % tr == 0 and split + width == wt.shape[0]
    n_a = width // tr
    return pl.pallas_call(
        functools.partial(_pack_w_in_kernel, n_a=n_a, split=split),
        grid=(2 * n_a,),
        in_specs=[pl.BlockSpec(memory_space=pl.ANY)],
        out_specs=pl.BlockSpec((tr, d), lambda j: (j, 0)),
        out_shape=jax.ShapeDtypeStruct((2 * width, d), BF16),
        scratch_shapes=[pltpu.VMEM((2, tr, d), F32), pltpu.SemaphoreType.DMA((2,))],
        compiler_params=_cparams(1, VMEM_LIMIT),
        name="pack_w_in",
    )(wt)


def _in_proj_kernel(x_ref, nw_ref, *refs):
    w_refs, o_ref, h_scr = refs[:-2], refs[-2], refs[-1]

    @pl.when(pl.program_id(1) == 0)
    def _():
        h_scr[...] = _rms(x_ref[...], nw_ref[...]).astype(BF16)

    h = h_scr[...]
    sub = w_refs[0].shape[0]
    for q, w_ref in enumerate(w_refs):
        o_ref[:, q * sub:(q + 1) * sub] = lax.dot_general(h, w_ref[...], (((1,), (1,)), ((), ())),
                                                          preferred_element_type=F32)


def _in_proj(x, norm_w, wt, *, tm, tn, n_streams):
    t, d = x.shape
    n = wt.shape[0]
    sub = tn // n_streams
    w_specs = [pl.BlockSpec((sub, d), lambda i, j, q=q: (j * n_streams + q, 0)) for q in range(n_streams)]
    return pl.pallas_call(
        _in_proj_kernel,
        grid=(t // tm, n // tn),
        in_specs=[pl.BlockSpec((tm, d), lambda i, j: (i, 0), pipeline_mode=pl.Buffered(1)),
                  pl.BlockSpec((1, d), lambda i, j: (0, 0))] + w_specs,
        out_specs=pl.BlockSpec((tm, tn), lambda i, j: (i, j)),
        out_shape=jax.ShapeDtypeStruct((t, n), F32),
        scratch_shapes=[pltpu.VMEM((tm, d), BF16)],
        compiler_params=_cparams(2, VMEM_LIMIT),
        name="in_proj",
    )(x, norm_w, *([wt] * n_streams))


def _out_proj_kernel(ya_ref, yb_ref, wa_ref, wb_ref, x_ref, o_ref):
    o_ref[...] = (x_ref[...] + jnp.dot(ya_ref[...], wa_ref[...], preferred_element_type=F32)
                  + jnp.dot(yb_ref[...], wb_ref[...], preferred_element_type=F32))


def _out_proj(ya, yb, w, x, *, tm, tn):
    t, kh = ya.shape
    n = w.shape[1]
    return pl.pallas_call(
        _out_proj_kernel,
        grid=(t // tm, n // tn),
        in_specs=[pl.BlockSpec((tm, kh), lambda i, j: (i, 0)),
                  pl.BlockSpec((tm, kh), lambda i, j: (i, 0)),
                  pl.BlockSpec((kh, tn), lambda i, j: (0, j)),
                  pl.BlockSpec((kh, tn), lambda i, j: (1, j)),
                  pl.BlockSpec((tm, tn), lambda i, j: (i, j))],
        out_specs=pl.BlockSpec((tm, tn), lambda i, j: (i, j)),
        out_shape=jax.ShapeDtypeStruct((t, n), F32),
        compiler_params=_cparams(2, VMEM_LIMIT),
        name="out_proj",
    )(ya, yb, w, w, x)


def _router_kernel(xp_ref, xs_ref, nw_ref, wh_ref, wl_ref, h_ref, lg_ref, *, nb_p):
    def emit(x_ref):
        h = _rms(x_ref[...], nw_ref[...])
        h_ref[...] = _pack_pairs(h, h.shape[1] // 2)
        h_hi, h_lo = _split2(h)
        dot = lambda a, b: jnp.dot(a, b, preferred_element_type=F32)
        lg_ref[...] = dot(h_hi, wh_ref[...]) + (dot(h_hi, wl_ref[...]) + dot(h_lo, wh_ref[...]))

    pl.when(pl.program_id(0) < nb_p)(lambda: emit(xp_ref))
    pl.when(pl.program_id(0) >= nb_p)(lambda: emit(xs_ref))


def _router(xp, xs, norm_w, wr, *, tm):
    d = xp.shape[1]
    nb_p, nb_s = xp.shape[0] // tm, xs.shape[0] // tm
    t = xp.shape[0] + xs.shape[0]
    wr_hi, wr_lo = _split2(wr)
    wspec = pl.BlockSpec((d, LANES), lambda i: (0, 0))
    return pl.pallas_call(
        functools.partial(_router_kernel, nb_p=nb_p),
        grid=(nb_p + nb_s,),
        in_specs=[pl.BlockSpec((tm, d), lambda i: (jnp.minimum(i, nb_p - 1), 0)),
                  pl.BlockSpec((tm, d), lambda i: (jnp.maximum(i - nb_p, 0), 0)),
                  pl.BlockSpec((1, d), lambda i: (0, 0)), wspec, wspec],
        out_specs=[pl.BlockSpec((tm, d // 2), lambda i: (i, 0)), pl.BlockSpec((tm, LANES), lambda i: (i, 0))],
        out_shape=[jax.ShapeDtypeStruct((t, d // 2), jnp.uint32), jax.ShapeDtypeStruct((t, LANES), F32)],
        compiler_params=_cparams(1, VMEM_LIMIT),
        name="router",
    )(xp, xs, norm_w, wr_hi, wr_lo)


def _row_gather(src_hbm, dst_vmem, idx_ref, base, n_rows, sem):
    def copy(r):
        return pltpu.make_async_copy(src_hbm.at[pl.ds(idx_ref[base + r], 1), :], dst_vmem.at[pl.ds(r, 1), :], sem)

    def start():
        def body(r, carry):
            copy(r).start(priority=1)
            return carry
        lax.fori_loop(0, n_rows, body, 0, unroll=8)

    def wait():
        def body(r, carry):
            copy(r).wait()
            return carry
        lax.fori_loop(0, n_rows, body, 0, unroll=8)

    return start, wait


def _moe_gather_kernel(tok_ref, nused_ref, h_hbm, o_ref, buf, sem, *, tm):
    i = pl.program_id(0)
    nused = nused_ref[0]
    gather = lambda blk: _row_gather(h_hbm, buf.at[blk & 1], tok_ref, blk * tm, tm, sem.at[blk & 1])

    pl.when(jnp.logical_and(i == 0, nused > 0))(lambda: gather(i)[0]())
    pl.when(i + 1 < nused)(lambda: gather(i + 1)[0]())

    @pl.when(i < nused)
    def _():
        gather(i)[1]()
        o_ref[...] = _unpack_pairs(buf[i & 1], buf.shape[2]).astype(o_ref.dtype)

    @pl.when(i >= nused)
    def _():
        o_ref[...] = jnp.zeros_like(o_ref)


def _moe_gather(slot_tok, nused, h, *, tm):
    p = slot_tok.shape[0]
    half = h.shape[1]
    return pl.pallas_call(
        functools.partial(_moe_gather_kernel, tm=tm),
        grid_spec=pltpu.PrefetchScalarGridSpec(
            num_scalar_prefetch=2,
            grid=(p // tm,),
            in_specs=[pl.BlockSpec(memory_space=pl.ANY)],
            out_specs=pl.BlockSpec((tm, 2 * half), lambda i, tok, nu: (i, 0)),
            scratch_shapes=[pltpu.VMEM((2, tm, half), jnp.uint32), pltpu.SemaphoreType.DMA((2,))]),
        out_shape=jax.ShapeDtypeStruct((p, 2 * half), BF16),
        compiler_params=_cparams(1, VMEM_LIMIT),
        name="moe_gather",
    )(slot_tok, nused, h)


def _expert_row_loop(n_blocks, in_copy, out_copy, compute):
    in_copy(0, 0).start(priority=1)

    def body(r, carry):
        slot = r & 1
        pl.when(r + 1 < n_blocks)(lambda: in_copy(r + 1, 1 - slot).start(priority=1))
        in_copy(r, slot).wait()
        pl.when(r >= 2)(lambda: out_copy(r - 2, slot).wait())
        compute(r, slot)
        out_copy(r, slot).start(priority=1)
        return carry

    lax.fori_loop(0, n_blocks, body, 0)
    pl.when(n_blocks >= 2)(lambda: out_copy(n_blocks - 2, n_blocks & 1).wait())
    out_copy(n_blocks - 1, (n_blocks - 1) & 1).wait()


def _zero_tail_blocks(first, zero_block, dst_hbm, sem, tm):
    zero_block[...] = jnp.zeros_like(zero_block)
    copy = lambda t: pltpu.make_async_copy(zero_block, dst_hbm.at[pl.ds(pl.multiple_of(t * tm, tm), tm), :], sem)
    n_total = dst_hbm.shape[0] // tm

    def start(t, carry):
        copy(t).start()
        return carry

    def wait(t, carry):
        copy(t).wait()
        return carry

    lax.fori_loop(first, n_total, start, 0)
    lax.fori_loop(first, n_total, wait, 0)


_W_SLAB = 256
_SLABS_PER_BLOCK = 8
_SLABS_IN_FLIGHT = 4


def _moe_up_kernel(bstart_ref, bcount_ref, w1_hbm, w3_hbm, xs_hbm, h_hbm, wb, stage, xbuf, obuf,
                   w_sem, in_sem, out_sem, *, tm):
    e = pl.program_id(0)
    n_exp = pl.num_programs(0)
    n_slab = wb.shape[2] // _W_SLAB
    cur = e & 1
    n_blocks = bcount_ref[e]
    row = lambda r: pl.multiple_of((bstart_ref[e] + r) * tm, tm)
    slab_rows = lambda s: pl.ds(pl.multiple_of(s * _W_SLAB, _W_SLAB), _W_SLAB)

    depth = stage.shape[0]

    def slab_copies(expert, s):
        slot = s & (depth - 1)
        return [pltpu.make_async_copy(w_hbm.at[expert, slab_rows(s), :], stage.at[slot, m], w_sem.at[slot])
                for m, w_hbm in enumerate((w1_hbm, w3_hbm))]

    def start_slab(expert, s):
        for cp in slab_copies(expert, s):
            cp.start()

    def convert_slab(expert, s, dst):
        for cp in slab_copies(expert, s):
            cp.wait()
        for m in range(2):
            wb[dst, m, slab_rows(s), :] = stage[s & (depth - 1), m].astype(BF16)
        pl.when(s + depth < n_slab)(lambda: start_slab(expert, s + depth))

    def convert_range(expert, lo, hi, dst):
        def body(s, carry):
            convert_slab(expert, s, dst)
            return carry
        lax.fori_loop(lo, hi, body, 0)

    @pl.when(e == 0)
    def _():
        for s in range(depth):
            start_slab(0, s)
        convert_range(0, 0, n_slab, 0)

    has_next = e + 1 < n_exp

    @pl.when(has_next)
    def _():
        for s in range(depth):
            start_slab(e + 1, s)

    def in_copy(r, slot):
        return pltpu.make_async_copy(xs_hbm.at[pl.ds(row(r), tm), :], xbuf.at[slot], in_sem.at[slot])

    def out_copy(r, slot):
        return pltpu.make_async_copy(obuf.at[slot], h_hbm.at[pl.ds(row(r), tm), :], out_sem.at[slot])

    def compute(r, slot):
        n_tiles = _SLABS_PER_BLOCK // 2
        tf = wb.shape[3] // n_tiles
        for t in range(n_tiles):
            x = xbuf[slot]
            cols = pl.ds(t * tf, tf)
            a = jnp.dot(x, wb[cur, 0, :, cols], preferred_element_type=F32)
            b = jnp.dot(x, wb[cur, 1, :, cols], preferred_element_type=F32)
            obuf[slot, :, cols] = (a * _sigmoid(a) * b).astype(BF16)
            for q in range(2):
                s = r * _SLABS_PER_BLOCK + 2 * t + q
                pl.when(jnp.logical_and(has_next, s < n_slab))(lambda s=s: convert_slab(e + 1, s, 1 - cur))

    pl.when(n_blocks > 0)(lambda: _expert_row_loop(n_blocks, in_copy, out_copy, compute))

    @pl.when(has_next)
    def _():
        convert_range(e + 1, jnp.minimum(n_blocks * _SLABS_PER_BLOCK, n_slab), n_slab, 1 - cur)

    @pl.when(e == n_exp - 1)
    def _():
        _zero_tail_blocks(bstart_ref[e] + n_blocks, obuf.at[0], h_hbm, out_sem.at[0], tm)


def _moe_up(bstart, bcount, xs, w1, w3, *, tm):
    p, d = xs.shape
    n_exp, _, ff = w1.shape
    any_spec = pl.BlockSpec(memory_space=pl.ANY)
    return pl.pallas_call(
        functools.partial(_moe_up_kernel, tm=tm),
        grid_spec=pltpu.PrefetchScalarGridSpec(
            num_scalar_prefetch=2,
            grid=(n_exp,),
            in_specs=[any_spec] * 3,
            out_specs=any_spec,
            scratch_shapes=[pltpu.VMEM((2, 2, d, ff), BF16), pltpu.VMEM((_SLABS_IN_FLIGHT, 2, _W_SLAB, ff), F32),
                            pltpu.VMEM((2, tm, d), BF16), pltpu.VMEM((2, tm, ff), BF16),
                            pltpu.SemaphoreType.DMA((_SLABS_IN_FLIGHT,)), pltpu.SemaphoreType.DMA((2,)),
                            pltpu.SemaphoreType.DMA((2,))]),
        out_shape=jax.ShapeDtypeStruct((p, ff), BF16),
        compiler_params=_cparams(1, VMEM_LIMIT),
        name="moe_up",
    )(bstart, bcount, w1, w3, xs)


def _moe_down_kernel(bstart_ref, bcount_ref, *refs, tm, tn, n_streams):
    w2_refs = refs[:n_streams]
    hs_hbm, yb_hbm, w2b, hbuf, obuf, in_sem, out_sem = refs[n_streams:]
    e = pl.program_id(0)
    n_blocks = bcount_ref[e]
    row = lambda r: pl.multiple_of((bstart_ref[e] + r) * tm, tm)

    @pl.when(n_blocks > 0)
    def _():
        sub = w2b.shape[0] // n_streams
        for q, w2_ref in enumerate(w2_refs):
            w2b[q * sub:(q + 1) * sub, :] = w2_ref[0].astype(BF16)

        def in_copy(r, slot):
            return pltpu.make_async_copy(hs_hbm.at[pl.ds(row(r), tm), :], hbuf.at[slot], in_sem.at[slot])

        def out_copy(r, slot):
            return pltpu.make_async_copy(obuf.at[slot], yb_hbm.at[pl.ds(row(r), tm), :], out_sem.at[slot])

        def compute(r, slot):
            h = hbuf[slot]
            for n in range(w2b.shape[1] // tn):
                y = jnp.dot(h, w2b[:, n * tn:(n + 1) * tn], preferred_element_type=F32)
                obuf[slot, :, n * tn // 2:(n + 1) * tn // 2] = _pack_pairs(y, tn // 2)

        _expert_row_loop(n_blocks, in_copy, out_copy, compute)

    @pl.when(e == pl.num_programs(0) - 1)
    def _():
        _zero_tail_blocks(bstart_ref[e] + n_blocks, obuf.at[0], yb_hbm, out_sem.at[0], tm)


def _moe_down(bstart, bcount, hs, w2, *, tm, tn, n_streams):
    p = hs.shape[0]
    n_exp, ff, d = w2.shape
    w_specs = [pl.BlockSpec((1, ff // n_streams, d), lambda e, bs, bc, q=q: (e, q, 0)) for q in range(n_streams)]
    return pl.pallas_call(
        functools.partial(_moe_down_kernel, tm=tm, tn=tn, n_streams=n_streams),
        grid_spec=pltpu.PrefetchScalarGridSpec(
            num_scalar_prefetch=2,
            grid=(n_exp,),
            in_specs=w_specs + [pl.BlockSpec(memory_space=pl.ANY)],
            out_specs=pl.BlockSpec(memory_space=pl.ANY),
            scratch_shapes=[pltpu.VMEM((ff, d), BF16), pltpu.VMEM((2, tm, ff), BF16),
                            pltpu.VMEM((2, tm, d // 2), jnp.uint32),
                            pltpu.SemaphoreType.DMA((2,)), pltpu.SemaphoreType.DMA((2,))]),
        out_shape=jax.ShapeDtypeStruct((p, d // 2), jnp.uint32),
        compiler_params=_cparams(1, VMEM_LIMIT),
        name="moe_down",
    )(bstart, bcount, *([w2] * n_streams), hs)


def _moe_combine_kernel(p0_ref, p1_ref, xp_ref, xs_ref, g0_ref, g1_ref, nw_ref, yb_hbm, op_ref, os_ref,
                        buf0, buf1, sem, *, tc, nb_p, half):
    i = pl.program_id(0)
    slot = i & 1

    def gathers(blk):
        s = blk & 1
        return [_row_gather(yb_hbm, buf.at[s], p_ref, blk * tc, tc, sem.at[s])
                for buf, p_ref in ((buf0, p0_ref), (buf1, p1_ref))]

    def start_all(blk):
        for start, _ in gathers(blk):
            start()

    pl.when(i == 0)(lambda: start_all(i))
    pl.when(i + 1 < pl.num_programs(0))(lambda: start_all(i + 1))
    for _, wait in gathers(i):
        wait()

    def emit(x_ref, o_ref):
        x = (x_ref[...] + g0_ref[:, 0:1] * _unpack_pairs(buf0[slot], half)
             + g1_ref[:, 0:1] * _unpack_pairs(buf1[slot], half))
        o_ref[...] = _rms(x, nw_ref[...])

    pl.when(i < nb_p)(lambda: emit(xp_ref, op_ref))
    pl.when(i >= nb_p)(lambda: emit(xs_ref, os_ref))


def _moe_combine(pos0, pos1, xp, xs, g0, g1, norm_w, yb, *, tc, half):
    d = xp.shape[1]
    nb_p, nb_s = xp.shape[0] // tc, xs.shape[0] // tc
    p_idx = lambda i, a, b: (jnp.minimum(i, nb_p - 1), 0)
    s_idx = lambda i, a, b: (jnp.maximum(i - nb_p, 0), 0)
    return pl.pallas_call(
        functools.partial(_moe_combine_kernel, tc=tc, nb_p=nb_p, half=half),
        grid_spec=pltpu.PrefetchScalarGridSpec(
            num_scalar_prefetch=2,
            grid=(nb_p + nb_s,),
            in_specs=[pl.BlockSpec((tc, d), p_idx),
                      pl.BlockSpec((tc, d), s_idx),
                      pl.BlockSpec((tc, LANES), lambda i, a, b: (i, 0)),
                      pl.BlockSpec((tc, LANES), lambda i, a, b: (i, 0)),
                      pl.BlockSpec((1, d), lambda i, a, b: (0, 0)),
                      pl.BlockSpec(memory_space=pl.ANY)],
            out_specs=[pl.BlockSpec((tc, d), p_idx), pl.BlockSpec((tc, d), s_idx)],
            scratch_shapes=[pltpu.VMEM((2, tc, d // 2), jnp.uint32), pltpu.VMEM((2, tc, d // 2), jnp.uint32),
                            pltpu.SemaphoreType.DMA((2,))]),
        out_shape=[jax.ShapeDtypeStruct(xp.shape, F32), jax.ShapeDtypeStruct(xs.shape, F32)],
        compiler_params=_cparams(1, VMEM_LIMIT),
        name="moe_combine",
    )(pos0, pos1, xp, xs, g0, g1, norm_w, yb)


def _route(logits, bg, be, *, tm):
    t = logits.shape[0]
    pg = jax.nn.softmax(logits[:, :N_GROUPS] + bg, axis=-1)
    g_idx = jnp.argmax(pg, axis=-1).astype(jnp.int32)
    p_sel = jnp.take_along_axis(pg, g_idx[:, None], axis=-1)
    le = (logits[:, N_GROUPS:N_GROUPS + N_EXPERTS] + be).reshape(t, N_GROUPS, EXPERTS_PER_GROUP)
    le = jnp.take_along_axis(le, g_idx[:, None, None], axis=1)[:, 0]
    pe = jax.nn.softmax(le, axis=-1)
    top_v, top_i = lax.top_k(pe, TOP_K)
    gate = p_sel * top_v / jnp.sum(top_v, axis=-1, keepdims=True)
    e_flat = (g_idx[:, None] * EXPERTS_PER_GROUP + top_i.astype(jnp.int32)).reshape(-1)
    a = t * TOP_K
    onehot = (e_flat[:, None] == jnp.arange(N_EXPERTS, dtype=jnp.int32)[None, :]).astype(jnp.int32)
    rank = jnp.sum((jnp.cumsum(onehot, axis=0) - onehot) * onehot, axis=1)
    counts = jnp.sum(onehot, axis=0)
    padded = (counts + tm - 1) // tm * tm
    pad_end = jnp.cumsum(padded)
    dest = (pad_end - padded)[e_flat] + rank
    n_blocks = -(-(a + N_EXPERTS * (tm - 1)) // tm)
    tok = jnp.arange(a, dtype=jnp.int32) // TOP_K
    slot_tok = jnp.zeros((n_blocks * tm,), jnp.int32).at[dest].set(tok)
    bstart = ((pad_end - padded) // tm).astype(jnp.int32)
    bcount = (padded // tm).astype(jnp.int32)
    nused = (pad_end[-1] // tm).astype(jnp.int32).reshape(1)
    dest = dest.reshape(t, TOP_K)
    return gate, dest[:, 0], dest[:, 1], slot_tok, bstart, bcount, nused


def _moe(xp, xs, norm2_w, wr, bg, be, w1, w3, w2, final_w, *, tm_route, tm_blk, tn, tc):
    h, logits = _router(xp, xs, norm2_w, wr, tm=tm_route)
    gate, pos0, pos1, slot_tok, bstart, bcount, nused = _route(logits, bg, be, tm=tm_blk)
    xg = _moe_gather(slot_tok, nused, h, tm=tm_blk)
    hs = _moe_up(bstart, bcount, xg, w1, w3, tm=tm_blk)
    yb = _moe_down(bstart, bcount, hs, w2, tm=tm_blk, tn=tn, n_streams=4)
    g0 = jnp.broadcast_to(gate[:, 0:1], (h.shape[0], LANES))
    g1 = jnp.broadcast_to(gate[:, 1:2], (h.shape[0], LANES))
    return _moe_combine(pos0, pos1, xp, xs, g0, g1, final_w, yb, tc=tc, half=tn // 2)


def kernel(x_prompt, x_sample, state_mlstm_c, state_mlstm_n, state_mlstm_m, state_rwkv, state_rwkv_shift, norm1_w, w_in, w_out, ml_b_i, ml_b_f, ml_norm_w, rw_mu, rw_w0, rw_w2, rw_a0, rw_a2, rw_g2, rw_k_k, rw_k_a, rw_r_k, rw_ln_w, rw_ln_b, norm2_w, router_group_w, router_group_b, router_expert_w, router_expert_b, moe_w1, moe_w3, moe_w2, final_norm_w):
    assert w_in.shape[0] == 1, "single-layer trunk"
    bp, seq, d = x_prompt.shape
    bs = x_sample.shape[0]
    tp = bp * seq
    ml_proj = 2 * ML_HEADS * ML_DK + 2 * ML_WIDTH + 2 * ML_HEADS
    rw_proj = 3 * RW_WIDTH + RW_LORA
    rw_col0 = -(-ml_proj // RW_GROUP) * RW_GROUP
    n_groups = RW_WIDTH // RW_GROUP

    w_cat = _pack_w_in(jnp.swapaxes(w_in[0], 0, 1), split=ml_proj, width=rw_col0, tr=RW_GROUP)
    w_o = w_out[0].astype(BF16)
    mu = rw_mu[0]
    rows = [rw_w0[0], rw_a0[0], rw_k_k[0], rw_k_a[0], rw_r_k[0].reshape(-1), rw_ln_w[0], rw_ln_b[0],
            mu[:RW_WIDTH], mu[RW_WIDTH:2 * RW_WIDTH], mu[2 * RW_WIDTH:3 * RW_WIDTH]]
    prm = jnp.concatenate([jnp.stack(rows), jnp.zeros((16 - len(rows), RW_WIDTH), F32)], axis=0)
    mu_l = mu[3 * RW_WIDTH:][None]
    wl = jnp.zeros((RW_LORA, 3, RW_WIDTH), F32)
    wl = wl.at[:RW_DECAY_LORA, 0].set(rw_w2[0]).at[RW_DECAY_LORA:RW_DECAY_LORA + RW_A_LORA, 1].set(rw_a2[0])
    wl = wl.at[RW_DECAY_LORA + RW_A_LORA:, 2].set(rw_g2[0])
    wl = wl.reshape(RW_LORA, 3, n_groups, RW_GROUP).transpose(2, 0, 1, 3).reshape(n_groups, RW_LORA, 3 * RW_GROUP)
    wl = wl.astype(BF16)
    gate_bias = jnp.zeros((1, LANES), F32).at[0, :ML_HEADS].set(ml_b_i[0]).at[0, ML_HEADS:2 * ML_HEADS].set(ml_b_f[0])
    ml_nw = ml_norm_w[0][None]
    wr = jnp.concatenate([router_group_w[0], router_expert_w[0],
                          jnp.zeros((d, LANES - N_GROUPS - N_EXPERTS), F32)], axis=1)

    xp = x_prompt.reshape(tp, d)
    xs = x_sample.reshape(bs, d)
    proj_p = _in_proj(xp, norm1_w, w_cat, tm=512, tn=1280, n_streams=5)
    proj_s = _in_proj(xs, norm1_w, w_cat, tm=bs, tn=1280, n_streams=5)

    y_ml_p, p_c, p_nm = _mlstm_prompt(proj_p, gate_bias, ml_nw, batch=bp, seq=seq)
    y_rw_p, p_s = _rwkv_prompt(proj_p, prm, mu_l, wl, batch=bp, seq=seq, rw_col0=rw_col0, n_groups=n_groups, gps=2)
    p_sh = jnp.concatenate([lax.slice(proj_p, (b * seq + seq - 1, rw_col0), (b * seq + seq, rw_col0 + rw_proj))
                            for b in range(bp)], axis=0)

    y_ml_s, s_c, s_n, s_m = _mlstm_sample(proj_s[:, None, :], gate_bias, ml_nw, state_mlstm_c[0], state_mlstm_n[0],
                                          state_mlstm_m[0][:, None, :])
    r, k, v, g, rt, kt, vt, kkt, kat, dt = _rwkv_sample_prep(proj_s, state_rwkv_shift[0], prm, mu_l, wl, rows=bs,
                                                             rw_col0=rw_col0, n_groups=n_groups)
    s_t, y_t = _rwkv_sample_state(jnp.transpose(state_rwkv[0], (1, 2, 3, 0)), rt, kt, kkt, kat, dt, vt)
    s_s = jnp.transpose(s_t, (3, 0, 1, 2))
    y_rw_s = _rwkv_sample_post(y_t, r, k, v, g, prm)
    s_sh = proj_s[:, rw_col0:rw_col0 + rw_proj]

    x2_p = _out_proj(y_ml_p, y_rw_p, w_o, xp, tm=512, tn=1024)
    x2_s = _out_proj(y_ml_s.reshape(bs, ML_WIDTH), y_rw_s, w_o, xs, tm=bs, tn=1024)
    y_p, y_s = _moe(x2_p, x2_s, norm2_w, wr, router_group_b[0], router_expert_b[0], moe_w1[0], moe_w3[0],
                    moe_w2[0], final_norm_w[None], tm_route=128, tm_blk=256, tn=1024, tc=128)

    lead = lambda a: a[None]
    return (y_p.reshape(bp, seq, d), y_s.reshape(bs, 1, d),
            lead(p_c), lead(p_nm[:, :, 0]), lead(p_nm[:, :, 1, 0]), lead(p_s), lead(p_sh),
            lead(s_c), lead(s_n), lead(s_m[:, 0]), lead(s_s), lead(s_sh))
```
